```python
import jax, jax.numpy as jnp
from jax import lax
import numpy as np

D_MODEL = 1024
BATCH = 8
SEQ = 2048
DEPTH = 1
DEC_BATCH = 128
DEC_SEQ = 8
PAST_LEN = 16384
PAGE_SIZE = 128

D_MIX = D_MODEL
D_A = D_MIX // 2
D_B = D_MIX - D_A
GROUP_DIM = 64
N_GROUPS_A = D_A // GROUP_DIM
N_GROUPS_B = D_B // GROUP_DIM
CONV_A = 3
CONV_B = 31
IN_COLS = 3 * D_A + 2 * D_B
N_EXPERT_GROUPS = 4
EXPERTS_PER_GROUP = 8
N_EXPERTS = N_EXPERT_GROUPS * EXPERTS_PER_GROUP
TOP_K_FINE = 2
D_EXPERT = D_MODEL // 4
RMS_EPS = 1e-6
LN_EPS = 1e-5

kernel_name = "hymba_conv_hier_moe_step"


def _rmsnorm(x, g):
    xf = x.astype(jnp.float32)
    y = xf * lax.rsqrt(jnp.mean(xf * xf, axis=-1, keepdims=True) + RMS_EPS)
    return (y * g.astype(jnp.float32)).astype(x.dtype)


def _layernorm(x, g, b):
    xf = x.astype(jnp.float32)
    mu = jnp.mean(xf, axis=-1, keepdims=True)
    var = jnp.mean(jnp.square(xf - mu), axis=-1, keepdims=True)
    y = (xf - mu) * lax.rsqrt(var + LN_EPS)
    return (y * g.astype(jnp.float32) + b.astype(jnp.float32)).astype(x.dtype)


def _causal_dwconv(u_pad, w):
    c = u_pad.shape[-1]
    return lax.conv_general_dilated(
        u_pad, w[:, None, :].astype(u_pad.dtype), window_strides=(1,), padding='VALID',
        dimension_numbers=('NWC', 'WIO', 'NWC'), feature_group_count=c)


def _hier_moe(h, w_coarse, b_coarse, w_fine, b_fine, w_gate, w_up, w_down):
    n, t_len, d = h.shape
    t = h.reshape(n * t_len, d)
    coarse_logits = (t @ w_coarse).astype(jnp.float32) + b_coarse.astype(jnp.float32)
    coarse_prob = jax.nn.softmax(coarse_logits, axis=-1)
    grp = jnp.argmax(coarse_logits, axis=-1)
    p_grp = jnp.take_along_axis(coarse_prob, grp[:, None], axis=-1)
    fine_logits = jnp.einsum('td,gde->tge', t, w_fine).astype(jnp.float32) + b_fine.astype(jnp.float32)
    fine_sel = jnp.take_along_axis(fine_logits, grp[:, None, None], axis=1)[:, 0]
    top_v, top_i = lax.top_k(fine_sel, TOP_K_FINE)
    p_exp = jax.nn.softmax(top_v, axis=-1) * p_grp
    expert_idx = grp[:, None] * EXPERTS_PER_GROUP + top_i
    combine = jnp.sum(jax.nn.one_hot(expert_idx, N_EXPERTS, dtype=jnp.float32) * p_exp[..., None], axis=1)
    out = jnp.zeros((n * t_len, d), jnp.float32)
    for e in range(N_EXPERTS):
        hid = jax.nn.silu(t @ w_gate[e]) * (t @ w_up[e])
        out = out + combine[:, e:e + 1] * (hid @ w_down[e]).astype(jnp.float32)
    return out.astype(h.dtype).reshape(n, t_len, d)


def _layer(x, prev_a, prev_b, g_mix, w_in, conv_a_w, conv_b_w, conv_b_bias, ln_g, ln_b, w_out,
           g_ffn, w_coarse, b_coarse, w_fine, b_fine, w_gate, w_up, w_down):
    h = _rmsnorm(x, g_mix)
    proj = h @ w_in
    b_a = proj[..., :D_A]
    c_a = proj[..., D_A:2 * D_A]
    v_a = proj[..., 2 * D_A:3 * D_A]
    v_b = proj[..., 3 * D_A:3 * D_A + D_B]
    g_b = proj[..., 3 * D_A + D_B:]
    u_pad = jnp.concatenate([prev_a.astype(x.dtype), c_a * v_a], axis=1)
    z_a = b_a * _causal_dwconv(u_pad, conv_a_w)
    new_a = u_pad[:, -(CONV_A - 1):]
    g_pad = jnp.concatenate([prev_b.astype(x.dtype), v_b * jax.nn.sigmoid(g_b)], axis=1)
    z_b = _causal_dwconv(g_pad, conv_b_w) + conv_b_bias
    z_b = jax.nn.silu(_layernorm(z_b, ln_g, ln_b))
    new_b = g_pad[:, -(CONV_B - 1):]
    x = x + jnp.concatenate([z_a, z_b], axis=-1) @ w_out
    x = x + _hier_moe(_rmsnorm(x, g_ffn), w_coarse, b_coarse, w_fine, b_fine, w_gate, w_up, w_down)
    return x, new_a, new_b


def setup_inputs(seed: int = 0) -> dict:
    key = jax.random.key(seed)
    ks = jax.random.split(key, 24)
    f32 = jnp.float32
    nrm = lambda k, shape, s: jax.random.normal(k, shape, f32) * s
    return {
        "x_prompt": nrm(ks[0], (BATCH, SEQ, D_MODEL), 1.0),
        "x_sample": nrm(ks[1], (DEC_BATCH, DEC_SEQ, D_MODEL), 1.0),
        "state_conv_a": nrm(ks[2], (DEPTH, DEC_BATCH, CONV_A - 1, D_A), 1.0),
        "state_conv_b": nrm(ks[3], (DEPTH, DEC_BATCH, CONV_B - 1, D_B), 1.0),
        "g_mix": 1.0 + nrm(ks[4], (DEPTH, D_MODEL), 0.02),
        "w_in": nrm(ks[5], (DEPTH, D_MODEL, IN_COLS), D_MODEL ** -0.5),
        "conv_a_w": nrm(ks[6], (DEPTH, CONV_A, D_A), CONV_A ** -0.5),
        "conv_b_w": nrm(ks[7], (DEPTH, CONV_B, D_B), CONV_B ** -0.5),
        "conv_b_bias": nrm(ks[8], (DEPTH, D_B), 0.02),
        "ln_g": 1.0 + nrm(ks[9], (DEPTH, D_B), 0.02),
        "ln_b": nrm(ks[10], (DEPTH, D_B), 0.02),
        "w_out": nrm(ks[11], (DEPTH, D_MIX, D_MODEL), D_MIX ** -0.5),
        "g_ffn": 1.0 + nrm(ks[12], (DEPTH, D_MODEL), 0.02),
        "w_coarse": nrm(ks[13], (DEPTH, D_MODEL, N_EXPERT_GROUPS), D_MODEL ** -0.5),
        "b_coarse": nrm(ks[14], (DEPTH, N_EXPERT_GROUPS), 0.01),
        "w_fine": nrm(ks[15], (DEPTH, N_EXPERT_GROUPS, D_MODEL, EXPERTS_PER_GROUP), D_MODEL ** -0.5),
        "b_fine": nrm(ks[16], (DEPTH, N_EXPERT_GROUPS, EXPERTS_PER_GROUP), 0.01),
        "w_gate": nrm(ks[17], (DEPTH, N_EXPERTS, D_MODEL, D_EXPERT), D_MODEL ** -0.5),
        "w_up": nrm(ks[18], (DEPTH, N_EXPERTS, D_MODEL, D_EXPERT), D_MODEL ** -0.5),
        "w_down": nrm(ks[19], (DEPTH, N_EXPERTS, D_EXPERT, D_MODEL), D_EXPERT ** -0.5),
        "g_final": 1.0 + nrm(ks[20], (D_MODEL,), 0.02),
    }


def reference(x_prompt, x_sample, state_conv_a, state_conv_b, g_mix, w_in, conv_a_w, conv_b_w,
              conv_b_bias, ln_g, ln_b, w_out, g_ffn, w_coarse, b_coarse, w_fine, b_fine,
              w_gate, w_up, w_down, g_final):
    xp, xs = x_prompt, x_sample
    na_p, nb_p, na_s, nb_s = [], [], [], []
    for l in range(DEPTH):
        params = (g_mix[l], w_in[l], conv_a_w[l], conv_b_w[l], conv_b_bias[l], ln_g[l], ln_b[l],
                  w_out[l], g_ffn[l], w_coarse[l], b_coarse[l], w_fine[l], b_fine[l],
                  w_gate[l], w_up[l], w_down[l])
        zero_a = jnp.zeros((xp.shape[0], CONV_A - 1, D_A), xp.dtype)
        zero_b = jnp.zeros((xp.shape[0], CONV_B - 1, D_B), xp.dtype)
        xp, a_p, b_p = _layer(xp, zero_a, zero_b, *params)
        xs, a_s, b_s = _layer(xs, state_conv_a[l], state_conv_b[l], *params)
        na_p.append(a_p); nb_p.append(b_p); na_s.append(a_s); nb_s.append(b_s)
    y_prompt = _rmsnorm(xp, g_final)
    y_sample = _rmsnorm(xs, g_final)
    new_conv_a_prompt = jnp.stack(na_p, axis=0)
    new_conv_b_prompt = jnp.stack(nb_p, axis=0)
    new_conv_a_sample = jnp.stack(na_s, axis=0)
    new_conv_b_sample = jnp.stack(nb_s, axis=0)
    return (y_prompt, y_sample, new_conv_a_prompt, new_conv_b_prompt, new_conv_a_sample, new_conv_b_sample)
```

```python
import functools

import jax
import jax.numpy as jnp
from jax import lax
from jax.experimental import pallas as pl
from jax.experimental.pallas import tpu as pltpu

D_MODEL = 1024
D_A = 512
D_B = 512
CONV_A = 3
CONV_B = 31
HALO_A = CONV_A - 1
HALO_B = CONV_B - 1
IN_COLS = 3 * D_A + 2 * D_B
N_EXPERT_GROUPS = 4
EXPERTS_PER_GROUP = 8
N_EXPERTS = N_EXPERT_GROUPS * EXPERTS_PER_GROUP
D_EXPERT = D_MODEL // 4
RMS_EPS = 1e-6
LN_EPS = 1e-5

SUBLANES = 8
PAD_A = SUBLANES
PAD_B = 32
ROUTER_ROWS = SUBLANES + N_EXPERTS
NEG_BIG = -1e30
VMEM_LIMIT = 56 * 1024 * 1024


def _rms_scale(x):
    return x * lax.rsqrt(jnp.mean(x * x, axis=-1, keepdims=True) + RMS_EPS)


def _dot_nt(a, b):
    return lax.dot_general(a, b, (((1,), (1,)), ((), ())), preferred_element_type=jnp.float32)


def _route(logits_t):
    rows = logits_t.shape[1]
    iota = lax.broadcasted_iota(jnp.int32, (SUBLANES, rows), 0)
    lc = logits_t[0:SUBLANES]
    cmax = jnp.max(lc, axis=0, keepdims=True)
    grp = jnp.min(jnp.where(lc == cmax, iota, SUBLANES), axis=0, keepdims=True)
    p_grp = 1.0 / jnp.sum(jnp.exp(lc - cmax), axis=0, keepdims=True)
    sel = logits_t[SUBLANES:2 * SUBLANES]
    for g in range(1, N_EXPERT_GROUPS):
        sel = jnp.where(grp == g, logits_t[(g + 1) * SUBLANES:(g + 2) * SUBLANES], sel)
    v1 = jnp.max(sel, axis=0, keepdims=True)
    i1 = jnp.min(jnp.where(sel == v1, iota, SUBLANES), axis=0, keepdims=True)
    sel2 = jnp.where(iota == i1, -jnp.inf, sel)
    v2 = jnp.max(sel2, axis=0, keepdims=True)
    i2 = jnp.min(jnp.where(sel2 == v2, iota, SUBLANES), axis=0, keepdims=True)
    e2 = jnp.exp(v2 - v1)
    den = 1.0 + e2
    p1 = p_grp / den
    p2 = p_grp * e2 / den
    base = grp * EXPERTS_PER_GROUP
    return (base + i1, base + i2), (p1, p2)


def _mixer_body(x_ref, sa_ref, sb_ref, gmix_ref, win_ref, caw_ref, cbw_ref, cbb_ref, lng_ref, lnb_ref,
                wout_ref, gffn_ref, wrh_ref, wrl_ref, br_ref,
                x1_ref, h2_ref, eid_ref, p_ref, na_ref, nb_ref,
                proj_ref, uext_ref, gext_ref, z_ref, *, nseq, tt, seq_chunk, row_chunk):
    rows = nseq * tt
    carried = sa_ref is None
    t = pl.program_id(1) if carried else None

    if carried:
        @pl.when(t == 0)
        def _():
            uext_ref[:, 0:PAD_A, :] = jnp.zeros((nseq, PAD_A, D_A), jnp.float32)
            gext_ref[:, 0:PAD_B, :] = jnp.zeros((nseq, PAD_B, D_B), jnp.float32)
    else:
        uext_ref[:, PAD_A - HALO_A:PAD_A, :] = sa_ref[...]
        gext_ref[:, PAD_B - HALO_B:PAD_B, :] = sb_ref[...]

    x = x_ref[...].reshape(rows, D_MODEL)
    h = (_rms_scale(x) * gmix_ref[...]).astype(jnp.bfloat16)
    proj_ref[...] = jnp.dot(h, win_ref[...], preferred_element_type=jnp.float32)

    c_a = proj_ref[:, D_A:2 * D_A]
    v_a = proj_ref[:, 2 * D_A:3 * D_A]
    uext_ref[:, PAD_A:PAD_A + tt, :] = (c_a * v_a).reshape(nseq, tt, D_A)
    v_b = proj_ref[:, 3 * D_A:3 * D_A + D_B]
    g_b = proj_ref[:, 3 * D_A + D_B:]
    gext_ref[:, PAD_B:PAD_B + tt, :] = (v_b * jax.nn.sigmoid(g_b)).reshape(nseq, tt, D_B)

    caw = caw_ref[...]
    cbw = cbw_ref[...]
    for s0 in range(0, nseq, seq_chunk):
        for r0 in range(0, tt, row_chunk):
            lo = s0 * tt + r0
            n = seq_chunk * row_chunk
            acc_a = None
            for k in range(CONV_A):
                off = PAD_A - HALO_A + k + r0
                term = uext_ref[s0:s0 + seq_chunk, off:off + row_chunk, :] * caw[k:k + 1, :]
                acc_a = term if acc_a is None else acc_a + term
            z_a = proj_ref[lo:lo + n, 0:D_A] * acc_a.reshape(n, D_A)
            acc_b = None
            for k in range(CONV_B):
                off = PAD_B - HALO_B + k + r0
                term = gext_ref[s0:s0 + seq_chunk, off:off + row_chunk, :] * cbw[k:k + 1, :]
                acc_b = term if acc_b is None else acc_b + term
            zb = acc_b.reshape(n, D_B) + cbb_ref[...]
            mu = jnp.mean(zb, axis=-1, keepdims=True)
            zc = zb - mu
            var = jnp.mean(zc * zc, axis=-1, keepdims=True)
            y = zc * lax.rsqrt(var + LN_EPS) * lng_ref[...] + lnb_ref[...]
            z_ref[lo:lo + n, 0:D_A] = z_a.astype(jnp.bfloat16)
            z_ref[lo:lo + n, D_A:] = (y * jax.nn.sigmoid(y)).astype(jnp.bfloat16)

    if carried:
        @pl.when(t == pl.num_programs(1) - 1)
        def _():
            na_ref[...] = uext_ref[:, PAD_A + tt - HALO_A:PAD_A + tt, :]
            nb_ref[...] = gext_ref[:, PAD_B + tt - HALO_B:PAD_B + tt, :]
        uext_ref[:, 0:PAD_A, :] = uext_ref[:, tt:tt + PAD_A, :]
        gext_ref[:, 0:PAD_B, :] = gext_ref[:, tt:tt + PAD_B, :]
    else:
        na_ref[...] = uext_ref[:, PAD_A + tt - HALO_A:PAD_A + tt, :]
        nb_ref[...] = gext_ref[:, PAD_B + tt - HALO_B:PAD_B + tt, :]

    x1 = x + jnp.dot(z_ref[...], wout_ref[...], preferred_element_type=jnp.float32)
    x1_ref[...] = x1
    h2 = _rms_scale(x1) * gffn_ref[...]
    h2_hi = h2.astype(jnp.bfloat16)
    h2_ref[...] = h2_hi
    h2_lo = (h2 - h2_hi.astype(jnp.float32)).astype(jnp.bfloat16)
    wrh = wrh_ref[...]
    logits_t = _dot_nt(wrh, h2_hi) + _dot_nt(wrh, h2_lo) + _dot_nt(wrl_ref[...], h2_hi) + br_ref[...]
    (e1, e2), (p1, p2) = _route(logits_t)
    iota = lax.broadcasted_iota(jnp.int32, (SUBLANES, rows), 0)
    eid_ref[...] = jnp.where(iota == 0, e1, jnp.where(iota == 1, e2, 0))
    p_ref[...] = jnp.where(iota == 0, p1, jnp.where(iota == 1, p2, 0.0))


def _mixer_prompt_kernel(x_ref, *refs, **kw):
    _mixer_body(x_ref, None, None, *refs, **kw)


def _mixer_sample_kernel(x_ref, sa_ref, sb_ref, *refs, **kw):
    n_params = 12
    params = refs[:n_params]
    rest = refs[n_params + 4:]
    _mixer_body(x_ref, sa_ref, sb_ref, *params, *rest, **kw)


def _full(shape):
    return pl.BlockSpec(shape, lambda *_: (0,) * len(shape))


def _mixer_param_specs():
    return [
        _full((1, D_MODEL)),
        _full((D_MODEL, IN_COLS)),
        _full((CONV_A, D_A)),
        _full((CONV_B, D_B)),
        _full((1, D_B)),
        _full((1, D_B)),
        _full((1, D_B)),
        _full((D_MODEL, D_MODEL)),
        _full((1, D_MODEL)),
        _full((ROUTER_ROWS, D_MODEL)),
        _full((ROUTER_ROWS, D_MODEL)),
        _full((ROUTER_ROWS, 1)),
    ]


def _mixer_scratch(nseq, tt):
    rows = nseq * tt
    return [
        pltpu.VMEM((rows, IN_COLS), jnp.float32),
        pltpu.VMEM((nseq, PAD_A + tt, D_A), jnp.float32),
        pltpu.VMEM((nseq, PAD_B + tt, D_B), jnp.float32),
        pltpu.VMEM((rows, D_MODEL), jnp.bfloat16),
    ]


def _mixer_prompt(x, params, n_tokens_total, tt):
    batch, seq, _ = x.shape
    n_t = seq // tt
    tok = lambda b, t: (b * n_t + t, 0)
    lane_tok = lambda b, t: (0, b * n_t + t)
    out_shape = [
        jax.ShapeDtypeStruct((n_tokens_total, D_MODEL), jnp.float32),
        jax.ShapeDtypeStruct((n_tokens_total, D_MODEL), jnp.bfloat16),
        jax.ShapeDtypeStruct((SUBLANES, n_tokens_total), jnp.int32),
        jax.ShapeDtypeStruct((SUBLANES, n_tokens_total), jnp.float32),
        jax.ShapeDtypeStruct((batch, HALO_A, D_A), jnp.float32),
        jax.ShapeDtypeStruct((batch, HALO_B, D_B), jnp.float32),
    ]
    out_specs = [
        pl.BlockSpec((tt, D_MODEL), tok),
        pl.BlockSpec((tt, D_MODEL), tok),
        pl.BlockSpec((SUBLANES, tt), lane_tok),
        pl.BlockSpec((SUBLANES, tt), lane_tok),
        pl.BlockSpec((1, HALO_A, D_A), lambda b, t: (b, 0, 0)),
        pl.BlockSpec((1, HALO_B, D_B), lambda b, t: (b, 0, 0)),
    ]
    return pl.pallas_call(
        functools.partial(_mixer_prompt_kernel, nseq=1, tt=tt, seq_chunk=1, row_chunk=64),
        grid=(batch, n_t),
        in_specs=[pl.BlockSpec((1, tt, D_MODEL), lambda b, t: (b, t, 0))] + _mixer_param_specs(),
        out_specs=out_specs,
        out_shape=out_shape,
        scratch_shapes=_mixer_scratch(1, tt),
        compiler_params=pltpu.CompilerParams(
            dimension_semantics=("arbitrary", "arbitrary"), vmem_limit_bytes=VMEM_LIMIT),
        name="mixer_prompt",
    )(x, *params)


def _mixer_sample(x, state_a, state_b, params, bufs, row_offset, nseq):
    batch, tt, _ = x.shape
    rows = nseq * tt
    first = row_offset // rows
    tok = lambda i: (first + i, 0)
    lane_tok = lambda i: (0, first + i)
    x1, h2, eid, p = bufs
    out_shape = [
        jax.ShapeDtypeStruct(x1.shape, x1.dtype),
        jax.ShapeDtypeStruct(h2.shape, h2.dtype),
        jax.ShapeDtypeStruct(eid.shape, eid.dtype),
        jax.ShapeDtypeStruct(p.shape, p.dtype),
        jax.ShapeDtypeStruct((batch, HALO_A, D_A), jnp.float32),
        jax.ShapeDtypeStruct((batch, HALO_B, D_B), jnp.float32),
    ]
    out_specs = [
        pl.BlockSpec((rows, D_MODEL), tok),
        pl.BlockSpec((rows, D_MODEL), tok),
        pl.BlockSpec((SUBLANES, rows), lane_tok),
        pl.BlockSpec((SUBLANES, rows), lane_tok),
        pl.BlockSpec((nseq, HALO_A, D_A), lambda i: (i, 0, 0)),
        pl.BlockSpec((nseq, HALO_B, D_B), lambda i: (i, 0, 0)),
    ]
    any_spec = pl.BlockSpec(memory_space=pl.ANY)
    in_specs = ([pl.BlockSpec((nseq, tt, D_MODEL), lambda i: (i, 0, 0)),
                 pl.BlockSpec((nseq, HALO_A, D_A), lambda i: (i, 0, 0)),
                 pl.BlockSpec((nseq, HALO_B, D_B), lambda i: (i, 0, 0))]
                + _mixer_param_specs() + [any_spec] * 4)
    n_in = len(in_specs)
    return pl.pallas_call(
        functools.partial(_mixer_sample_kernel, nseq=nseq, tt=tt, seq_chunk=8, row_chunk=tt),
        grid=(batch // nseq,),
        in_specs=in_specs,
        out_specs=out_specs,
        out_shape=out_shape,
        scratch_shapes=_mixer_scratch(nseq, tt),
        input_output_aliases={n_in - 4: 0, n_in - 3: 1, n_in - 2: 2, n_in - 1: 3},
        compiler_params=pltpu.CompilerParams(
            dimension_semantics=("arbitrary",), vmem_limit_bytes=VMEM_LIMIT),
        name="mixer_sample",
    )(x, state_a, state_b, *params, x1, h2, eid, p)


def _moe_dense_kernel(h2_ref, x1_ref, eid_ref, p_ref, wg_ref, wu_ref, wd_ref, gfin_ref, y_ref, acc_ref):
    e = pl.program_id(1)

    @pl.when(e == 0)
    def _():
        acc_ref[...] = jnp.zeros_like(acc_ref)

    eid = eid_ref[...]
    p = p_ref[...]
    c_row = jnp.sum(jnp.where(eid == e, p, 0.0), axis=0, keepdims=True)
    c = jnp.transpose(jnp.broadcast_to(c_row, (SUBLANES, c_row.shape[1])))[:, 0:1]
    h = h2_ref[...]
    gate = jnp.dot(h, wg_ref[0], preferred_element_type=jnp.float32)
    up = jnp.dot(h, wu_ref[0], preferred_element_type=jnp.float32)
    hid = (gate * jax.nn.sigmoid(gate) * up * c).astype(jnp.bfloat16)
    acc_ref[...] += jnp.dot(hid, wd_ref[0], preferred_element_type=jnp.float32)

    @pl.when(e == pl.num_programs(1) - 1)
    def _():
        x2 = x1_ref[...] + acc_ref[...]
        y_ref[...] = _rms_scale(x2) * gfin_ref[...]


def _moe_dense(h2, x1, eid, p, wg, wu, wd, gfin, row_offset, n_rows, tm):
    first = row_offset // tm
    tok = lambda i, e: (first + i, 0)
    lane_tok = lambda i, e: (0, first + i)
    return pl.pallas_call(
        _moe_dense_kernel,
        grid=(n_rows // tm, N_EXPERTS),
        in_specs=[
            pl.BlockSpec((tm, D_MODEL), tok),
            pl.BlockSpec((tm, D_MODEL), tok),
            pl.BlockSpec((SUBLANES, tm), lane_tok),
            pl.BlockSpec((SUBLANES, tm), lane_tok),
            pl.BlockSpec((1, D_MODEL, D_EXPERT), lambda i, e: (e, 0, 0)),
            pl.BlockSpec((1, D_MODEL, D_EXPERT), lambda i, e: (e, 0, 0)),
            pl.BlockSpec((1, D_EXPERT, D_MODEL), lambda i, e: (e, 0, 0)),
            pl.BlockSpec((1, D_MODEL), lambda i, e: (0, 0)),
        ],
        out_specs=pl.BlockSpec((tm, D_MODEL), lambda i, e: (i, 0)),
        out_shape=jax.ShapeDtypeStruct((n_rows, D_MODEL), jnp.float32),
        scratch_shapes=[pltpu.VMEM((tm, D_MODEL), jnp.float32)],
        compiler_params=pltpu.CompilerParams(
            dimension_semantics=("arbitrary", "arbitrary"), vmem_limit_bytes=VMEM_LIMIT),
        name="moe_dense",
    )(h2, x1, eid, p, wg, wu, wd, gfin)


def kernel(x_prompt, x_sample, state_conv_a, state_conv_b, g_mix, w_in, conv_a_w, conv_b_w, conv_b_bias,
           ln_g, ln_b, w_out, g_ffn, w_coarse, b_coarse, w_fine, b_fine, w_gate, w_up, w_down, g_final):
    assert g_mix.shape[0] == 1, "single trunk layer"
    batch, seq, _ = x_prompt.shape
    dec_batch, dec_seq, _ = x_sample.shape
    n_prompt = batch * seq
    n_sample = dec_batch * dec_seq
    n_tokens = n_prompt + n_sample
    bf16 = jnp.bfloat16

    wr = jnp.concatenate([
        w_coarse[0], jnp.zeros((D_MODEL, SUBLANES - N_EXPERT_GROUPS), jnp.float32),
        jnp.transpose(w_fine[0], (1, 0, 2)).reshape(D_MODEL, N_EXPERTS)], axis=1).T
    wr_hi = wr.astype(bf16)
    wr_lo = (wr - wr_hi.astype(jnp.float32)).astype(bf16)
    br = jnp.concatenate([
        b_coarse[0], jnp.full((SUBLANES - N_EXPERT_GROUPS,), NEG_BIG, jnp.float32),
        b_fine[0].reshape(N_EXPERTS)]).reshape(ROUTER_ROWS, 1)

    params = (g_mix, w_in[0].astype(bf16), conv_a_w[0], conv_b_w[0], conv_b_bias, ln_g, ln_b,
              w_out[0].astype(bf16), g_ffn, wr_hi, wr_lo, br)

    x1, h2, eid, p, na_p, nb_p = _mixer_prompt(x_prompt, params, n_tokens, tt=512)
    x1, h2, eid, p, na_s, nb_s = _mixer_sample(
        x_sample, state_conv_a[0], state_conv_b[0], params, (x1, h2, eid, p), n_prompt, nseq=32)

    wg, wu, wd = w_gate[0].astype(bf16), w_up[0].astype(bf16), w_down[0].astype(bf16)
    gfin = g_final.reshape(1, D_MODEL)
    y_p = _moe_dense(h2, x1, eid, p, wg, wu, wd, gfin, 0, n_prompt, tm=1024)
    y_s = _moe_dense(h2, x1, eid, p, wg, wu, wd, gfin, n_prompt, n_sample, tm=1024)
    return (y_p.reshape(batch, seq, D_MODEL), y_s.reshape(dec_batch, dec_seq, D_MODEL),
            na_p[None], nb_p[None], na_s[None], nb_s[None])
```

```python
import functools

import jax
import jax.numpy as jnp
from jax import lax
from jax.experimental import pallas as pl
from jax.experimental.pallas import tpu as pltpu
from jax.experimental.pallas import tpu_sc as plsc

D_MODEL = 1024
D_A = 512
D_B = 512
CONV_A = 3
CONV_B = 31
HALO_A = CONV_A - 1
HALO_B = CONV_B - 1
IN_COLS = 3 * D_A + 2 * D_B
N_EXPERT_GROUPS = 4
EXPERTS_PER_GROUP = 8
N_EXPERTS = N_EXPERT_GROUPS * EXPERTS_PER_GROUP
TOP_K = 2
D_EXPERT = D_MODEL // 4
RMS_EPS = 1e-6
LN_EPS = 1e-5

SUBLANES = 8
LANES = 128
PAD_A = SUBLANES
PAD_B = 32
ROUTER_ROWS = SUBLANES + N_EXPERTS
NEG_BIG = -1e30
VMEM_LIMIT = 56 * 1024 * 1024
HALF = D_MODEL // 2
HI_MASK = 0xFFFF0000

SC_CORES = 2
SC_SUBCORES = 16
SC_WORKERS = SC_CORES * SC_SUBCORES
SC_BATCH = 64

EXPERT_TILE = 256
PLAN_TILE = 512


def _rms_scale(x):
    return x * lax.rsqrt(jnp.mean(x * x, axis=-1, keepdims=True) + RMS_EPS)


def _dot_nt(a, b):
    return lax.dot_general(a, b, (((1,), (1,)), ((), ())), preferred_element_type=jnp.float32)


def _pack_bf16_halves(x):
    bits = lax.bitcast_convert_type(x.astype(jnp.bfloat16).astype(jnp.float32), jnp.uint32)
    return bits[:, :HALF] | (bits[:, HALF:] >> 16)


def _unpack_bf16_halves(w):
    hi = lax.bitcast_convert_type(w & jnp.uint32(HI_MASK), jnp.float32)
    lo = lax.bitcast_convert_type(w << 16, jnp.float32)
    return hi, lo


def _route(logits_t):
    rows = logits_t.shape[1]
    iota = lax.broadcasted_iota(jnp.int32, (SUBLANES, rows), 0)
    lc = logits_t[0:SUBLANES]
    cmax = jnp.max(lc, axis=0, keepdims=True)
    grp = jnp.min(jnp.where(lc == cmax, iota, SUBLANES), axis=0, keepdims=True)
    p_grp = 1.0 / jnp.sum(jnp.exp(lc - cmax), axis=0, keepdims=True)
    sel = logits_t[SUBLANES:2 * SUBLANES]
    for g in range(1, N_EXPERT_GROUPS):
        sel = jnp.where(grp == g, logits_t[(g + 1) * SUBLANES:(g + 2) * SUBLANES], sel)
    v1 = jnp.max(sel, axis=0, keepdims=True)
    i1 = jnp.min(jnp.where(sel == v1, iota, SUBLANES), axis=0, keepdims=True)
    sel2 = jnp.where(iota == i1, -jnp.inf, sel)
    v2 = jnp.max(sel2, axis=0, keepdims=True)
    i2 = jnp.min(jnp.where(sel2 == v2, iota, SUBLANES), axis=0, keepdims=True)
    e2 = jnp.exp(v2 - v1)
    den = 1.0 + e2
    p1 = p_grp / den
    p2 = p_grp * e2 / den
    base = grp * EXPERTS_PER_GROUP
    return (base + i1, base + i2), (p1, p2)


def _mixer_body(x_ref, sa_ref, sb_ref, gmix_ref, win_ref, caw_ref, cbw_ref, cbb_ref, lng_ref, lnb_ref,
                wout_ref, gffn_ref, wrh_ref, wrl_ref, br_ref,
                x1_ref, h2p_ref, eid_ref, p_ref, na_ref, nb_ref,
                proj_ref, uext_ref, gext_ref, z_ref, *, nseq, tt, seq_chunk, row_chunk):
    rows = nseq * tt
    carried = sa_ref is None
    t = pl.program_id(1) if carried else None

    if carried:
        @pl.when(t == 0)
        def _():
            uext_ref[:, 0:PAD_A, :] = jnp.zeros((nseq, PAD_A, D_A), jnp.float32)
            gext_ref[:, 0:PAD_B, :] = jnp.zeros((nseq, PAD_B, D_B), jnp.float32)
    else:
        uext_ref[:, PAD_A - HALO_A:PAD_A, :] = sa_ref[...]
        gext_ref[:, PAD_B - HALO_B:PAD_B, :] = sb_ref[...]

    x = x_ref[...].reshape(rows, D_MODEL)
    h = (_rms_scale(x) * gmix_ref[...]).astype(jnp.bfloat16)
    proj_ref[...] = jnp.dot(h, win_ref[...], preferred_element_type=jnp.float32)

    c_a = proj_ref[:, D_A:2 * D_A]
    v_a = proj_ref[:, 2 * D_A:3 * D_A]
    uext_ref[:, PAD_A:PAD_A + tt, :] = (c_a * v_a).reshape(nseq, tt, D_A)
    v_b = proj_ref[:, 3 * D_A:3 * D_A + D_B]
    g_b = proj_ref[:, 3 * D_A + D_B:]
    gext_ref[:, PAD_B:PAD_B + tt, :] = (v_b * jax.nn.sigmoid(g_b)).reshape(nseq, tt, D_B)

    caw = caw_ref[...]
    cbw = cbw_ref[...]
    for s0 in range(0, nseq, seq_chunk):
        for r0 in range(0, tt, row_chunk):
            lo = s0 * tt + r0
            n = seq_chunk * row_chunk
            acc_a = None
            for k in range(CONV_A):
                off = PAD_A - HALO_A + k + r0
                term = uext_ref[s0:s0 + seq_chunk, off:off + row_chunk, :] * caw[k:k + 1, :]
                acc_a = term if acc_a is None else acc_a + term
            z_a = proj_ref[lo:lo + n, 0:D_A] * acc_a.reshape(n, D_A)
            acc_b = None
            for k in range(CONV_B):
                off = PAD_B - HALO_B + k + r0
                term = gext_ref[s0:s0 + seq_chunk, off:off + row_chunk, :] * cbw[k:k + 1, :]
                acc_b = term if acc_b is None else acc_b + term
            zb = acc_b.reshape(n, D_B) + cbb_ref[...]
            mu = jnp.mean(zb, axis=-1, keepdims=True)
            zc = zb - mu
            var = jnp.mean(zc * zc, axis=-1, keepdims=True)
            y = zc * lax.rsqrt(var + LN_EPS) * lng_ref[...] + lnb_ref[...]
            z_ref[lo:lo + n, 0:D_A] = z_a.astype(jnp.bfloat16)
            z_ref[lo:lo + n, D_A:] = (y * jax.nn.sigmoid(y)).astype(jnp.bfloat16)

    if carried:
        @pl.when(t == pl.num_programs(1) - 1)
        def _():
            na_ref[...] = uext_ref[:, PAD_A + tt - HALO_A:PAD_A + tt, :]
            nb_ref[...] = gext_ref[:, PAD_B + tt - HALO_B:PAD_B + tt, :]
        uext_ref[:, 0:PAD_A, :] = uext_ref[:, tt:tt + PAD_A, :]
        gext_ref[:, 0:PAD_B, :] = gext_ref[:, tt:tt + PAD_B, :]
    else:
        na_ref[...] = uext_ref[:, PAD_A + tt - HALO_A:PAD_A + tt, :]
        nb_ref[...] = gext_ref[:, PAD_B + tt - HALO_B:PAD_B + tt, :]

    x1 = x + jnp.dot(z_ref[...], wout_ref[...], preferred_element_type=jnp.float32)
    x1_ref[...] = x1
    h2 = _rms_scale(x1) * gffn_ref[...]
    h2_hi = h2.astype(jnp.bfloat16)
    h2p_ref[...] = _pack_bf16_halves(h2)
    h2_lo = (h2 - h2_hi.astype(jnp.float32)).astype(jnp.bfloat16)
    wrh = wrh_ref[...]
    logits_t = _dot_nt(wrh, h2_hi) + _dot_nt(wrh, h2_lo) + _dot_nt(wrl_ref[...], h2_hi) + br_ref[...]
    (e1, e2), (p1, p2) = _route(logits_t)
    iota = lax.broadcasted_iota(jnp.int32, (SUBLANES, rows), 0)
    eid_ref[...] = jnp.where(iota == 0, e1, jnp.where(iota == 1, e2, 0))
    p_ref[...] = jnp.where(iota == 0, p1, jnp.where(iota == 1, p2, 0.0))


def _mixer_prompt_kernel(x_ref, *refs, **kw):
    _mixer_body(x_ref, None, None, *refs, **kw)


def _mixer_sample_kernel(x_ref, sa_ref, sb_ref, *refs, **kw):
    n_params = 12
    params = refs[:n_params]
    rest = refs[n_params + 4:]
    _mixer_body(x_ref, sa_ref, sb_ref, *params, *rest, **kw)


def _full(shape):
    return pl.BlockSpec(shape, lambda *_: (0,) * len(shape))


def _mixer_param_specs():
    return [
        _full((1, D_MODEL)),
        _full((D_MODEL, IN_COLS)),
        _full((CONV_A, D_A)),
        _full((CONV_B, D_B)),
        _full((1, D_B)),
        _full((1, D_B)),
        _full((1, D_B)),
        _full((D_MODEL, D_MODEL)),
        _full((1, D_MODEL)),
        _full((ROUTER_ROWS, D_MODEL)),
        _full((ROUTER_ROWS, D_MODEL)),
        _full((ROUTER_ROWS, 1)),
    ]


def _mixer_scratch(nseq, tt):
    rows = nseq * tt
    return [
        pltpu.VMEM((rows, IN_COLS), jnp.float32),
        pltpu.VMEM((nseq, PAD_A + tt, D_A), jnp.float32),
        pltpu.VMEM((nseq, PAD_B + tt, D_B), jnp.float32),
        pltpu.VMEM((rows, D_MODEL), jnp.bfloat16),
    ]


def _mixer_prompt(x, params, n_tokens_total, tt):
    batch, seq, _ = x.shape
    n_t = seq // tt
    tok = lambda b, t: (b * n_t + t, 0)
    lane_tok = lambda b, t: (0, b * n_t + t)
    out_shape = [
        jax.ShapeDtypeStruct((n_tokens_total, D_MODEL), jnp.float32),
        jax.ShapeDtypeStruct((n_tokens_total, HALF), jnp.uint32),
        jax.ShapeDtypeStruct((SUBLANES, n_tokens_total), jnp.int32),
        jax.ShapeDtypeStruct((SUBLANES, n_tokens_total), jnp.float32),
        jax.ShapeDtypeStruct((batch, HALO_A, D_A), jnp.float32),
        jax.ShapeDtypeStruct((batch, HALO_B, D_B), jnp.float32),
    ]
    out_specs = [
        pl.BlockSpec((tt, D_MODEL), tok),
        pl.BlockSpec((tt, HALF), tok),
        pl.BlockSpec((SUBLANES, tt), lane_tok),
        pl.BlockSpec((SUBLANES, tt), lane_tok),
        pl.BlockSpec((1, HALO_A, D_A), lambda b, t: (b, 0, 0)),
        pl.BlockSpec((1, HALO_B, D_B), lambda b, t: (b, 0, 0)),
    ]
    return pl.pallas_call(
        functools.partial(_mixer_prompt_kernel, nseq=1, tt=tt, seq_chunk=1, row_chunk=64),
        grid=(batch, n_t),
        in_specs=[pl.BlockSpec((1, tt, D_MODEL), lambda b, t: (b, t, 0))] + _mixer_param_specs(),
        out_specs=out_specs,
        out_shape=out_shape,
        scratch_shapes=_mixer_scratch(1, tt),
        compiler_params=pltpu.CompilerParams(
            dimension_semantics=("arbitrary", "arbitrary"), vmem_limit_bytes=VMEM_LIMIT),
        name="mixer_prompt",
    )(x, *params)


def _mixer_sample(x, state_a, state_b, params, bufs, row_offset, nseq):
    batch, tt, _ = x.shape
    rows = nseq * tt
    first = row_offset // rows
    tok = lambda i: (first + i, 0)
    lane_tok = lambda i: (0, first + i)
    x1, h2p, eid, p = bufs
    out_shape = [
        jax.ShapeDtypeStruct(x1.shape, x1.dtype),
        jax.ShapeDtypeStruct(h2p.shape, h2p.dtype),
        jax.ShapeDtypeStruct(eid.shape, eid.dtype),
        jax.ShapeDtypeStruct(p.shape, p.dtype),
        jax.ShapeDtypeStruct((batch, HALO_A, D_A), jnp.float32),
        jax.ShapeDtypeStruct((batch, HALO_B, D_B), jnp.float32),
    ]
    out_specs = [
        pl.BlockSpec((rows, D_MODEL), tok),
        pl.BlockSpec((rows, HALF), tok),
        pl.BlockSpec((SUBLANES, rows), lane_tok),
        pl.BlockSpec((SUBLANES, rows), lane_tok),
        pl.BlockSpec((nseq, HALO_A, D_A), lambda i: (i, 0, 0)),
        pl.BlockSpec((nseq, HALO_B, D_B), lambda i: (i, 0, 0)),
    ]
    any_spec = pl.BlockSpec(memory_space=pl.ANY)
    in_specs = ([pl.BlockSpec((nseq, tt, D_MODEL), lambda i: (i, 0, 0)),
                 pl.BlockSpec((nseq, HALO_A, D_A), lambda i: (i, 0, 0)),
                 pl.BlockSpec((nseq, HALO_B, D_B), lambda i: (i, 0, 0))]
                + _mixer_param_specs() + [any_spec] * 4)
    n_in = len(in_specs)
    return pl.pallas_call(
        functools.partial(_mixer_sample_kernel, nseq=nseq, tt=tt, seq_chunk=8, row_chunk=tt),
        grid=(batch // nseq,),
        in_specs=in_specs,
        out_specs=out_specs,
        out_shape=out_shape,
        scratch_shapes=_mixer_scratch(nseq, tt),
        input_output_aliases={n_in - 4: 0, n_in - 3: 1, n_in - 2: 2, n_in - 1: 3},
        compiler_params=pltpu.CompilerParams(
            dimension_semantics=("arbitrary",), vmem_limit_bytes=VMEM_LIMIT),
        name="mixer_sample",
    )(x, state_a, state_b, *params, x1, h2p, eid, p)


def _plan_kernel(eid_ref, rank_ref, cnt_ref, carry_ref):
    i = pl.program_id(0)
    tt = eid_ref.shape[1]

    @pl.when(i == 0)
    def _():
        carry_ref[...] = jnp.zeros_like(carry_ref)

    eid = eid_ref[...]
    experts = lax.broadcasted_iota(jnp.int32, (N_EXPERTS, tt), 0)
    oh0 = experts == eid[0:1]
    oh1 = experts == eid[1:2]
    oh = jnp.where(oh0 | oh1, 1.0, 0.0)
    earlier = (lax.broadcasted_iota(jnp.int32, (tt, tt), 0)
               < lax.broadcasted_iota(jnp.int32, (tt, tt), 1))
    within = jnp.dot(oh.astype(jnp.bfloat16), jnp.where(earlier, 1.0, 0.0).astype(jnp.bfloat16),
                     preferred_element_type=jnp.float32)
    before = within + carry_ref[:, 0:1]
    r0 = jnp.sum(jnp.where(oh0, before, 0.0), axis=0, keepdims=True)
    r1 = jnp.sum(jnp.where(oh1, before, 0.0), axis=0, keepdims=True)
    slot = lax.broadcasted_iota(jnp.int32, (SUBLANES, tt), 0)
    rank_ref[...] = jnp.where(slot == 0, r0, jnp.where(slot == 1, r1, 0.0)).astype(jnp.int32)
    carry_ref[...] = carry_ref[...] + jnp.sum(oh, axis=1, keepdims=True)

    @pl.when(i == pl.num_programs(0) - 1)
    def _():
        cnt_ref[...] = carry_ref[...].astype(jnp.int32)


def _plan(eid):
    n_tokens = eid.shape[1]
    return pl.pallas_call(
        _plan_kernel,
        grid=(n_tokens // PLAN_TILE,),
        in_specs=[pl.BlockSpec((SUBLANES, PLAN_TILE), lambda i: (0, i))],
        out_specs=[pl.BlockSpec((SUBLANES, PLAN_TILE), lambda i: (0, i)),
                   pl.BlockSpec((N_EXPERTS, LANES), lambda i: (0, 0))],
        out_shape=[jax.ShapeDtypeStruct((SUBLANES, n_tokens), jnp.int32),
                   jax.ShapeDtypeStruct((N_EXPERTS, LANES), jnp.int32)],
        scratch_shapes=[pltpu.VMEM((N_EXPERTS, LANES), jnp.float32)],
        compiler_params=pltpu.CompilerParams(dimension_semantics=("arbitrary",)),
        name="route_plan",
    )(eid)


def _sc_gather_rows(table, idx):
    n_workers, n_batches, batch = idx.shape
    words = table.shape[1]
    mesh = plsc.VectorSubcoreMesh(core_axis_name="c", subcore_axis_name="s")

    @functools.partial(
        pl.kernel, mesh=mesh,
        out_type=jax.ShapeDtypeStruct((n_workers * n_batches * batch, words), table.dtype),
        scratch_types=[pltpu.VMEM((n_batches, batch), jnp.int32),
                       pltpu.VMEM((batch, words), table.dtype),
                       pltpu.SemaphoreType.DMA],
    )
    def gather(table_hbm, idx_hbm, out_hbm, idx_v, rows_v, sem):
        worker = lax.axis_index("s") * SC_CORES + lax.axis_index("c")
        pltpu.sync_copy(idx_hbm.at[worker], idx_v)

        @pl.loop(0, n_batches)
        def _(b):
            pltpu.async_copy(table_hbm.at[idx_v.at[b]], rows_v, sem).wait()
            pltpu.sync_copy(rows_v, out_hbm.at[pl.ds((worker * n_batches + b) * batch, batch)])

    return gather(table, idx)


def _experts_kernel(te_ref, nv_ref, xs_ref, wg_ref, wu_ref, wd_ref, ys_ref):
    i = pl.program_id(0)

    @pl.when(i < nv_ref[0])
    def _():
        hi, lo = _unpack_bf16_halves(xs_ref[...])
        hi = hi.astype(jnp.bfloat16)
        lo = lo.astype(jnp.bfloat16)
        wg = wg_ref[0]
        wu = wu_ref[0]
        gate = (jnp.dot(hi, wg[:HALF], preferred_element_type=jnp.float32)
                + jnp.dot(lo, wg[HALF:], preferred_element_type=jnp.float32))
        up = (jnp.dot(hi, wu[:HALF], preferred_element_type=jnp.float32)
              + jnp.dot(lo, wu[HALF:], preferred_element_type=jnp.float32))
        hid = (gate * jax.nn.sigmoid(gate) * up).astype(jnp.bfloat16)
        ys_ref[...] = _pack_bf16_halves(jnp.dot(hid, wd_ref[0], preferred_element_type=jnp.float32))


def _experts(xs, tile_expert, n_valid, wg, wu, wd):
    n_slots = xs.shape[0]
    tm = EXPERT_TILE
    row_block = lambda i, te, nv: (jnp.minimum(i, nv[0] - 1), 0)
    w_block = lambda i, te, nv: (te[i], 0, 0)
    return pl.pallas_call(
        _experts_kernel,
        grid_spec=pltpu.PrefetchScalarGridSpec(
            num_scalar_prefetch=2,
            grid=(n_slots // tm,),
            in_specs=[pl.BlockSpec((tm, HALF), row_block),
                      pl.BlockSpec((1, D_MODEL, D_EXPERT), w_block),
                      pl.BlockSpec((1, D_MODEL, D_EXPERT), w_block),
                      pl.BlockSpec((1, D_EXPERT, D_MODEL), w_block)],
            out_specs=pl.BlockSpec((tm, HALF), row_block),
        ),
        out_shape=jax.ShapeDtypeStruct((n_slots, HALF), jnp.uint32),
        compiler_params=pltpu.CompilerParams(
            dimension_semantics=("arbitrary",), vmem_limit_bytes=VMEM_LIMIT),
        name="experts",
    )(tile_expert, n_valid, xs, wg, wu, wd)


def _final_kernel(x1_ref, y0_ref, y1_ref, p_ref, gfin_ref, out_ref):
    pt = jnp.transpose(p_ref[...])
    p0 = pt[:, 0:1]
    p1 = pt[:, 1:2]
    a_hi, a_lo = _unpack_bf16_halves(y0_ref[...])
    b_hi, b_lo = _unpack_bf16_halves(y1_ref[...])
    x1 = x1_ref[...]
    x2_hi = x1[:, :HALF] + (p0 * a_hi + p1 * b_hi)
    x2_lo = x1[:, HALF:] + (p0 * a_lo + p1 * b_lo)
    ms = (jnp.sum(x2_hi * x2_hi, axis=-1, keepdims=True)
          + jnp.sum(x2_lo * x2_lo, axis=-1, keepdims=True)) / D_MODEL
    scale = lax.rsqrt(ms + RMS_EPS)
    g = gfin_ref[...]
    out_ref[:, :HALF] = x2_hi * scale * g[:, :HALF]
    out_ref[:, HALF:] = x2_lo * scale * g[:, HALF:]


def _final(x1, yk, p, gfin, row_offset, n_rows, n_tokens, tm):
    first = row_offset // tm
    second = (n_tokens + row_offset) // tm
    return pl.pallas_call(
        _final_kernel,
        grid=(n_rows // tm,),
        in_specs=[pl.BlockSpec((tm, D_MODEL), lambda i: (first + i, 0)),
                  pl.BlockSpec((tm, HALF), lambda i: (first + i, 0)),
                  pl.BlockSpec((tm, HALF), lambda i: (second + i, 0)),
                  pl.BlockSpec((SUBLANES, tm), lambda i: (0, first + i)),
                  pl.BlockSpec((1, D_MODEL), lambda i: (0, 0))],
        out_specs=pl.BlockSpec((tm, D_MODEL), lambda i: (i, 0)),
        out_shape=jax.ShapeDtypeStruct((n_rows, D_MODEL), jnp.float32),
        compiler_params=pltpu.CompilerParams(
            dimension_semantics=("arbitrary",), vmem_limit_bytes=VMEM_LIMIT),
        name="final",
    )(x1, yk, yk, p, gfin)


def _slot_capacity(n_tokens):
    worst = TOP_K * n_tokens + N_EXPERTS * (EXPERT_TILE - 1)
    unit = SC_WORKERS * SC_BATCH
    return -(-worst // unit) * unit


def kernel(x_prompt, x_sample, state_conv_a, state_conv_b, g_mix, w_in, conv_a_w, conv_b_w, conv_b_bias,
           ln_g, ln_b, w_out, g_ffn, w_coarse, b_coarse, w_fine, b_fine, w_gate, w_up, w_down, g_final):
    assert g_mix.shape[0] == 1, "single trunk layer"
    batch, seq, _ = x_prompt.shape
    dec_batch, dec_seq, _ = x_sample.shape
    n_prompt = batch * seq
    n_sample = dec_batch * dec_seq
    n_tokens = n_prompt + n_sample
    assert (TOP_K * n_tokens) % (SC_WORKERS * SC_BATCH) == 0
    bf16 = jnp.bfloat16

    wr = jnp.concatenate([
        w_coarse[0], jnp.zeros((D_MODEL, SUBLANES - N_EXPERT_GROUPS), jnp.float32),
        jnp.transpose(w_fine[0], (1, 0, 2)).reshape(D_MODEL, N_EXPERTS)], axis=1).T
    wr_hi = wr.astype(bf16)
    wr_lo = (wr - wr_hi.astype(jnp.float32)).astype(bf16)
    br = jnp.concatenate([
        b_coarse[0], jnp.full((SUBLANES - N_EXPERT_GROUPS,), NEG_BIG, jnp.float32),
        b_fine[0].reshape(N_EXPERTS)]).reshape(ROUTER_ROWS, 1)

    params = (g_mix, w_in[0].astype(bf16), conv_a_w[0], conv_b_w[0], conv_b_bias, ln_g, ln_b,
              w_out[0].astype(bf16), g_ffn, wr_hi, wr_lo, br)

    x1, h2p, eid, p, na_p, nb_p = _mixer_prompt(x_prompt, params, n_tokens, tt=512)
    x1, h2p, eid, p, na_s, nb_s = _mixer_sample(
        x_sample, state_conv_a[0], state_conv_b[0], params, (x1, h2p, eid, p), n_prompt, nseq=32)

    rank, counts = _plan(eid)
    counts = counts[:, 0]
    padded = -(-counts // EXPERT_TILE) * EXPERT_TILE
    seg_end = jnp.cumsum(padded)
    seg_start = seg_end - padded
    pos = seg_start[eid[:TOP_K]] + rank[:TOP_K]
    n_slots = _slot_capacity(n_tokens)
    n_tiles = n_slots // EXPERT_TILE
    n_valid = (seg_end[-1] // EXPERT_TILE).astype(jnp.int32)
    tile_expert = jnp.searchsorted(seg_end, jnp.arange(n_tiles, dtype=jnp.int32) * EXPERT_TILE, side="right")
    tile_expert = jnp.minimum(tile_expert, tile_expert[n_valid - 1]).astype(jnp.int32)
    token_ids = jnp.tile(jnp.arange(n_tokens, dtype=jnp.int32), TOP_K)
    src = jnp.zeros((n_slots,), jnp.int32).at[pos.reshape(-1)].set(token_ids)

    xs = _sc_gather_rows(h2p, src.reshape(SC_WORKERS, -1, SC_BATCH))
    ys = _experts(xs, tile_expert, n_valid.reshape(1), w_gate[0].astype(bf16), w_up[0].astype(bf16),
                  w_down[0].astype(bf16))
    yk = _sc_gather_rows(ys, pos.reshape(SC_WORKERS, -1, SC_BATCH))

    gfin = g_final.reshape(1, D_MODEL)
    y_p = _final(x1, yk, p, gfin, 0, n_prompt, n_tokens, tm=512)
    y_s = _final(x1, yk, p, gfin, n_prompt, n_sample, n_tokens, tm=512)
    return (y_p.reshape(batch, seq, D_MODEL), y_s.reshape(dec_batch, dec_seq, D_MODEL),
            na_p[None], nb_p[None], na_s[None], nb_s[None])
```

```python
import functools

import jax
import jax.numpy as jnp
from jax import lax
from jax.experimental import pallas as pl
from jax.experimental.pallas import tpu as pltpu
from jax.experimental.pallas import tpu_sc as plsc

D_MODEL = 1024
D_A = 512
D_B = 512
CONV_A = 3
CONV_B = 31
HALO_A = CONV_A - 1
HALO_B = CONV_B - 1
IN_COLS = 3 * D_A + 2 * D_B
N_EXPERT_GROUPS = 4
EXPERTS_PER_GROUP = 8
N_EXPERTS = N_EXPERT_GROUPS * EXPERTS_PER_GROUP
TOP_K = 2
D_EXPERT = D_MODEL // 4
RMS_EPS = 1e-6
LN_EPS = 1e-5

SUBLANES = 8
LANES = 128
PAD_A = SUBLANES
PAD_B = 32
ROUTER_ROWS = SUBLANES + N_EXPERTS
NEG_BIG = -1e30
VMEM_LIMIT = 56 * 1024 * 1024
HALF = D_MODEL // 2
HI_MASK = 0xFFFF0000

SC_CORES = 2
SC_SUBCORES = 16
SC_WORKERS = SC_CORES * SC_SUBCORES
SC_BATCH = 64

EXPERT_TILE = 1024
PLAN_TILE = 512


def _rms_scale(x):
    return x * lax.rsqrt(jnp.mean(x * x, axis=-1, keepdims=True) + RMS_EPS)


def _dot_nt(a, b):
    return lax.dot_general(a, b, (((1,), (1,)), ((), ())), preferred_element_type=jnp.float32)


def _pack_bf16_halves(x):
    bits = lax.bitcast_convert_type(x.astype(jnp.bfloat16).astype(jnp.float32), jnp.uint32)
    return bits[:, :HALF] | (bits[:, HALF:] >> 16)


def _unpack_bf16_halves(w):
    hi = lax.bitcast_convert_type(w & jnp.uint32(HI_MASK), jnp.float32)
    lo = lax.bitcast_convert_type(w << 16, jnp.float32)
    return hi, lo


def _route(logits_t):
    rows = logits_t.shape[1]
    iota = lax.broadcasted_iota(jnp.int32, (SUBLANES, rows), 0)
    lc = logits_t[0:SUBLANES]
    cmax = jnp.max(lc, axis=0, keepdims=True)
    grp = jnp.min(jnp.where(lc == cmax, iota, SUBLANES), axis=0, keepdims=True)
    p_grp = 1.0 / jnp.sum(jnp.exp(lc - cmax), axis=0, keepdims=True)
    sel = logits_t[SUBLANES:2 * SUBLANES]
    for g in range(1, N_EXPERT_GROUPS):
        sel = jnp.where(grp == g, logits_t[(g + 1) * SUBLANES:(g + 2) * SUBLANES], sel)
    v1 = jnp.max(sel, axis=0, keepdims=True)
    i1 = jnp.min(jnp.where(sel == v1, iota, SUBLANES), axis=0, keepdims=True)
    sel2 = jnp.where(iota == i1, -jnp.inf, sel)
    v2 = jnp.max(sel2, axis=0, keepdims=True)
    i2 = jnp.min(jnp.where(sel2 == v2, iota, SUBLANES), axis=0, keepdims=True)
    e2 = jnp.exp(v2 - v1)
    den = 1.0 + e2
    p1 = p_grp / den
    p2 = p_grp * e2 / den
    base = grp * EXPERTS_PER_GROUP
    return (base + i1, base + i2), (p1, p2)


def _mixer_body(x_ref, sa_ref, sb_ref, gmix_ref, win_ref, caw_ref, cbw_ref, cbb_ref, lng_ref, lnb_ref,
                wout_ref, gffn_ref, wrh_ref, wrl_ref, br_ref,
                x1_ref, h2p_ref, eid_ref, p_ref, na_ref, nb_ref,
                proj_ref, uext_ref, gext_ref, z_ref, *, nseq, tt, seq_chunk, row_chunk):
    rows = nseq * tt
    carried = sa_ref is None
    t = pl.program_id(1) if carried else None

    if carried:
        @pl.when(t == 0)
        def _():
            uext_ref[:, 0:PAD_A, :] = jnp.zeros((nseq, PAD_A, D_A), jnp.float32)
            gext_ref[:, 0:PAD_B, :] = jnp.zeros((nseq, PAD_B, D_B), jnp.float32)
    else:
        uext_ref[:, PAD_A - HALO_A:PAD_A, :] = sa_ref[...]
        gext_ref[:, PAD_B - HALO_B:PAD_B, :] = sb_ref[...]

    x = x_ref[...].reshape(rows, D_MODEL)
    h = (_rms_scale(x) * gmix_ref[...]).astype(jnp.bfloat16)
    proj_ref[...] = jnp.dot(h, win_ref[...], preferred_element_type=jnp.float32)

    c_a = proj_ref[:, D_A:2 * D_A]
    v_a = proj_ref[:, 2 * D_A:3 * D_A]
    uext_ref[:, PAD_A:PAD_A + tt, :] = (c_a * v_a).reshape(nseq, tt, D_A)
    v_b = proj_ref[:, 3 * D_A:3 * D_A + D_B]
    g_b = proj_ref[:, 3 * D_A + D_B:]
    gext_ref[:, PAD_B:PAD_B + tt, :] = (v_b * jax.nn.sigmoid(g_b)).reshape(nseq, tt, D_B)

    caw = caw_ref[...]
    cbw = cbw_ref[...]
    for s0 in range(0, nseq, seq_chunk):
        for r0 in range(0, tt, row_chunk):
            lo = s0 * tt + r0
            n = seq_chunk * row_chunk
            acc_a = None
            for k in range(CONV_A):
                off = PAD_A - HALO_A + k + r0
                term = uext_ref[s0:s0 + seq_chunk, off:off + row_chunk, :] * caw[k:k + 1, :]
                acc_a = term if acc_a is None else acc_a + term
            z_a = proj_ref[lo:lo + n, 0:D_A] * acc_a.reshape(n, D_A)
            acc_b = None
            for k in range(CONV_B):
                off = PAD_B - HALO_B + k + r0
                term = gext_ref[s0:s0 + seq_chunk, off:off + row_chunk, :] * cbw[k:k + 1, :]
                acc_b = term if acc_b is None else acc_b + term
            zb = acc_b.reshape(n, D_B) + cbb_ref[...]
            mu = jnp.mean(zb, axis=-1, keepdims=True)
            zc = zb - mu
            var = jnp.mean(zc * zc, axis=-1, keepdims=True)
            y = zc * lax.rsqrt(var + LN_EPS) * lng_ref[...] + lnb_ref[...]
            z_ref[lo:lo + n, 0:D_A] = z_a.astype(jnp.bfloat16)
            z_ref[lo:lo + n, D_A:] = (y * jax.nn.sigmoid(y)).astype(jnp.bfloat16)

    if carried:
        @pl.when(t == pl.num_programs(1) - 1)
        def _():
            na_ref[...] = uext_ref[:, PAD_A + tt - HALO_A:PAD_A + tt, :]
            nb_ref[...] = gext_ref[:, PAD_B + tt - HALO_B:PAD_B + tt, :]
        uext_ref[:, 0:PAD_A, :] = uext_ref[:, tt:tt + PAD_A, :]
        gext_ref[:, 0:PAD_B, :] = gext_ref[:, tt:tt + PAD_B, :]
    else:
        na_ref[...] = uext_ref[:, PAD_A + tt - HALO_A:PAD_A + tt, :]
        nb_ref[...] = gext_ref[:, PAD_B + tt - HALO_B:PAD_B + tt, :]

    x1 = x + jnp.dot(z_ref[...], wout_ref[...], preferred_element_type=jnp.float32)
    x1_ref[...] = x1
    h2 = _rms_scale(x1) * gffn_ref[...]
    h2_hi = h2.astype(jnp.bfloat16)
    h2p_ref[...] = _pack_bf16_halves(h2)
    h2_lo = (h2 - h2_hi.astype(jnp.float32)).astype(jnp.bfloat16)
    wrh = wrh_ref[...]
    logits_t = _dot_nt(wrh, h2_hi) + _dot_nt(wrh, h2_lo) + _dot_nt(wrl_ref[...], h2_hi) + br_ref[...]
    (e1, e2), (p1, p2) = _route(logits_t)
    iota = lax.broadcasted_iota(jnp.int32, (SUBLANES, rows), 0)
    eid_ref[...] = jnp.where(iota == 0, e1, jnp.where(iota == 1, e2, 0))
    p_ref[...] = jnp.where(iota == 0, p1, jnp.where(iota == 1, p2, 0.0))


def _mixer_prompt_kernel(x_ref, *refs, **kw):
    _mixer_body(x_ref, None, None, *refs, **kw)


def _mixer_sample_kernel(x_ref, sa_ref, sb_ref, *refs, **kw):
    n_params = 12
    params = refs[:n_params]
    rest = refs[n_params + 4:]
    _mixer_body(x_ref, sa_ref, sb_ref, *params, *rest, **kw)


def _full(shape):
    return pl.BlockSpec(shape, lambda *_: (0,) * len(shape))


def _mixer_param_specs():
    return [
        _full((1, D_MODEL)),
        _full((D_MODEL, IN_COLS)),
        _full((CONV_A, D_A)),
        _full((CONV_B, D_B)),
        _full((1, D_B)),
        _full((1, D_B)),
        _full((1, D_B)),
        _full((D_MODEL, D_MODEL)),
        _full((1, D_MODEL)),
        _full((ROUTER_ROWS, D_MODEL)),
        _full((ROUTER_ROWS, D_MODEL)),
        _full((ROUTER_ROWS, 1)),
    ]


def _mixer_scratch(nseq, tt):
    rows = nseq * tt
    return [
        pltpu.VMEM((rows, IN_COLS), jnp.float32),
        pltpu.VMEM((nseq, PAD_A + tt, D_A), jnp.float32),
        pltpu.VMEM((nseq, PAD_B + tt, D_B), jnp.float32),
        pltpu.VMEM((rows, D_MODEL), jnp.bfloat16),
    ]


def _mixer_prompt(x, params, n_tokens_total, tt):
    batch, seq, _ = x.shape
    n_t = seq // tt
    tok = lambda b, t: (b * n_t + t, 0)
    lane_tok = lambda b, t: (0, b * n_t + t)
    out_shape = [
        jax.ShapeDtypeStruct((n_tokens_total, D_MODEL), jnp.float32),
        jax.ShapeDtypeStruct((n_tokens_total, HALF), jnp.uint32),
        jax.ShapeDtypeStruct((SUBLANES, n_tokens_total), jnp.int32),
        jax.ShapeDtypeStruct((SUBLANES, n_tokens_total), jnp.float32),
        jax.ShapeDtypeStruct((batch, HALO_A, D_A), jnp.float32),
        jax.ShapeDtypeStruct((batch, HALO_B, D_B), jnp.float32),
    ]
    out_specs = [
        pl.BlockSpec((tt, D_MODEL), tok),
        pl.BlockSpec((tt, HALF), tok),
        pl.BlockSpec((SUBLANES, tt), lane_tok),
        pl.BlockSpec((SUBLANES, tt), lane_tok),
        pl.BlockSpec((1, HALO_A, D_A), lambda b, t: (b, 0, 0)),
        pl.BlockSpec((1, HALO_B, D_B), lambda b, t: (b, 0, 0)),
    ]
    return pl.pallas_call(
        functools.partial(_mixer_prompt_kernel, nseq=1, tt=tt, seq_chunk=1, row_chunk=64),
        grid=(batch, n_t),
        in_specs=[pl.BlockSpec((1, tt, D_MODEL), lambda b, t: (b, t, 0))] + _mixer_param_specs(),
        out_specs=out_specs,
        out_shape=out_shape,
        scratch_shapes=_mixer_scratch(1, tt),
        compiler_params=pltpu.CompilerParams(
            dimension_semantics=("arbitrary", "arbitrary"), vmem_limit_bytes=VMEM_LIMIT),
        name="mixer_prompt",
    )(x, *params)


def _mixer_sample(x, state_a, state_b, params, bufs, row_offset, nseq):
    batch, tt, _ = x.shape
    rows = nseq * tt
    first = row_offset // rows
    tok = lambda i: (first + i, 0)
    lane_tok = lambda i: (0, first + i)
    x1, h2p, eid, p = bufs
    out_shape = [
        jax.ShapeDtypeStruct(x1.shape, x1.dtype),
        jax.ShapeDtypeStruct(h2p.shape, h2p.dtype),
        jax.ShapeDtypeStruct(eid.shape, eid.dtype),
        jax.ShapeDtypeStruct(p.shape, p.dtype),
        jax.ShapeDtypeStruct((batch, HALO_A, D_A), jnp.float32),
        jax.ShapeDtypeStruct((batch, HALO_B, D_B), jnp.float32),
    ]
    out_specs = [
        pl.BlockSpec((rows, D_MODEL), tok),
        pl.BlockSpec((rows, HALF), tok),
        pl.BlockSpec((SUBLANES, rows), lane_tok),
        pl.BlockSpec((SUBLANES, rows), lane_tok),
        pl.BlockSpec((nseq, HALO_A, D_A), lambda i: (i, 0, 0)),
        pl.BlockSpec((nseq, HALO_B, D_B), lambda i: (i, 0, 0)),
    ]
    any_spec = pl.BlockSpec(memory_space=pl.ANY)
    in_specs = ([pl.BlockSpec((nseq, tt, D_MODEL), lambda i: (i, 0, 0)),
                 pl.BlockSpec((nseq, HALO_A, D_A), lambda i: (i, 0, 0)),
                 pl.BlockSpec((nseq, HALO_B, D_B), lambda i: (i, 0, 0))]
                + _mixer_param_specs() + [any_spec] * 4)
    n_in = len(in_specs)
    return pl.pallas_call(
        functools.partial(_mixer_sample_kernel, nseq=nseq, tt=tt, seq_chunk=8, row_chunk=tt),
        grid=(batch // nseq,),
        in_specs=in_specs,
        out_specs=out_specs,
        out_shape=out_shape,
        scratch_shapes=_mixer_scratch(nseq, tt),
        input_output_aliases={n_in - 4: 0, n_in - 3: 1, n_in - 2: 2, n_in - 1: 3},
        compiler_params=pltpu.CompilerParams(
            dimension_semantics=("arbitrary",), vmem_limit_bytes=VMEM_LIMIT),
        name="mixer_sample",
    )(x, state_a, state_b, *params, x1, h2p, eid, p)


def _plan_kernel(eid_ref, pos_ref, seg_end_ref, carry_ref, start_ref):
    ph = pl.program_id(0)
    i = pl.program_id(1)
    tt = eid_ref.shape[1]

    @pl.when(i == 0)
    def _():
        @pl.when(ph == 1)
        def _():
            tiles = jnp.ceil(carry_ref[...] / EXPERT_TILE)
            below = (lax.broadcasted_iota(jnp.int32, (N_EXPERTS, N_EXPERTS), 0)
                     > lax.broadcasted_iota(jnp.int32, (N_EXPERTS, N_EXPERTS), 1))
            start = jnp.dot(jnp.where(below, 1.0, 0.0).astype(jnp.bfloat16), tiles.astype(jnp.bfloat16),
                            preferred_element_type=jnp.float32) * EXPERT_TILE
            start_ref[...] = start
            seg_end_ref[...] = (start + tiles * EXPERT_TILE).astype(jnp.int32)
        carry_ref[...] = jnp.zeros_like(carry_ref)

    eid = eid_ref[...]
    experts = lax.broadcasted_iota(jnp.int32, (N_EXPERTS, tt), 0)
    oh0 = experts == eid[0:1]
    oh1 = experts == eid[1:2]
    oh = jnp.where(oh0 | oh1, 1.0, 0.0)

    @pl.when(ph == 1)
    def _():
        earlier = (lax.broadcasted_iota(jnp.int32, (tt, tt), 0)
                   < lax.broadcasted_iota(jnp.int32, (tt, tt), 1))
        within = jnp.dot(oh.astype(jnp.bfloat16), jnp.where(earlier, 1.0, 0.0).astype(jnp.bfloat16),
                         preferred_element_type=jnp.float32)
        slot_of = within + carry_ref[:, 0:1] + start_ref[:, 0:1]
        s0 = jnp.sum(jnp.where(oh0, slot_of, 0.0), axis=0, keepdims=True)
        s1 = jnp.sum(jnp.where(oh1, slot_of, 0.0), axis=0, keepdims=True)
        k = lax.broadcasted_iota(jnp.int32, (SUBLANES, tt), 0)
        pos_ref[...] = jnp.where(k == 0, s0, jnp.where(k == 1, s1, 0.0)).astype(jnp.int32)

    carry_ref[...] = carry_ref[...] + jnp.sum(oh, axis=1, keepdims=True)


def _plan(eid):
    n_tokens = eid.shape[1]
    return pl.pallas_call(
        _plan_kernel,
        grid=(2, n_tokens // PLAN_TILE),
        in_specs=[pl.BlockSpec((SUBLANES, PLAN_TILE), lambda ph, i: (0, i))],
        out_specs=[pl.BlockSpec((SUBLANES, PLAN_TILE), lambda ph, i: (0, i * ph)),
                   pl.BlockSpec((N_EXPERTS, LANES), lambda ph, i: (0, 0))],
        out_shape=[jax.ShapeDtypeStruct((SUBLANES, n_tokens), jnp.int32),
                   jax.ShapeDtypeStruct((N_EXPERTS, LANES), jnp.int32)],
        scratch_shapes=[pltpu.VMEM((N_EXPERTS, LANES), jnp.float32),
                        pltpu.VMEM((N_EXPERTS, LANES), jnp.float32)],
        compiler_params=pltpu.CompilerParams(dimension_semantics=("arbitrary", "arbitrary")),
        name="route_plan",
    )(eid)


def _sc_scatter_rows(table, pos, n_slots):
    n_workers, n_batches, batch = pos.shape
    n_rows, words = table.shape
    assert n_rows % batch == 0 and n_workers * n_batches * batch == TOP_K * n_rows
    mesh = plsc.VectorSubcoreMesh(core_axis_name="c", subcore_axis_name="s")

    @functools.partial(
        pl.kernel, mesh=mesh,
        out_type=jax.ShapeDtypeStruct((n_slots, words), table.dtype),
        scratch_types=[pltpu.VMEM((n_batches, batch), jnp.int32),
                       pltpu.VMEM((batch, words), table.dtype)],
    )
    def scatter(table_hbm, pos_hbm, out_hbm, idx_v, rows_v):
        worker = lax.axis_index("s") * SC_CORES + lax.axis_index("c")
        pltpu.sync_copy(pos_hbm.at[worker], idx_v)

        @pl.loop(0, n_batches)
        def _(b):
            a0 = (worker * n_batches + b) * batch
            t0 = jnp.where(a0 >= n_rows, a0 - n_rows, a0)
            pltpu.sync_copy(table_hbm.at[pl.ds(t0, batch)], rows_v)
            pltpu.sync_copy(rows_v, out_hbm.at[idx_v.at[b]])

    return scatter(table, pos)


def _sc_gather_rows(table, idx):
    n_workers, n_batches, batch = idx.shape
    words = table.shape[1]
    mesh = plsc.VectorSubcoreMesh(core_axis_name="c", subcore_axis_name="s")

    @functools.partial(
        pl.kernel, mesh=mesh,
        out_type=jax.ShapeDtypeStruct((n_workers * n_batches * batch, words), table.dtype),
        scratch_types=[pltpu.VMEM((n_batches, batch), jnp.int32),
                       pltpu.VMEM((batch, words), table.dtype),
                       pltpu.SemaphoreType.DMA],
    )
    def gather(table_hbm, idx_hbm, out_hbm, idx_v, rows_v, sem):
        worker = lax.axis_index("s") * SC_CORES + lax.axis_index("c")
        pltpu.sync_copy(idx_hbm.at[worker], idx_v)

        @pl.loop(0, n_batches)
        def _(b):
            pltpu.async_copy(table_hbm.at[idx_v.at[b]], rows_v, sem).wait()
            pltpu.sync_copy(rows_v, out_hbm.at[pl.ds((worker * n_batches + b) * batch, batch)])

    return gather(table, idx)


def _experts_kernel(te_ref, nv_ref, xs_ref, wg_ref, wu_ref, wd_ref, ys_ref):
    i = pl.program_id(0)

    @pl.when(i < nv_ref[0])
    def _():
        hi, lo = _unpack_bf16_halves(xs_ref[...])
        hi = hi.astype(jnp.bfloat16)
        lo = lo.astype(jnp.bfloat16)
        wg = wg_ref[0].astype(jnp.bfloat16)
        wu = wu_ref[0].astype(jnp.bfloat16)
        wd = wd_ref[0].astype(jnp.bfloat16)
        gate = (jnp.dot(hi, wg[:HALF], preferred_element_type=jnp.float32)
                + jnp.dot(lo, wg[HALF:], preferred_element_type=jnp.float32))
        up = (jnp.dot(hi, wu[:HALF], preferred_element_type=jnp.float32)
              + jnp.dot(lo, wu[HALF:], preferred_element_type=jnp.float32))
        hid = (gate * jax.nn.sigmoid(gate) * up).astype(jnp.bfloat16)
        ys_ref[...] = _pack_bf16_halves(jnp.dot(hid, wd, preferred_element_type=jnp.float32))


def _experts(xs, tile_expert, n_valid, wg, wu, wd):
    n_slots = xs.shape[0]
    tm = EXPERT_TILE
    row_block = lambda i, te, nv: (jnp.minimum(i, nv[0] - 1), 0)
    w_block = lambda i, te, nv: (te[i], 0, 0)
    return pl.pallas_call(
        _experts_kernel,
        grid_spec=pltpu.PrefetchScalarGridSpec(
            num_scalar_prefetch=2,
            grid=(n_slots // tm,),
            in_specs=[pl.BlockSpec((tm, HALF), row_block),
                      pl.BlockSpec((1, D_MODEL, D_EXPERT), w_block),
                      pl.BlockSpec((1, D_MODEL, D_EXPERT), w_block),
                      pl.BlockSpec((1, D_EXPERT, D_MODEL), w_block)],
            out_specs=pl.BlockSpec((tm, HALF), row_block),
        ),
        out_shape=jax.ShapeDtypeStruct((n_slots, HALF), jnp.uint32),
        compiler_params=pltpu.CompilerParams(
            dimension_semantics=("arbitrary",), vmem_limit_bytes=VMEM_LIMIT),
        name="experts",
    )(tile_expert, n_valid, xs, wg, wu, wd)


def _final_kernel(x1_ref, y0_ref, y1_ref, p_ref, gfin_ref, out_ref):
    pt = jnp.transpose(p_ref[...])
    p0 = pt[:, 0:1]
    p1 = pt[:, 1:2]
    a_hi, a_lo = _unpack_bf16_halves(y0_ref[...])
    b_hi, b_lo = _unpack_bf16_halves(y1_ref[...])
    x1 = x1_ref[...]
    x2_hi = x1[:, :HALF] + (p0 * a_hi + p1 * b_hi)
    x2_lo = x1[:, HALF:] + (p0 * a_lo + p1 * b_lo)
    ms = (jnp.sum(x2_hi * x2_hi, axis=-1, keepdims=True)
          + jnp.sum(x2_lo * x2_lo, axis=-1, keepdims=True)) / D_MODEL
    scale = lax.rsqrt(ms + RMS_EPS)
    g = gfin_ref[...]
    out_ref[:, :HALF] = x2_hi * scale * g[:, :HALF]
    out_ref[:, HALF:] = x2_lo * scale * g[:, HALF:]


def _final(x1, yk, p, gfin, row_offset, n_rows, n_tokens, tm):
    first = row_offset // tm
    second = (n_tokens + row_offset) // tm
    return pl.pallas_call(
        _final_kernel,
        grid=(n_rows // tm,),
        in_specs=[pl.BlockSpec((tm, D_MODEL), lambda i: (first + i, 0)),
                  pl.BlockSpec((tm, HALF), lambda i: (first + i, 0)),
                  pl.BlockSpec((tm, HALF), lambda i: (second + i, 0)),
                  pl.BlockSpec((SUBLANES, tm), lambda i: (0, first + i)),
                  pl.BlockSpec((1, D_MODEL), lambda i: (0, 0))],
        out_specs=pl.BlockSpec((tm, D_MODEL), lambda i: (i, 0)),
        out_shape=jax.ShapeDtypeStruct((n_rows, D_MODEL), jnp.float32),
        compiler_params=pltpu.CompilerParams(
            dimension_semantics=("arbitrary",), vmem_limit_bytes=VMEM_LIMIT),
        name="final",
    )(x1, yk, yk, p, gfin)


def _slot_capacity(n_tokens):
    worst = TOP_K * n_tokens + N_EXPERTS * (EXPERT_TILE - 1)
    unit = SC_WORKERS * SC_BATCH
    return -(-worst // unit) * unit


def kernel(x_prompt, x_sample, state_conv_a, state_conv_b, g_mix, w_in, conv_a_w, conv_b_w, conv_b_bias,
           ln_g, ln_b, w_out, g_ffn, w_coarse, b_coarse, w_fine, b_fine, w_gate, w_up, w_down, g_final):
    assert g_mix.shape[0] == 1, "single trunk layer"
    batch, seq, _ = x_prompt.shape
    dec_batch, dec_seq, _ = x_sample.shape
    n_prompt = batch * seq
    n_sample = dec_batch * dec_seq
    n_tokens = n_prompt + n_sample
    assert (TOP_K * n_tokens) % (SC_WORKERS * SC_BATCH) == 0
    bf16 = jnp.bfloat16

    wr = jnp.concatenate([
        w_coarse[0], jnp.zeros((D_MODEL, SUBLANES - N_EXPERT_GROUPS), jnp.float32),
        jnp.transpose(w_fine[0], (1, 0, 2)).reshape(D_MODEL, N_EXPERTS)], axis=1).T
    wr_hi = wr.astype(bf16)
    wr_lo = (wr - wr_hi.astype(jnp.float32)).astype(bf16)
    br = jnp.concatenate([
        b_coarse[0], jnp.full((SUBLANES - N_EXPERT_GROUPS,), NEG_BIG, jnp.float32),
        b_fine[0].reshape(N_EXPERTS)]).reshape(ROUTER_ROWS, 1)

    params = (g_mix, w_in[0].astype(bf16), conv_a_w[0], conv_b_w[0], conv_b_bias, ln_g, ln_b,
              w_out[0].astype(bf16), g_ffn, wr_hi, wr_lo, br)

    x1, h2p, eid, p, na_p, nb_p = _mixer_prompt(x_prompt, params, n_tokens, tt=512)
    x1, h2p, eid, p, na_s, nb_s = _mixer_sample(
        x_sample, state_conv_a[0], state_conv_b[0], params, (x1, h2p, eid, p), n_prompt, nseq=32)

    pos, seg_end = _plan(eid)
    pos = pos[:TOP_K].reshape(SC_WORKERS, -1, SC_BATCH)
    seg_end = seg_end[:, 0]
    n_slots = _slot_capacity(n_tokens)
    n_tiles = n_slots // EXPERT_TILE
    n_valid = seg_end[N_EXPERTS - 1:] // EXPERT_TILE
    tile_row = jnp.arange(n_tiles, dtype=jnp.int32) * EXPERT_TILE
    tile_expert = jnp.minimum(jnp.sum(seg_end[None, :] <= tile_row[:, None], axis=1), N_EXPERTS - 1)

    xs = _sc_scatter_rows(h2p, pos, n_slots)
    ys = _experts(xs, tile_expert.astype(jnp.int32), n_valid, w_gate[0], w_up[0], w_down[0])
    yk = _sc_gather_rows(ys, pos)

    gfin = g_final.reshape(1, D_MODEL)
    y_p = _final(x1, yk, p, gfin, 0, n_prompt, n_tokens, tm=512)
    y_s = _final(x1, yk, p, gfin, n_prompt, n_sample, n_tokens, tm=512)
    return (y_p.reshape(batch, seq, D_MODEL), y_s.reshape(dec_batch, dec_seq, D_MODEL),
            na_p[None], nb_p[None], na_s[None], nb_s[None])
```

```python
import functools

import jax
import jax.numpy as jnp
from jax import lax
from jax.experimental import pallas as pl
from jax.experimental.pallas import tpu as pltpu
from jax.experimental.pallas import tpu_sc as plsc

D_MODEL = 1024
D_A = 512
D_B = 512
CONV_A = 3
CONV_B = 31
HALO_A = CONV_A - 1
HALO_B = CONV_B - 1
IN_COLS = 3 * D_A + 2 * D_B
N_EXPERT_GROUPS = 4
EXPERTS_PER_GROUP = 8
N_EXPERTS = N_EXPERT_GROUPS * EXPERTS_PER_GROUP
TOP_K = 2
D_EXPERT = D_MODEL // 4
RMS_EPS = 1e-6
LN_EPS = 1e-5

SUBLANES = 8
LANES = 128
PAD_A = SUBLANES
PAD_B = 32
ROUTER_ROWS = SUBLANES + N_EXPERTS
NEG_BIG = -1e30
VMEM_LIMIT = 56 * 1024 * 1024
HALF = D_MODEL // 2
HI_MASK = 0xFFFF0000

SC_CORES = 2
SC_SUBCORES = 16
SC_WORKERS = SC_CORES * SC_SUBCORES
SC_BATCH = 64

EXPERT_TILE = 1024
PLAN_TILE = 512


def _rms_scale(x):
    return x * lax.rsqrt(jnp.mean(x * x, axis=-1, keepdims=True) + RMS_EPS)


def _dot_nt(a, b):
    return lax.dot_general(a, b, (((1,), (1,)), ((), ())), preferred_element_type=jnp.float32)


def _pack_bf16_halves(x):
    bits = lax.bitcast_convert_type(x.astype(jnp.bfloat16).astype(jnp.float32), jnp.uint32)
    return bits[:, :HALF] | (bits[:, HALF:] >> 16)


def _unpack_bf16_halves(w):
    hi = lax.bitcast_convert_type(w & jnp.uint32(HI_MASK), jnp.float32)
    lo = lax.bitcast_convert_type(w << 16, jnp.float32)
    return hi, lo


def _route(logits_t):
    rows = logits_t.shape[1]
    iota = lax.broadcasted_iota(jnp.int32, (SUBLANES, rows), 0)
    lc = logits_t[0:SUBLANES]
    cmax = jnp.max(lc, axis=0, keepdims=True)
    grp = jnp.min(jnp.where(lc == cmax, iota, SUBLANES), axis=0, keepdims=True)
    p_grp = 1.0 / jnp.sum(jnp.exp(lc - cmax), axis=0, keepdims=True)
    sel = logits_t[SUBLANES:2 * SUBLANES]
    for g in range(1, N_EXPERT_GROUPS):
        sel = jnp.where(grp == g, logits_t[(g + 1) * SUBLANES:(g + 2) * SUBLANES], sel)
    v1 = jnp.max(sel, axis=0, keepdims=True)
    i1 = jnp.min(jnp.where(sel == v1, iota, SUBLANES), axis=0, keepdims=True)
    sel2 = jnp.where(iota == i1, -jnp.inf, sel)
    v2 = jnp.max(sel2, axis=0, keepdims=True)
    i2 = jnp.min(jnp.where(sel2 == v2, iota, SUBLANES), axis=0, keepdims=True)
    e2 = jnp.exp(v2 - v1)
    den = 1.0 + e2
    p1 = p_grp / den
    p2 = p_grp * e2 / den
    base = grp * EXPERTS_PER_GROUP
    return (base + i1, base + i2), (p1, p2)


def _mixer_body(x_ref, sa_ref, sb_ref, gmix_ref, win_ref, caw_ref, cbw_ref, cbb_ref, lng_ref, lnb_ref,
                wout_ref, gffn_ref, wrh_ref, wrl_ref, br_ref,
                x1_ref, h2p_ref, eid_ref, p_ref, na_ref, nb_ref,
                proj_ref, uext_ref, gext_ref, z_ref, ush_ref, gsh_ref, *, nseq, tt, seq_chunk, row_chunk, n_sub):
    carried = sa_ref is None
    t = pl.program_id(1) if carried else None
    assert n_sub == 1 or nseq == 1
    sub = tt // n_sub

    if carried:
        @pl.when(t == 0)
        def _():
            uext_ref[:, 0:PAD_A, :] = jnp.zeros((nseq, PAD_A, D_A), jnp.float32)
            gext_ref[:, 0:PAD_B, :] = jnp.zeros((nseq, PAD_B, D_B), jnp.float32)
    else:
        uext_ref[:, PAD_A - HALO_A:PAD_A, :] = sa_ref[...]
        gext_ref[:, PAD_B - HALO_B:PAD_B, :] = sb_ref[...]

    def window(base_ref, shifted_ref, first_shift, s0, off, n_rows):
        r = off % SUBLANES
        a8 = off - r
        if r == 0:
            return base_ref[s0:s0 + seq_chunk, a8:a8 + n_rows, :]
        return shifted_ref[r - first_shift, s0:s0 + seq_chunk, a8:a8 + n_rows, :]

    n = seq_chunk * row_chunk
    col_chunk = 2 * LANES
    caw = caw_ref[...]
    cbw = cbw_ref[...]
    hs_bf = {}

    def rows_of(si):
        q0 = si * sub
        f0 = q0 if nseq == 1 else 0
        return q0, f0, nseq * sub

    def prep(si):
        q0, _, m = rows_of(si)
        x = x_ref[:, q0:q0 + sub, :].reshape(m, D_MODEL)
        hs_bf[si] = (_rms_scale(x) * gmix_ref[...]).astype(jnp.bfloat16)

    def dot_b_items(si):
        def item(c0):
            def run():
                q0, _, _ = rows_of(si)
                v_b = jnp.dot(hs_bf[si], win_ref[:, 3 * D_A + c0:3 * D_A + c0 + col_chunk],
                              preferred_element_type=jnp.float32)
                g_b = jnp.dot(hs_bf[si], win_ref[:, 3 * D_A + D_B + c0:3 * D_A + D_B + c0 + col_chunk],
                              preferred_element_type=jnp.float32)
                gext_ref[:, PAD_B + q0:PAD_B + q0 + sub, c0:c0 + col_chunk] = (
                    v_b * jax.nn.sigmoid(g_b)).reshape(nseq, sub, col_chunk)
            return run
        return [item(c0) for c0 in range(0, D_B, col_chunk)]

    def dot_a_items(si):
        def item(c0):
            def run():
                _, f0, m = rows_of(si)
                proj_ref[f0:f0 + m, c0:c0 + col_chunk] = jnp.dot(
                    hs_bf[si], win_ref[:, c0:c0 + col_chunk], preferred_element_type=jnp.float32)
            return run
        return [item(c0) for c0 in range(0, 3 * D_A, col_chunk)]

    def shift_b(si):
        q0, _, _ = rows_of(si)
        j_lo = 0 if si == 0 else q0 + PAD_B - SUBLANES
        j_hi = q0 + sub + PAD_B - SUBLANES
        for r in range(1, SUBLANES):
            gsh_ref[r - 1, :, j_lo:j_hi, :] = gext_ref[:, j_lo + r:j_hi + r, :]

    def chunks_of(si):
        q0, _, _ = rows_of(si)
        return [(s0, r0) for s0 in range(0, nseq, seq_chunk) for r0 in range(q0, q0 + sub, row_chunk)]

    def conv_b_items(si):
        def item(s0, r0):
            def run():
                lo = s0 * tt + r0
                acc_b = None
                for k in range(CONV_B):
                    off = PAD_B - HALO_B + k + r0
                    term = window(gext_ref, gsh_ref, 1, s0, off, row_chunk) * cbw[k:k + 1, :]
                    acc_b = term if acc_b is None else acc_b + term
                zb = acc_b.reshape(n, D_B) + cbb_ref[...]
                mu = jnp.mean(zb, axis=-1, keepdims=True)
                zc = zb - mu
                var = jnp.mean(zc * zc, axis=-1, keepdims=True)
                y = zc * lax.rsqrt(var + LN_EPS) * lng_ref[...] + lnb_ref[...]
                z_ref[lo:lo + n, D_A:] = (y * jax.nn.sigmoid(y)).astype(jnp.bfloat16)
            return run
        return [item(s0, r0) for s0, r0 in chunks_of(si)]

    def conv_a_item(si):
        def run():
            q0, f0, m = rows_of(si)
            c_a = proj_ref[f0:f0 + m, D_A:2 * D_A]
            v_a = proj_ref[f0:f0 + m, 2 * D_A:3 * D_A]
            uext_ref[:, PAD_A + q0:PAD_A + q0 + sub, :] = (c_a * v_a).reshape(nseq, sub, D_A)
            for r in range(SUBLANES - HALO_A, SUBLANES):
                ush_ref[r - (SUBLANES - HALO_A), :, q0:q0 + sub, :] = uext_ref[:, q0 + r:q0 + sub + r, :]
            for s0, r0 in chunks_of(si):
                lo = s0 * tt + r0
                acc_a = None
                for k in range(CONV_A):
                    off = PAD_A - HALO_A + k + r0
                    term = window(uext_ref, ush_ref, SUBLANES - HALO_A, s0, off, row_chunk) * caw[k:k + 1, :]
                    acc_a = term if acc_a is None else acc_a + term
                z_a = proj_ref[lo:lo + n, 0:D_A] * acc_a.reshape(n, D_A)
                z_ref[lo:lo + n, 0:D_A] = z_a.astype(jnp.bfloat16)
        return run

    def finish_item(si):
        def run():
            q0, f0, m = rows_of(si)
            x = x_ref[:, q0:q0 + sub, :].reshape(m, D_MODEL)
            x1 = x + jnp.dot(z_ref[f0:f0 + m, :], wout_ref[...], preferred_element_type=jnp.float32)
            x1_ref[f0:f0 + m, :] = x1
            h2 = _rms_scale(x1) * gffn_ref[...]
            h2_hi = h2.astype(jnp.bfloat16)
            h2p_ref[f0:f0 + m, :] = _pack_bf16_halves(h2)
            h2_lo = (h2 - h2_hi.astype(jnp.float32)).astype(jnp.bfloat16)
            wrh = wrh_ref[...]
            logits_t = _dot_nt(wrh, h2_hi) + _dot_nt(wrh, h2_lo) + _dot_nt(wrl_ref[...], h2_hi) + br_ref[...]
            (e1, e2), (p1, p2) = _route(logits_t)
            iota = lax.broadcasted_iota(jnp.int32, (SUBLANES, m), 0)
            eid_ref[:, f0:f0 + m] = jnp.where(iota == 0, e1, jnp.where(iota == 1, e2, 0))
            p_ref[:, f0:f0 + m] = jnp.where(iota == 0, p1, jnp.where(iota == 1, p2, 0.0))
        return run

    def interleave(main, fill):
        for i, item in enumerate(main):
            item()
            for f in fill[i * len(fill) // len(main):(i + 1) * len(fill) // len(main)]:
                f()

    for si in range(n_sub):
        prep(si)
    for item in dot_b_items(0):
        item()
    shift_b(0)
    carry_over = []
    for si in range(n_sub):
        fill = carry_over + dot_a_items(si)
        if si + 1 < n_sub:
            fill = fill + dot_b_items(si + 1)
        interleave(conv_b_items(si), fill + [conv_a_item(si)])
        if si + 1 < n_sub:
            shift_b(si + 1)
        carry_over = [finish_item(si)]
    for item in carry_over:
        item()

    if carried:
        @pl.when(t == pl.num_programs(1) - 1)
        def _():
            na_ref[...] = uext_ref[:, PAD_A + tt - HALO_A:PAD_A + tt, :]
            nb_ref[...] = gext_ref[:, PAD_B + tt - HALO_B:PAD_B + tt, :]
        uext_ref[:, 0:PAD_A, :] = uext_ref[:, tt:tt + PAD_A, :]
        gext_ref[:, 0:PAD_B, :] = gext_ref[:, tt:tt + PAD_B, :]
    else:
        na_ref[...] = uext_ref[:, PAD_A + tt - HALO_A:PAD_A + tt, :]
        nb_ref[...] = gext_ref[:, PAD_B + tt - HALO_B:PAD_B + tt, :]


def _mixer_prompt_kernel(x_ref, *refs, **kw):
    _mixer_body(x_ref, None, None, *refs, **kw)


def _mixer_sample_kernel(x_ref, sa_ref, sb_ref, *refs, **kw):
    n_params = 12
    params = refs[:n_params]
    rest = refs[n_params + 4:]
    _mixer_body(x_ref, sa_ref, sb_ref, *params, *rest, **kw)


def _full(shape):
    return pl.BlockSpec(shape, lambda *_: (0,) * len(shape))


def _mixer_param_specs():
    return [
        _full((1, D_MODEL)),
        _full((D_MODEL, IN_COLS)),
        _full((CONV_A, D_A)),
        _full((CONV_B, D_B)),
        _full((1, D_B)),
        _full((1, D_B)),
        _full((1, D_B)),
        _full((D_MODEL, D_MODEL)),
        _full((1, D_MODEL)),
        _full((ROUTER_ROWS, D_MODEL)),
        _full((ROUTER_ROWS, D_MODEL)),
        _full((ROUTER_ROWS, 1)),
    ]


def _mixer_scratch(nseq, tt):
    rows = nseq * tt
    return [
        pltpu.VMEM((rows, 3 * D_A), jnp.float32),
        pltpu.VMEM((nseq, PAD_A + tt, D_A), jnp.float32),
        pltpu.VMEM((nseq, PAD_B + tt, D_B), jnp.float32),
        pltpu.VMEM((rows, D_MODEL), jnp.bfloat16),
        pltpu.VMEM((HALO_A, nseq, PAD_A + tt - SUBLANES, D_A), jnp.float32),
        pltpu.VMEM((SUBLANES - 1, nseq, PAD_B + tt - SUBLANES, D_B), jnp.float32),
    ]


def _mixer_prompt(x, params, n_tokens_total, tt):
    batch, seq, _ = x.shape
    n_t = seq // tt
    tok = lambda b, t: (b * n_t + t, 0)
    lane_tok = lambda b, t: (0, b * n_t + t)
    out_shape = [
        jax.ShapeDtypeStruct((n_tokens_total, D_MODEL), jnp.float32),
        jax.ShapeDtypeStruct((n_tokens_total, HALF), jnp.uint32),
        jax.ShapeDtypeStruct((SUBLANES, n_tokens_total), jnp.int32),
        jax.ShapeDtypeStruct((SUBLANES, n_tokens_total), jnp.float32),
        jax.ShapeDtypeStruct((batch, HALO_A, D_A), jnp.float32),
        jax.ShapeDtypeStruct((batch, HALO_B, D_B), jnp.float32),
    ]
    out_specs = [
        pl.BlockSpec((tt, D_MODEL), tok),
        pl.BlockSpec((tt, HALF), tok),
        pl.BlockSpec((SUBLANES, tt), lane_tok),
        pl.BlockSpec((SUBLANES, tt), lane_tok),
        pl.BlockSpec((1, HALO_A, D_A), lambda b, t: (b, 0, 0)),
        pl.BlockSpec((1, HALO_B, D_B), lambda b, t: (b, 0, 0)),
    ]
    return pl.pallas_call(
        functools.partial(_mixer_prompt_kernel, nseq=1, tt=tt, seq_chunk=1, row_chunk=64, n_sub=2),
        grid=(batch, n_t),
        in_specs=[pl.BlockSpec((1, tt, D_MODEL), lambda b, t: (b, t, 0))] + _mixer_param_specs(),
        out_specs=out_specs,
        out_shape=out_shape,
        scratch_shapes=_mixer_scratch(1, tt),
        compiler_params=pltpu.CompilerParams(
            dimension_semantics=("arbitrary", "arbitrary"), vmem_limit_bytes=VMEM_LIMIT),
        name="mixer_prompt",
    )(x, *params)


def _mixer_sample(x, state_a, state_b, params, bufs, row_offset, nseq):
    batch, tt, _ = x.shape
    rows = nseq * tt
    first = row_offset // rows
    tok = lambda i: (first + i, 0)
    lane_tok = lambda i: (0, first + i)
    x1, h2p, eid, p = bufs
    out_shape = [
        jax.ShapeDtypeStruct(x1.shape, x1.dtype),
        jax.ShapeDtypeStruct(h2p.shape, h2p.dtype),
        jax.ShapeDtypeStruct(eid.shape, eid.dtype),
        jax.ShapeDtypeStruct(p.shape, p.dtype),
        jax.ShapeDtypeStruct((batch, HALO_A, D_A), jnp.float32),
        jax.ShapeDtypeStruct((batch, HALO_B, D_B), jnp.float32),
    ]
    out_specs = [
        pl.BlockSpec((rows, D_MODEL), tok),
        pl.BlockSpec((rows, HALF), tok),
        pl.BlockSpec((SUBLANES, rows), lane_tok),
        pl.BlockSpec((SUBLANES, rows), lane_tok),
        pl.BlockSpec((nseq, HALO_A, D_A), lambda i: (i, 0, 0)),
        pl.BlockSpec((nseq, HALO_B, D_B), lambda i: (i, 0, 0)),
    ]
    any_spec = pl.BlockSpec(memory_space=pl.ANY)
    in_specs = ([pl.BlockSpec((nseq, tt, D_MODEL), lambda i: (i, 0, 0)),
                 pl.BlockSpec((nseq, HALO_A, D_A), lambda i: (i, 0, 0)),
                 pl.BlockSpec((nseq, HALO_B, D_B), lambda i: (i, 0, 0))]
                + _mixer_param_specs() + [any_spec] * 4)
    n_in = len(in_specs)
    return pl.pallas_call(
        functools.partial(_mixer_sample_kernel, nseq=nseq, tt=tt, seq_chunk=8, row_chunk=tt, n_sub=1),
        grid=(batch // nseq,),
        in_specs=in_specs,
        out_specs=out_specs,
        out_shape=out_shape,
        scratch_shapes=_mixer_scratch(nseq, tt),
        input_output_aliases={n_in - 4: 0, n_in - 3: 1, n_in - 2: 2, n_in - 1: 3},
        compiler_params=pltpu.CompilerParams(
            dimension_semantics=("arbitrary",), vmem_limit_bytes=VMEM_LIMIT),
        name="mixer_sample",
    )(x, state_a, state_b, *params, x1, h2p, eid, p)


def _plan_kernel(eid_ref, pos_ref, seg_end_ref, carry_ref, start_ref):
    ph = pl.program_id(0)
    i = pl.program_id(1)
    tt = eid_ref.shape[1]

    @pl.when(i == 0)
    def _():
        @pl.when(ph == 1)
        def _():
            tiles = jnp.ceil(carry_ref[...] / EXPERT_TILE)
            below = (lax.broadcasted_iota(jnp.int32, (N_EXPERTS, N_EXPERTS), 0)
                     > lax.broadcasted_iota(jnp.int32, (N_EXPERTS, N_EXPERTS), 1))
            start = jnp.dot(jnp.where(below, 1.0, 0.0).astype(jnp.bfloat16), tiles.astype(jnp.bfloat16),
                            preferred_element_type=jnp.float32) * EXPERT_TILE
            start_ref[...] = start
            seg_end_ref[...] = (start + tiles * EXPERT_TILE).astype(jnp.int32)
        carry_ref[...] = jnp.zeros_like(carry_ref)

    eid = eid_ref[...]
    experts = lax.broadcasted_iota(jnp.int32, (N_EXPERTS, tt), 0)
    oh0 = experts == eid[0:1]
    oh1 = experts == eid[1:2]
    oh = jnp.where(oh0 | oh1, 1.0, 0.0)

    @pl.when(ph == 1)
    def _():
        earlier = (lax.broadcasted_iota(jnp.int32, (tt, tt), 0)
                   < lax.broadcasted_iota(jnp.int32, (tt, tt), 1))
        within = jnp.dot(oh.astype(jnp.bfloat16), jnp.where(earlier, 1.0, 0.0).astype(jnp.bfloat16),
                         preferred_element_type=jnp.float32)
        slot_of = within + carry_ref[:, 0:1] + start_ref[:, 0:1]
        s0 = jnp.sum(jnp.where(oh0, slot_of, 0.0), axis=0, keepdims=True)
        s1 = jnp.sum(jnp.where(oh1, slot_of, 0.0), axis=0, keepdims=True)
        k = lax.broadcasted_iota(jnp.int32, (SUBLANES, tt), 0)
        pos_ref[...] = jnp.where(k == 0, s0, jnp.where(k == 1, s1, 0.0)).astype(jnp.int32)

    carry_ref[...] = carry_ref[...] + jnp.sum(oh, axis=1, keepdims=True)


def _plan(eid):
    n_tokens = eid.shape[1]
    return pl.pallas_call(
        _plan_kernel,
        grid=(2, n_tokens // PLAN_TILE),
        in_specs=[pl.BlockSpec((SUBLANES, PLAN_TILE), lambda ph, i: (0, i))],
        out_specs=[pl.BlockSpec((SUBLANES, PLAN_TILE), lambda ph, i: (0, i * ph)),
                   pl.BlockSpec((N_EXPERTS, LANES), lambda ph, i: (0, 0))],
        out_shape=[jax.ShapeDtypeStruct((SUBLANES, n_tokens), jnp.int32),
                   jax.ShapeDtypeStruct((N_EXPERTS, LANES), jnp.int32)],
        scratch_shapes=[pltpu.VMEM((N_EXPERTS, LANES), jnp.float32),
                        pltpu.VMEM((N_EXPERTS, LANES), jnp.float32)],
        compiler_params=pltpu.CompilerParams(dimension_semantics=("arbitrary", "arbitrary")),
        name="route_plan",
    )(eid)


def _sc_scatter_rows(table, pos, n_slots):
    n_workers, n_batches, batch = pos.shape
    n_rows, words = table.shape
    assert n_rows % batch == 0 and n_workers * n_batches * batch == TOP_K * n_rows
    mesh = plsc.VectorSubcoreMesh(core_axis_name="c", subcore_axis_name="s")

    @functools.partial(
        pl.kernel, mesh=mesh,
        out_type=jax.ShapeDtypeStruct((n_slots, words), table.dtype),
        scratch_types=[pltpu.VMEM((n_batches, batch), jnp.int32),
                       pltpu.VMEM((batch, words), table.dtype)],
    )
    def scatter(table_hbm, pos_hbm, out_hbm, idx_v, rows_v):
        worker = lax.axis_index("s") * SC_CORES + lax.axis_index("c")
        pltpu.sync_copy(pos_hbm.at[worker], idx_v)

        @pl.loop(0, n_batches)
        def _(b):
            a0 = (worker * n_batches + b) * batch
            t0 = jnp.where(a0 >= n_rows, a0 - n_rows, a0)
            pltpu.sync_copy(table_hbm.at[pl.ds(t0, batch)], rows_v)
            pltpu.sync_copy(rows_v, out_hbm.at[idx_v.at[b]])

    return scatter(table, pos)


def _sc_gather_rows(table, idx):
    n_workers, n_batches, batch = idx.shape
    words = table.shape[1]
    mesh = plsc.VectorSubcoreMesh(core_axis_name="c", subcore_axis_name="s")

    @functools.partial(
        pl.kernel, mesh=mesh,
        out_type=jax.ShapeDtypeStruct((n_workers * n_batches * batch, words), table.dtype),
        scratch_types=[pltpu.VMEM((n_batches, batch), jnp.int32),
                       pltpu.VMEM((batch, words), table.dtype),
                       pltpu.SemaphoreType.DMA],
    )
    def gather(table_hbm, idx_hbm, out_hbm, idx_v, rows_v, sem):
        worker = lax.axis_index("s") * SC_CORES + lax.axis_index("c")
        pltpu.sync_copy(idx_hbm.at[worker], idx_v)

        @pl.loop(0, n_batches)
        def _(b):
            pltpu.async_copy(table_hbm.at[idx_v.at[b]], rows_v, sem).wait()
            pltpu.sync_copy(rows_v, out_hbm.at[pl.ds((worker * n_batches + b) * batch, batch)])

    return gather(table, idx)


def _experts_kernel(te_ref, nv_ref, xs_ref, wg_ref, wu_ref, wd_ref, ys_ref):
    i = pl.program_id(0)

    @pl.when(i < nv_ref[0])
    def _():
        hi, lo = _unpack_bf16_halves(xs_ref[...])
        hi = hi.astype(jnp.bfloat16)
        lo = lo.astype(jnp.bfloat16)
        wg = wg_ref[0].astype(jnp.bfloat16)
        wu = wu_ref[0].astype(jnp.bfloat16)
        wd = wd_ref[0].astype(jnp.bfloat16)
        gate = (jnp.dot(hi, wg[:HALF], preferred_element_type=jnp.float32)
                + jnp.dot(lo, wg[HALF:], preferred_element_type=jnp.float32))
        up = (jnp.dot(hi, wu[:HALF], preferred_element_type=jnp.float32)
              + jnp.dot(lo, wu[HALF:], preferred_element_type=jnp.float32))
        hid = (gate * jax.nn.sigmoid(gate) * up).astype(jnp.bfloat16)
        ys_ref[...] = _pack_bf16_halves(jnp.dot(hid, wd, preferred_element_type=jnp.float32))


def _experts(xs, tile_expert, n_valid, wg, wu, wd):
    n_slots = xs.shape[0]
    tm = EXPERT_TILE
    row_block = lambda i, te, nv: (jnp.minimum(i, nv[0] - 1), 0)
    w_block = lambda i, te, nv: (te[i], 0, 0)
    return pl.pallas_call(
        _experts_kernel,
        grid_spec=pltpu.PrefetchScalarGridSpec(
            num_scalar_prefetch=2,
            grid=(n_slots // tm,),
            in_specs=[pl.BlockSpec((tm, HALF), row_block),
                      pl.BlockSpec((1, D_MODEL, D_EXPERT), w_block),
                      pl.BlockSpec((1, D_MODEL, D_EXPERT), w_block),
                      pl.BlockSpec((1, D_EXPERT, D_MODEL), w_block)],
            out_specs=pl.BlockSpec((tm, HALF), row_block),
        ),
        out_shape=jax.ShapeDtypeStruct((n_slots, HALF), jnp.uint32),
        compiler_params=pltpu.CompilerParams(
            dimension_semantics=("arbitrary",), vmem_limit_bytes=VMEM_LIMIT),
        name="experts",
    )(tile_expert, n_valid, xs, wg, wu, wd)


def _final_kernel(x1_ref, y0_ref, y1_ref, p_ref, gfin_ref, out_ref):
    pt = jnp.transpose(p_ref[...])
    p0 = pt[:, 0:1]
    p1 = pt[:, 1:2]
    a_hi, a_lo = _unpack_bf16_halves(y0_ref[...])
    b_hi, b_lo = _unpack_bf16_halves(y1_ref[...])
    x1 = x1_ref[...]
    x2_hi = x1[:, :HALF] + (p0 * a_hi + p1 * b_hi)
    x2_lo = x1[:, HALF:] + (p0 * a_lo + p1 * b_lo)
    ms = (jnp.sum(x2_hi * x2_hi, axis=-1, keepdims=True)
          + jnp.sum(x2_lo * x2_lo, axis=-1, keepdims=True)) / D_MODEL
    scale = lax.rsqrt(ms + RMS_EPS)
    g = gfin_ref[...]
    out_ref[:, :HALF] = x2_hi * scale * g[:, :HALF]
    out_ref[:, HALF:] = x2_lo * scale * g[:, HALF:]


def _final(x1, yk, p, gfin, row_offset, n_rows, n_tokens, tm):
    first = row_offset // tm
    second = (n_tokens + row_offset) // tm
    return pl.pallas_call(
        _final_kernel,
        grid=(n_rows // tm,),
        in_specs=[pl.BlockSpec((tm, D_MODEL), lambda i: (first + i, 0)),
                  pl.BlockSpec((tm, HALF), lambda i: (first + i, 0)),
                  pl.BlockSpec((tm, HALF), lambda i: (second + i, 0)),
                  pl.BlockSpec((SUBLANES, tm), lambda i: (0, first + i)),
                  pl.BlockSpec((1, D_MODEL), lambda i: (0, 0))],
        out_specs=pl.BlockSpec((tm, D_MODEL), lambda i: (i, 0)),
        out_shape=jax.ShapeDtypeStruct((n_rows, D_MODEL), jnp.float32),
        compiler_params=pltpu.CompilerParams(
            dimension_semantics=("arbitrary",), vmem_limit_bytes=VMEM_LIMIT),
        name="final",
    )(x1, yk, yk, p, gfin)


def _slot_capacity(n_tokens):
    worst = TOP_K * n_tokens + N_EXPERTS * (EXPERT_TILE - 1)
    unit = SC_WORKERS * SC_BATCH
    return -(-worst // unit) * unit


def kernel(x_prompt, x_sample, state_conv_a, state_conv_b, g_mix, w_in, conv_a_w, conv_b_w, conv_b_bias,
           ln_g, ln_b, w_out, g_ffn, w_coarse, b_coarse, w_fine, b_fine, w_gate, w_up, w_down, g_final):
    assert g_mix.shape[0] == 1, "single trunk layer"
    batch, seq, _ = x_prompt.shape
    dec_batch, dec_seq, _ = x_sample.shape
    n_prompt = batch * seq
    n_sample = dec_batch * dec_seq
    n_tokens = n_prompt + n_sample
    assert (TOP_K * n_tokens) % (SC_WORKERS * SC_BATCH) == 0
    bf16 = jnp.bfloat16

    wr = jnp.concatenate([
        w_coarse[0], jnp.zeros((D_MODEL, SUBLANES - N_EXPERT_GROUPS), jnp.float32),
        jnp.transpose(w_fine[0], (1, 0, 2)).reshape(D_MODEL, N_EXPERTS)], axis=1).T
    wr_hi = wr.astype(bf16)
    wr_lo = (wr - wr_hi.astype(jnp.float32)).astype(bf16)
    br = jnp.concatenate([
        b_coarse[0], jnp.full((SUBLANES - N_EXPERT_GROUPS,), NEG_BIG, jnp.float32),
        b_fine[0].reshape(N_EXPERTS)]).reshape(ROUTER_ROWS, 1)

    params = (g_mix, w_in[0].astype(bf16), conv_a_w[0], conv_b_w[0], conv_b_bias, ln_g, ln_b,
              w_out[0].astype(bf16), g_ffn, wr_hi, wr_lo, br)

    x1, h2p, eid, p, na_p, nb_p = _mixer_prompt(x_prompt, params, n_tokens, tt=512)
    x1, h2p, eid, p, na_s, nb_s = _mixer_sample(
        x_sample, state_conv_a[0], state_conv_b[0], params, (x1, h2p, eid, p), n_prompt, nseq=32)

    pos, seg_end = _plan(eid)
    pos = pos[:TOP_K].reshape(SC_WORKERS, -1, SC_BATCH)
    seg_end = seg_end[:, 0]
    n_slots = _slot_capacity(n_tokens)
    n_tiles = n_slots // EXPERT_TILE
    n_valid = seg_end[N_EXPERTS - 1:] // EXPERT_TILE
    tile_row = jnp.arange(n_tiles, dtype=jnp.int32) * EXPERT_TILE
    tile_expert = jnp.minimum(jnp.sum(seg_end[None, :] <= tile_row[:, None], axis=1), N_EXPERTS - 1)

    xs = _sc_scatter_rows(h2p, pos, n_slots)
    ys = _experts(xs, tile_expert.astype(jnp.int32), n_valid, w_gate[0], w_up[0], w_down[0])
    yk = _sc_gather_rows(ys, pos)

    gfin = g_final.reshape(1, D_MODEL)
    y_p = _final(x1, yk, p, gfin, 0, n_prompt, n_tokens, tm=512)
    y_s = _final(x1, yk, p, gfin, n_prompt, n_sample, n_tokens, tm=512)
    return (y_p.reshape(batch, seq, D_MODEL), y_s.reshape(dec_batch, dec_seq, D_MODEL),
            na_p[None], nb_p[None], na_s[None], nb_s[None])
```

```python
import functools

import jax
import jax.numpy as jnp
from jax import lax
from jax.experimental import pallas as pl
from jax.experimental.pallas import tpu as pltpu
from jax.experimental.pallas import tpu_sc as plsc

D_MODEL = 1024
D_A = 512
D_B = 512
CONV_A = 3
CONV_B = 31
HALO_A = CONV_A - 1
HALO_B = CONV_B - 1
IN_COLS = 3 * D_A + 2 * D_B
N_EXPERT_GROUPS = 4
EXPERTS_PER_GROUP = 8
N_EXPERTS = N_EXPERT_GROUPS * EXPERTS_PER_GROUP
TOP_K = 2
D_EXPERT = D_MODEL // 4
RMS_EPS = 1e-6
LN_EPS = 1e-5

SUBLANES = 8
LANES = 128
PAD_A = SUBLANES
PAD_B = 32
ROUTER_ROWS = SUBLANES + N_EXPERTS
NEG_BIG = -1e30
VMEM_LIMIT = 56 * 1024 * 1024
HALF = D_MODEL // 2
HI_MASK = 0xFFFF0000

SC_CORES = 2
SC_SUBCORES = 16
SC_WORKERS = SC_CORES * SC_SUBCORES
SC_BATCH = 64

EXPERT_TILE = 1280
PLAN_TILE = 1024


def _rms_scale(x):
    return x * lax.rsqrt(jnp.mean(x * x, axis=-1, keepdims=True) + RMS_EPS)


def _dot_nt(a, b):
    return lax.dot_general(a, b, (((1,), (1,)), ((), ())), preferred_element_type=jnp.float32)


def _pack_bf16_halves(x):
    bits = lax.bitcast_convert_type(x.astype(jnp.bfloat16).astype(jnp.float32), jnp.uint32)
    return bits[:, :HALF] | (bits[:, HALF:] >> 16)


def _unpack_bf16_halves(w):
    hi = lax.bitcast_convert_type(w & jnp.uint32(HI_MASK), jnp.float32)
    lo = lax.bitcast_convert_type(w << 16, jnp.float32)
    return hi, lo


def _route(logits_t):
    rows = logits_t.shape[1]
    iota = lax.broadcasted_iota(jnp.int32, (SUBLANES, rows), 0)
    lc = logits_t[0:SUBLANES]
    cmax = jnp.max(lc, axis=0, keepdims=True)
    grp = jnp.min(jnp.where(lc == cmax, iota, SUBLANES), axis=0, keepdims=True)
    p_grp = 1.0 / jnp.sum(jnp.exp(lc - cmax), axis=0, keepdims=True)
    sel = logits_t[SUBLANES:2 * SUBLANES]
    for g in range(1, N_EXPERT_GROUPS):
        sel = jnp.where(grp == g, logits_t[(g + 1) * SUBLANES:(g + 2) * SUBLANES], sel)
    v1 = jnp.max(sel, axis=0, keepdims=True)
    i1 = jnp.min(jnp.where(sel == v1, iota, SUBLANES), axis=0, keepdims=True)
    sel2 = jnp.where(iota == i1, -jnp.inf, sel)
    v2 = jnp.max(sel2, axis=0, keepdims=True)
    i2 = jnp.min(jnp.where(sel2 == v2, iota, SUBLANES), axis=0, keepdims=True)
    e2 = jnp.exp(v2 - v1)
    den = 1.0 + e2
    p1 = p_grp / den
    p2 = p_grp * e2 / den
    base = grp * EXPERTS_PER_GROUP
    return (base + i1, base + i2), (p1, p2)


def _mixer_body(x_ref, sa_ref, sb_ref, gmix_ref, win_ref, caw_ref, cbw_ref, cbb_ref, lng_ref, lnb_ref,
                wout_ref, gffn_ref, wrh_ref, wrl_ref, br_ref,
                x1_ref, h2p_ref, eid_ref, p_ref, na_ref, nb_ref,
                proj_ref, uext_ref, gext_ref, z_ref, ush_ref, gsh_ref, *, nseq, tt, seq_chunk, row_chunk, n_sub):
    carried = sa_ref is None
    t = pl.program_id(1) if carried else None
    assert n_sub == 1 or nseq == 1
    sub = tt // n_sub

    if carried:
        @pl.when(t == 0)
        def _():
            uext_ref[:, 0:PAD_A, :] = jnp.zeros((nseq, PAD_A, D_A), jnp.float32)
            gext_ref[:, 0:PAD_B, :] = jnp.zeros((nseq, PAD_B, D_B), jnp.float32)
    else:
        uext_ref[:, PAD_A - HALO_A:PAD_A, :] = sa_ref[...]
        gext_ref[:, PAD_B - HALO_B:PAD_B, :] = sb_ref[...]

    def window(base_ref, shifted_ref, first_shift, s0, off, n_rows):
        r = off % SUBLANES
        a8 = off - r
        if r == 0:
            return base_ref[s0:s0 + seq_chunk, a8:a8 + n_rows, :]
        return shifted_ref[r - first_shift, s0:s0 + seq_chunk, a8:a8 + n_rows, :]

    n = seq_chunk * row_chunk
    col_chunk = 2 * LANES
    caw = caw_ref[...]
    cbw = cbw_ref[...]
    hs_bf = {}

    def rows_of(si):
        q0 = si * sub
        f0 = q0 if nseq == 1 else 0
        return q0, f0, nseq * sub

    def prep(si):
        q0, _, m = rows_of(si)
        x = x_ref[:, q0:q0 + sub, :].reshape(m, D_MODEL)
        hs_bf[si] = (_rms_scale(x) * gmix_ref[...]).astype(jnp.bfloat16)

    def dot_b_items(si):
        def item(c0):
            def run():
                q0, _, _ = rows_of(si)
                v_b = jnp.dot(hs_bf[si], win_ref[:, 3 * D_A + c0:3 * D_A + c0 + col_chunk],
                              preferred_element_type=jnp.float32)
                g_b = jnp.dot(hs_bf[si], win_ref[:, 3 * D_A + D_B + c0:3 * D_A + D_B + c0 + col_chunk],
                              preferred_element_type=jnp.float32)
                gext_ref[:, PAD_B + q0:PAD_B + q0 + sub, c0:c0 + col_chunk] = (
                    v_b * jax.nn.sigmoid(g_b)).reshape(nseq, sub, col_chunk)
            return run
        return [item(c0) for c0 in range(0, D_B, col_chunk)]

    def dot_a_items(si):
        def item(c0):
            def run():
                _, f0, m = rows_of(si)
                proj_ref[f0:f0 + m, c0:c0 + col_chunk] = jnp.dot(
                    hs_bf[si], win_ref[:, c0:c0 + col_chunk], preferred_element_type=jnp.float32)
            return run
        return [item(c0) for c0 in range(0, 3 * D_A, col_chunk)]

    def shift_b(si):
        q0, _, _ = rows_of(si)
        j_lo = 0 if si == 0 else q0 + PAD_B - SUBLANES
        j_hi = q0 + sub + PAD_B - SUBLANES
        for r in range(1, SUBLANES):
            gsh_ref[r - 1, :, j_lo:j_hi, :] = gext_ref[:, j_lo + r:j_hi + r, :]

    def chunks_of(si):
        q0, _, _ = rows_of(si)
        return [(s0, r0) for s0 in range(0, nseq, seq_chunk) for r0 in range(q0, q0 + sub, row_chunk)]

    def conv_b_items(si):
        def item(s0, r0):
            def run():
                lo = s0 * tt + r0
                acc_b = None
                for k in range(CONV_B):
                    off = PAD_B - HALO_B + k + r0
                    term = window(gext_ref, gsh_ref, 1, s0, off, row_chunk) * cbw[k:k + 1, :]
                    acc_b = term if acc_b is None else acc_b + term
                zb = acc_b.reshape(n, D_B) + cbb_ref[...]
                mu = jnp.mean(zb, axis=-1, keepdims=True)
                zc = zb - mu
                var = jnp.mean(zc * zc, axis=-1, keepdims=True)
                y = zc * lax.rsqrt(var + LN_EPS) * lng_ref[...] + lnb_ref[...]
                z_ref[lo:lo + n, D_A:] = (y * jax.nn.sigmoid(y)).astype(jnp.bfloat16)
            return run
        return [item(s0, r0) for s0, r0 in chunks_of(si)]

    def conv_a_item(si):
        def run():
            q0, f0, m = rows_of(si)
            c_a = proj_ref[f0:f0 + m, D_A:2 * D_A]
            v_a = proj_ref[f0:f0 + m, 2 * D_A:3 * D_A]
            uext_ref[:, PAD_A + q0:PAD_A + q0 + sub, :] = (c_a * v_a).reshape(nseq, sub, D_A)
            for r in range(SUBLANES - HALO_A, SUBLANES):
                ush_ref[r - (SUBLANES - HALO_A), :, q0:q0 + sub, :] = uext_ref[:, q0 + r:q0 + sub + r, :]
            for s0, r0 in chunks_of(si):
                lo = s0 * tt + r0
                acc_a = None
                for k in range(CONV_A):
                    off = PAD_A - HALO_A + k + r0
                    term = window(uext_ref, ush_ref, SUBLANES - HALO_A, s0, off, row_chunk) * caw[k:k + 1, :]
                    acc_a = term if acc_a is None else acc_a + term
                z_a = proj_ref[lo:lo + n, 0:D_A] * acc_a.reshape(n, D_A)
                z_ref[lo:lo + n, 0:D_A] = z_a.astype(jnp.bfloat16)
        return run

    def finish_item(si):
        def run():
            q0, f0, m = rows_of(si)
            x = x_ref[:, q0:q0 + sub, :].reshape(m, D_MODEL)
            x1 = x + jnp.dot(z_ref[f0:f0 + m, :], wout_ref[...], preferred_element_type=jnp.float32)
            x1_ref[f0:f0 + m, :] = x1
            h2 = _rms_scale(x1) * gffn_ref[...]
            h2_hi = h2.astype(jnp.bfloat16)
            h2p_ref[f0:f0 + m, :] = _pack_bf16_halves(h2)
            h2_lo = (h2 - h2_hi.astype(jnp.float32)).astype(jnp.bfloat16)
            wrh = wrh_ref[...]
            logits_t = _dot_nt(wrh, h2_hi) + _dot_nt(wrh, h2_lo) + _dot_nt(wrl_ref[...], h2_hi) + br_ref[...]
            (e1, e2), (p1, p2) = _route(logits_t)
            iota = lax.broadcasted_iota(jnp.int32, (SUBLANES, m), 0)
            eid_ref[:, f0:f0 + m] = jnp.where(iota == 0, e1, jnp.where(iota == 1, e2, 0))
            p_ref[:, f0:f0 + m] = jnp.where(iota == 0, p1, jnp.where(iota == 1, p2, 0.0))
        return run

    def interleave(main, fill):
        for i, item in enumerate(main):
            item()
            for f in fill[i * len(fill) // len(main):(i + 1) * len(fill) // len(main)]:
                f()

    for si in range(n_sub):
        prep(si)
    for item in dot_b_items(0):
        item()
    shift_b(0)
    carry_over = []
    for si in range(n_sub):
        fill = carry_over + dot_a_items(si)
        if si + 1 < n_sub:
            fill = fill + dot_b_items(si + 1)
        interleave(conv_b_items(si), fill + [conv_a_item(si)])
        if si + 1 < n_sub:
            shift_b(si + 1)
        carry_over = [finish_item(si)]
    for item in carry_over:
        item()

    if carried:
        @pl.when(t == pl.num_programs(1) - 1)
        def _():
            na_ref[...] = uext_ref[:, PAD_A + tt - HALO_A:PAD_A + tt, :]
            nb_ref[...] = gext_ref[:, PAD_B + tt - HALO_B:PAD_B + tt, :]
        uext_ref[:, 0:PAD_A, :] = uext_ref[:, tt:tt + PAD_A, :]
        gext_ref[:, 0:PAD_B, :] = gext_ref[:, tt:tt + PAD_B, :]
    else:
        na_ref[...] = uext_ref[:, PAD_A + tt - HALO_A:PAD_A + tt, :]
        nb_ref[...] = gext_ref[:, PAD_B + tt - HALO_B:PAD_B + tt, :]


def _mixer_prompt_kernel(x_ref, *refs, **kw):
    _mixer_body(x_ref, None, None, *refs, **kw)


def _mixer_sample_kernel(x_ref, sa_ref, sb_ref, *refs, **kw):
    n_params = 12
    params = refs[:n_params]
    rest = refs[n_params + 4:]
    _mixer_body(x_ref, sa_ref, sb_ref, *params, *rest, **kw)


def _full(shape):
    return pl.BlockSpec(shape, lambda *_: (0,) * len(shape))


def _mixer_param_specs():
    return [
        _full((1, D_MODEL)),
        _full((D_MODEL, IN_COLS)),
        _full((CONV_A, D_A)),
        _full((CONV_B, D_B)),
        _full((1, D_B)),
        _full((1, D_B)),
        _full((1, D_B)),
        _full((D_MODEL, D_MODEL)),
        _full((1, D_MODEL)),
        _full((ROUTER_ROWS, D_MODEL)),
        _full((ROUTER_ROWS, D_MODEL)),
        _full((ROUTER_ROWS, 1)),
    ]


def _mixer_scratch(nseq, tt):
    rows = nseq * tt
    return [
        pltpu.VMEM((rows, 3 * D_A), jnp.float32),
        pltpu.VMEM((nseq, PAD_A + tt, D_A), jnp.float32),
        pltpu.VMEM((nseq, PAD_B + tt, D_B), jnp.float32),
        pltpu.VMEM((rows, D_MODEL), jnp.bfloat16),
        pltpu.VMEM((HALO_A, nseq, PAD_A + tt - SUBLANES, D_A), jnp.float32),
        pltpu.VMEM((SUBLANES - 1, nseq, PAD_B + tt - SUBLANES, D_B), jnp.float32),
    ]


def _mixer_prompt(x, params, n_tokens_total, tt):
    batch, seq, _ = x.shape
    n_t = seq // tt
    tok = lambda b, t: (b * n_t + t, 0)
    lane_tok = lambda b, t: (0, b * n_t + t)
    out_shape = [
        jax.ShapeDtypeStruct((n_tokens_total, D_MODEL), jnp.float32),
        jax.ShapeDtypeStruct((n_tokens_total, HALF), jnp.uint32),
        jax.ShapeDtypeStruct((SUBLANES, n_tokens_total), jnp.int32),
        jax.ShapeDtypeStruct((SUBLANES, n_tokens_total), jnp.float32),
        jax.ShapeDtypeStruct((batch, HALO_A, D_A), jnp.float32),
        jax.ShapeDtypeStruct((batch, HALO_B, D_B), jnp.float32),
    ]
    out_specs = [
        pl.BlockSpec((tt, D_MODEL), tok),
        pl.BlockSpec((tt, HALF), tok),
        pl.BlockSpec((SUBLANES, tt), lane_tok),
        pl.BlockSpec((SUBLANES, tt), lane_tok),
        pl.BlockSpec((1, HALO_A, D_A), lambda b, t: (b, 0, 0)),
        pl.BlockSpec((1, HALO_B, D_B), lambda b, t: (b, 0, 0)),
    ]
    return pl.pallas_call(
        functools.partial(_mixer_prompt_kernel, nseq=1, tt=tt, seq_chunk=1, row_chunk=64, n_sub=2),
        grid=(batch, n_t),
        in_specs=[pl.BlockSpec((1, tt, D_MODEL), lambda b, t: (b, t, 0))] + _mixer_param_specs(),
        out_specs=out_specs,
        out_shape=out_shape,
        scratch_shapes=_mixer_scratch(1, tt),
        compiler_params=pltpu.CompilerParams(
            dimension_semantics=("arbitrary", "arbitrary"), vmem_limit_bytes=VMEM_LIMIT),
        name="mixer_prompt",
    )(x, *params)


def _mixer_sample(x, state_a, state_b, params, bufs, row_offset, nseq):
    batch, tt, _ = x.shape
    rows = nseq * tt
    first = row_offset // rows
    tok = lambda i: (first + i, 0)
    lane_tok = lambda i: (0, first + i)
    x1, h2p, eid, p = bufs
    out_shape = [
        jax.ShapeDtypeStruct(x1.shape, x1.dtype),
        jax.ShapeDtypeStruct(h2p.shape, h2p.dtype),
        jax.ShapeDtypeStruct(eid.shape, eid.dtype),
        jax.ShapeDtypeStruct(p.shape, p.dtype),
        jax.ShapeDtypeStruct((batch, HALO_A, D_A), jnp.float32),
        jax.ShapeDtypeStruct((batch, HALO_B, D_B), jnp.float32),
    ]
    out_specs = [
        pl.BlockSpec((rows, D_MODEL), tok),
        pl.BlockSpec((rows, HALF), tok),
        pl.BlockSpec((SUBLANES, rows), lane_tok),
        pl.BlockSpec((SUBLANES, rows), lane_tok),
        pl.BlockSpec((nseq, HALO_A, D_A), lambda i: (i, 0, 0)),
        pl.BlockSpec((nseq, HALO_B, D_B), lambda i: (i, 0, 0)),
    ]
    any_spec = pl.BlockSpec(memory_space=pl.ANY)
    in_specs = ([pl.BlockSpec((nseq, tt, D_MODEL), lambda i: (i, 0, 0)),
                 pl.BlockSpec((nseq, HALO_A, D_A), lambda i: (i, 0, 0)),
                 pl.BlockSpec((nseq, HALO_B, D_B), lambda i: (i, 0, 0))]
                + _mixer_param_specs() + [any_spec] * 4)
    n_in = len(in_specs)
    return pl.pallas_call(
        functools.partial(_mixer_sample_kernel, nseq=nseq, tt=tt, seq_chunk=8, row_chunk=tt, n_sub=1),
        grid=(batch // nseq,),
        in_specs=in_specs,
        out_specs=out_specs,
        out_shape=out_shape,
        scratch_shapes=_mixer_scratch(nseq, tt),
        input_output_aliases={n_in - 4: 0, n_in - 3: 1, n_in - 2: 2, n_in - 1: 3},
        compiler_params=pltpu.CompilerParams(
            dimension_semantics=("arbitrary",), vmem_limit_bytes=VMEM_LIMIT),
        name="mixer_sample",
    )(x, state_a, state_b, *params, x1, h2p, eid, p)


def _plan_kernel(eid_ref, pos_ref, seg_end_ref, carry_ref, start_ref, earlier_ref):
    ph = pl.program_id(0)
    i = pl.program_id(1)
    tt = eid_ref.shape[1]

    @pl.when(i == 0)
    def _():
        @pl.when(ph == 1)
        def _():
            tiles = jnp.floor((carry_ref[...] + (EXPERT_TILE - 0.5)) * (1.0 / EXPERT_TILE))
            below = (lax.broadcasted_iota(jnp.int32, (N_EXPERTS, N_EXPERTS), 0)
                     > lax.broadcasted_iota(jnp.int32, (N_EXPERTS, N_EXPERTS), 1))
            start = jnp.dot(jnp.where(below, 1.0, 0.0).astype(jnp.bfloat16), tiles.astype(jnp.bfloat16),
                            preferred_element_type=jnp.float32) * EXPERT_TILE
            start_ref[...] = start
            seg_end_ref[...] = (start + tiles * EXPERT_TILE).astype(jnp.int32)
            earlier = (lax.broadcasted_iota(jnp.int32, (tt, tt), 0)
                       < lax.broadcasted_iota(jnp.int32, (tt, tt), 1))
            earlier_ref[...] = jnp.where(earlier, 1.0, 0.0).astype(jnp.bfloat16)
        carry_ref[...] = jnp.zeros_like(carry_ref)

    eid = eid_ref[...]
    experts = lax.broadcasted_iota(jnp.int32, (N_EXPERTS, tt), 0)
    oh0 = experts == eid[0:1]
    oh1 = experts == eid[1:2]
    oh = jnp.where(oh0 | oh1, 1.0, 0.0)

    @pl.when(ph == 1)
    def _():
        within = jnp.dot(oh.astype(jnp.bfloat16), earlier_ref[...], preferred_element_type=jnp.float32)
        slot_of = within + carry_ref[:, 0:1] + start_ref[:, 0:1]
        s0 = jnp.sum(jnp.where(oh0, slot_of, 0.0), axis=0, keepdims=True)
        s1 = jnp.sum(jnp.where(oh1, slot_of, 0.0), axis=0, keepdims=True)
        k = lax.broadcasted_iota(jnp.int32, (SUBLANES, tt), 0)
        pos_ref[...] = jnp.where(k == 0, s0, jnp.where(k == 1, s1, 0.0)).astype(jnp.int32)

    carry_ref[...] = carry_ref[...] + jnp.sum(oh, axis=1, keepdims=True)


def _plan(eid):
    n_tokens = eid.shape[1]
    return pl.pallas_call(
        _plan_kernel,
        grid=(2, n_tokens // PLAN_TILE),
        in_specs=[pl.BlockSpec((SUBLANES, PLAN_TILE), lambda ph, i: (0, i))],
        out_specs=[pl.BlockSpec((SUBLANES, PLAN_TILE), lambda ph, i: (0, i * ph)),
                   pl.BlockSpec((N_EXPERTS, LANES), lambda ph, i: (0, 0))],
        out_shape=[jax.ShapeDtypeStruct((SUBLANES, n_tokens), jnp.int32),
                   jax.ShapeDtypeStruct((N_EXPERTS, LANES), jnp.int32)],
        scratch_shapes=[pltpu.VMEM((N_EXPERTS, LANES), jnp.float32),
                        pltpu.VMEM((N_EXPERTS, LANES), jnp.float32),
                        pltpu.VMEM((PLAN_TILE, PLAN_TILE), jnp.bfloat16)],
        compiler_params=pltpu.CompilerParams(dimension_semantics=("arbitrary", "arbitrary")),
        name="route_plan",
    )(eid)


def _sc_move_rows(table, idx, n_out, *, scatter):
    n_workers, n_batches, batch = idx.shape
    n_rows, words = table.shape
    assert n_rows % batch == 0 and n_batches >= 2
    mesh = plsc.VectorSubcoreMesh(core_axis_name="c", subcore_axis_name="s")

    @functools.partial(
        pl.kernel, mesh=mesh,
        out_type=jax.ShapeDtypeStruct((n_out, words), table.dtype),
        scratch_types=[pltpu.VMEM((n_batches, batch), jnp.int32),
                       pltpu.VMEM((2, batch, words), table.dtype),
                       pltpu.SemaphoreType.DMA((2,)),
                       pltpu.SemaphoreType.DMA((2,))],
    )
    def move(table_hbm, idx_hbm, out_hbm, idx_v, rows_v, sem_in, sem_out):
        worker = lax.axis_index("s") * SC_CORES + lax.axis_index("c")
        pltpu.sync_copy(idx_hbm.at[worker], idx_v)

        def in_order(ref, b, wrap):
            a0 = (worker * n_batches + b) * batch
            if wrap:
                a0 = jnp.where(a0 >= n_rows, a0 - n_rows, a0)
            return ref.at[pl.ds(a0, batch)]

        def read(b):
            src = in_order(table_hbm, b, True) if scatter else table_hbm.at[idx_v.at[b]]
            return pltpu.async_copy(src, rows_v.at[b % 2], sem_in.at[b % 2])

        def write(b):
            dst = out_hbm.at[idx_v.at[b]] if scatter else in_order(out_hbm, b, False)
            return pltpu.async_copy(rows_v.at[b % 2], dst, sem_out.at[b % 2])

        reads = {0: read(0)}
        writes = {}
        for b in range(n_batches):
            reads[b].wait()
            if b + 1 < n_batches:
                if b >= 1:
                    writes[b - 1].wait()
                reads[b + 1] = read(b + 1)
            writes[b] = write(b)
        writes[n_batches - 2].wait()
        writes[n_batches - 1].wait()

    return move(table, idx)


def _experts_kernel(te_ref, nv_ref, xs_ref, wg_ref, wu_ref, wd_ref, ys_ref):
    i = pl.program_id(0)

    @pl.when(i < nv_ref[0])
    def _():
        hi, lo = _unpack_bf16_halves(xs_ref[...])
        hi = hi.astype(jnp.bfloat16)
        lo = lo.astype(jnp.bfloat16)
        wg = wg_ref[0].astype(jnp.bfloat16)
        wu = wu_ref[0].astype(jnp.bfloat16)
        wd = wd_ref[0].astype(jnp.bfloat16)
        gate = (jnp.dot(hi, wg[:HALF], preferred_element_type=jnp.float32)
                + jnp.dot(lo, wg[HALF:], preferred_element_type=jnp.float32))
        up = (jnp.dot(hi, wu[:HALF], preferred_element_type=jnp.float32)
              + jnp.dot(lo, wu[HALF:], preferred_element_type=jnp.float32))
        hid = (gate * jax.nn.sigmoid(gate) * up).astype(jnp.bfloat16)
        ys_ref[...] = _pack_bf16_halves(jnp.dot(hid, wd, preferred_element_type=jnp.float32))


def _experts(xs, tile_expert, n_valid, wg, wu, wd):
    n_slots = xs.shape[0]
    tm = EXPERT_TILE
    row_block = lambda i, te, nv: (jnp.minimum(i, nv[0] - 1), 0)
    w_block = lambda i, te, nv: (te[i], 0, 0)
    return pl.pallas_call(
        _experts_kernel,
        grid_spec=pltpu.PrefetchScalarGridSpec(
            num_scalar_prefetch=2,
            grid=(n_slots // tm,),
            in_specs=[pl.BlockSpec((tm, HALF), row_block),
                      pl.BlockSpec((1, D_MODEL, D_EXPERT), w_block),
                      pl.BlockSpec((1, D_MODEL, D_EXPERT), w_block),
                      pl.BlockSpec((1, D_EXPERT, D_MODEL), w_block)],
            out_specs=pl.BlockSpec((tm, HALF), row_block),
        ),
        out_shape=jax.ShapeDtypeStruct((n_slots, HALF), jnp.uint32),
        compiler_params=pltpu.CompilerParams(
            dimension_semantics=("arbitrary",), vmem_limit_bytes=VMEM_LIMIT),
        name="experts",
    )(tile_expert, n_valid, xs, wg, wu, wd)


def _final_kernel(x1_ref, y0_ref, y1_ref, p_ref, gfin_ref, out_ref):
    pt = jnp.transpose(p_ref[...])
    p0 = pt[:, 0:1]
    p1 = pt[:, 1:2]
    a_hi, a_lo = _unpack_bf16_halves(y0_ref[...])
    b_hi, b_lo = _unpack_bf16_halves(y1_ref[...])
    x1 = x1_ref[...]
    x2_hi = x1[:, :HALF] + (p0 * a_hi + p1 * b_hi)
    x2_lo = x1[:, HALF:] + (p0 * a_lo + p1 * b_lo)
    ms = (jnp.sum(x2_hi * x2_hi, axis=-1, keepdims=True)
          + jnp.sum(x2_lo * x2_lo, axis=-1, keepdims=True)) / D_MODEL
    scale = lax.rsqrt(ms + RMS_EPS)
    g = gfin_ref[...]
    out_ref[:, :HALF] = x2_hi * scale * g[:, :HALF]
    out_ref[:, HALF:] = x2_lo * scale * g[:, HALF:]


def _final(x1, yk, p, gfin, row_offset, n_rows, n_tokens, tm):
    first = row_offset // tm
    second = (n_tokens + row_offset) // tm
    return pl.pallas_call(
        _final_kernel,
        grid=(n_rows // tm,),
        in_specs=[pl.BlockSpec((tm, D_MODEL), lambda i: (first + i, 0)),
                  pl.BlockSpec((tm, HALF), lambda i: (first + i, 0)),
                  pl.BlockSpec((tm, HALF), lambda i: (second + i, 0)),
                  pl.BlockSpec((SUBLANES, tm), lambda i: (0, first + i)),
                  pl.BlockSpec((1, D_MODEL), lambda i: (0, 0))],
        out_specs=pl.BlockSpec((tm, D_MODEL), lambda i: (i, 0)),
        out_shape=jax.ShapeDtypeStruct((n_rows, D_MODEL), jnp.float32),
        compiler_params=pltpu.CompilerParams(
            dimension_semantics=("arbitrary",), vmem_limit_bytes=VMEM_LIMIT),
        name="final",
    )(x1, yk, yk, p, gfin)


def _slot_capacity(n_tokens):
    worst = TOP_K * n_tokens + N_EXPERTS * (EXPERT_TILE - 1)
    return -(-worst // EXPERT_TILE) * EXPERT_TILE


def kernel(x_prompt, x_sample, state_conv_a, state_conv_b, g_mix, w_in, conv_a_w, conv_b_w, conv_b_bias,
           ln_g, ln_b, w_out, g_ffn, w_coarse, b_coarse, w_fine, b_fine, w_gate, w_up, w_down, g_final):
    assert g_mix.shape[0] == 1, "single trunk layer"
    batch, seq, _ = x_prompt.shape
    dec_batch, dec_seq, _ = x_sample.shape
    n_prompt = batch * seq
    n_sample = dec_batch * dec_seq
    n_tokens = n_prompt + n_sample
    assert (TOP_K * n_tokens) % (SC_WORKERS * SC_BATCH) == 0
    bf16 = jnp.bfloat16

    wr = jnp.concatenate([
        w_coarse[0], jnp.zeros((D_MODEL, SUBLANES - N_EXPERT_GROUPS), jnp.float32),
        jnp.transpose(w_fine[0], (1, 0, 2)).reshape(D_MODEL, N_EXPERTS)], axis=1).T
    wr_hi = wr.astype(bf16)
    wr_lo = (wr - wr_hi.astype(jnp.float32)).astype(bf16)
    br = jnp.concatenate([
        b_coarse[0], jnp.full((SUBLANES - N_EXPERT_GROUPS,), NEG_BIG, jnp.float32),
        b_fine[0].reshape(N_EXPERTS)]).reshape(ROUTER_ROWS, 1)

    params = (g_mix, w_in[0].astype(bf16), conv_a_w[0], conv_b_w[0], conv_b_bias, ln_g, ln_b,
              w_out[0].astype(bf16), g_ffn, wr_hi, wr_lo, br)

    x1, h2p, eid, p, na_p, nb_p = _mixer_prompt(x_prompt, params, n_tokens, tt=512)
    x1, h2p, eid, p, na_s, nb_s = _mixer_sample(
        x_sample, state_conv_a[0], state_conv_b[0], params, (x1, h2p, eid, p), n_prompt, nseq=32)

    pos, seg_end = _plan(eid)
    pos = pos[:TOP_K].reshape(SC_WORKERS, -1, SC_BATCH)
    seg_end = seg_end[:, 0]
    n_slots = _slot_capacity(n_tokens)
    n_tiles = n_slots // EXPERT_TILE
    n_valid = seg_end[N_EXPERTS - 1:] // EXPERT_TILE
    tile_row = jnp.arange(n_tiles, dtype=jnp.int32) * EXPERT_TILE
    tile_expert = jnp.minimum(jnp.sum(seg_end[None, :] <= tile_row[:, None], axis=1), N_EXPERTS - 1)

    xs = _sc_move_rows(h2p, pos, n_slots, scatter=True)
    ys = _experts(xs, tile_expert.astype(jnp.int32), n_valid, w_gate[0], w_up[0], w_down[0])
    yk = _sc_move_rows(ys, pos, TOP_K * n_tokens, scatter=False)

    gfin = g_final.reshape(1, D_MODEL)
    y_p = _final(x1, yk, p, gfin, 0, n_prompt, n_tokens, tm=512)
    y_s = _final(x1, yk, p, gfin, n_prompt, n_sample, n_tokens, tm=512)
    return (y_p.reshape(batch, seq, D_MODEL), y_s.reshape(dec_batch, dec_seq, D_MODEL),
            na_p[None], nb_p[None], na_s[None], nb_s[None])
```

```python
import functools

import jax
import jax.numpy as jnp
from jax import lax
from jax.experimental import pallas as pl
from jax.experimental.pallas import tpu as pltpu
from jax.experimental.pallas import tpu_sc as plsc

D_MODEL = 1024
D_A = 512
D_B = 512
CONV_A = 3
CONV_B = 31
HALO_A = CONV_A - 1
HALO_B = CONV_B - 1
IN_COLS = 3 * D_A + 2 * D_B
N_EXPERT_GROUPS = 4
EXPERTS_PER_GROUP = 8
N_EXPERTS = N_EXPERT_GROUPS * EXPERTS_PER_GROUP
TOP_K = 2
D_EXPERT = D_MODEL // 4
RMS_EPS = 1e-6
LN_EPS = 1e-5

SUBLANES = 8
LANES = 128
PAD_A = SUBLANES
PAD_B = 32
ROUTER_ROWS = SUBLANES + N_EXPERTS
NEG_BIG = -1e30
VMEM_LIMIT = 56 * 1024 * 1024
HALF = D_MODEL // 2
HI_MASK = 0xFFFF0000

SC_CORES = 2
SC_SUBCORES = 16
SC_WORKERS = SC_CORES * SC_SUBCORES
SC_BATCH = 64

PLAN_TILE = 1024


def _rms_scale(x):
    return x * lax.rsqrt(jnp.mean(x * x, axis=-1, keepdims=True) + RMS_EPS)


def _dot_nt(a, b):
    return lax.dot_general(a, b, (((1,), (1,)), ((), ())), preferred_element_type=jnp.float32)


def _pack_bf16_halves(x):
    bits = lax.bitcast_convert_type(x.astype(jnp.bfloat16).astype(jnp.float32), jnp.uint32)
    return bits[:, :HALF] | (bits[:, HALF:] >> 16)


def _unpack_bf16_halves(w):
    hi = lax.bitcast_convert_type(w & jnp.uint32(HI_MASK), jnp.float32)
    lo = lax.bitcast_convert_type(w << 16, jnp.float32)
    return hi, lo


def _route(logits_t):
    rows = logits_t.shape[1]
    iota = lax.broadcasted_iota(jnp.int32, (SUBLANES, rows), 0)
    lc = logits_t[0:SUBLANES]
    cmax = jnp.max(lc, axis=0, keepdims=True)
    grp = jnp.min(jnp.where(lc == cmax, iota, SUBLANES), axis=0, keepdims=True)
    p_grp = 1.0 / jnp.sum(jnp.exp(lc - cmax), axis=0, keepdims=True)
    sel = logits_t[SUBLANES:2 * SUBLANES]
    for g in range(1, N_EXPERT_GROUPS):
        sel = jnp.where(grp == g, logits_t[(g + 1) * SUBLANES:(g + 2) * SUBLANES], sel)
    v1 = jnp.max(sel, axis=0, keepdims=True)
    i1 = jnp.min(jnp.where(sel == v1, iota, SUBLANES), axis=0, keepdims=True)
    sel2 = jnp.where(iota == i1, -jnp.inf, sel)
    v2 = jnp.max(sel2, axis=0, keepdims=True)
    i2 = jnp.min(jnp.where(sel2 == v2, iota, SUBLANES), axis=0, keepdims=True)
    e2 = jnp.exp(v2 - v1)
    den = 1.0 + e2
    p1 = p_grp / den
    p2 = p_grp * e2 / den
    base = grp * EXPERTS_PER_GROUP
    return (base + i1, base + i2), (p1, p2)


def _mixer_body(x_ref, sa_ref, sb_ref, gmix_ref, win_ref, caw_ref, cbw_ref, cbb_ref, lng_ref, lnb_ref,
                wout_ref, gffn_ref, wrh_ref, wrl_ref, br_ref,
                x1_ref, h2p_ref, eid_ref, p_ref, na_ref, nb_ref,
                proj_ref, uext_ref, gext_ref, z_ref, ush_ref, gsh_ref, *, nseq, tt, seq_chunk, row_chunk, n_sub):
    carried = sa_ref is None
    t = pl.program_id(1) if carried else None
    assert n_sub == 1 or nseq == 1
    sub = tt // n_sub

    if carried:
        @pl.when(t == 0)
        def _():
            uext_ref[:, 0:PAD_A, :] = jnp.zeros((nseq, PAD_A, D_A), jnp.float32)
            gext_ref[:, 0:PAD_B, :] = jnp.zeros((nseq, PAD_B, D_B), jnp.float32)
    else:
        uext_ref[:, PAD_A - HALO_A:PAD_A, :] = sa_ref[...]
        gext_ref[:, PAD_B - HALO_B:PAD_B, :] = sb_ref[...]

    def window(base_ref, shifted_ref, first_shift, s0, off, n_rows):
        r = off % SUBLANES
        a8 = off - r
        if r == 0:
            return base_ref[s0:s0 + seq_chunk, a8:a8 + n_rows, :]
        return shifted_ref[r - first_shift, s0:s0 + seq_chunk, a8:a8 + n_rows, :]

    n = seq_chunk * row_chunk
    col_chunk = 2 * LANES
    caw = caw_ref[...]
    cbw = cbw_ref[...]
    hs_bf = {}

    def rows_of(si):
        q0 = si * sub
        f0 = q0 if nseq == 1 else 0
        return q0, f0, nseq * sub

    def prep(si):
        q0, _, m = rows_of(si)
        x = x_ref[:, q0:q0 + sub, :].reshape(m, D_MODEL)
        hs_bf[si] = (_rms_scale(x) * gmix_ref[...]).astype(jnp.bfloat16)

    def dot_b_items(si):
        def item(c0):
            def run():
                q0, _, _ = rows_of(si)
                v_b = jnp.dot(hs_bf[si], win_ref[:, 3 * D_A + c0:3 * D_A + c0 + col_chunk],
                              preferred_element_type=jnp.float32)
                g_b = jnp.dot(hs_bf[si], win_ref[:, 3 * D_A + D_B + c0:3 * D_A + D_B + c0 + col_chunk],
                              preferred_element_type=jnp.float32)
                gext_ref[:, PAD_B + q0:PAD_B + q0 + sub, c0:c0 + col_chunk] = (
                    v_b * jax.nn.sigmoid(g_b)).reshape(nseq, sub, col_chunk)
            return run
        return [item(c0) for c0 in range(0, D_B, col_chunk)]

    def dot_a_items(si):
        def item(c0):
            def run():
                _, f0, m = rows_of(si)
                proj_ref[f0:f0 + m, c0:c0 + col_chunk] = jnp.dot(
                    hs_bf[si], win_ref[:, c0:c0 + col_chunk], preferred_element_type=jnp.float32)
            return run
        return [item(c0) for c0 in range(0, 3 * D_A, col_chunk)]

    def shift_b(si):
        q0, _, _ = rows_of(si)
        j_lo = 0 if si == 0 else q0 + PAD_B - SUBLANES
        j_hi = q0 + sub + PAD_B - SUBLANES
        for r in range(1, SUBLANES):
            gsh_ref[r - 1, :, j_lo:j_hi, :] = gext_ref[:, j_lo + r:j_hi + r, :]

    def chunks_of(si):
        q0, _, _ = rows_of(si)
        return [(s0, r0) for s0 in range(0, nseq, seq_chunk) for r0 in range(q0, q0 + sub, row_chunk)]

    def conv_b_items(si):
        def item(s0, r0):
            def run():
                lo = s0 * tt + r0
                acc_b = None
                for k in range(CONV_B):
                    off = PAD_B - HALO_B + k + r0
                    term = window(gext_ref, gsh_ref, 1, s0, off, row_chunk) * cbw[k:k + 1, :]
                    acc_b = term if acc_b is None else acc_b + term
                zb = acc_b.reshape(n, D_B) + cbb_ref[...]
                mu = jnp.mean(zb, axis=-1, keepdims=True)
                zc = zb - mu
                var = jnp.mean(zc * zc, axis=-1, keepdims=True)
                y = zc * lax.rsqrt(var + LN_EPS) * lng_ref[...] + lnb_ref[...]
                z_ref[lo:lo + n, D_A:] = (y * jax.nn.sigmoid(y)).astype(jnp.bfloat16)
            return run
        return [item(s0, r0) for s0, r0 in chunks_of(si)]

    def conv_a_item(si):
        def run():
            q0, f0, m = rows_of(si)
            c_a = proj_ref[f0:f0 + m, D_A:2 * D_A]
            v_a = proj_ref[f0:f0 + m, 2 * D_A:3 * D_A]
            uext_ref[:, PAD_A + q0:PAD_A + q0 + sub, :] = (c_a * v_a).reshape(nseq, sub, D_A)
            for r in range(SUBLANES - HALO_A, SUBLANES):
                ush_ref[r - (SUBLANES - HALO_A), :, q0:q0 + sub, :] = uext_ref[:, q0 + r:q0 + sub + r, :]
            for s0, r0 in chunks_of(si):
                lo = s0 * tt + r0
                acc_a = None
                for k in range(CONV_A):
                    off = PAD_A - HALO_A + k + r0
                    term = window(uext_ref, ush_ref, SUBLANES - HALO_A, s0, off, row_chunk) * caw[k:k + 1, :]
                    acc_a = term if acc_a is None else acc_a + term
                z_a = proj_ref[lo:lo + n, 0:D_A] * acc_a.reshape(n, D_A)
                z_ref[lo:lo + n, 0:D_A] = z_a.astype(jnp.bfloat16)
        return run

    def finish_item(si):
        def run():
            q0, f0, m = rows_of(si)
            x = x_ref[:, q0:q0 + sub, :].reshape(m, D_MODEL)
            x1 = x + jnp.dot(z_ref[f0:f0 + m, :], wout_ref[...], preferred_element_type=jnp.float32)
            x1_ref[f0:f0 + m, :] = x1
            h2 = _rms_scale(x1) * gffn_ref[...]
            h2_hi = h2.astype(jnp.bfloat16)
            h2p_ref[f0:f0 + m, :] = _pack_bf16_halves(h2)
            h2_lo = (h2 - h2_hi.astype(jnp.float32)).astype(jnp.bfloat16)
            wrh = wrh_ref[...]
            logits_t = _dot_nt(wrh, h2_hi) + _dot_nt(wrh, h2_lo) + _dot_nt(wrl_ref[...], h2_hi) + br_ref[...]
            (e1, e2), (p1, p2) = _route(logits_t)
            iota = lax.broadcasted_iota(jnp.int32, (SUBLANES, m), 0)
            eid_ref[:, f0:f0 + m] = jnp.where(iota == 0, e1, jnp.where(iota == 1, e2, 0))
            p_ref[:, f0:f0 + m] = jnp.where(iota == 0, p1, jnp.where(iota == 1, p2, 0.0))
        return run

    def interleave(main, fill):
        for i, item in enumerate(main):
            item()
            for f in fill[i * len(fill) // len(main):(i + 1) * len(fill) // len(main)]:
                f()

    for si in range(n_sub):
        prep(si)
    for item in dot_b_items(0):
        item()
    shift_b(0)
    carry_over = []
    for si in range(n_sub):
        fill = carry_over + dot_a_items(si)
        if si + 1 < n_sub:
            fill = fill + dot_b_items(si + 1)
        interleave(conv_b_items(si), fill + [conv_a_item(si)])
        if si + 1 < n_sub:
            shift_b(si + 1)
        carry_over = [finish_item(si)]
    for item in carry_over:
        item()

    if carried:
        @pl.when(t == pl.num_programs(1) - 1)
        def _():
            na_ref[...] = uext_ref[:, PAD_A + tt - HALO_A:PAD_A + tt, :]
            nb_ref[...] = gext_ref[:, PAD_B + tt - HALO_B:PAD_B + tt, :]
        uext_ref[:, 0:PAD_A, :] = uext_ref[:, tt:tt + PAD_A, :]
        gext_ref[:, 0:PAD_B, :] = gext_ref[:, tt:tt + PAD_B, :]
    else:
        na_ref[...] = uext_ref[:, PAD_A + tt - HALO_A:PAD_A + tt, :]
        nb_ref[...] = gext_ref[:, PAD_B + tt - HALO_B:PAD_B + tt, :]


def _mixer_prompt_kernel(x_ref, *refs, **kw):
    _mixer_body(x_ref, None, None, *refs, **kw)


def _mixer_sample_kernel(x_ref, sa_ref, sb_ref, *refs, **kw):
    n_params = 12
    params = refs[:n_params]
    rest = refs[n_params + 4:]
    _mixer_body(x_ref, sa_ref, sb_ref, *params, *rest, **kw)


def _full(shape):
    return pl.BlockSpec(shape, lambda *_: (0,) * len(shape))


def _mixer_param_specs():
    return [
        _full((1, D_MODEL)),
        _full((D_MODEL, IN_COLS)),
        _full((CONV_A, D_A)),
        _full((CONV_B, D_B)),
        _full((1, D_B)),
        _full((1, D_B)),
        _full((1, D_B)),
        _full((D_MODEL, D_MODEL)),
        _full((1, D_MODEL)),
        _full((ROUTER_ROWS, D_MODEL)),
        _full((ROUTER_ROWS, D_MODEL)),
        _full((ROUTER_ROWS, 1)),
    ]


def _mixer_scratch(nseq, tt):
    rows = nseq * tt
    return [
        pltpu.VMEM((rows, 3 * D_A), jnp.float32),
        pltpu.VMEM((nseq, PAD_A + tt, D_A), jnp.float32),
        pltpu.VMEM((nseq, PAD_B + tt, D_B), jnp.float32),
        pltpu.VMEM((rows, D_MODEL), jnp.bfloat16),
        pltpu.VMEM((HALO_A, nseq, PAD_A + tt - SUBLANES, D_A), jnp.float32),
        pltpu.VMEM((SUBLANES - 1, nseq, PAD_B + tt - SUBLANES, D_B), jnp.float32),
    ]


def _mixer_prompt(x, params, n_tokens_total, tt, seq_first, batch):
    seq = x.shape[1]
    n_t = seq // tt
    tok = lambda b, t: (b * n_t + t, 0)
    lane_tok = lambda b, t: (0, b * n_t + t)
    out_shape = [
        jax.ShapeDtypeStruct((n_tokens_total, D_MODEL), jnp.float32),
        jax.ShapeDtypeStruct((n_tokens_total, HALF), jnp.uint32),
        jax.ShapeDtypeStruct((SUBLANES, n_tokens_total), jnp.int32),
        jax.ShapeDtypeStruct((SUBLANES, n_tokens_total), jnp.float32),
        jax.ShapeDtypeStruct((batch, HALO_A, D_A), jnp.float32),
        jax.ShapeDtypeStruct((batch, HALO_B, D_B), jnp.float32),
    ]
    out_specs = [
        pl.BlockSpec((tt, D_MODEL), tok),
        pl.BlockSpec((tt, HALF), tok),
        pl.BlockSpec((SUBLANES, tt), lane_tok),
        pl.BlockSpec((SUBLANES, tt), lane_tok),
        pl.BlockSpec((1, HALO_A, D_A), lambda b, t: (b, 0, 0)),
        pl.BlockSpec((1, HALO_B, D_B), lambda b, t: (b, 0, 0)),
    ]
    return pl.pallas_call(
        functools.partial(_mixer_prompt_kernel, nseq=1, tt=tt, seq_chunk=1, row_chunk=64, n_sub=2),
        grid=(batch, n_t),
        in_specs=[pl.BlockSpec((1, tt, D_MODEL), lambda b, t: (b + seq_first, t, 0))] + _mixer_param_specs(),
        out_specs=out_specs,
        out_shape=out_shape,
        scratch_shapes=_mixer_scratch(1, tt),
        compiler_params=pltpu.CompilerParams(
            dimension_semantics=("arbitrary", "arbitrary"), vmem_limit_bytes=VMEM_LIMIT),
        name="mixer_prompt",
    )(x, *params)


def _mixer_sample(x, state_a, state_b, params, bufs, row_offset, nseq):
    batch, tt, _ = x.shape
    rows = nseq * tt
    first = row_offset // rows
    tok = lambda i: (first + i, 0)
    lane_tok = lambda i: (0, first + i)
    x1, h2p, eid, p = bufs
    out_shape = [
        jax.ShapeDtypeStruct(x1.shape, x1.dtype),
        jax.ShapeDtypeStruct(h2p.shape, h2p.dtype),
        jax.ShapeDtypeStruct(eid.shape, eid.dtype),
        jax.ShapeDtypeStruct(p.shape, p.dtype),
        jax.ShapeDtypeStruct((batch, HALO_A, D_A), jnp.float32),
        jax.ShapeDtypeStruct((batch, HALO_B, D_B), jnp.float32),
    ]
    out_specs = [
        pl.BlockSpec((rows, D_MODEL), tok),
        pl.BlockSpec((rows, HALF), tok),
        pl.BlockSpec((SUBLANES, rows), lane_tok),
        pl.BlockSpec((SUBLANES, rows), lane_tok),
        pl.BlockSpec((nseq, HALO_A, D_A), lambda i: (i, 0, 0)),
        pl.BlockSpec((nseq, HALO_B, D_B), lambda i: (i, 0, 0)),
    ]
    any_spec = pl.BlockSpec(memory_space=pl.ANY)
    in_specs = ([pl.BlockSpec((nseq, tt, D_MODEL), lambda i: (i, 0, 0)),
                 pl.BlockSpec((nseq, HALO_A, D_A), lambda i: (i, 0, 0)),
                 pl.BlockSpec((nseq, HALO_B, D_B), lambda i: (i, 0, 0))]
                + _mixer_param_specs() + [any_spec] * 4)
    n_in = len(in_specs)
    return pl.pallas_call(
        functools.partial(_mixer_sample_kernel, nseq=nseq, tt=tt, seq_chunk=8, row_chunk=tt, n_sub=1),
        grid=(batch // nseq,),
        in_specs=in_specs,
        out_specs=out_specs,
        out_shape=out_shape,
        scratch_shapes=_mixer_scratch(nseq, tt),
        input_output_aliases={n_in - 4: 0, n_in - 3: 1, n_in - 2: 2, n_in - 1: 3},
        compiler_params=pltpu.CompilerParams(
            dimension_semantics=("arbitrary",), vmem_limit_bytes=VMEM_LIMIT),
        name="mixer_sample",
    )(x, state_a, state_b, *params, x1, h2p, eid, p)


def _plan_kernel(eid_ref, pos_ref, seg_end_ref, carry_ref, start_ref, earlier_ref, *, tile):
    ph = pl.program_id(0)
    i = pl.program_id(1)
    tt = eid_ref.shape[1]

    @pl.when(i == 0)
    def _():
        @pl.when(ph == 1)
        def _():
            tiles = jnp.floor((carry_ref[...] + (tile - 0.5)) * (1.0 / tile))
            below = (lax.broadcasted_iota(jnp.int32, (N_EXPERTS, N_EXPERTS), 0)
                     > lax.broadcasted_iota(jnp.int32, (N_EXPERTS, N_EXPERTS), 1))
            start = jnp.dot(jnp.where(below, 1.0, 0.0).astype(jnp.bfloat16), tiles.astype(jnp.bfloat16),
                            preferred_element_type=jnp.float32) * tile
            start_ref[...] = start
            seg_end_ref[...] = (start + tiles * tile).astype(jnp.int32)
            earlier = (lax.broadcasted_iota(jnp.int32, (tt, tt), 0)
                       < lax.broadcasted_iota(jnp.int32, (tt, tt), 1))
            earlier_ref[...] = jnp.where(earlier, 1.0, 0.0).astype(jnp.bfloat16)
        carry_ref[...] = jnp.zeros_like(carry_ref)

    eid = eid_ref[...]
    experts = lax.broadcasted_iota(jnp.int32, (N_EXPERTS, tt), 0)
    oh0 = experts == eid[0:1]
    oh1 = experts == eid[1:2]
    oh = jnp.where(oh0 | oh1, 1.0, 0.0)

    @pl.when(ph == 1)
    def _():
        within = jnp.dot(oh.astype(jnp.bfloat16), earlier_ref[...], preferred_element_type=jnp.float32)
        slot_of = within + carry_ref[:, 0:1] + start_ref[:, 0:1]
        s0 = jnp.sum(jnp.where(oh0, slot_of, 0.0), axis=0, keepdims=True)
        s1 = jnp.sum(jnp.where(oh1, slot_of, 0.0), axis=0, keepdims=True)
        k = lax.broadcasted_iota(jnp.int32, (SUBLANES, tt), 0)
        pos_ref[...] = jnp.where(k == 0, s0, jnp.where(k == 1, s1, 0.0)).astype(jnp.int32)

    carry_ref[...] = carry_ref[...] + jnp.sum(oh, axis=1, keepdims=True)


def _plan(eid, tile):
    n_tokens = eid.shape[1]
    return pl.pallas_call(
        functools.partial(_plan_kernel, tile=tile),
        grid=(2, n_tokens // PLAN_TILE),
        in_specs=[pl.BlockSpec((SUBLANES, PLAN_TILE), lambda ph, i: (0, i))],
        out_specs=[pl.BlockSpec((SUBLANES, PLAN_TILE), lambda ph, i: (0, i * ph)),
                   pl.BlockSpec((N_EXPERTS, LANES), lambda ph, i: (0, 0))],
        out_shape=[jax.ShapeDtypeStruct((SUBLANES, n_tokens), jnp.int32),
                   jax.ShapeDtypeStruct((N_EXPERTS, LANES), jnp.int32)],
        scratch_shapes=[pltpu.VMEM((N_EXPERTS, LANES), jnp.float32),
                        pltpu.VMEM((N_EXPERTS, LANES), jnp.float32),
                        pltpu.VMEM((PLAN_TILE, PLAN_TILE), jnp.bfloat16)],
        compiler_params=pltpu.CompilerParams(dimension_semantics=("arbitrary", "arbitrary")),
        name="route_plan",
    )(eid)


def _sc_move_rows(table, idx, n_out, *, scatter):
    n_workers, n_batches, batch = idx.shape
    n_rows, words = table.shape
    assert n_rows % batch == 0 and n_batches >= 2
    mesh = plsc.VectorSubcoreMesh(core_axis_name="c", subcore_axis_name="s")

    @functools.partial(
        pl.kernel, mesh=mesh,
        out_type=jax.ShapeDtypeStruct((n_out, words), table.dtype),
        scratch_types=[pltpu.VMEM((n_batches, batch), jnp.int32),
                       pltpu.VMEM((2, batch, words), table.dtype),
                       pltpu.SemaphoreType.DMA((2,)),
                       pltpu.SemaphoreType.DMA((2,))],
    )
    def move(table_hbm, idx_hbm, out_hbm, idx_v, rows_v, sem_in, sem_out):
        worker = lax.axis_index("s") * SC_CORES + lax.axis_index("c")
        pltpu.sync_copy(idx_hbm.at[worker], idx_v)

        def in_order(ref, b, wrap):
            a0 = (worker * n_batches + b) * batch
            if wrap:
                a0 = jnp.where(a0 >= n_rows, a0 - n_rows, a0)
            return ref.at[pl.ds(a0, batch)]

        def read(b):
            src = in_order(table_hbm, b, True) if scatter else table_hbm.at[idx_v.at[b]]
            return pltpu.async_copy(src, rows_v.at[b % 2], sem_in.at[b % 2])

        def write(b):
            dst = out_hbm.at[idx_v.at[b]] if scatter else in_order(out_hbm, b, False)
            return pltpu.async_copy(rows_v.at[b % 2], dst, sem_out.at[b % 2])

        reads = {0: read(0)}
        writes = {}
        for b in range(n_batches):
            reads[b].wait()
            if b + 1 < n_batches:
                if b >= 1:
                    writes[b - 1].wait()
                reads[b + 1] = read(b + 1)
            writes[b] = write(b)
        writes[n_batches - 2].wait()
        writes[n_batches - 1].wait()

    return move(table, idx)


def _experts_kernel(te_ref, nv_ref, xs_ref, wg_ref, wu_ref, wd_ref, ys_ref):
    i = pl.program_id(0)

    @pl.when(i < nv_ref[0])
    def _():
        hi, lo = _unpack_bf16_halves(xs_ref[...])
        hi = hi.astype(jnp.bfloat16)
        lo = lo.astype(jnp.bfloat16)
        wg = wg_ref[0].astype(jnp.bfloat16)
        wu = wu_ref[0].astype(jnp.bfloat16)
        wd = wd_ref[0].astype(jnp.bfloat16)
        gate = (jnp.dot(hi, wg[:HALF], preferred_element_type=jnp.float32)
                + jnp.dot(lo, wg[HALF:], preferred_element_type=jnp.float32))
        up = (jnp.dot(hi, wu[:HALF], preferred_element_type=jnp.float32)
              + jnp.dot(lo, wu[HALF:], preferred_element_type=jnp.float32))
        hid = (gate * jax.nn.sigmoid(gate) * up).astype(jnp.bfloat16)
        ys_ref[...] = _pack_bf16_halves(jnp.dot(hid, wd, preferred_element_type=jnp.float32))


def _experts(xs, tile_expert, n_valid, wg, wu, wd, tm):
    n_slots = xs.shape[0]
    row_block = lambda i, te, nv: (jnp.minimum(i, nv[0] - 1), 0)
    w_block = lambda i, te, nv: (te[i], 0, 0)
    return pl.pallas_call(
        _experts_kernel,
        grid_spec=pltpu.PrefetchScalarGridSpec(
            num_scalar_prefetch=2,
            grid=(n_slots // tm,),
            in_specs=[pl.BlockSpec((tm, HALF), row_block),
                      pl.BlockSpec((1, D_MODEL, D_EXPERT), w_block),
                      pl.BlockSpec((1, D_MODEL, D_EXPERT), w_block),
                      pl.BlockSpec((1, D_EXPERT, D_MODEL), w_block)],
            out_specs=pl.BlockSpec((tm, HALF), row_block),
        ),
        out_shape=jax.ShapeDtypeStruct((n_slots, HALF), jnp.uint32),
        compiler_params=pltpu.CompilerParams(
            dimension_semantics=("arbitrary",), vmem_limit_bytes=VMEM_LIMIT),
        name="experts",
    )(tile_expert, n_valid, xs, wg, wu, wd)


def _final_kernel(x1_ref, y0_ref, y1_ref, p_ref, gfin_ref, *rest):
    out_ref = rest[-1]
    pt = jnp.transpose(p_ref[...])
    p0 = pt[:, 0:1]
    p1 = pt[:, 1:2]
    a_hi, a_lo = _unpack_bf16_halves(y0_ref[...])
    b_hi, b_lo = _unpack_bf16_halves(y1_ref[...])
    x1 = x1_ref[...]
    x2_hi = x1[:, :HALF] + (p0 * a_hi + p1 * b_hi)
    x2_lo = x1[:, HALF:] + (p0 * a_lo + p1 * b_lo)
    ms = (jnp.sum(x2_hi * x2_hi, axis=-1, keepdims=True)
          + jnp.sum(x2_lo * x2_lo, axis=-1, keepdims=True)) / D_MODEL
    scale = lax.rsqrt(ms + RMS_EPS)
    g = gfin_ref[...]
    out_ref[:, :HALF] = x2_hi * scale * g[:, :HALF]
    out_ref[:, HALF:] = x2_lo * scale * g[:, HALF:]


def _final(x1, yk, p, gfin, row_offset, n_rows, tm, out_rows, out_offset, out_buf=None):
    n_tokens = x1.shape[0]
    first = row_offset // tm
    second = (n_tokens + row_offset) // tm
    out_first = out_offset // tm
    in_specs = [pl.BlockSpec((tm, D_MODEL), lambda i: (first + i, 0)),
                pl.BlockSpec((tm, HALF), lambda i: (first + i, 0)),
                pl.BlockSpec((tm, HALF), lambda i: (second + i, 0)),
                pl.BlockSpec((SUBLANES, tm), lambda i: (0, first + i)),
                pl.BlockSpec((1, D_MODEL), lambda i: (0, 0))]
    args = [x1, yk, yk, p, gfin]
    aliases = {}
    if out_buf is not None:
        in_specs.append(pl.BlockSpec(memory_space=pl.ANY))
        args.append(out_buf)
        aliases = {len(args) - 1: 0}
    return pl.pallas_call(
        _final_kernel,
        grid=(n_rows // tm,),
        in_specs=in_specs,
        out_specs=pl.BlockSpec((tm, D_MODEL), lambda i: (out_first + i, 0)),
        out_shape=jax.ShapeDtypeStruct((out_rows, D_MODEL), jnp.float32),
        input_output_aliases=aliases,
        compiler_params=pltpu.CompilerParams(
            dimension_semantics=("arbitrary",), vmem_limit_bytes=VMEM_LIMIT),
        name="final",
    )(*args)


def _expert_tile(n_tokens):
    mean_rows = TOP_K * n_tokens // N_EXPERTS
    return -(-(mean_rows * 6 // 5) // SC_BATCH) * SC_BATCH


def _routed_experts(h2p, eid, w_gate, w_up, w_down):
    n_tokens = h2p.shape[0]
    assert (TOP_K * n_tokens) % (SC_WORKERS * SC_BATCH) == 0 and n_tokens % PLAN_TILE == 0
    tile = _expert_tile(n_tokens)
    n_tiles = -(-(TOP_K * n_tokens + N_EXPERTS * (tile - 1)) // tile)
    pos, seg_end = _plan(eid, tile)
    pos = pos[:TOP_K].reshape(SC_WORKERS, -1, SC_BATCH)
    seg_end = seg_end[:, 0]
    n_valid = seg_end[N_EXPERTS - 1:] // tile
    tile_row = jnp.arange(n_tiles, dtype=jnp.int32) * tile
    tile_expert = jnp.minimum(jnp.sum(seg_end[None, :] <= tile_row[:, None], axis=1), N_EXPERTS - 1)
    xs = _sc_move_rows(h2p, pos, n_tiles * tile, scatter=True)
    ys = _experts(xs, tile_expert.astype(jnp.int32), n_valid, w_gate, w_up, w_down, tile)
    return _sc_move_rows(ys, pos, TOP_K * n_tokens, scatter=False)


def kernel(x_prompt, x_sample, state_conv_a, state_conv_b, g_mix, w_in, conv_a_w, conv_b_w, conv_b_bias,
           ln_g, ln_b, w_out, g_ffn, w_coarse, b_coarse, w_fine, b_fine, w_gate, w_up, w_down, g_final):
    assert g_mix.shape[0] == 1, "single trunk layer"
    batch, seq, _ = x_prompt.shape
    dec_batch, dec_seq, _ = x_sample.shape
    n_prompt = batch * seq
    n_sample = dec_batch * dec_seq
    bf16 = jnp.bfloat16

    wr = jnp.concatenate([
        w_coarse[0], jnp.zeros((D_MODEL, SUBLANES - N_EXPERT_GROUPS), jnp.float32),
        jnp.transpose(w_fine[0], (1, 0, 2)).reshape(D_MODEL, N_EXPERTS)], axis=1).T
    wr_hi = wr.astype(bf16)
    wr_lo = (wr - wr_hi.astype(jnp.float32)).astype(bf16)
    br = jnp.concatenate([
        b_coarse[0], jnp.full((SUBLANES - N_EXPERT_GROUPS,), NEG_BIG, jnp.float32),
        b_fine[0].reshape(N_EXPERTS)]).reshape(ROUTER_ROWS, 1)

    params = (g_mix, w_in[0].astype(bf16), conv_a_w[0], conv_b_w[0], conv_b_bias, ln_g, ln_b,
              w_out[0].astype(bf16), g_ffn, wr_hi, wr_lo, br)

    experts = (w_gate[0], w_up[0], w_down[0])
    gfin = g_final.reshape(1, D_MODEL)

    half = batch // 2
    n_half = half * seq
    x1_a, h2p_a, eid_a, p_a, na_pa, nb_pa = _mixer_prompt(x_prompt, params, n_half, 512, 0, half)
    yk_a = _routed_experts(h2p_a, eid_a, *experts)

    bufs = _mixer_prompt(x_prompt, params, (batch - half) * seq + n_sample, 512, half, batch - half)
    na_pb, nb_pb = bufs[4:]
    x1_b, h2p_b, eid_b, p_b, na_s, nb_s = _mixer_sample(
        x_sample, state_conv_a[0], state_conv_b[0], params, bufs[:4], (batch - half) * seq, nseq=32)
    yk_b = _routed_experts(h2p_b, eid_b, *experts)

    y_p = _final(x1_a, yk_a, p_a, gfin, 0, n_half, 512, n_prompt, 0)
    y_p = _final(x1_b, yk_b, p_b, gfin, 0, n_prompt - n_half, 512, n_prompt, n_half, out_buf=y_p)
    y_s = _final(x1_b, yk_b, p_b, gfin, n_prompt - n_half, n_sample, 512, n_sample, 0)
    na_p = jnp.concatenate([na_pa, na_pb], axis=0)
    nb_p = jnp.concatenate([nb_pa, nb_pb], axis=0)
    return (y_p.reshape(batch, seq, D_MODEL), y_s.reshape(dec_batch, dec_seq, D_MODEL),
            na_p[None], nb_p[None], na_s[None], nb_s[None])
```

```python
import functools

import jax
import jax.numpy as jnp
from jax import lax
from jax.experimental import pallas as pl
from jax.experimental.pallas import tpu as pltpu
from jax.experimental.pallas import tpu_sc as plsc

D_MODEL = 1024
D_A = 512
D_B = 512
CONV_A = 3
CONV_B = 31
HALO_A = CONV_A - 1
HALO_B = CONV_B - 1
IN_COLS = 3 * D_A + 2 * D_B
N_EXPERT_GROUPS = 4
EXPERTS_PER_GROUP = 8
N_EXPERTS = N_EXPERT_GROUPS * EXPERTS_PER_GROUP
TOP_K = 2
D_EXPERT = D_MODEL // 4
RMS_EPS = 1e-6
LN_EPS = 1e-5

SUBLANES = 8
LANES = 128
PAD_A = SUBLANES
PAD_B = 32
ROUTER_ROWS = SUBLANES + N_EXPERTS
ROUTER_PAD = 48
NEG_BIG = -1e30
VMEM_LIMIT = 56 * 1024 * 1024
HALF = D_MODEL // 2
HI_MASK = 0xFFFF0000

SC_CORES = 2
SC_SUBCORES = 16
SC_WORKERS = SC_CORES * SC_SUBCORES
SC_BATCH = 64
SC_DISPATCH_BATCH = 32

PLAN_TILE = 1024
FINAL_TILE = 1024
MIXER_TILE = 512


def _rms_scale(x):
    return x * lax.rsqrt(jnp.mean(x * x, axis=-1, keepdims=True) + RMS_EPS)


def _dot_nt(a, b):
    return lax.dot_general(a, b, (((1,), (1,)), ((), ())), preferred_element_type=jnp.float32)


def _pack_bf16_halves(x):
    bits = lax.bitcast_convert_type(x.astype(jnp.bfloat16).astype(jnp.float32), jnp.uint32)
    return bits[:, :HALF] | (bits[:, HALF:] >> 16)


def _unpack_bf16_halves(w):
    hi = lax.bitcast_convert_type(w & jnp.uint32(HI_MASK), jnp.float32)
    lo = lax.bitcast_convert_type(w << 16, jnp.float32)
    return hi, lo


def _route(logits_t):
    rows = logits_t.shape[1]
    iota = lax.broadcasted_iota(jnp.int32, (SUBLANES, rows), 0)
    lc = logits_t[0:SUBLANES]
    cmax = jnp.max(lc, axis=0, keepdims=True)
    grp = jnp.min(jnp.where(lc == cmax, iota, SUBLANES), axis=0, keepdims=True)
    p_grp = 1.0 / jnp.sum(jnp.exp(lc - cmax), axis=0, keepdims=True)
    sel = logits_t[SUBLANES:2 * SUBLANES]
    for g in range(1, N_EXPERT_GROUPS):
        sel = jnp.where(grp == g, logits_t[(g + 1) * SUBLANES:(g + 2) * SUBLANES], sel)
    v1 = jnp.max(sel, axis=0, keepdims=True)
    i1 = jnp.min(jnp.where(sel == v1, iota, SUBLANES), axis=0, keepdims=True)
    sel2 = jnp.where(iota == i1, -jnp.inf, sel)
    v2 = jnp.max(sel2, axis=0, keepdims=True)
    i2 = jnp.min(jnp.where(sel2 == v2, iota, SUBLANES), axis=0, keepdims=True)
    e2 = jnp.exp(v2 - v1)
    den = 1.0 + e2
    p1 = p_grp / den
    p2 = p_grp * e2 / den
    base = grp * EXPERTS_PER_GROUP
    return (base + i1, base + i2), (p1, p2)


def _mixer_body(x_ref, sa_ref, sb_ref, gmix_ref, win_ref, caw_ref, cbw_ref, cbb_ref, lng_ref, lnb_ref,
                wout_ref, gffn_ref, wr_ref, br_ref,
                x1_ref, h2p_ref, eid_ref, p_ref, na_ref, nb_ref,
                proj_ref, uext_ref, gext_ref, z_ref, ush_ref, gsh_ref, *, nseq, tt, seq_chunk, row_chunk, n_sub):
    carried = sa_ref is None
    t = pl.program_id(1) if carried else None
    assert n_sub == 1 or nseq == 1
    sub = tt // n_sub

    if carried:
        @pl.when(t == 0)
        def _():
            uext_ref[:, 0:PAD_A, :] = jnp.zeros((nseq, PAD_A, D_A), jnp.float32)
            gext_ref[:, 0:PAD_B, :] = jnp.zeros((nseq, PAD_B, D_B), jnp.float32)
    else:
        uext_ref[:, PAD_A - HALO_A:PAD_A, :] = sa_ref[...]
        gext_ref[:, PAD_B - HALO_B:PAD_B, :] = sb_ref[...]

    def window(base_ref, shifted_ref, first_shift, s0, off, n_rows):
        r = off % SUBLANES
        a8 = off - r
        if r == 0:
            return base_ref[s0:s0 + seq_chunk, a8:a8 + n_rows, :]
        return shifted_ref[r - first_shift, s0:s0 + seq_chunk, a8:a8 + n_rows, :]

    n = seq_chunk * row_chunk
    col_chunk = 2 * LANES
    caw = caw_ref[...]
    cbw = cbw_ref[...]
    hs_bf = {}

    def rows_of(si):
        q0 = si * sub
        f0 = q0 if nseq == 1 else 0
        return q0, f0, nseq * sub

    def prep(si):
        q0, _, m = rows_of(si)
        x = x_ref[:, q0:q0 + sub, :].reshape(m, D_MODEL)
        hs_bf[si] = (_rms_scale(x) * gmix_ref[...]).astype(jnp.bfloat16)

    def dot_b_items(si):
        def item(c0):
            def run():
                q0, _, _ = rows_of(si)
                v_b = jnp.dot(hs_bf[si], win_ref[:, 3 * D_A + c0:3 * D_A + c0 + col_chunk],
                              preferred_element_type=jnp.float32)
                g_b = jnp.dot(hs_bf[si], win_ref[:, 3 * D_A + D_B + c0:3 * D_A + D_B + c0 + col_chunk],
                              preferred_element_type=jnp.float32)
                gext_ref[:, PAD_B + q0:PAD_B + q0 + sub, c0:c0 + col_chunk] = (
                    v_b * jax.nn.sigmoid(g_b)).reshape(nseq, sub, col_chunk)
            return run
        return [item(c0) for c0 in range(0, D_B, col_chunk)]

    def dot_a_items(si):
        def item(c0):
            def run():
                _, f0, m = rows_of(si)
                proj_ref[f0:f0 + m, c0:c0 + col_chunk] = jnp.dot(
                    hs_bf[si], win_ref[:, c0:c0 + col_chunk], preferred_element_type=jnp.float32)
            return run
        return [item(c0) for c0 in range(0, 3 * D_A, col_chunk)]

    def shift_b(si):
        q0, _, _ = rows_of(si)
        j_lo = 0 if si == 0 else q0 + PAD_B - SUBLANES
        j_hi = q0 + sub + PAD_B - SUBLANES
        for r in range(1, SUBLANES):
            gsh_ref[r - 1, :, j_lo:j_hi, :] = gext_ref[:, j_lo + r:j_hi + r, :]

    def chunks_of(si):
        q0, _, _ = rows_of(si)
        return [(s0, r0) for s0 in range(0, nseq, seq_chunk) for r0 in range(q0, q0 + sub, row_chunk)]

    def conv_b_items(si):
        def item(s0, r0):
            def run():
                lo = s0 * tt + r0
                acc_b = None
                for k in range(CONV_B):
                    off = PAD_B - HALO_B + k + r0
                    term = window(gext_ref, gsh_ref, 1, s0, off, row_chunk) * cbw[k:k + 1, :]
                    acc_b = term if acc_b is None else acc_b + term
                zb = acc_b.reshape(n, D_B) + cbb_ref[...]
                mu = jnp.mean(zb, axis=-1, keepdims=True)
                zc = zb - mu
                var = jnp.mean(zc * zc, axis=-1, keepdims=True)
                y = zc * lax.rsqrt(var + LN_EPS) * lng_ref[...] + lnb_ref[...]
                z_ref[lo:lo + n, D_A:] = (y * jax.nn.sigmoid(y)).astype(jnp.bfloat16)
            return run
        return [item(s0, r0) for s0, r0 in chunks_of(si)]

    def conv_a_item(si):
        def run():
            q0, f0, m = rows_of(si)
            c_a = proj_ref[f0:f0 + m, D_A:2 * D_A]
            v_a = proj_ref[f0:f0 + m, 2 * D_A:3 * D_A]
            uext_ref[:, PAD_A + q0:PAD_A + q0 + sub, :] = (c_a * v_a).reshape(nseq, sub, D_A)
            for r in range(SUBLANES - HALO_A, SUBLANES):
                ush_ref[r - (SUBLANES - HALO_A), :, q0:q0 + sub, :] = uext_ref[:, q0 + r:q0 + sub + r, :]
            for s0, r0 in chunks_of(si):
                lo = s0 * tt + r0
                acc_a = None
                for k in range(CONV_A):
                    off = PAD_A - HALO_A + k + r0
                    term = window(uext_ref, ush_ref, SUBLANES - HALO_A, s0, off, row_chunk) * caw[k:k + 1, :]
                    acc_a = term if acc_a is None else acc_a + term
                z_a = proj_ref[lo:lo + n, 0:D_A] * acc_a.reshape(n, D_A)
                z_ref[lo:lo + n, 0:D_A] = z_a.astype(jnp.bfloat16)
        return run

    def finish_item(si):
        def run():
            q0, f0, m = rows_of(si)
            x = x_ref[:, q0:q0 + sub, :].reshape(m, D_MODEL)
            x1 = x + jnp.dot(z_ref[f0:f0 + m, :], wout_ref[...], preferred_element_type=jnp.float32)
            x1_ref[f0:f0 + m, :] = x1
            h2 = _rms_scale(x1) * gffn_ref[...]
            h2_hi = h2.astype(jnp.bfloat16)
            h2p_ref[f0:f0 + m, :] = _pack_bf16_halves(h2)
            h2_lo = (h2 - h2_hi.astype(jnp.float32)).astype(jnp.bfloat16)
            both = _dot_nt(wr_ref[...], h2_hi)
            cross = _dot_nt(wr_ref[0:ROUTER_PAD, :], h2_lo)
            logits_t = (both[0:ROUTER_ROWS] + both[ROUTER_PAD:ROUTER_PAD + ROUTER_ROWS]
                        + cross[0:ROUTER_ROWS] + br_ref[...])
            (e1, e2), (p1, p2) = _route(logits_t)
            iota = lax.broadcasted_iota(jnp.int32, (SUBLANES, m), 0)
            eid_ref[:, f0:f0 + m] = jnp.where(iota == 0, e1, jnp.where(iota == 1, e2, 0))
            p_ref[:, f0:f0 + m] = jnp.where(iota == 0, p1, jnp.where(iota == 1, p2, 0.0))
        return run

    def interleave(main, fill):
        for i, item in enumerate(main):
            item()
            for f in fill[i * len(fill) // len(main):(i + 1) * len(fill) // len(main)]:
                f()

    for si in range(n_sub):
        prep(si)
    for item in dot_b_items(0):
        item()
    shift_b(0)
    carry_over = []
    for si in range(n_sub):
        fill = carry_over + dot_a_items(si)
        if si + 1 < n_sub:
            fill = fill + dot_b_items(si + 1)
        interleave(conv_b_items(si), fill + [conv_a_item(si)])
        if si + 1 < n_sub:
            shift_b(si + 1)
        carry_over = [finish_item(si)]
    for item in carry_over:
        item()

    if carried:
        @pl.when(t == pl.num_programs(1) - 1)
        def _():
            na_ref[...] = uext_ref[:, PAD_A + tt - HALO_A:PAD_A + tt, :]
            nb_ref[...] = gext_ref[:, PAD_B + tt - HALO_B:PAD_B + tt, :]
        uext_ref[:, 0:PAD_A, :] = uext_ref[:, tt:tt + PAD_A, :]
        gext_ref[:, 0:PAD_B, :] = gext_ref[:, tt:tt + PAD_B, :]
    else:
        na_ref[...] = uext_ref[:, PAD_A + tt - HALO_A:PAD_A + tt, :]
        nb_ref[...] = gext_ref[:, PAD_B + tt - HALO_B:PAD_B + tt, :]


def _mixer_prompt_kernel(x_ref, *refs, **kw):
    _mixer_body(x_ref, None, None, *refs, **kw)


def _mixer_sample_kernel(x_ref, sa_ref, sb_ref, *refs, **kw):
    n_params = 11
    params = refs[:n_params]
    rest = refs[n_params + 4:]
    _mixer_body(x_ref, sa_ref, sb_ref, *params, *rest, **kw)


def _full(shape):
    return pl.BlockSpec(shape, lambda *_: (0,) * len(shape))


def _mixer_param_specs():
    return [
        _full((1, D_MODEL)),
        _full((D_MODEL, IN_COLS)),
        _full((CONV_A, D_A)),
        _full((CONV_B, D_B)),
        _full((1, D_B)),
        _full((1, D_B)),
        _full((1, D_B)),
        _full((D_MODEL, D_MODEL)),
        _full((1, D_MODEL)),
        _full((2 * ROUTER_PAD, D_MODEL)),
        _full((ROUTER_ROWS, 1)),
    ]


def _mixer_scratch(nseq, tt):
    rows = nseq * tt
    return [
        pltpu.VMEM((rows, 3 * D_A), jnp.float32),
        pltpu.VMEM((nseq, PAD_A + tt, D_A), jnp.float32),
        pltpu.VMEM((nseq, PAD_B + tt, D_B), jnp.float32),
        pltpu.VMEM((rows, D_MODEL), jnp.bfloat16),
        pltpu.VMEM((HALO_A, nseq, PAD_A + tt - SUBLANES, D_A), jnp.float32),
        pltpu.VMEM((SUBLANES - 1, nseq, PAD_B + tt - SUBLANES, D_B), jnp.float32),
    ]


def _mixer_prompt(x, params, n_tokens_total, tt, seq_first, batch):
    seq = x.shape[1]
    n_t = seq // tt
    tok = lambda b, t: (b * n_t + t, 0)
    lane_tok = lambda b, t: (0, b * n_t + t)
    out_shape = [
        jax.ShapeDtypeStruct((n_tokens_total, D_MODEL), jnp.float32),
        jax.ShapeDtypeStruct((n_tokens_total, HALF), jnp.uint32),
        jax.ShapeDtypeStruct((SUBLANES, n_tokens_total), jnp.int32),
        jax.ShapeDtypeStruct((SUBLANES, n_tokens_total), jnp.float32),
        jax.ShapeDtypeStruct((batch, HALO_A, D_A), jnp.float32),
        jax.ShapeDtypeStruct((batch, HALO_B, D_B), jnp.float32),
    ]
    out_specs = [
        pl.BlockSpec((tt, D_MODEL), tok),
        pl.BlockSpec((tt, HALF), tok),
        pl.BlockSpec((SUBLANES, tt), lane_tok),
        pl.BlockSpec((SUBLANES, tt), lane_tok),
        pl.BlockSpec((1, HALO_A, D_A), lambda b, t: (b, 0, 0)),
        pl.BlockSpec((1, HALO_B, D_B), lambda b, t: (b, 0, 0)),
    ]
    return pl.pallas_call(
        functools.partial(_mixer_prompt_kernel, nseq=1, tt=tt, seq_chunk=1, row_chunk=64, n_sub=2),
        grid=(batch, n_t),
        in_specs=[pl.BlockSpec((1, tt, D_MODEL), lambda b, t: (b + seq_first, t, 0))] + _mixer_param_specs(),
        out_specs=out_specs,
        out_shape=out_shape,
        scratch_shapes=_mixer_scratch(1, tt),
        compiler_params=pltpu.CompilerParams(
            dimension_semantics=("arbitrary", "arbitrary"), vmem_limit_bytes=VMEM_LIMIT),
        name="mixer_prompt",
    )(x, *params)


def _mixer_sample(x, state_a, state_b, params, bufs, row_offset, nseq):
    batch, tt, _ = x.shape
    rows = nseq * tt
    first = row_offset // rows
    tok = lambda i: (first + i, 0)
    lane_tok = lambda i: (0, first + i)
    x1, h2p, eid, p = bufs
    out_shape = [
        jax.ShapeDtypeStruct(x1.shape, x1.dtype),
        jax.ShapeDtypeStruct(h2p.shape, h2p.dtype),
        jax.ShapeDtypeStruct(eid.shape, eid.dtype),
        jax.ShapeDtypeStruct(p.shape, p.dtype),
        jax.ShapeDtypeStruct((batch, HALO_A, D_A), jnp.float32),
        jax.ShapeDtypeStruct((batch, HALO_B, D_B), jnp.float32),
    ]
    out_specs = [
        pl.BlockSpec((rows, D_MODEL), tok),
        pl.BlockSpec((rows, HALF), tok),
        pl.BlockSpec((SUBLANES, rows), lane_tok),
        pl.BlockSpec((SUBLANES, rows), lane_tok),
        pl.BlockSpec((nseq, HALO_A, D_A), lambda i: (i, 0, 0)),
        pl.BlockSpec((nseq, HALO_B, D_B), lambda i: (i, 0, 0)),
    ]
    any_spec = pl.BlockSpec(memory_space=pl.ANY)
    in_specs = ([pl.BlockSpec((nseq, tt, D_MODEL), lambda i: (i, 0, 0)),
                 pl.BlockSpec((nseq, HALO_A, D_A), lambda i: (i, 0, 0)),
                 pl.BlockSpec((nseq, HALO_B, D_B), lambda i: (i, 0, 0))]
                + _mixer_param_specs() + [any_spec] * 4)
    n_in = len(in_specs)
    return pl.pallas_call(
        functools.partial(_mixer_sample_kernel, nseq=nseq, tt=tt, seq_chunk=8, row_chunk=tt, n_sub=1),
        grid=(batch // nseq,),
        in_specs=in_specs,
        out_specs=out_specs,
        out_shape=out_shape,
        scratch_shapes=_mixer_scratch(nseq, tt),
        input_output_aliases={n_in - 4: 0, n_in - 3: 1, n_in - 2: 2, n_in - 1: 3},
        compiler_params=pltpu.CompilerParams(
            dimension_semantics=("arbitrary",), vmem_limit_bytes=VMEM_LIMIT),
        name="mixer_sample",
    )(x, state_a, state_b, *params, x1, h2p, eid, p)


def _plan_kernel(eid_ref, pos_ref, seg_end_ref, carry_ref, start_ref, earlier_ref, *, tile):
    ph = pl.program_id(0)
    i = pl.program_id(1)
    tt = eid_ref.shape[1]

    @pl.when(i == 0)
    def _():
        @pl.when(ph == 1)
        def _():
            tiles = jnp.floor((carry_ref[...] + (tile - 0.5)) * (1.0 / tile))
            below = (lax.broadcasted_iota(jnp.int32, (N_EXPERTS, N_EXPERTS), 0)
                     > lax.broadcasted_iota(jnp.int32, (N_EXPERTS, N_EXPERTS), 1))
            start = jnp.dot(jnp.where(below, 1.0, 0.0).astype(jnp.bfloat16), tiles.astype(jnp.bfloat16),
                            preferred_element_type=jnp.float32) * tile
            start_ref[...] = start
            seg_end_ref[...] = (start + tiles * tile).astype(jnp.int32)
            earlier = (lax.broadcasted_iota(jnp.int32, (tt, tt), 0)
                       < lax.broadcasted_iota(jnp.int32, (tt, tt), 1))
            earlier_ref[...] = jnp.where(earlier, 1.0, 0.0).astype(jnp.bfloat16)
        carry_ref[...] = jnp.zeros_like(carry_ref)

    eid = eid_ref[...]
    experts = lax.broadcasted_iota(jnp.int32, (N_EXPERTS, tt), 0)
    oh0 = experts == eid[0:1]
    oh1 = experts == eid[1:2]
    oh = jnp.where(oh0 | oh1, 1.0, 0.0)

    @pl.when(ph == 1)
    def _():
        within = jnp.dot(oh.astype(jnp.bfloat16), earlier_ref[...], preferred_element_type=jnp.float32)
        slot_of = within + carry_ref[:, 0:1] + start_ref[:, 0:1]
        s0 = jnp.sum(jnp.where(oh0, slot_of, 0.0), axis=0, keepdims=True)
        s1 = jnp.sum(jnp.where(oh1, slot_of, 0.0), axis=0, keepdims=True)
        k = lax.broadcasted_iota(jnp.int32, (SUBLANES, tt), 0)
        pos_ref[...] = jnp.where(k == 0, s0, jnp.where(k == 1, s1, 0.0)).astype(jnp.int32)

    carry_ref[...] = carry_ref[...] + jnp.sum(oh, axis=1, keepdims=True)


def _plan(eid, tile):
    n_tokens = eid.shape[1]
    return pl.pallas_call(
        functools.partial(_plan_kernel, tile=tile),
        grid=(2, n_tokens // PLAN_TILE),
        in_specs=[pl.BlockSpec((SUBLANES, PLAN_TILE), lambda ph, i: (0, i))],
        out_specs=[pl.BlockSpec((SUBLANES, PLAN_TILE), lambda ph, i: (0, i * ph)),
                   pl.BlockSpec((N_EXPERTS, LANES), lambda ph, i: (0, 0))],
        out_shape=[jax.ShapeDtypeStruct((SUBLANES, n_tokens), jnp.int32),
                   jax.ShapeDtypeStruct((N_EXPERTS, LANES), jnp.int32)],
        scratch_shapes=[pltpu.VMEM((N_EXPERTS, LANES), jnp.float32),
                        pltpu.VMEM((N_EXPERTS, LANES), jnp.float32),
                        pltpu.VMEM((PLAN_TILE, PLAN_TILE), jnp.bfloat16)],
        compiler_params=pltpu.CompilerParams(dimension_semantics=("arbitrary", "arbitrary")),
        name="route_plan",
    )(eid)


def _sc_mesh():
    return plsc.VectorSubcoreMesh(core_axis_name="c", subcore_axis_name="s")


def _sc_dispatch_rows(table, pos, n_out):
    n_workers, n_batches, top_k, batch = pos.shape
    n_rows, words = table.shape
    assert n_workers * n_batches * batch == n_rows and n_batches >= 2

    @functools.partial(
        pl.kernel, mesh=_sc_mesh(),
        out_type=jax.ShapeDtypeStruct((n_out, words), table.dtype),
        scratch_types=[pltpu.VMEM((n_batches, top_k, batch), jnp.int32),
                       pltpu.VMEM((2, batch, words), table.dtype),
                       pltpu.SemaphoreType.DMA((2,)),
                       pltpu.SemaphoreType.DMA((2, top_k))],
    )
    def dispatch(table_hbm, pos_hbm, out_hbm, idx_v, rows_v, sem_in, sem_out):
        worker = lax.axis_index("s") * SC_CORES + lax.axis_index("c")
        pltpu.sync_copy(pos_hbm.at[worker], idx_v)

        def read(b):
            src = table_hbm.at[pl.ds((worker * n_batches + b) * batch, batch)]
            return pltpu.async_copy(src, rows_v.at[b % 2], sem_in.at[b % 2])

        def write(b):
            return [pltpu.async_copy(rows_v.at[b % 2], out_hbm.at[idx_v.at[b, k]], sem_out.at[b % 2, k])
                    for k in range(top_k)]

        reads = {0: read(0)}
        writes = {}
        for b in range(n_batches):
            reads[b].wait()
            if b + 1 < n_batches:
                if b >= 1:
                    for w in writes[b - 1]:
                        w.wait()
                reads[b + 1] = read(b + 1)
            writes[b] = write(b)
        for b in (n_batches - 2, n_batches - 1):
            for w in writes[b]:
                w.wait()

    return dispatch(table, pos)


def _sc_gather_rows(table, idx):
    n_workers, n_batches, batch = idx.shape
    words = table.shape[1]
    assert n_batches >= 2

    @functools.partial(
        pl.kernel, mesh=_sc_mesh(),
        out_type=jax.ShapeDtypeStruct((n_workers * n_batches * batch, words), table.dtype),
        scratch_types=[pltpu.VMEM((n_batches, batch), jnp.int32),
                       pltpu.VMEM((2, batch, words), table.dtype),
                       pltpu.SemaphoreType.DMA((2,)),
                       pltpu.SemaphoreType.DMA((2,))],
    )
    def gather(table_hbm, idx_hbm, out_hbm, idx_v, rows_v, sem_in, sem_out):
        worker = lax.axis_index("s") * SC_CORES + lax.axis_index("c")
        pltpu.sync_copy(idx_hbm.at[worker], idx_v)

        def read(b):
            return pltpu.async_copy(table_hbm.at[idx_v.at[b]], rows_v.at[b % 2], sem_in.at[b % 2])

        def write(b):
            dst = out_hbm.at[pl.ds((worker * n_batches + b) * batch, batch)]
            return pltpu.async_copy(rows_v.at[b % 2], dst, sem_out.at[b % 2])

        reads = {0: read(0)}
        writes = {}
        for b in range(n_batches):
            reads[b].wait()
            if b + 1 < n_batches:
                if b >= 1:
                    writes[b - 1].wait()
                reads[b + 1] = read(b + 1)
            writes[b] = write(b)
        writes[n_batches - 2].wait()
        writes[n_batches - 1].wait()

    return gather(table, idx)


def _experts_kernel(te_ref, nv_ref, xs_ref, wg_ref, wu_ref, wd_ref, ys_ref):
    i = pl.program_id(0)

    @pl.when(i < nv_ref[0])
    def _():
        hi, lo = _unpack_bf16_halves(xs_ref[...])
        hi = hi.astype(jnp.bfloat16)
        lo = lo.astype(jnp.bfloat16)
        wg = wg_ref[0].astype(jnp.bfloat16)
        wu = wu_ref[0].astype(jnp.bfloat16)
        wd = wd_ref[0].astype(jnp.bfloat16)
        gate = (jnp.dot(hi, wg[:HALF], preferred_element_type=jnp.float32)
                + jnp.dot(lo, wg[HALF:], preferred_element_type=jnp.float32))
        up = (jnp.dot(hi, wu[:HALF], preferred_element_type=jnp.float32)
              + jnp.dot(lo, wu[HALF:], preferred_element_type=jnp.float32))
        hid = (gate * jax.nn.sigmoid(gate) * up).astype(jnp.bfloat16)
        ys_ref[...] = _pack_bf16_halves(jnp.dot(hid, wd, preferred_element_type=jnp.float32))


def _experts(xs, tile_expert, n_valid, wg, wu, wd, tm):
    n_slots = xs.shape[0]
    row_block = lambda i, te, nv: (jnp.minimum(i, nv[0] - 1), 0)
    w_block = lambda i, te, nv: (te[i], 0, 0)
    return pl.pallas_call(
        _experts_kernel,
        grid_spec=pltpu.PrefetchScalarGridSpec(
            num_scalar_prefetch=2,
            grid=(n_slots // tm,),
            in_specs=[pl.BlockSpec((tm, HALF), row_block),
                      pl.BlockSpec((1, D_MODEL, D_EXPERT), w_block),
                      pl.BlockSpec((1, D_MODEL, D_EXPERT), w_block),
                      pl.BlockSpec((1, D_EXPERT, D_MODEL), w_block)],
            out_specs=pl.BlockSpec((tm, HALF), row_block),
        ),
        out_shape=jax.ShapeDtypeStruct((n_slots, HALF), jnp.uint32),
        compiler_params=pltpu.CompilerParams(
            dimension_semantics=("arbitrary",), vmem_limit_bytes=VMEM_LIMIT),
        name="experts",
    )(tile_expert, n_valid, xs, wg, wu, wd)


def _final_kernel(x1_ref, y0_ref, y1_ref, p_ref, gfin_ref, *rest):
    out_ref = rest[-1]
    pt = jnp.transpose(p_ref[...])
    p0 = pt[:, 0:1]
    p1 = pt[:, 1:2]
    a_hi, a_lo = _unpack_bf16_halves(y0_ref[...])
    b_hi, b_lo = _unpack_bf16_halves(y1_ref[...])
    x1 = x1_ref[...]
    x2_hi = x1[:, :HALF] + (p0 * a_hi + p1 * b_hi)
    x2_lo = x1[:, HALF:] + (p0 * a_lo + p1 * b_lo)
    ms = (jnp.sum(x2_hi * x2_hi, axis=-1, keepdims=True)
          + jnp.sum(x2_lo * x2_lo, axis=-1, keepdims=True)) / D_MODEL
    scale = lax.rsqrt(ms + RMS_EPS)
    g = gfin_ref[...]
    out_ref[:, :HALF] = x2_hi * scale * g[:, :HALF]
    out_ref[:, HALF:] = x2_lo * scale * g[:, HALF:]


def _final(x1, yk, p, gfin, row_offset, n_rows, tm, out_rows, out_offset, out_buf=None):
    n_tokens = x1.shape[0]
    first = row_offset // tm
    second = (n_tokens + row_offset) // tm
    out_first = out_offset // tm
    in_specs = [pl.BlockSpec((tm, D_MODEL), lambda i: (first + i, 0)),
                pl.BlockSpec((tm, HALF), lambda i: (first + i, 0)),
                pl.BlockSpec((tm, HALF), lambda i: (second + i, 0)),
                pl.BlockSpec((SUBLANES, tm), lambda i: (0, first + i)),
                pl.BlockSpec((1, D_MODEL), lambda i: (0, 0))]
    args = [x1, yk, yk, p, gfin]
    aliases = {}
    if out_buf is not None:
        in_specs.append(pl.BlockSpec(memory_space=pl.ANY))
        args.append(out_buf)
        aliases = {len(args) - 1: 0}
    return pl.pallas_call(
        _final_kernel,
        grid=(n_rows // tm,),
        in_specs=in_specs,
        out_specs=pl.BlockSpec((tm, D_MODEL), lambda i: (out_first + i, 0)),
        out_shape=jax.ShapeDtypeStruct((out_rows, D_MODEL), jnp.float32),
        input_output_aliases=aliases,
        compiler_params=pltpu.CompilerParams(
            dimension_semantics=("arbitrary",), vmem_limit_bytes=VMEM_LIMIT),
        name="final",
    )(*args)


def _expert_tile(n_tokens):
    mean_rows = TOP_K * n_tokens // N_EXPERTS
    return -(-(mean_rows * 6 // 5) // SC_BATCH) * SC_BATCH


def _routed_experts(h2p, eid, w_gate, w_up, w_down):
    n_tokens = h2p.shape[0]
    assert n_tokens % (SC_WORKERS * SC_DISPATCH_BATCH) == 0 and n_tokens % PLAN_TILE == 0
    assert (TOP_K * n_tokens) % (SC_WORKERS * SC_BATCH) == 0
    tile = _expert_tile(n_tokens)
    n_tiles = -(-(TOP_K * n_tokens + N_EXPERTS * (tile - 1)) // tile)
    pos, seg_end = _plan(eid, tile)
    pos = pos[:TOP_K]
    by_token = jnp.transpose(pos.reshape(TOP_K, SC_WORKERS, -1, SC_DISPATCH_BATCH), (1, 2, 0, 3))
    seg_end = seg_end[:, 0]
    n_valid = seg_end[N_EXPERTS - 1:] // tile
    tile_row = jnp.arange(n_tiles, dtype=jnp.int32) * tile
    tile_expert = jnp.minimum(jnp.sum(seg_end[None, :] <= tile_row[:, None], axis=1), N_EXPERTS - 1)
    xs = _sc_dispatch_rows(h2p, by_token, n_tiles * tile)
    ys = _experts(xs, tile_expert.astype(jnp.int32), n_valid, w_gate, w_up, w_down, tile)
    return _sc_gather_rows(ys, pos.reshape(SC_WORKERS, -1, SC_BATCH))


def kernel(x_prompt, x_sample, state_conv_a, state_conv_b, g_mix, w_in, conv_a_w, conv_b_w, conv_b_bias,
           ln_g, ln_b, w_out, g_ffn, w_coarse, b_coarse, w_fine, b_fine, w_gate, w_up, w_down, g_final):
    assert g_mix.shape[0] == 1, "single trunk layer"
    batch, seq, _ = x_prompt.shape
    dec_batch, dec_seq, _ = x_sample.shape
    n_prompt = batch * seq
    n_sample = dec_batch * dec_seq
    bf16 = jnp.bfloat16

    wr = jnp.concatenate([
        w_coarse[0], jnp.zeros((D_MODEL, SUBLANES - N_EXPERT_GROUPS), jnp.float32),
        jnp.transpose(w_fine[0], (1, 0, 2)).reshape(D_MODEL, N_EXPERTS)], axis=1).T
    wr = jnp.pad(wr, ((0, ROUTER_PAD - ROUTER_ROWS), (0, 0)))
    wr_hi = wr.astype(bf16)
    wr_lo = (wr - wr_hi.astype(jnp.float32)).astype(bf16)
    wr_both = jnp.concatenate([wr_hi, wr_lo], axis=0)
    br = jnp.concatenate([
        b_coarse[0], jnp.full((SUBLANES - N_EXPERT_GROUPS,), NEG_BIG, jnp.float32),
        b_fine[0].reshape(N_EXPERTS)]).reshape(ROUTER_ROWS, 1)

    params = (g_mix, w_in[0].astype(bf16), conv_a_w[0], conv_b_w[0], conv_b_bias, ln_g, ln_b,
              w_out[0].astype(bf16), g_ffn, wr_both, br)

    experts = (w_gate[0], w_up[0], w_down[0])
    gfin = g_final.reshape(1, D_MODEL)

    bufs = _mixer_prompt(x_prompt, params, n_prompt + n_sample, MIXER_TILE, 0, batch)
    na_p, nb_p = bufs[4:]
    x1, h2p, eid, p, na_s, nb_s = _mixer_sample(
        x_sample, state_conv_a[0], state_conv_b[0], params, bufs[:4], n_prompt, nseq=32)
    yk = _routed_experts(h2p, eid, *experts)
    y_p = _final(x1, yk, p, gfin, 0, n_prompt, FINAL_TILE, n_prompt, 0)
    y_s = _final(x1, yk, p, gfin, n_prompt, n_sample, FINAL_TILE, n_sample, 0)
    return (y_p.reshape(batch, seq, D_MODEL), y_s.reshape(dec_batch, dec_seq, D_MODEL),
            na_p[None], nb_p[None], na_s[None], nb_s[None])
```

```python
import functools

import jax
import jax.numpy as jnp
from jax import lax
from jax.experimental import pallas as pl
from jax.experimental.pallas import tpu as pltpu
from jax.experimental.pallas import tpu_sc as plsc

D_MODEL = 1024
D_A = 512
D_B = 512
CONV_A = 3
CONV_B = 31
HALO_A = CONV_A - 1
HALO_B = CONV_B - 1
IN_COLS = 3 * D_A + 2 * D_B
N_EXPERT_GROUPS = 4
EXPERTS_PER_GROUP = 8
N_EXPERTS = N_EXPERT_GROUPS * EXPERTS_PER_GROUP
TOP_K = 2
D_EXPERT = D_MODEL // 4
RMS_EPS = 1e-6
LN_EPS = 1e-5

SUBLANES = 8
LANES = 128
PAD_A = SUBLANES
PAD_B = 32
ROUTER_ROWS = SUBLANES + N_EXPERTS
NEG_BIG = -1e30
VMEM_LIMIT = 56 * 1024 * 1024
HALF = D_MODEL // 2
HI_MASK = 0xFFFF0000

SC_CORES = 2
SC_SUBCORES = 16
SC_WORKERS = SC_CORES * SC_SUBCORES
SC_BATCH = 64
SC_DISPATCH_BATCH = 32

PLAN_TILE = 1024
FINAL_TILE = 1024
MIXER_TILE = 512


def _rms_scale(x):
    return x * lax.rsqrt(jnp.mean(x * x, axis=-1, keepdims=True) + RMS_EPS)


def _pack_bf16_halves(x):
    bits = lax.bitcast_convert_type(x.astype(jnp.bfloat16).astype(jnp.float32), jnp.uint32)
    return bits[:, :HALF] | (bits[:, HALF:] >> 16)


def _unpack_bf16_halves(w):
    hi = lax.bitcast_convert_type(w & jnp.uint32(HI_MASK), jnp.float32)
    lo = lax.bitcast_convert_type(w << 16, jnp.float32)
    return hi, lo


def _route(logits_t):
    rows = logits_t.shape[1]
    iota = lax.broadcasted_iota(jnp.int32, (SUBLANES, rows), 0)
    lc = logits_t[0:SUBLANES]
    cmax = jnp.max(lc, axis=0, keepdims=True)
    grp = jnp.min(jnp.where(lc == cmax, iota, SUBLANES), axis=0, keepdims=True)
    p_grp = 1.0 / jnp.sum(jnp.exp(lc - cmax), axis=0, keepdims=True)
    sel = logits_t[SUBLANES:2 * SUBLANES]
    for g in range(1, N_EXPERT_GROUPS):
        sel = jnp.where(grp == g, logits_t[(g + 1) * SUBLANES:(g + 2) * SUBLANES], sel)
    v1 = jnp.max(sel, axis=0, keepdims=True)
    i1 = jnp.min(jnp.where(sel == v1, iota, SUBLANES), axis=0, keepdims=True)
    sel2 = jnp.where(iota == i1, -jnp.inf, sel)
    v2 = jnp.max(sel2, axis=0, keepdims=True)
    i2 = jnp.min(jnp.where(sel2 == v2, iota, SUBLANES), axis=0, keepdims=True)
    e2 = jnp.exp(v2 - v1)
    den = 1.0 + e2
    p1 = p_grp / den
    p2 = p_grp * e2 / den
    base = grp * EXPERTS_PER_GROUP
    return (base + i1, base + i2), (p1, p2)


def _mixer_body(x_ref, sa_ref, sb_ref, gmix_ref, win_ref, caw_ref, cbw_ref, cbb_ref, lng_ref, lnb_ref,
                wout_ref, gffn_ref, wr_ref, br_ref,
                x1_ref, h2p_ref, eid_ref, p_ref, na_ref, nb_ref,
                proj_ref, uext_ref, gext_ref, z_ref, ush_ref, gsh_ref, *, nseq, tt, seq_chunk, row_chunk, n_sub):
    carried = sa_ref is None
    t = pl.program_id(1) if carried else None
    assert n_sub == 1 or nseq == 1
    sub = tt // n_sub

    if carried:
        @pl.when(t == 0)
        def _():
            uext_ref[:, 0:PAD_A, :] = jnp.zeros((nseq, PAD_A, D_A), jnp.float32)
            gext_ref[:, 0:PAD_B, :] = jnp.zeros((nseq, PAD_B, D_B), jnp.float32)
    else:
        uext_ref[:, PAD_A - HALO_A:PAD_A, :] = sa_ref[...]
        gext_ref[:, PAD_B - HALO_B:PAD_B, :] = sb_ref[...]

    def window(base_ref, shifted_ref, first_shift, s0, off, n_rows):
        r = off % SUBLANES
        a8 = off - r
        if r == 0:
            return base_ref[s0:s0 + seq_chunk, a8:a8 + n_rows, :]
        return shifted_ref[r - first_shift, s0:s0 + seq_chunk, a8:a8 + n_rows, :]

    n = seq_chunk * row_chunk
    col_chunk = 2 * LANES
    caw = caw_ref[...]
    cbw = cbw_ref[...]
    hs_bf = {}

    def rows_of(si):
        q0 = si * sub
        f0 = q0 if nseq == 1 else 0
        return q0, f0, nseq * sub

    def prep(si):
        q0, _, m = rows_of(si)
        x = x_ref[:, q0:q0 + sub, :].reshape(m, D_MODEL)
        hs_bf[si] = (_rms_scale(x) * gmix_ref[...]).astype(jnp.bfloat16)

    def dot_b_items(si):
        def item(c0):
            def run():
                q0, _, _ = rows_of(si)
                v_b = jnp.dot(hs_bf[si], win_ref[:, 3 * D_A + c0:3 * D_A + c0 + col_chunk],
                              preferred_element_type=jnp.float32)
                g_b = jnp.dot(hs_bf[si], win_ref[:, 3 * D_A + D_B + c0:3 * D_A + D_B + c0 + col_chunk],
                              preferred_element_type=jnp.float32)
                gext_ref[:, PAD_B + q0:PAD_B + q0 + sub, c0:c0 + col_chunk] = (
                    v_b * jax.nn.sigmoid(g_b)).reshape(nseq, sub, col_chunk)
            return run
        return [item(c0) for c0 in range(0, D_B, col_chunk)]

    def dot_a_items(si):
        def item(c0):
            def run():
                _, f0, m = rows_of(si)
                proj_ref[f0:f0 + m, c0:c0 + col_chunk] = jnp.dot(
                    hs_bf[si], win_ref[:, c0:c0 + col_chunk], preferred_element_type=jnp.float32)
            return run
        return [item(c0) for c0 in range(0, 3 * D_A, col_chunk)]

    def shift_b(si):
        q0, _, _ = rows_of(si)
        j_lo = 0 if si == 0 else q0 + PAD_B - SUBLANES
        j_hi = q0 + sub + PAD_B - SUBLANES
        for r in range(1, SUBLANES):
            gsh_ref[r - 1, :, j_lo:j_hi, :] = gext_ref[:, j_lo + r:j_hi + r, :]

    def chunks_of(si):
        q0, _, _ = rows_of(si)
        return [(s0, r0) for s0 in range(0, nseq, seq_chunk) for r0 in range(q0, q0 + sub, row_chunk)]

    def conv_b_items(si):
        def item(s0, r0):
            def run():
                lo = s0 * tt + r0
                acc_b = None
                for k in range(CONV_B):
                    off = PAD_B - HALO_B + k + r0
                    term = window(gext_ref, gsh_ref, 1, s0, off, row_chunk) * cbw[k:k + 1, :]
                    acc_b = term if acc_b is None else acc_b + term
                zb = acc_b.reshape(n, D_B) + cbb_ref[...]
                mu = jnp.mean(zb, axis=-1, keepdims=True)
                zc = zb - mu
                var = jnp.mean(zc * zc, axis=-1, keepdims=True)
                y = zc * lax.rsqrt(var + LN_EPS) * lng_ref[...] + lnb_ref[...]
                z_ref[lo:lo + n, D_A:] = (y * jax.nn.sigmoid(y)).astype(jnp.bfloat16)
            return run
        return [item(s0, r0) for s0, r0 in chunks_of(si)]

    def conv_a_item(si):
        def run():
            q0, f0, m = rows_of(si)
            c_a = proj_ref[f0:f0 + m, D_A:2 * D_A]
            v_a = proj_ref[f0:f0 + m, 2 * D_A:3 * D_A]
            uext_ref[:, PAD_A + q0:PAD_A + q0 + sub, :] = (c_a * v_a).reshape(nseq, sub, D_A)
            for r in range(SUBLANES - HALO_A, SUBLANES):
                ush_ref[r - (SUBLANES - HALO_A), :, q0:q0 + sub, :] = uext_ref[:, q0 + r:q0 + sub + r, :]
            for s0, r0 in chunks_of(si):
                lo = s0 * tt + r0
                acc_a = None
                for k in range(CONV_A):
                    off = PAD_A - HALO_A + k + r0
                    term = window(uext_ref, ush_ref, SUBLANES - HALO_A, s0, off, row_chunk) * caw[k:k + 1, :]
                    acc_a = term if acc_a is None else acc_a + term
                z_a = proj_ref[lo:lo + n, 0:D_A] * acc_a.reshape(n, D_A)
                z_ref[lo:lo + n, 0:D_A] = z_a.astype(jnp.bfloat16)
        return run

    def finish_item(si):
        def run():
            q0, f0, m = rows_of(si)
            x = x_ref[:, q0:q0 + sub, :].reshape(m, D_MODEL)
            x1 = x + jnp.dot(z_ref[f0:f0 + m, :], wout_ref[...], preferred_element_type=jnp.float32)
            x1_ref[f0:f0 + m, :] = x1
            h2 = _rms_scale(x1) * gffn_ref[...]
            h2_hi = h2.astype(jnp.bfloat16)
            h2p_ref[f0:f0 + m, :] = _pack_bf16_halves(h2)
            h2_lo = (h2 - h2_hi.astype(jnp.float32)).astype(jnp.bfloat16)
            both = jnp.dot(h2_hi, wr_ref[...], preferred_element_type=jnp.float32)
            cross = jnp.dot(h2_lo, wr_ref[:, 0:LANES], preferred_element_type=jnp.float32)
            logits = both[:, 0:LANES] + both[:, LANES:] + cross
            logits_t = jnp.transpose(logits)[0:ROUTER_ROWS] + br_ref[...]
            (e1, e2), (p1, p2) = _route(logits_t)
            iota = lax.broadcasted_iota(jnp.int32, (SUBLANES, m), 0)
            eid_ref[:, f0:f0 + m] = jnp.where(iota == 0, e1, jnp.where(iota == 1, e2, 0))
            p_ref[:, f0:f0 + m] = jnp.where(iota == 0, p1, jnp.where(iota == 1, p2, 0.0))
        return run

    def interleave(main, fill):
        for i, item in enumerate(main):
            item()
            for f in fill[i * len(fill) // len(main):(i + 1) * len(fill) // len(main)]:
                f()

    for si in range(n_sub):
        prep(si)
    for item in dot_b_items(0):
        item()
    shift_b(0)
    carry_over = []
    for si in range(n_sub):
        fill = carry_over + dot_a_items(si)
        if si + 1 < n_sub:
            fill = fill + dot_b_items(si + 1)
        interleave(conv_b_items(si), fill + [conv_a_item(si)])
        if si + 1 < n_sub:
            shift_b(si + 1)
        carry_over = [finish_item(si)]
    for item in carry_over:
        item()

    if carried:
        @pl.when(t == pl.num_programs(1) - 1)
        def _():
            na_ref[...] = uext_ref[:, PAD_A + tt - HALO_A:PAD_A + tt, :]
            nb_ref[...] = gext_ref[:, PAD_B + tt - HALO_B:PAD_B + tt, :]
        uext_ref[:, 0:PAD_A, :] = uext_ref[:, tt:tt + PAD_A, :]
        gext_ref[:, 0:PAD_B, :] = gext_ref[:, tt:tt + PAD_B, :]
    else:
        na_ref[...] = uext_ref[:, PAD_A + tt - HALO_A:PAD_A + tt, :]
        nb_ref[...] = gext_ref[:, PAD_B + tt - HALO_B:PAD_B + tt, :]


def _mixer_prompt_kernel(x_ref, *refs, **kw):
    _mixer_body(x_ref, None, None, *refs, **kw)


def _mixer_sample_kernel(x_ref, sa_ref, sb_ref, *refs, **kw):
    n_params = 11
    params = refs[:n_params]
    rest = refs[n_params + 4:]
    _mixer_body(x_ref, sa_ref, sb_ref, *params, *rest, **kw)


def _full(shape):
    return pl.BlockSpec(shape, lambda *_: (0,) * len(shape))


def _mixer_param_specs():
    return [
        _full((1, D_MODEL)),
        _full((D_MODEL, IN_COLS)),
        _full((CONV_A, D_A)),
        _full((CONV_B, D_B)),
        _full((1, D_B)),
        _full((1, D_B)),
        _full((1, D_B)),
        _full((D_MODEL, D_MODEL)),
        _full((1, D_MODEL)),
        _full((D_MODEL, 2 * LANES)),
        _full((ROUTER_ROWS, 1)),
    ]


def _mixer_scratch(nseq, tt):
    rows = nseq * tt
    return [
        pltpu.VMEM((rows, 3 * D_A), jnp.float32),
        pltpu.VMEM((nseq, PAD_A + tt, D_A), jnp.float32),
        pltpu.VMEM((nseq, PAD_B + tt, D_B), jnp.float32),
        pltpu.VMEM((rows, D_MODEL), jnp.bfloat16),
        pltpu.VMEM((HALO_A, nseq, PAD_A + tt - SUBLANES, D_A), jnp.float32),
        pltpu.VMEM((SUBLANES - 1, nseq, PAD_B + tt - SUBLANES, D_B), jnp.float32),
    ]


def _mixer_prompt(x, params, n_tokens_total, tt, seq_first, batch):
    seq = x.shape[1]
    n_t = seq // tt
    tok = lambda b, t: (b * n_t + t, 0)
    lane_tok = lambda b, t: (0, b * n_t + t)
    out_shape = [
        jax.ShapeDtypeStruct((n_tokens_total, D_MODEL), jnp.float32),
        jax.ShapeDtypeStruct((n_tokens_total, HALF), jnp.uint32),
        jax.ShapeDtypeStruct((SUBLANES, n_tokens_total), jnp.int32),
        jax.ShapeDtypeStruct((SUBLANES, n_tokens_total), jnp.float32),
        jax.ShapeDtypeStruct((batch, HALO_A, D_A), jnp.float32),
        jax.ShapeDtypeStruct((batch, HALO_B, D_B), jnp.float32),
    ]
    out_specs = [
        pl.BlockSpec((tt, D_MODEL), tok),
        pl.BlockSpec((tt, HALF), tok),
        pl.BlockSpec((SUBLANES, tt), lane_tok),
        pl.BlockSpec((SUBLANES, tt), lane_tok),
        pl.BlockSpec((1, HALO_A, D_A), lambda b, t: (b, 0, 0)),
        pl.BlockSpec((1, HALO_B, D_B), lambda b, t: (b, 0, 0)),
    ]
    return pl.pallas_call(
        functools.partial(_mixer_prompt_kernel, nseq=1, tt=tt, seq_chunk=1, row_chunk=64, n_sub=2),
        grid=(batch, n_t),
        in_specs=[pl.BlockSpec((1, tt, D_MODEL), lambda b, t: (b + seq_first, t, 0))] + _mixer_param_specs(),
        out_specs=out_specs,
        out_shape=out_shape,
        scratch_shapes=_mixer_scratch(1, tt),
        compiler_params=pltpu.CompilerParams(
            dimension_semantics=("arbitrary", "arbitrary"), vmem_limit_bytes=VMEM_LIMIT),
        name="mixer_prompt",
    )(x, *params)


def _mixer_sample(x, state_a, state_b, params, bufs, row_offset, nseq):
    batch, tt, _ = x.shape
    rows = nseq * tt
    first = row_offset // rows
    tok = lambda i: (first + i, 0)
    lane_tok = lambda i: (0, first + i)
    x1, h2p, eid, p = bufs
    out_shape = [
        jax.ShapeDtypeStruct(x1.shape, x1.dtype),
        jax.ShapeDtypeStruct(h2p.shape, h2p.dtype),
        jax.ShapeDtypeStruct(eid.shape, eid.dtype),
        jax.ShapeDtypeStruct(p.shape, p.dtype),
        jax.ShapeDtypeStruct((batch, HALO_A, D_A), jnp.float32),
        jax.ShapeDtypeStruct((batch, HALO_B, D_B), jnp.float32),
    ]
    out_specs = [
        pl.BlockSpec((rows, D_MODEL), tok),
        pl.BlockSpec((rows, HALF), tok),
        pl.BlockSpec((SUBLANES, rows), lane_tok),
        pl.BlockSpec((SUBLANES, rows), lane_tok),
        pl.BlockSpec((nseq, HALO_A, D_A), lambda i: (i, 0, 0)),
        pl.BlockSpec((nseq, HALO_B, D_B), lambda i: (i, 0, 0)),
    ]
    any_spec = pl.BlockSpec(memory_space=pl.ANY)
    in_specs = ([pl.BlockSpec((nseq, tt, D_MODEL), lambda i: (i, 0, 0)),
                 pl.BlockSpec((nseq, HALO_A, D_A), lambda i: (i, 0, 0)),
                 pl.BlockSpec((nseq, HALO_B, D_B), lambda i: (i, 0, 0))]
                + _mixer_param_specs() + [any_spec] * 4)
    n_in = len(in_specs)
    return pl.pallas_call(
        functools.partial(_mixer_sample_kernel, nseq=nseq, tt=tt, seq_chunk=8, row_chunk=tt, n_sub=1),
        grid=(batch // nseq,),
        in_specs=in_specs,
        out_specs=out_specs,
        out_shape=out_shape,
        scratch_shapes=_mixer_scratch(nseq, tt),
        input_output_aliases={n_in - 4: 0, n_in - 3: 1, n_in - 2: 2, n_in - 1: 3},
        compiler_params=pltpu.CompilerParams(
            dimension_semantics=("arbitrary",), vmem_limit_bytes=VMEM_LIMIT),
        name="mixer_sample",
    )(x, state_a, state_b, *params, x1, h2p, eid, p)


def _plan_kernel(eid_ref, pos_ref, seg_end_ref, carry_ref, start_ref, earlier_ref, *, tile):
    ph = pl.program_id(0)
    i = pl.program_id(1)
    tt = eid_ref.shape[1]

    @pl.when(i == 0)
    def _():
        @pl.when(ph == 1)
        def _():
            tiles = jnp.floor((carry_ref[...] + (tile - 0.5)) * (1.0 / tile))
            below = (lax.broadcasted_iota(jnp.int32, (N_EXPERTS, N_EXPERTS), 0)
                     > lax.broadcasted_iota(jnp.int32, (N_EXPERTS, N_EXPERTS), 1))
            start = jnp.dot(jnp.where(below, 1.0, 0.0).astype(jnp.bfloat16), tiles.astype(jnp.bfloat16),
                            preferred_element_type=jnp.float32) * tile
            start_ref[...] = start
            seg_end_ref[...] = (start + tiles * tile).astype(jnp.int32)
            earlier = (lax.broadcasted_iota(jnp.int32, (tt, tt), 0)
                       < lax.broadcasted_iota(jnp.int32, (tt, tt), 1))
            earlier_ref[...] = jnp.where(earlier, 1.0, 0.0).astype(jnp.bfloat16)
        carry_ref[...] = jnp.zeros_like(carry_ref)

    eid = eid_ref[...]
    experts = lax.broadcasted_iota(jnp.int32, (N_EXPERTS, tt), 0)
    oh0 = experts == eid[0:1]
    oh1 = experts == eid[1:2]
    oh = jnp.where(oh0 | oh1, 1.0, 0.0)

    @pl.when(ph == 1)
    def _():
        within = jnp.dot(oh.astype(jnp.bfloat16), earlier_ref[...], preferred_element_type=jnp.float32)
        slot_of = within + carry_ref[:, 0:1] + start_ref[:, 0:1]
        s0 = jnp.sum(jnp.where(oh0, slot_of, 0.0), axis=0, keepdims=True)
        s1 = jnp.sum(jnp.where(oh1, slot_of, 0.0), axis=0, keepdims=True)
        k = lax.broadcasted_iota(jnp.int32, (SUBLANES, tt), 0)
        pos_ref[...] = jnp.where(k == 0, s0, jnp.where(k == 1, s1, 0.0)).astype(jnp.int32)

    carry_ref[...] = carry_ref[...] + jnp.sum(oh, axis=1, keepdims=True)


def _plan(eid, tile):
    n_tokens = eid.shape[1]
    return pl.pallas_call(
        functools.partial(_plan_kernel, tile=tile),
        grid=(2, n_tokens // PLAN_TILE),
        in_specs=[pl.BlockSpec((SUBLANES, PLAN_TILE), lambda ph, i: (0, i))],
        out_specs=[pl.BlockSpec((SUBLANES, PLAN_TILE), lambda ph, i: (0, i * ph)),
                   pl.BlockSpec((N_EXPERTS, LANES), lambda ph, i: (0, 0))],
        out_shape=[jax.ShapeDtypeStruct((SUBLANES, n_tokens), jnp.int32),
                   jax.ShapeDtypeStruct((N_EXPERTS, LANES), jnp.int32)],
        scratch_shapes=[pltpu.VMEM((N_EXPERTS, LANES), jnp.float32),
                        pltpu.VMEM((N_EXPERTS, LANES), jnp.float32),
                        pltpu.VMEM((PLAN_TILE, PLAN_TILE), jnp.bfloat16)],
        compiler_params=pltpu.CompilerParams(dimension_semantics=("arbitrary", "arbitrary")),
        name="route_plan",
    )(eid)


def _sc_mesh():
    return plsc.VectorSubcoreMesh(core_axis_name="c", subcore_axis_name="s")


def _sc_dispatch_rows(table, pos, n_out):
    n_workers, n_batches, top_k, batch = pos.shape
    n_rows, words = table.shape
    assert n_workers * n_batches * batch == n_rows and n_batches >= 2

    @functools.partial(
        pl.kernel, mesh=_sc_mesh(),
        out_type=jax.ShapeDtypeStruct((n_out, words), table.dtype),
        scratch_types=[pltpu.VMEM((n_batches, top_k, batch), jnp.int32),
                       pltpu.VMEM((2, batch, words), table.dtype),
                       pltpu.SemaphoreType.DMA((2,)),
                       pltpu.SemaphoreType.DMA((2, top_k))],
    )
    def dispatch(table_hbm, pos_hbm, out_hbm, idx_v, rows_v, sem_in, sem_out):
        worker = lax.axis_index("s") * SC_CORES + lax.axis_index("c")
        pltpu.sync_copy(pos_hbm.at[worker], idx_v)

        def read(b):
            src = table_hbm.at[pl.ds((worker * n_batches + b) * batch, batch)]
            return pltpu.async_copy(src, rows_v.at[b % 2], sem_in.at[b % 2])

        def write(b):
            return [pltpu.async_copy(rows_v.at[b % 2], out_hbm.at[idx_v.at[b, k]], sem_out.at[b % 2, k])
                    for k in range(top_k)]

        reads = {0: read(0)}
        writes = {}
        for b in range(n_batches):
            reads[b].wait()
            if b + 1 < n_batches:
                if b >= 1:
                    for w in writes[b - 1]:
                        w.wait()
                reads[b + 1] = read(b + 1)
            writes[b] = write(b)
        for b in (n_batches - 2, n_batches - 1):
            for w in writes[b]:
                w.wait()

    return dispatch(table, pos)


def _sc_gather_rows(table, idx):
    n_workers, n_batches, batch = idx.shape
    words = table.shape[1]
    assert n_batches >= 2

    @functools.partial(
        pl.kernel, mesh=_sc_mesh(),
        out_type=jax.ShapeDtypeStruct((n_workers * n_batches * batch, words), table.dtype),
        scratch_types=[pltpu.VMEM((n_batches, batch), jnp.int32),
                       pltpu.VMEM((2, batch, words), table.dtype),
                       pltpu.SemaphoreType.DMA((2,)),
                       pltpu.SemaphoreType.DMA((2,))],
    )
    def gather(table_hbm, idx_hbm, out_hbm, idx_v, rows_v, sem_in, sem_out):
        worker = lax.axis_index("s") * SC_CORES + lax.axis_index("c")
        pltpu.sync_copy(idx_hbm.at[worker], idx_v)

        def read(b):
            return pltpu.async_copy(table_hbm.at[idx_v.at[b]], rows_v.at[b % 2], sem_in.at[b % 2])

        def write(b):
            dst = out_hbm.at[pl.ds((worker * n_batches + b) * batch, batch)]
            return pltpu.async_copy(rows_v.at[b % 2], dst, sem_out.at[b % 2])

        reads = {0: read(0)}
        writes = {}
        for b in range(n_batches):
            reads[b].wait()
            if b + 1 < n_batches:
                if b >= 1:
                    writes[b - 1].wait()
                reads[b + 1] = read(b + 1)
            writes[b] = write(b)
        writes[n_batches - 2].wait()
        writes[n_batches - 1].wait()

    return gather(table, idx)


def _experts_kernel(te_ref, nv_ref, xs_ref, wg_ref, wu_ref, wd_ref, ys_ref):
    i = pl.program_id(0)

    @pl.when(i < nv_ref[0])
    def _():
        hi, lo = _unpack_bf16_halves(xs_ref[...])
        hi = hi.astype(jnp.bfloat16)
        lo = lo.astype(jnp.bfloat16)
        wg = wg_ref[0].astype(jnp.bfloat16)
        wu = wu_ref[0].astype(jnp.bfloat16)
        wd = wd_ref[0].astype(jnp.bfloat16)
        gate = (jnp.dot(hi, wg[:HALF], preferred_element_type=jnp.float32)
                + jnp.dot(lo, wg[HALF:], preferred_element_type=jnp.float32))
        up = (jnp.dot(hi, wu[:HALF], preferred_element_type=jnp.float32)
              + jnp.dot(lo, wu[HALF:], preferred_element_type=jnp.float32))
        hid = (gate * jax.nn.sigmoid(gate) * up).astype(jnp.bfloat16)
        ys_ref[...] = _pack_bf16_halves(jnp.dot(hid, wd, preferred_element_type=jnp.float32))


def _experts(xs, tile_expert, n_valid, wg, wu, wd, tm):
    n_slots = xs.shape[0]
    row_block = lambda i, te, nv: (jnp.minimum(i, nv[0] - 1), 0)
    w_block = lambda i, te, nv: (te[i], 0, 0)
    return pl.pallas_call(
        _experts_kernel,
        grid_spec=pltpu.PrefetchScalarGridSpec(
            num_scalar_prefetch=2,
            grid=(n_slots // tm,),
            in_specs=[pl.BlockSpec((tm, HALF), row_block),
                      pl.BlockSpec((1, D_MODEL, D_EXPERT), w_block),
                      pl.BlockSpec((1, D_MODEL, D_EXPERT), w_block),
                      pl.BlockSpec((1, D_EXPERT, D_MODEL), w_block)],
            out_specs=pl.BlockSpec((tm, HALF), row_block),
        ),
        out_shape=jax.ShapeDtypeStruct((n_slots, HALF), jnp.uint32),
        compiler_params=pltpu.CompilerParams(
            dimension_semantics=("arbitrary",), vmem_limit_bytes=VMEM_LIMIT),
        name="experts",
    )(tile_expert, n_valid, xs, wg, wu, wd)


def _final_kernel(x1_ref, y0_ref, y1_ref, p_ref, gfin_ref, *rest):
    out_ref = rest[-1]
    pt = jnp.transpose(p_ref[...])
    p0 = pt[:, 0:1]
    p1 = pt[:, 1:2]
    a_hi, a_lo = _unpack_bf16_halves(y0_ref[...])
    b_hi, b_lo = _unpack_bf16_halves(y1_ref[...])
    x1 = x1_ref[...]
    x2_hi = x1[:, :HALF] + (p0 * a_hi + p1 * b_hi)
    x2_lo = x1[:, HALF:] + (p0 * a_lo + p1 * b_lo)
    ms = (jnp.sum(x2_hi * x2_hi, axis=-1, keepdims=True)
          + jnp.sum(x2_lo * x2_lo, axis=-1, keepdims=True)) / D_MODEL
    scale = lax.rsqrt(ms + RMS_EPS)
    g = gfin_ref[...]
    out_ref[:, :HALF] = x2_hi * scale * g[:, :HALF]
    out_ref[:, HALF:] = x2_lo * scale * g[:, HALF:]


def _final(x1, yk, p, gfin, row_offset, n_rows, tm, out_rows, out_offset, out_buf=None):
    n_tokens = x1.shape[0]
    first = row_offset // tm
    second = (n_tokens + row_offset) // tm
    out_first = out_offset // tm
    in_specs = [pl.BlockSpec((tm, D_MODEL), lambda i: (first + i, 0)),
                pl.BlockSpec((tm, HALF), lambda i: (first + i, 0)),
                pl.BlockSpec((tm, HALF), lambda i: (second + i, 0)),
                pl.BlockSpec((SUBLANES, tm), lambda i: (0, first + i)),
                pl.BlockSpec((1, D_MODEL), lambda i: (0, 0))]
    args = [x1, yk, yk, p, gfin]
    aliases = {}
    if out_buf is not None:
        in_specs.append(pl.BlockSpec(memory_space=pl.ANY))
        args.append(out_buf)
        aliases = {len(args) - 1: 0}
    return pl.pallas_call(
        _final_kernel,
        grid=(n_rows // tm,),
        in_specs=in_specs,
        out_specs=pl.BlockSpec((tm, D_MODEL), lambda i: (out_first + i, 0)),
        out_shape=jax.ShapeDtypeStruct((out_rows, D_MODEL), jnp.float32),
        input_output_aliases=aliases,
        compiler_params=pltpu.CompilerParams(
            dimension_semantics=("arbitrary",), vmem_limit_bytes=VMEM_LIMIT),
        name="final",
    )(*args)


def _expert_tile(n_tokens):
    mean_rows = TOP_K * n_tokens // N_EXPERTS
    return -(-(mean_rows * 6 // 5) // SC_BATCH) * SC_BATCH


def _routed_experts(h2p, eid, w_gate, w_up, w_down):
    n_tokens = h2p.shape[0]
    assert n_tokens % (SC_WORKERS * SC_DISPATCH_BATCH) == 0 and n_tokens % PLAN_TILE == 0
    assert (TOP_K * n_tokens) % (SC_WORKERS * SC_BATCH) == 0
    tile = _expert_tile(n_tokens)
    n_tiles = -(-(TOP_K * n_tokens + N_EXPERTS * (tile - 1)) // tile)
    pos, seg_end = _plan(eid, tile)
    pos = pos[:TOP_K]
    by_token = jnp.transpose(pos.reshape(TOP_K, SC_WORKERS, -1, SC_DISPATCH_BATCH), (1, 2, 0, 3))
    seg_end = seg_end[:, 0]
    n_valid = seg_end[N_EXPERTS - 1:] // tile
    tile_row = jnp.arange(n_tiles, dtype=jnp.int32) * tile
    tile_expert = jnp.minimum(jnp.sum(seg_end[None, :] <= tile_row[:, None], axis=1), N_EXPERTS - 1)
    xs = _sc_dispatch_rows(h2p, by_token, n_tiles * tile)
    ys = _experts(xs, tile_expert.astype(jnp.int32), n_valid, w_gate, w_up, w_down, tile)
    return _sc_gather_rows(ys, pos.reshape(SC_WORKERS, -1, SC_BATCH))


def kernel(x_prompt, x_sample, state_conv_a, state_conv_b, g_mix, w_in, conv_a_w, conv_b_w, conv_b_bias,
           ln_g, ln_b, w_out, g_ffn, w_coarse, b_coarse, w_fine, b_fine, w_gate, w_up, w_down, g_final):
    assert g_mix.shape[0] == 1, "single trunk layer"
    batch, seq, _ = x_prompt.shape
    dec_batch, dec_seq, _ = x_sample.shape
    n_prompt = batch * seq
    n_sample = dec_batch * dec_seq
    bf16 = jnp.bfloat16

    wr = jnp.concatenate([
        w_coarse[0], jnp.zeros((D_MODEL, SUBLANES - N_EXPERT_GROUPS), jnp.float32),
        jnp.transpose(w_fine[0], (1, 0, 2)).reshape(D_MODEL, N_EXPERTS),
        jnp.zeros((D_MODEL, LANES - ROUTER_ROWS), jnp.float32)], axis=1)
    wr_hi = wr.astype(bf16)
    wr_lo = (wr - wr_hi.astype(jnp.float32)).astype(bf16)
    wr_both = jnp.concatenate([wr_hi, wr_lo], axis=1)
    br = jnp.concatenate([
        b_coarse[0], jnp.full((SUBLANES - N_EXPERT_GROUPS,), NEG_BIG, jnp.float32),
        b_fine[0].reshape(N_EXPERTS)]).reshape(ROUTER_ROWS, 1)

    params = (g_mix, w_in[0].astype(bf16), conv_a_w[0], conv_b_w[0], conv_b_bias, ln_g, ln_b,
              w_out[0].astype(bf16), g_ffn, wr_both, br)

    experts = (w_gate[0], w_up[0], w_down[0])
    gfin = g_final.reshape(1, D_MODEL)

    bufs = _mixer_prompt(x_prompt, params, n_prompt + n_sample, MIXER_TILE, 0, batch)
    na_p, nb_p = bufs[4:]
    x1, h2p, eid, p, na_s, nb_s = _mixer_sample(
        x_sample, state_conv_a[0], state_conv_b[0], params, bufs[:4], n_prompt, nseq=32)
    yk = _routed_experts(h2p, eid, *experts)
    y_p = _final(x1, yk, p, gfin, 0, n_prompt, FINAL_TILE, n_prompt, 0)
    y_s = _final(x1, yk, p, gfin, n_prompt, n_sample, FINAL_TILE, n_sample, 0)
    return (y_p.reshape(batch, seq, D_MODEL), y_s.reshape(dec_batch, dec_seq, D_MODEL),
            na_p[None], nb_p[None], na_s[None], nb_s[None])
```

```python
import functools

import jax
import jax.numpy as jnp
from jax import lax
from jax.experimental import pallas as pl
from jax.experimental.pallas import tpu as pltpu
from jax.experimental.pallas import tpu_sc as plsc

D_MODEL = 1024
D_A = 512
D_B = 512
CONV_A = 3
CONV_B = 31
HALO_A = CONV_A - 1
HALO_B = CONV_B - 1
IN_COLS = 3 * D_A + 2 * D_B
N_EXPERT_GROUPS = 4
EXPERTS_PER_GROUP = 8
N_EXPERTS = N_EXPERT_GROUPS * EXPERTS_PER_GROUP
TOP_K = 2
D_EXPERT = D_MODEL // 4
RMS_EPS = 1e-6
LN_EPS = 1e-5

SUBLANES = 8
LANES = 128
PAD_A = SUBLANES
PAD_B = 32
ROUTER_ROWS = SUBLANES + N_EXPERTS
NEG_BIG = -1e30
VMEM_LIMIT = 56 * 1024 * 1024
HALF = D_MODEL // 2
HI_MASK = 0xFFFF0000

SC_CORES = 2
SC_SUBCORES = 16
SC_WORKERS = SC_CORES * SC_SUBCORES
SC_BATCH = 64
SC_DISPATCH_BATCH = 32

PLAN_TILE = 1024
FINAL_TILE = 1024
MIXER_TILE = 512


def _rms_scale(x):
    return x * lax.rsqrt(jnp.mean(x * x, axis=-1, keepdims=True) + RMS_EPS)


def _pack_bf16_halves(x):
    bits = lax.bitcast_convert_type(x.astype(jnp.bfloat16).astype(jnp.float32), jnp.uint32)
    return bits[:, :HALF] | (bits[:, HALF:] >> 16)


def _unpack_bf16_halves(w):
    hi = lax.bitcast_convert_type(w & jnp.uint32(HI_MASK), jnp.float32)
    lo = lax.bitcast_convert_type(w << 16, jnp.float32)
    return hi, lo


def _route(logits_t):
    rows = logits_t.shape[1]
    iota = lax.broadcasted_iota(jnp.int32, (SUBLANES, rows), 0)
    lc = logits_t[0:SUBLANES]
    cmax = jnp.max(lc, axis=0, keepdims=True)
    grp = jnp.min(jnp.where(lc == cmax, iota, SUBLANES), axis=0, keepdims=True)
    p_grp = 1.0 / jnp.sum(jnp.exp(lc - cmax), axis=0, keepdims=True)
    sel = logits_t[SUBLANES:2 * SUBLANES]
    for g in range(1, N_EXPERT_GROUPS):
        sel = jnp.where(grp == g, logits_t[(g + 1) * SUBLANES:(g + 2) * SUBLANES], sel)
    v1 = jnp.max(sel, axis=0, keepdims=True)
    i1 = jnp.min(jnp.where(sel == v1, iota, SUBLANES), axis=0, keepdims=True)
    sel2 = jnp.where(iota == i1, -jnp.inf, sel)
    v2 = jnp.max(sel2, axis=0, keepdims=True)
    i2 = jnp.min(jnp.where(sel2 == v2, iota, SUBLANES), axis=0, keepdims=True)
    e2 = jnp.exp(v2 - v1)
    den = 1.0 + e2
    p1 = p_grp / den
    p2 = p_grp * e2 / den
    base = grp * EXPERTS_PER_GROUP
    return (base + i1, base + i2), (p1, p2)


def _mixer_body(x_ref, sa_ref, sb_ref, gmix_ref, win_ref, caw_ref, cbw_ref, cbb_ref, lng_ref, lnb_ref,
                wout_ref, gffn_ref, wr_ref, br_ref,
                x1_ref, h2p_ref, eid_ref, p_ref, na_ref, nb_ref,
                proj_ref, uext_ref, gext_ref, z_ref, ush_ref, gsh_ref, *, nseq, tt, seq_chunk, row_chunk, n_sub):
    carried = sa_ref is None
    t = pl.program_id(1) if carried else None
    assert n_sub == 1 or nseq == 1
    sub = tt // n_sub

    if carried:
        @pl.when(t == 0)
        def _():
            uext_ref[:, 0:PAD_A, :] = jnp.zeros((nseq, PAD_A, D_A), jnp.float32)
            gext_ref[:, 0:PAD_B, :] = jnp.zeros((nseq, PAD_B, D_B), jnp.float32)
    else:
        uext_ref[:, PAD_A - HALO_A:PAD_A, :] = sa_ref[...]
        gext_ref[:, PAD_B - HALO_B:PAD_B, :] = sb_ref[...]

    def window(base_ref, shifted_ref, first_shift, s0, off, n_rows):
        r = off % SUBLANES
        a8 = off - r
        if r == 0:
            return base_ref[s0:s0 + seq_chunk, a8:a8 + n_rows, :]
        return shifted_ref[r - first_shift, s0:s0 + seq_chunk, a8:a8 + n_rows, :]

    n = seq_chunk * row_chunk
    col_chunk = 2 * LANES
    caw = caw_ref[...]
    cbw = cbw_ref[...]
    hs_bf = {}

    def rows_of(si):
        q0 = si * sub
        f0 = q0 if nseq == 1 else 0
        return q0, f0, nseq * sub

    def prep(si):
        q0, _, m = rows_of(si)
        x = x_ref[:, q0:q0 + sub, :].reshape(m, D_MODEL)
        hs_bf[si] = (_rms_scale(x) * gmix_ref[...]).astype(jnp.bfloat16)

    def dot_b_items(si):
        def item(c0):
            def run():
                q0, _, _ = rows_of(si)
                v_b = jnp.dot(hs_bf[si], win_ref[:, 3 * D_A + c0:3 * D_A + c0 + col_chunk],
                              preferred_element_type=jnp.float32)
                g_b = jnp.dot(hs_bf[si], win_ref[:, 3 * D_A + D_B + c0:3 * D_A + D_B + c0 + col_chunk],
                              preferred_element_type=jnp.float32)
                gext_ref[:, PAD_B + q0:PAD_B + q0 + sub, c0:c0 + col_chunk] = (
                    v_b * jax.nn.sigmoid(g_b)).reshape(nseq, sub, col_chunk)
            return run
        return [item(c0) for c0 in range(0, D_B, col_chunk)]

    def dot_a_items(si):
        def item(c0):
            def run():
                _, f0, m = rows_of(si)
                proj_ref[f0:f0 + m, c0:c0 + col_chunk] = jnp.dot(
                    hs_bf[si], win_ref[:, c0:c0 + col_chunk], preferred_element_type=jnp.float32)
            return run
        return [item(c0) for c0 in range(0, 3 * D_A, col_chunk)]

    def shift_b(si):
        q0, _, _ = rows_of(si)
        j_lo = 0 if si == 0 else q0 + PAD_B - SUBLANES
        j_hi = q0 + sub + PAD_B - SUBLANES
        for r in range(1, SUBLANES):
            gsh_ref[r - 1, :, j_lo:j_hi, :] = gext_ref[:, j_lo + r:j_hi + r, :]

    def chunks_of(si):
        q0, _, _ = rows_of(si)
        return [(s0, r0) for s0 in range(0, nseq, seq_chunk) for r0 in range(q0, q0 + sub, row_chunk)]

    def conv_b_items(si):
        def item(s0, r0):
            def run():
                lo = s0 * tt + r0
                acc_b = None
                for k in range(CONV_B):
                    off = PAD_B - HALO_B + k + r0
                    term = window(gext_ref, gsh_ref, 1, s0, off, row_chunk) * cbw[k:k + 1, :]
                    acc_b = term if acc_b is None else acc_b + term
                zb = acc_b.reshape(n, D_B) + cbb_ref[...]
                mu = jnp.mean(zb, axis=-1, keepdims=True)
                zc = zb - mu
                var = jnp.mean(zc * zc, axis=-1, keepdims=True)
                y = zc * lax.rsqrt(var + LN_EPS) * lng_ref[...] + lnb_ref[...]
                z_ref[lo:lo + n, D_A:] = (y * jax.nn.sigmoid(y)).astype(jnp.bfloat16)
            return run
        return [item(s0, r0) for s0, r0 in chunks_of(si)]

    def conv_a_item(si):
        def run():
            q0, f0, m = rows_of(si)
            c_a = proj_ref[f0:f0 + m, D_A:2 * D_A]
            v_a = proj_ref[f0:f0 + m, 2 * D_A:3 * D_A]
            uext_ref[:, PAD_A + q0:PAD_A + q0 + sub, :] = (c_a * v_a).reshape(nseq, sub, D_A)
            for r in range(SUBLANES - HALO_A, SUBLANES):
                ush_ref[r - (SUBLANES - HALO_A), :, q0:q0 + sub, :] = uext_ref[:, q0 + r:q0 + sub + r, :]
            for s0, r0 in chunks_of(si):
                lo = s0 * tt + r0
                acc_a = None
                for k in range(CONV_A):
                    off = PAD_A - HALO_A + k + r0
                    term = window(uext_ref, ush_ref, SUBLANES - HALO_A, s0, off, row_chunk) * caw[k:k + 1, :]
                    acc_a = term if acc_a is None else acc_a + term
                z_a = proj_ref[lo:lo + n, 0:D_A] * acc_a.reshape(n, D_A)
                z_ref[lo:lo + n, 0:D_A] = z_a.astype(jnp.bfloat16)
        return run

    def finish_item(si):
        def run():
            q0, f0, m = rows_of(si)
            x = x_ref[:, q0:q0 + sub, :].reshape(m, D_MODEL)
            x1 = x + jnp.dot(z_ref[f0:f0 + m, :], wout_ref[...], preferred_element_type=jnp.float32)
            x1_ref[f0:f0 + m, :] = x1
            h2 = _rms_scale(x1) * gffn_ref[...]
            h2_hi = h2.astype(jnp.bfloat16)
            h2p_ref[f0:f0 + m, :] = _pack_bf16_halves(h2)
            h2_lo = (h2 - h2_hi.astype(jnp.float32)).astype(jnp.bfloat16)
            both = jnp.dot(h2_hi, wr_ref[...], preferred_element_type=jnp.float32)
            cross = jnp.dot(h2_lo, wr_ref[:, 0:LANES], preferred_element_type=jnp.float32)
            logits = both[:, 0:LANES] + both[:, LANES:] + cross
            logits_t = jnp.transpose(logits)[0:ROUTER_ROWS] + br_ref[...]
            (e1, e2), (p1, p2) = _route(logits_t)
            iota = lax.broadcasted_iota(jnp.int32, (SUBLANES, m), 0)
            eid_ref[:, f0:f0 + m] = jnp.where(iota == 0, e1, jnp.where(iota == 1, e2, 0))
            p_ref[:, f0:f0 + m] = jnp.where(iota == 0, p1, jnp.where(iota == 1, p2, 0.0))
        return run

    def interleave(main, fill):
        for i, item in enumerate(main):
            item()
            for f in fill[i * len(fill) // len(main):(i + 1) * len(fill) // len(main)]:
                f()

    for si in range(n_sub):
        prep(si)
    for item in dot_b_items(0):
        item()
    shift_b(0)
    carry_over = []
    for si in range(n_sub):
        fill = carry_over + dot_a_items(si)
        if si + 1 < n_sub:
            fill = fill + dot_b_items(si + 1)
        interleave(conv_b_items(si), fill + [conv_a_item(si)])
        if si + 1 < n_sub:
            shift_b(si + 1)
        carry_over = [finish_item(si)]
    for item in carry_over:
        item()

    if carried:
        @pl.when(t == pl.num_programs(1) - 1)
        def _():
            na_ref[...] = uext_ref[:, PAD_A + tt - HALO_A:PAD_A + tt, :]
            nb_ref[...] = gext_ref[:, PAD_B + tt - HALO_B:PAD_B + tt, :]
        uext_ref[:, 0:PAD_A, :] = uext_ref[:, tt:tt + PAD_A, :]
        gext_ref[:, 0:PAD_B, :] = gext_ref[:, tt:tt + PAD_B, :]
    else:
        na_ref[...] = uext_ref[:, PAD_A + tt - HALO_A:PAD_A + tt, :]
        nb_ref[...] = gext_ref[:, PAD_B + tt - HALO_B:PAD_B + tt, :]


def _mixer_prompt_kernel(x_ref, *refs, **kw):
    _mixer_body(x_ref, None, None, *refs, **kw)


def _mixer_sample_kernel(x_ref, sa_ref, sb_ref, *refs, **kw):
    n_params = 11
    params = refs[:n_params]
    rest = refs[n_params + 4:]
    _mixer_body(x_ref, sa_ref, sb_ref, *params, *rest, **kw)


def _full(shape):
    return pl.BlockSpec(shape, lambda *_: (0,) * len(shape))


def _mixer_param_specs():
    return [
        _full((1, D_MODEL)),
        _full((D_MODEL, IN_COLS)),
        pl.BlockSpec((None, CONV_A, D_A), lambda *_: (0, 0, 0)),
        pl.BlockSpec((None, CONV_B, D_B), lambda *_: (0, 0, 0)),
        _full((1, D_B)),
        _full((1, D_B)),
        _full((1, D_B)),
        _full((D_MODEL, D_MODEL)),
        _full((1, D_MODEL)),
        _full((D_MODEL, 2 * LANES)),
        _full((ROUTER_ROWS, 1)),
    ]


def _mixer_scratch(nseq, tt):
    rows = nseq * tt
    return [
        pltpu.VMEM((rows, 3 * D_A), jnp.float32),
        pltpu.VMEM((nseq, PAD_A + tt, D_A), jnp.float32),
        pltpu.VMEM((nseq, PAD_B + tt, D_B), jnp.float32),
        pltpu.VMEM((rows, D_MODEL), jnp.bfloat16),
        pltpu.VMEM((HALO_A, nseq, PAD_A + tt - SUBLANES, D_A), jnp.float32),
        pltpu.VMEM((SUBLANES - 1, nseq, PAD_B + tt - SUBLANES, D_B), jnp.float32),
    ]


def _mixer_prompt(x, params, n_tokens_total, tt, seq_first, batch):
    seq = x.shape[1]
    n_t = seq // tt
    tok = lambda b, t: (b * n_t + t, 0)
    lane_tok = lambda b, t: (0, b * n_t + t)
    out_shape = [
        jax.ShapeDtypeStruct((n_tokens_total, D_MODEL), jnp.float32),
        jax.ShapeDtypeStruct((n_tokens_total, HALF), jnp.uint32),
        jax.ShapeDtypeStruct((SUBLANES, n_tokens_total), jnp.int32),
        jax.ShapeDtypeStruct((SUBLANES, n_tokens_total), jnp.float32),
        jax.ShapeDtypeStruct((batch, HALO_A, D_A), jnp.float32),
        jax.ShapeDtypeStruct((batch, HALO_B, D_B), jnp.float32),
    ]
    out_specs = [
        pl.BlockSpec((tt, D_MODEL), tok),
        pl.BlockSpec((tt, HALF), tok),
        pl.BlockSpec((SUBLANES, tt), lane_tok),
        pl.BlockSpec((SUBLANES, tt), lane_tok),
        pl.BlockSpec((1, HALO_A, D_A), lambda b, t: (b, 0, 0)),
        pl.BlockSpec((1, HALO_B, D_B), lambda b, t: (b, 0, 0)),
    ]
    return pl.pallas_call(
        functools.partial(_mixer_prompt_kernel, nseq=1, tt=tt, seq_chunk=1, row_chunk=64, n_sub=2),
        grid=(batch, n_t),
        in_specs=[pl.BlockSpec((1, tt, D_MODEL), lambda b, t: (b + seq_first, t, 0))] + _mixer_param_specs(),
        out_specs=out_specs,
        out_shape=out_shape,
        scratch_shapes=_mixer_scratch(1, tt),
        compiler_params=pltpu.CompilerParams(
            dimension_semantics=("arbitrary", "arbitrary"), vmem_limit_bytes=VMEM_LIMIT),
        name="mixer_prompt",
    )(x, *params)


def _mixer_sample(x, state_a, state_b, params, bufs, row_offset, nseq):
    batch, tt, _ = x.shape
    rows = nseq * tt
    first = row_offset // rows
    tok = lambda i: (first + i, 0)
    lane_tok = lambda i: (0, first + i)
    x1, h2p, eid, p = bufs
    out_shape = [
        jax.ShapeDtypeStruct(x1.shape, x1.dtype),
        jax.ShapeDtypeStruct(h2p.shape, h2p.dtype),
        jax.ShapeDtypeStruct(eid.shape, eid.dtype),
        jax.ShapeDtypeStruct(p.shape, p.dtype),
        jax.ShapeDtypeStruct((batch, HALO_A, D_A), jnp.float32),
        jax.ShapeDtypeStruct((batch, HALO_B, D_B), jnp.float32),
    ]
    out_specs = [
        pl.BlockSpec((rows, D_MODEL), tok),
        pl.BlockSpec((rows, HALF), tok),
        pl.BlockSpec((SUBLANES, rows), lane_tok),
        pl.BlockSpec((SUBLANES, rows), lane_tok),
        pl.BlockSpec((nseq, HALO_A, D_A), lambda i: (i, 0, 0)),
        pl.BlockSpec((nseq, HALO_B, D_B), lambda i: (i, 0, 0)),
    ]
    any_spec = pl.BlockSpec(memory_space=pl.ANY)
    in_specs = ([pl.BlockSpec((nseq, tt, D_MODEL), lambda i: (i, 0, 0)),
                 pl.BlockSpec((None, nseq, HALO_A, D_A), lambda i: (0, i, 0, 0)),
                 pl.BlockSpec((None, nseq, HALO_B, D_B), lambda i: (0, i, 0, 0))]
                + _mixer_param_specs() + [any_spec] * 4)
    n_in = len(in_specs)
    return pl.pallas_call(
        functools.partial(_mixer_sample_kernel, nseq=nseq, tt=tt, seq_chunk=8, row_chunk=tt, n_sub=1),
        grid=(batch // nseq,),
        in_specs=in_specs,
        out_specs=out_specs,
        out_shape=out_shape,
        scratch_shapes=_mixer_scratch(nseq, tt),
        input_output_aliases={n_in - 4: 0, n_in - 3: 1, n_in - 2: 2, n_in - 1: 3},
        compiler_params=pltpu.CompilerParams(
            dimension_semantics=("arbitrary",), vmem_limit_bytes=VMEM_LIMIT),
        name="mixer_sample",
    )(x, state_a, state_b, *params, x1, h2p, eid, p)


def _plan_kernel(eid_ref, pos_ref, seg_end_ref, carry_ref, start_ref, earlier_ref, *, tile):
    ph = pl.program_id(0)
    i = pl.program_id(1)
    tt = eid_ref.shape[1]

    @pl.when(i == 0)
    def _():
        @pl.when(ph == 1)
        def _():
            tiles = jnp.floor((carry_ref[...] + (tile - 0.5)) * (1.0 / tile))
            below = (lax.broadcasted_iota(jnp.int32, (N_EXPERTS, N_EXPERTS), 0)
                     > lax.broadcasted_iota(jnp.int32, (N_EXPERTS, N_EXPERTS), 1))
            start = jnp.dot(jnp.where(below, 1.0, 0.0).astype(jnp.bfloat16), tiles.astype(jnp.bfloat16),
                            preferred_element_type=jnp.float32) * tile
            start_ref[...] = start
            seg_end_ref[...] = (start + tiles * tile).astype(jnp.int32)
            earlier = (lax.broadcasted_iota(jnp.int32, (tt, tt), 0)
                       < lax.broadcasted_iota(jnp.int32, (tt, tt), 1))
            earlier_ref[...] = jnp.where(earlier, 1.0, 0.0).astype(jnp.bfloat16)
        carry_ref[...] = jnp.zeros_like(carry_ref)

    eid = eid_ref[...]
    experts = lax.broadcasted_iota(jnp.int32, (N_EXPERTS, tt), 0)
    oh0 = experts == eid[0:1]
    oh1 = experts == eid[1:2]
    oh = jnp.where(oh0 | oh1, 1.0, 0.0)

    @pl.when(ph == 1)
    def _():
        within = jnp.dot(oh.astype(jnp.bfloat16), earlier_ref[...], preferred_element_type=jnp.float32)
        slot_of = within + carry_ref[:, 0:1] + start_ref[:, 0:1]
        s0 = jnp.sum(jnp.where(oh0, slot_of, 0.0), axis=0, keepdims=True)
        s1 = jnp.sum(jnp.where(oh1, slot_of, 0.0), axis=0, keepdims=True)
        k = lax.broadcasted_iota(jnp.int32, (SUBLANES, tt), 0)
        pos_ref[...] = jnp.where(k == 0, s0, jnp.where(k == 1, s1, 0.0)).astype(jnp.int32)

    carry_ref[...] = carry_ref[...] + jnp.sum(oh, axis=1, keepdims=True)


def _plan(eid, tile):
    n_tokens = eid.shape[1]
    return pl.pallas_call(
        functools.partial(_plan_kernel, tile=tile),
        grid=(2, n_tokens // PLAN_TILE),
        in_specs=[pl.BlockSpec((SUBLANES, PLAN_TILE), lambda ph, i: (0, i))],
        out_specs=[pl.BlockSpec((SUBLANES, PLAN_TILE), lambda ph, i: (0, i * ph)),
                   pl.BlockSpec((N_EXPERTS, LANES), lambda ph, i: (0, 0))],
        out_shape=[jax.ShapeDtypeStruct((SUBLANES, n_tokens), jnp.int32),
                   jax.ShapeDtypeStruct((N_EXPERTS, LANES), jnp.int32)],
        scratch_shapes=[pltpu.VMEM((N_EXPERTS, LANES), jnp.float32),
                        pltpu.VMEM((N_EXPERTS, LANES), jnp.float32),
                        pltpu.VMEM((PLAN_TILE, PLAN_TILE), jnp.bfloat16)],
        compiler_params=pltpu.CompilerParams(dimension_semantics=("arbitrary", "arbitrary")),
        name="route_plan",
    )(eid)


def _sc_mesh():
    return plsc.VectorSubcoreMesh(core_axis_name="c", subcore_axis_name="s")


def _sc_dispatch_rows(table, pos, n_out):
    n_workers, n_batches, top_k, batch = pos.shape
    n_rows, words = table.shape
    assert n_workers * n_batches * batch == n_rows and n_batches >= 2

    @functools.partial(
        pl.kernel, mesh=_sc_mesh(),
        out_type=jax.ShapeDtypeStruct((n_out, words), table.dtype),
        scratch_types=[pltpu.VMEM((n_batches, top_k, batch), jnp.int32),
                       pltpu.VMEM((2, batch, words), table.dtype),
                       pltpu.SemaphoreType.DMA((2,)),
                       pltpu.SemaphoreType.DMA((2, top_k))],
    )
    def dispatch(table_hbm, pos_hbm, out_hbm, idx_v, rows_v, sem_in, sem_out):
        worker = lax.axis_index("s") * SC_CORES + lax.axis_index("c")
        pltpu.sync_copy(pos_hbm.at[worker], idx_v)

        def read(b):
            src = table_hbm.at[pl.ds((worker * n_batches + b) * batch, batch)]
            return pltpu.async_copy(src, rows_v.at[b % 2], sem_in.at[b % 2])

        def write(b):
            return [pltpu.async_copy(rows_v.at[b % 2], out_hbm.at[idx_v.at[b, k]], sem_out.at[b % 2, k])
                    for k in range(top_k)]

        reads = {0: read(0)}
        writes = {}
        for b in range(n_batches):
            reads[b].wait()
            if b + 1 < n_batches:
                if b >= 1:
                    for w in writes[b - 1]:
                        w.wait()
                reads[b + 1] = read(b + 1)
            writes[b] = write(b)
        for b in (n_batches - 2, n_batches - 1):
            for w in writes[b]:
                w.wait()

    return dispatch(table, pos)


def _sc_gather_rows(table, idx):
    n_workers, n_batches, batch = idx.shape
    words = table.shape[1]
    assert n_batches >= 2

    @functools.partial(
        pl.kernel, mesh=_sc_mesh(),
        out_type=jax.ShapeDtypeStruct((n_workers * n_batches * batch, words), table.dtype),
        scratch_types=[pltpu.VMEM((n_batches, batch), jnp.int32),
                       pltpu.VMEM((2, batch, words), table.dtype),
                       pltpu.SemaphoreType.DMA((2,)),
                       pltpu.SemaphoreType.DMA((2,))],
    )
    def gather(table_hbm, idx_hbm, out_hbm, idx_v, rows_v, sem_in, sem_out):
        worker = lax.axis_index("s") * SC_CORES + lax.axis_index("c")
        pltpu.sync_copy(idx_hbm.at[worker], idx_v)

        def read(b):
            return pltpu.async_copy(table_hbm.at[idx_v.at[b]], rows_v.at[b % 2], sem_in.at[b % 2])

        def write(b):
            dst = out_hbm.at[pl.ds((worker * n_batches + b) * batch, batch)]
            return pltpu.async_copy(rows_v.at[b % 2], dst, sem_out.at[b % 2])

        reads = {0: read(0)}
        writes = {}
        for b in range(n_batches):
            reads[b].wait()
            if b + 1 < n_batches:
                if b >= 1:
                    writes[b - 1].wait()
                reads[b + 1] = read(b + 1)
            writes[b] = write(b)
        writes[n_batches - 2].wait()
        writes[n_batches - 1].wait()

    return gather(table, idx)


def _experts_kernel(te_ref, xs_ref, wg_ref, wu_ref, wd_ref, ys_ref):
    hi, lo = _unpack_bf16_halves(xs_ref[...])
    hi = hi.astype(jnp.bfloat16)
    lo = lo.astype(jnp.bfloat16)
    wg = wg_ref[0].astype(jnp.bfloat16)
    wu = wu_ref[0].astype(jnp.bfloat16)
    wd = wd_ref[0].astype(jnp.bfloat16)
    gate = (jnp.dot(hi, wg[:HALF], preferred_element_type=jnp.float32)
            + jnp.dot(lo, wg[HALF:], preferred_element_type=jnp.float32))
    up = (jnp.dot(hi, wu[:HALF], preferred_element_type=jnp.float32)
          + jnp.dot(lo, wu[HALF:], preferred_element_type=jnp.float32))
    hid = (gate * jax.nn.sigmoid(gate) * up).astype(jnp.bfloat16)
    ys_ref[...] = _pack_bf16_halves(jnp.dot(hid, wd, preferred_element_type=jnp.float32))


def _experts(xs, tile_expert, n_valid, wg, wu, wd, tm):
    n_slots = xs.shape[0]
    row_block = lambda i, te: (i, 0)
    w_block = lambda i, te: (te[i], 0, 0)
    return pl.pallas_call(
        _experts_kernel,
        grid_spec=pltpu.PrefetchScalarGridSpec(
            num_scalar_prefetch=1,
            grid=(n_valid,),
            in_specs=[pl.BlockSpec((tm, HALF), row_block),
                      pl.BlockSpec((1, D_MODEL, D_EXPERT), w_block),
                      pl.BlockSpec((1, D_MODEL, D_EXPERT), w_block),
                      pl.BlockSpec((1, D_EXPERT, D_MODEL), w_block)],
            out_specs=pl.BlockSpec((tm, HALF), row_block),
        ),
        out_shape=jax.ShapeDtypeStruct((n_slots, HALF), jnp.uint32),
        compiler_params=pltpu.CompilerParams(
            dimension_semantics=("arbitrary",), vmem_limit_bytes=VMEM_LIMIT),
        name="experts",
    )(tile_expert, xs, wg, wu, wd)


def _final_kernel(x1_ref, y0_ref, y1_ref, p_ref, gfin_ref, *rest):
    out_ref = rest[-1]
    pt = jnp.transpose(p_ref[...])
    p0 = pt[:, 0:1]
    p1 = pt[:, 1:2]
    a_hi, a_lo = _unpack_bf16_halves(y0_ref[...])
    b_hi, b_lo = _unpack_bf16_halves(y1_ref[...])
    x1 = x1_ref[...]
    x2_hi = x1[:, :HALF] + (p0 * a_hi + p1 * b_hi)
    x2_lo = x1[:, HALF:] + (p0 * a_lo + p1 * b_lo)
    ms = (jnp.sum(x2_hi * x2_hi, axis=-1, keepdims=True)
          + jnp.sum(x2_lo * x2_lo, axis=-1, keepdims=True)) / D_MODEL
    scale = lax.rsqrt(ms + RMS_EPS)
    g = gfin_ref[...]
    out_ref[:, :HALF] = x2_hi * scale * g[:, :HALF]
    out_ref[:, HALF:] = x2_lo * scale * g[:, HALF:]


def _final(x1, yk, p, gfin, row_offset, n_rows, tm, out_rows, out_offset, out_buf=None):
    n_tokens = x1.shape[0]
    first = row_offset // tm
    second = (n_tokens + row_offset) // tm
    out_first = out_offset // tm
    in_specs = [pl.BlockSpec((tm, D_MODEL), lambda i: (first + i, 0)),
                pl.BlockSpec((tm, HALF), lambda i: (first + i, 0)),
                pl.BlockSpec((tm, HALF), lambda i: (second + i, 0)),
                pl.BlockSpec((SUBLANES, tm), lambda i: (0, first + i)),
                pl.BlockSpec((1, D_MODEL), lambda i: (0, 0))]
    args = [x1, yk, yk, p, gfin]
    aliases = {}
    if out_buf is not None:
        in_specs.append(pl.BlockSpec(memory_space=pl.ANY))
        args.append(out_buf)
        aliases = {len(args) - 1: 0}
    return pl.pallas_call(
        _final_kernel,
        grid=(n_rows // tm,),
        in_specs=in_specs,
        out_specs=pl.BlockSpec((tm, D_MODEL), lambda i: (out_first + i, 0)),
        out_shape=jax.ShapeDtypeStruct((out_rows, D_MODEL), jnp.float32),
        input_output_aliases=aliases,
        compiler_params=pltpu.CompilerParams(
            dimension_semantics=("arbitrary",), vmem_limit_bytes=VMEM_LIMIT),
        name="final",
    )(*args)


def _expert_tile(n_tokens):
    mean_rows = TOP_K * n_tokens // N_EXPERTS
    return -(-(mean_rows * 6 // 5) // SC_BATCH) * SC_BATCH


def _routed_experts(h2p, eid, w_gate, w_up, w_down):
    n_tokens = h2p.shape[0]
    assert n_tokens % (SC_WORKERS * SC_DISPATCH_BATCH) == 0 and n_tokens % PLAN_TILE == 0
    assert (TOP_K * n_tokens) % (SC_WORKERS * SC_BATCH) == 0
    tile = _expert_tile(n_tokens)
    n_tiles = -(-(TOP_K * n_tokens + N_EXPERTS * (tile - 1)) // tile)
    pos, seg_end = _plan(eid, tile)
    pos = pos[:TOP_K]
    by_token = jnp.transpose(pos.reshape(TOP_K, SC_WORKERS, -1, SC_DISPATCH_BATCH), (1, 2, 0, 3))
    seg_end = seg_end[:, 0]
    n_valid = seg_end[N_EXPERTS - 1] // tile
    tile_row = jnp.arange(n_tiles, dtype=jnp.int32) * tile
    tile_expert = jnp.minimum(jnp.sum(seg_end[None, :] <= tile_row[:, None], axis=1), N_EXPERTS - 1)
    xs = _sc_dispatch_rows(h2p, by_token, n_tiles * tile)
    ys = _experts(xs, tile_expert.astype(jnp.int32), n_valid, w_gate, w_up, w_down, tile)
    return _sc_gather_rows(ys, pos.reshape(SC_WORKERS, -1, SC_BATCH))


def kernel(x_prompt, x_sample, state_conv_a, state_conv_b, g_mix, w_in, conv_a_w, conv_b_w, conv_b_bias,
           ln_g, ln_b, w_out, g_ffn, w_coarse, b_coarse, w_fine, b_fine, w_gate, w_up, w_down, g_final):
    assert g_mix.shape[0] == 1, "single trunk layer"
    batch, seq, _ = x_prompt.shape
    dec_batch, dec_seq, _ = x_sample.shape
    n_prompt = batch * seq
    n_sample = dec_batch * dec_seq
    bf16 = jnp.bfloat16

    wr = jnp.concatenate([
        w_coarse[0], jnp.zeros((D_MODEL, SUBLANES - N_EXPERT_GROUPS), jnp.float32),
        jnp.transpose(w_fine[0], (1, 0, 2)).reshape(D_MODEL, N_EXPERTS),
        jnp.zeros((D_MODEL, LANES - ROUTER_ROWS), jnp.float32)], axis=1)
    wr_hi = wr.astype(bf16)
    wr_lo = (wr - wr_hi.astype(jnp.float32)).astype(bf16)
    wr_both = jnp.concatenate([wr_hi, wr_lo], axis=1)
    br = jnp.concatenate([
        b_coarse[0], jnp.full((SUBLANES - N_EXPERT_GROUPS,), NEG_BIG, jnp.float32),
        b_fine[0].reshape(N_EXPERTS)]).reshape(ROUTER_ROWS, 1)

    params = (g_mix, w_in[0].astype(bf16), conv_a_w, conv_b_w, conv_b_bias, ln_g, ln_b,
              w_out[0].astype(bf16), g_ffn, wr_both, br)

    experts = (w_gate[0], w_up[0], w_down[0])
    gfin = g_final.reshape(1, D_MODEL)

    bufs = _mixer_prompt(x_prompt, params, n_prompt + n_sample, MIXER_TILE, 0, batch)
    na_p, nb_p = bufs[4:]
    x1, h2p, eid, p, na_s, nb_s = _mixer_sample(
        x_sample, state_conv_a, state_conv_b, params, bufs[:4], n_prompt, nseq=32)
    yk = _routed_experts(h2p, eid, *experts)
    y_p = _final(x1, yk, p, gfin, 0, n_prompt, FINAL_TILE, n_prompt, 0)
    y_s = _final(x1, yk, p, gfin, n_prompt, n_sample, FINAL_TILE, n_sample, 0)
    return (y_p.reshape(batch, seq, D_MODEL), y_s.reshape(dec_batch, dec_seq, D_MODEL),
            na_p[None], nb_p[None], na_s[None], nb_s[None])
```

```python
import functools
import math

import jax
import jax.numpy as jnp
from jax import lax
from jax.experimental import pallas as pl
from jax.experimental.pallas import tpu as pltpu
from jax.experimental.pallas import tpu_sc as plsc

D_MODEL = 1024
D_A = 512
D_B = 512
CONV_A = 3
CONV_B = 31
HALO_A = CONV_A - 1
HALO_B = CONV_B - 1
IN_COLS = 3 * D_A + 2 * D_B
N_EXPERT_GROUPS = 4
EXPERTS_PER_GROUP = 8
N_EXPERTS = N_EXPERT_GROUPS * EXPERTS_PER_GROUP
TOP_K = 2
D_EXPERT = D_MODEL // 4
RMS_EPS = 1e-6
LN_EPS = 1e-5

SUBLANES = 8
LANES = 128
PAD_A = SUBLANES
PAD_B = 32
ROUTER_ROWS = SUBLANES + N_EXPERTS
NEG_BIG = -1e30
VMEM_LIMIT = 56 * 1024 * 1024
HALF = D_MODEL // 2
HI_MASK = 0xFFFF0000

SC_CORES = 2
SC_SUBCORES = 16
SC_WORKERS = SC_CORES * SC_SUBCORES
SC_BATCH = 64
SC_DISPATCH_BATCH = 32

PLAN_TILE = 1024
FINAL_TILE = 1024
MIXER_TILE = 512


def _rms_scale(x):
    return x * lax.rsqrt(jnp.mean(x * x, axis=-1, keepdims=True) + RMS_EPS)


def _pack_bf16_halves(x):
    bits = lax.bitcast_convert_type(x.astype(jnp.bfloat16).astype(jnp.float32), jnp.uint32)
    return bits[:, :HALF] | (bits[:, HALF:] >> 16)


def _unpack_bf16_halves(w):
    hi = lax.bitcast_convert_type(w & jnp.uint32(HI_MASK), jnp.float32)
    lo = lax.bitcast_convert_type(w << 16, jnp.float32)
    return hi, lo


def _route(logits_t):
    rows = logits_t.shape[1]
    iota = lax.broadcasted_iota(jnp.int32, (SUBLANES, rows), 0)
    lc = logits_t[0:SUBLANES]
    cmax = jnp.max(lc, axis=0, keepdims=True)
    grp = jnp.min(jnp.where(lc == cmax, iota, SUBLANES), axis=0, keepdims=True)
    p_grp = 1.0 / jnp.sum(jnp.exp(lc - cmax), axis=0, keepdims=True)
    sel = logits_t[SUBLANES:2 * SUBLANES]
    for g in range(1, N_EXPERT_GROUPS):
        sel = jnp.where(grp == g, logits_t[(g + 1) * SUBLANES:(g + 2) * SUBLANES], sel)
    v1 = jnp.max(sel, axis=0, keepdims=True)
    i1 = jnp.min(jnp.where(sel == v1, iota, SUBLANES), axis=0, keepdims=True)
    sel2 = jnp.where(iota == i1, -jnp.inf, sel)
    v2 = jnp.max(sel2, axis=0, keepdims=True)
    i2 = jnp.min(jnp.where(sel2 == v2, iota, SUBLANES), axis=0, keepdims=True)
    e2 = jnp.exp(v2 - v1)
    den = 1.0 + e2
    p1 = p_grp / den
    p2 = p_grp * e2 / den
    base = grp * EXPERTS_PER_GROUP
    return (base + i1, base + i2), (p1, p2)


def _mixer_body(x_ref, sa_ref, sb_ref, gmix_ref, win_ref, caw_ref, cbw_ref, cbb_ref, lng_ref, lnb_ref,
                wout_ref, gffn_ref, wr_ref, br_ref,
                x1_ref, h2p_ref, eid_ref, p_ref, na_ref, nb_ref,
                proj_ref, uext_ref, gext_ref, z_ref, ush_ref, gsh_ref, *, nseq, tt, seq_chunk, row_chunk, n_sub):
    carried = sa_ref is None
    t = pl.program_id(1) if carried else None
    assert n_sub == 1 or nseq == 1
    sub = tt // n_sub

    if carried:
        @pl.when(t == 0)
        def _():
            uext_ref[:, 0:PAD_A, :] = jnp.zeros((nseq, PAD_A, D_A), jnp.float32)
            gext_ref[:, 0:PAD_B, :] = jnp.zeros((nseq, PAD_B, D_B), jnp.float32)
    else:
        uext_ref[:, PAD_A - HALO_A:PAD_A, :] = sa_ref[...]
        gext_ref[:, PAD_B - HALO_B:PAD_B, :] = sb_ref[...]

    def window(base_ref, shifted_ref, first_shift, s0, off, n_rows):
        r = off % SUBLANES
        a8 = off - r
        if r == 0:
            return base_ref[s0:s0 + seq_chunk, a8:a8 + n_rows, :]
        return shifted_ref[r - first_shift, s0:s0 + seq_chunk, a8:a8 + n_rows, :]

    n = seq_chunk * row_chunk
    col_chunk = 2 * LANES
    caw = caw_ref[...]
    cbw = cbw_ref[...]
    hs_bf = {}

    def rows_of(si):
        q0 = si * sub
        f0 = q0 if nseq == 1 else 0
        return q0, f0, nseq * sub

    def prep(si):
        q0, _, m = rows_of(si)
        x = x_ref[:, q0:q0 + sub, :].reshape(m, D_MODEL)
        hs_bf[si] = (_rms_scale(x) * gmix_ref[...]).astype(jnp.bfloat16)

    def dot_b_items(si):
        def item(c0):
            def run():
                q0, _, _ = rows_of(si)
                v_b = jnp.dot(hs_bf[si], win_ref[:, 3 * D_A + c0:3 * D_A + c0 + col_chunk],
                              preferred_element_type=jnp.float32)
                g_b = jnp.dot(hs_bf[si], win_ref[:, 3 * D_A + D_B + c0:3 * D_A + D_B + c0 + col_chunk],
                              preferred_element_type=jnp.float32)
                gext_ref[:, PAD_B + q0:PAD_B + q0 + sub, c0:c0 + col_chunk] = (
                    v_b * jax.nn.sigmoid(g_b)).reshape(nseq, sub, col_chunk)
            return run
        return [item(c0) for c0 in range(0, D_B, col_chunk)]

    def dot_a_items(si):
        def item(c0):
            def run():
                _, f0, m = rows_of(si)
                proj_ref[f0:f0 + m, c0:c0 + col_chunk] = jnp.dot(
                    hs_bf[si], win_ref[:, c0:c0 + col_chunk], preferred_element_type=jnp.float32)
            return run
        return [item(c0) for c0 in range(0, 3 * D_A, col_chunk)]

    def shift_b(si):
        q0, _, _ = rows_of(si)
        j_lo = 0 if si == 0 else q0 + PAD_B - SUBLANES
        j_hi = q0 + sub + PAD_B - SUBLANES
        for r in range(1, SUBLANES):
            gsh_ref[r - 1, :, j_lo:j_hi, :] = gext_ref[:, j_lo + r:j_hi + r, :]

    def chunks_of(si):
        q0, _, _ = rows_of(si)
        return [(s0, r0) for s0 in range(0, nseq, seq_chunk) for r0 in range(q0, q0 + sub, row_chunk)]

    def conv_b_items(si):
        def item(s0, r0):
            def run():
                lo = s0 * tt + r0
                acc_b = None
                for k in range(CONV_B):
                    off = PAD_B - HALO_B + k + r0
                    term = window(gext_ref, gsh_ref, 1, s0, off, row_chunk) * cbw[k:k + 1, :]
                    acc_b = term if acc_b is None else acc_b + term
                zb = acc_b.reshape(n, D_B) + cbb_ref[...]
                mu = jnp.mean(zb, axis=-1, keepdims=True)
                zc = zb - mu
                var = jnp.mean(zc * zc, axis=-1, keepdims=True)
                y = zc * lax.rsqrt(var + LN_EPS) * lng_ref[...] + lnb_ref[...]
                z_ref[lo:lo + n, D_A:] = (y * jax.nn.sigmoid(y)).astype(jnp.bfloat16)
            return run
        return [item(s0, r0) for s0, r0 in chunks_of(si)]

    def conv_a_item(si):
        def run():
            q0, f0, m = rows_of(si)
            c_a = proj_ref[f0:f0 + m, D_A:2 * D_A]
            v_a = proj_ref[f0:f0 + m, 2 * D_A:3 * D_A]
            uext_ref[:, PAD_A + q0:PAD_A + q0 + sub, :] = (c_a * v_a).reshape(nseq, sub, D_A)
            for r in range(SUBLANES - HALO_A, SUBLANES):
                ush_ref[r - (SUBLANES - HALO_A), :, q0:q0 + sub, :] = uext_ref[:, q0 + r:q0 + sub + r, :]
            for s0, r0 in chunks_of(si):
                lo = s0 * tt + r0
                acc_a = None
                for k in range(CONV_A):
                    off = PAD_A - HALO_A + k + r0
                    term = window(uext_ref, ush_ref, SUBLANES - HALO_A, s0, off, row_chunk) * caw[k:k + 1, :]
                    acc_a = term if acc_a is None else acc_a + term
                z_a = proj_ref[lo:lo + n, 0:D_A] * acc_a.reshape(n, D_A)
                z_ref[lo:lo + n, 0:D_A] = z_a.astype(jnp.bfloat16)
        return run

    def finish_item(si):
        def run():
            q0, f0, m = rows_of(si)
            x = x_ref[:, q0:q0 + sub, :].reshape(m, D_MODEL)
            x1 = x + jnp.dot(z_ref[f0:f0 + m, :], wout_ref[...], preferred_element_type=jnp.float32)
            x1_ref[f0:f0 + m, :] = x1
            h2 = _rms_scale(x1) * gffn_ref[...]
            h2_hi = h2.astype(jnp.bfloat16)
            h2p_ref[f0:f0 + m, :] = _pack_bf16_halves(h2)
            h2_lo = (h2 - h2_hi.astype(jnp.float32)).astype(jnp.bfloat16)
            both = jnp.dot(h2_hi, wr_ref[...], preferred_element_type=jnp.float32)
            cross = jnp.dot(h2_lo, wr_ref[:, 0:LANES], preferred_element_type=jnp.float32)
            logits = both[:, 0:LANES] + both[:, LANES:] + cross
            logits_t = jnp.transpose(logits)[0:ROUTER_ROWS] + br_ref[...]
            (e1, e2), (p1, p2) = _route(logits_t)
            iota = lax.broadcasted_iota(jnp.int32, (SUBLANES, m), 0)
            eid_ref[:, f0:f0 + m] = jnp.where(iota == 0, e1, jnp.where(iota == 1, e2, 0))
            p_ref[:, f0:f0 + m] = jnp.where(iota == 0, p1, jnp.where(iota == 1, p2, 0.0))
        return run

    def interleave(main, fill):
        for i, item in enumerate(main):
            item()
            for f in fill[i * len(fill) // len(main):(i + 1) * len(fill) // len(main)]:
                f()

    for si in range(n_sub):
        prep(si)
    for item in dot_b_items(0):
        item()
    shift_b(0)
    carry_over = []
    for si in range(n_sub):
        fill = carry_over + dot_a_items(si)
        if si + 1 < n_sub:
            fill = fill + dot_b_items(si + 1)
        interleave(conv_b_items(si), fill + [conv_a_item(si)])
        if si + 1 < n_sub:
            shift_b(si + 1)
        carry_over = [finish_item(si)]
    for item in carry_over:
        item()

    if carried:
        @pl.when(t == pl.num_programs(1) - 1)
        def _():
            na_ref[...] = uext_ref[:, PAD_A + tt - HALO_A:PAD_A + tt, :]
            nb_ref[...] = gext_ref[:, PAD_B + tt - HALO_B:PAD_B + tt, :]
        uext_ref[:, 0:PAD_A, :] = uext_ref[:, tt:tt + PAD_A, :]
        gext_ref[:, 0:PAD_B, :] = gext_ref[:, tt:tt + PAD_B, :]
    else:
        na_ref[...] = uext_ref[:, PAD_A + tt - HALO_A:PAD_A + tt, :]
        nb_ref[...] = gext_ref[:, PAD_B + tt - HALO_B:PAD_B + tt, :]


def _mixer_prompt_kernel(x_ref, *refs, **kw):
    _mixer_body(x_ref, None, None, *refs, **kw)


def _mixer_sample_kernel(x_ref, sa_ref, sb_ref, *refs, **kw):
    n_params = 11
    params = refs[:n_params]
    rest = refs[n_params + 4:]
    _mixer_body(x_ref, sa_ref, sb_ref, *params, *rest, **kw)


def _full(shape):
    return pl.BlockSpec(shape, lambda *_: (0,) * len(shape))


def _mixer_param_specs():
    return [
        _full((1, D_MODEL)),
        _full((D_MODEL, IN_COLS)),
        pl.BlockSpec((None, CONV_A, D_A), lambda *_: (0, 0, 0)),
        pl.BlockSpec((None, CONV_B, D_B), lambda *_: (0, 0, 0)),
        _full((1, D_B)),
        _full((1, D_B)),
        _full((1, D_B)),
        _full((D_MODEL, D_MODEL)),
        _full((1, D_MODEL)),
        _full((D_MODEL, 2 * LANES)),
        _full((ROUTER_ROWS, 1)),
    ]


def _mixer_scratch(nseq, tt):
    rows = nseq * tt
    return [
        pltpu.VMEM((rows, 3 * D_A), jnp.float32),
        pltpu.VMEM((nseq, PAD_A + tt, D_A), jnp.float32),
        pltpu.VMEM((nseq, PAD_B + tt, D_B), jnp.float32),
        pltpu.VMEM((rows, D_MODEL), jnp.bfloat16),
        pltpu.VMEM((HALO_A, nseq, PAD_A + tt - SUBLANES, D_A), jnp.float32),
        pltpu.VMEM((SUBLANES - 1, nseq, PAD_B + tt - SUBLANES, D_B), jnp.float32),
    ]


def _mixer_prompt(x, params, n_tokens_total, tt, seq_first, batch):
    seq = x.shape[1]
    n_t = seq // tt
    tok = lambda b, t: (b * n_t + t, 0)
    lane_tok = lambda b, t: (0, b * n_t + t)
    out_shape = [
        jax.ShapeDtypeStruct((n_tokens_total, D_MODEL), jnp.float32),
        jax.ShapeDtypeStruct((n_tokens_total, HALF), jnp.uint32),
        jax.ShapeDtypeStruct((SUBLANES, n_tokens_total), jnp.int32),
        jax.ShapeDtypeStruct((SUBLANES, n_tokens_total), jnp.float32),
        jax.ShapeDtypeStruct((batch, HALO_A, D_A), jnp.float32),
        jax.ShapeDtypeStruct((batch, HALO_B, D_B), jnp.float32),
    ]
    out_specs = [
        pl.BlockSpec((tt, D_MODEL), tok),
        pl.BlockSpec((tt, HALF), tok),
        pl.BlockSpec((SUBLANES, tt), lane_tok),
        pl.BlockSpec((SUBLANES, tt), lane_tok),
        pl.BlockSpec((1, HALO_A, D_A), lambda b, t: (b, 0, 0)),
        pl.BlockSpec((1, HALO_B, D_B), lambda b, t: (b, 0, 0)),
    ]
    return pl.pallas_call(
        functools.partial(_mixer_prompt_kernel, nseq=1, tt=tt, seq_chunk=1, row_chunk=64, n_sub=2),
        grid=(batch, n_t),
        in_specs=[pl.BlockSpec((1, tt, D_MODEL), lambda b, t: (b + seq_first, t, 0))] + _mixer_param_specs(),
        out_specs=out_specs,
        out_shape=out_shape,
        scratch_shapes=_mixer_scratch(1, tt),
        compiler_params=pltpu.CompilerParams(
            dimension_semantics=("arbitrary", "arbitrary"), vmem_limit_bytes=VMEM_LIMIT),
        name="mixer_prompt",
    )(x, *params)


def _mixer_sample(x, state_a, state_b, params, bufs, row_offset, nseq):
    batch, tt, _ = x.shape
    rows = nseq * tt
    first = row_offset // rows
    tok = lambda i: (first + i, 0)
    lane_tok = lambda i: (0, first + i)
    x1, h2p, eid, p = bufs
    out_shape = [
        jax.ShapeDtypeStruct(x1.shape, x1.dtype),
        jax.ShapeDtypeStruct(h2p.shape, h2p.dtype),
        jax.ShapeDtypeStruct(eid.shape, eid.dtype),
        jax.ShapeDtypeStruct(p.shape, p.dtype),
        jax.ShapeDtypeStruct((batch, HALO_A, D_A), jnp.float32),
        jax.ShapeDtypeStruct((batch, HALO_B, D_B), jnp.float32),
    ]
    out_specs = [
        pl.BlockSpec((rows, D_MODEL), tok),
        pl.BlockSpec((rows, HALF), tok),
        pl.BlockSpec((SUBLANES, rows), lane_tok),
        pl.BlockSpec((SUBLANES, rows), lane_tok),
        pl.BlockSpec((nseq, HALO_A, D_A), lambda i: (i, 0, 0)),
        pl.BlockSpec((nseq, HALO_B, D_B), lambda i: (i, 0, 0)),
    ]
    any_spec = pl.BlockSpec(memory_space=pl.ANY)
    in_specs = ([pl.BlockSpec((nseq, tt, D_MODEL), lambda i: (i, 0, 0)),
                 pl.BlockSpec((None, nseq, HALO_A, D_A), lambda i: (0, i, 0, 0)),
                 pl.BlockSpec((None, nseq, HALO_B, D_B), lambda i: (0, i, 0, 0))]
                + _mixer_param_specs() + [any_spec] * 4)
    n_in = len(in_specs)
    return pl.pallas_call(
        functools.partial(_mixer_sample_kernel, nseq=nseq, tt=tt, seq_chunk=8, row_chunk=tt, n_sub=1),
        grid=(batch // nseq,),
        in_specs=in_specs,
        out_specs=out_specs,
        out_shape=out_shape,
        scratch_shapes=_mixer_scratch(nseq, tt),
        input_output_aliases={n_in - 4: 0, n_in - 3: 1, n_in - 2: 2, n_in - 1: 3},
        compiler_params=pltpu.CompilerParams(
            dimension_semantics=("arbitrary",), vmem_limit_bytes=VMEM_LIMIT),
        name="mixer_sample",
    )(x, state_a, state_b, *params, x1, h2p, eid, p)


def _plan_kernel(eid_ref, pos_ref, seg_end_ref, carry_ref, start_ref, earlier_ref, *, tile):
    ph = pl.program_id(0)
    i = pl.program_id(1)
    tt = eid_ref.shape[1]

    @pl.when(i == 0)
    def _():
        @pl.when(ph == 1)
        def _():
            tiles = jnp.floor((carry_ref[...] + (tile - 0.5)) * (1.0 / tile))
            below = (lax.broadcasted_iota(jnp.int32, (N_EXPERTS, N_EXPERTS), 0)
                     > lax.broadcasted_iota(jnp.int32, (N_EXPERTS, N_EXPERTS), 1))
            start = jnp.dot(jnp.where(below, 1.0, 0.0).astype(jnp.bfloat16), tiles.astype(jnp.bfloat16),
                            preferred_element_type=jnp.float32) * tile
            start_ref[...] = start
            seg_end_ref[...] = (start + tiles * tile).astype(jnp.int32)
            earlier = (lax.broadcasted_iota(jnp.int32, (tt, tt), 0)
                       < lax.broadcasted_iota(jnp.int32, (tt, tt), 1))
            earlier_ref[...] = jnp.where(earlier, 1.0, 0.0).astype(jnp.bfloat16)
        carry_ref[...] = jnp.zeros_like(carry_ref)

    eid = eid_ref[...]
    experts = lax.broadcasted_iota(jnp.int32, (N_EXPERTS, tt), 0)
    oh0 = experts == eid[0:1]
    oh1 = experts == eid[1:2]
    oh = jnp.where(oh0 | oh1, 1.0, 0.0)

    @pl.when(ph == 1)
    def _():
        within = jnp.dot(oh.astype(jnp.bfloat16), earlier_ref[...], preferred_element_type=jnp.float32)
        slot_of = within + carry_ref[:, 0:1] + start_ref[:, 0:1]
        s0 = jnp.sum(jnp.where(oh0, slot_of, 0.0), axis=0, keepdims=True)
        s1 = jnp.sum(jnp.where(oh1, slot_of, 0.0), axis=0, keepdims=True)
        k = lax.broadcasted_iota(jnp.int32, (SUBLANES, tt), 0)
        pos_ref[...] = jnp.where(k == 0, s0, jnp.where(k == 1, s1, 0.0)).astype(jnp.int32)

    carry_ref[...] = carry_ref[...] + jnp.sum(oh, axis=1, keepdims=True)


def _plan(eid, tile):
    n_tokens = eid.shape[1]
    return pl.pallas_call(
        functools.partial(_plan_kernel, tile=tile),
        grid=(2, n_tokens // PLAN_TILE),
        in_specs=[pl.BlockSpec((SUBLANES, PLAN_TILE), lambda ph, i: (0, i))],
        out_specs=[pl.BlockSpec((SUBLANES, PLAN_TILE), lambda ph, i: (0, i * ph)),
                   pl.BlockSpec((N_EXPERTS, LANES), lambda ph, i: (0, 0))],
        out_shape=[jax.ShapeDtypeStruct((SUBLANES, n_tokens), jnp.int32),
                   jax.ShapeDtypeStruct((N_EXPERTS, LANES), jnp.int32)],
        scratch_shapes=[pltpu.VMEM((N_EXPERTS, LANES), jnp.float32),
                        pltpu.VMEM((N_EXPERTS, LANES), jnp.float32),
                        pltpu.VMEM((PLAN_TILE, PLAN_TILE), jnp.bfloat16)],
        compiler_params=pltpu.CompilerParams(dimension_semantics=("arbitrary", "arbitrary")),
        name="route_plan",
    )(eid)


def _sc_mesh():
    return plsc.VectorSubcoreMesh(core_axis_name="c", subcore_axis_name="s")


def _sc_dispatch_rows(table, pos, n_out):
    n_workers, n_batches, top_k, batch = pos.shape
    n_rows, words = table.shape
    assert n_workers * n_batches * batch == n_rows and n_batches >= 2

    @functools.partial(
        pl.kernel, mesh=_sc_mesh(),
        out_type=jax.ShapeDtypeStruct((n_out, words), table.dtype),
        scratch_types=[pltpu.VMEM((n_batches, top_k, batch), jnp.int32),
                       pltpu.VMEM((2, batch, words), table.dtype),
                       pltpu.SemaphoreType.DMA((2,)),
                       pltpu.SemaphoreType.DMA((2, top_k))],
    )
    def dispatch(table_hbm, pos_hbm, out_hbm, idx_v, rows_v, sem_in, sem_out):
        worker = lax.axis_index("s") * SC_CORES + lax.axis_index("c")
        pltpu.sync_copy(pos_hbm.at[worker], idx_v)

        def read(b):
            src = table_hbm.at[pl.ds((worker * n_batches + b) * batch, batch)]
            return pltpu.async_copy(src, rows_v.at[b % 2], sem_in.at[b % 2])

        def write(b):
            return [pltpu.async_copy(rows_v.at[b % 2], out_hbm.at[idx_v.at[b, k]], sem_out.at[b % 2, k])
                    for k in range(top_k)]

        reads = {0: read(0)}
        writes = {}
        for b in range(n_batches):
            reads[b].wait()
            if b + 1 < n_batches:
                if b >= 1:
                    for w in writes[b - 1]:
                        w.wait()
                reads[b + 1] = read(b + 1)
            writes[b] = write(b)
        for b in (n_batches - 2, n_batches - 1):
            for w in writes[b]:
                w.wait()

    return dispatch(table, pos)


def _sc_gather_rows(table, idx):
    n_workers, n_batches, batch = idx.shape
    words = table.shape[1]
    assert n_batches >= 2

    @functools.partial(
        pl.kernel, mesh=_sc_mesh(),
        out_type=jax.ShapeDtypeStruct((n_workers * n_batches * batch, words), table.dtype),
        scratch_types=[pltpu.VMEM((n_batches, batch), jnp.int32),
                       pltpu.VMEM((2, batch, words), table.dtype),
                       pltpu.SemaphoreType.DMA((2,)),
                       pltpu.SemaphoreType.DMA((2,))],
    )
    def gather(table_hbm, idx_hbm, out_hbm, idx_v, rows_v, sem_in, sem_out):
        worker = lax.axis_index("s") * SC_CORES + lax.axis_index("c")
        pltpu.sync_copy(idx_hbm.at[worker], idx_v)

        def read(b):
            return pltpu.async_copy(table_hbm.at[idx_v.at[b]], rows_v.at[b % 2], sem_in.at[b % 2])

        def write(b):
            dst = out_hbm.at[pl.ds((worker * n_batches + b) * batch, batch)]
            return pltpu.async_copy(rows_v.at[b % 2], dst, sem_out.at[b % 2])

        reads = {0: read(0)}
        writes = {}
        for b in range(n_batches):
            reads[b].wait()
            if b + 1 < n_batches:
                if b >= 1:
                    writes[b - 1].wait()
                reads[b + 1] = read(b + 1)
            writes[b] = write(b)
        writes[n_batches - 2].wait()
        writes[n_batches - 1].wait()

    return gather(table, idx)


def _experts_kernel(te_ref, xs_ref, wg_ref, wu_ref, wd_ref, ys_ref):
    hi, lo = _unpack_bf16_halves(xs_ref[...])
    x = jnp.concatenate([hi.astype(jnp.bfloat16), lo.astype(jnp.bfloat16)], axis=1)
    w_gate_up = jnp.concatenate([wg_ref[0].astype(jnp.bfloat16), wu_ref[0].astype(jnp.bfloat16)], axis=1)
    wd = wd_ref[0].astype(jnp.bfloat16)
    gate_up = jnp.dot(x, w_gate_up, preferred_element_type=jnp.float32)
    gate = gate_up[:, :D_EXPERT]
    up = gate_up[:, D_EXPERT:]
    hid = (gate * jax.nn.sigmoid(gate) * up).astype(jnp.bfloat16)
    ys_ref[...] = _pack_bf16_halves(jnp.dot(hid, wd, preferred_element_type=jnp.float32))


def _experts(xs, tile_expert, n_valid, wg, wu, wd, tm):
    n_slots = xs.shape[0]
    row_block = lambda i, te: (i, 0)
    w_block = lambda i, te: (te[i], 0, 0)
    return pl.pallas_call(
        _experts_kernel,
        grid_spec=pltpu.PrefetchScalarGridSpec(
            num_scalar_prefetch=1,
            grid=(n_valid,),
            in_specs=[pl.BlockSpec((tm, HALF), row_block),
                      pl.BlockSpec((1, D_MODEL, D_EXPERT), w_block),
                      pl.BlockSpec((1, D_MODEL, D_EXPERT), w_block),
                      pl.BlockSpec((1, D_EXPERT, D_MODEL), w_block)],
            out_specs=pl.BlockSpec((tm, HALF), row_block),
        ),
        out_shape=jax.ShapeDtypeStruct((n_slots, HALF), jnp.uint32),
        compiler_params=pltpu.CompilerParams(
            dimension_semantics=("arbitrary",), vmem_limit_bytes=VMEM_LIMIT),
        name="experts",
    )(tile_expert, xs, wg, wu, wd)


def _final_kernel(x1_ref, y0_ref, y1_ref, p_ref, gfin_ref, *rest):
    out_ref = rest[-1]
    pt = jnp.transpose(p_ref[...])
    p0 = pt[:, 0:1]
    p1 = pt[:, 1:2]
    a_hi, a_lo = _unpack_bf16_halves(y0_ref[...])
    b_hi, b_lo = _unpack_bf16_halves(y1_ref[...])
    x1 = x1_ref[...]
    x2_hi = x1[:, :HALF] + (p0 * a_hi + p1 * b_hi)
    x2_lo = x1[:, HALF:] + (p0 * a_lo + p1 * b_lo)
    ms = (jnp.sum(x2_hi * x2_hi, axis=-1, keepdims=True)
          + jnp.sum(x2_lo * x2_lo, axis=-1, keepdims=True)) / D_MODEL
    scale = lax.rsqrt(ms + RMS_EPS)
    g = gfin_ref[...]
    out_ref[:, :HALF] = x2_hi * scale * g[:, :HALF]
    out_ref[:, HALF:] = x2_lo * scale * g[:, HALF:]


def _final(x1, yk, p, gfin, row_offset, n_rows, tm, out_rows, out_offset, out_buf=None):
    n_tokens = x1.shape[0]
    first = row_offset // tm
    second = (n_tokens + row_offset) // tm
    out_first = out_offset // tm
    in_specs = [pl.BlockSpec((tm, D_MODEL), lambda i: (first + i, 0)),
                pl.BlockSpec((tm, HALF), lambda i: (first + i, 0)),
                pl.BlockSpec((tm, HALF), lambda i: (second + i, 0)),
                pl.BlockSpec((SUBLANES, tm), lambda i: (0, first + i)),
                pl.BlockSpec((1, D_MODEL), lambda i: (0, 0))]
    args = [x1, yk, yk, p, gfin]
    aliases = {}
    if out_buf is not None:
        in_specs.append(pl.BlockSpec(memory_space=pl.ANY))
        args.append(out_buf)
        aliases = {len(args) - 1: 0}
    return pl.pallas_call(
        _final_kernel,
        grid=(n_rows // tm,),
        in_specs=in_specs,
        out_specs=pl.BlockSpec((tm, D_MODEL), lambda i: (out_first + i, 0)),
        out_shape=jax.ShapeDtypeStruct((out_rows, D_MODEL), jnp.float32),
        input_output_aliases=aliases,
        compiler_params=pltpu.CompilerParams(
            dimension_semantics=("arbitrary",), vmem_limit_bytes=VMEM_LIMIT),
        name="final",
    )(*args)


def _expert_tile(n_tokens):
    mean_rows = TOP_K * n_tokens // N_EXPERTS
    return -(-(mean_rows + 2 * math.isqrt(mean_rows)) // (2 * SUBLANES)) * (2 * SUBLANES)


def _routed_experts(h2p, eid, w_gate, w_up, w_down):
    n_tokens = h2p.shape[0]
    assert n_tokens % (SC_WORKERS * SC_DISPATCH_BATCH) == 0 and n_tokens % PLAN_TILE == 0
    assert (TOP_K * n_tokens) % (SC_WORKERS * SC_BATCH) == 0
    tile = _expert_tile(n_tokens)
    n_tiles = -(-(TOP_K * n_tokens + N_EXPERTS * (tile - 1)) // tile)
    pos, seg_end = _plan(eid, tile)
    pos = pos[:TOP_K]
    by_token = jnp.transpose(pos.reshape(TOP_K, SC_WORKERS, -1, SC_DISPATCH_BATCH), (1, 2, 0, 3))
    seg_end = seg_end[:, 0]
    n_valid = seg_end[N_EXPERTS - 1] // tile
    tile_row = jnp.arange(n_tiles, dtype=jnp.int32) * tile
    tile_expert = jnp.minimum(jnp.sum(seg_end[None, :] <= tile_row[:, None], axis=1), N_EXPERTS - 1)
    xs = _sc_dispatch_rows(h2p, by_token, n_tiles * tile)
    ys = _experts(xs, tile_expert.astype(jnp.int32), n_valid, w_gate, w_up, w_down, tile)
    return _sc_gather_rows(ys, pos.reshape(SC_WORKERS, -1, SC_BATCH))


def kernel(x_prompt, x_sample, state_conv_a, state_conv_b, g_mix, w_in, conv_a_w, conv_b_w, conv_b_bias,
           ln_g, ln_b, w_out, g_ffn, w_coarse, b_coarse, w_fine, b_fine, w_gate, w_up, w_down, g_final):
    assert g_mix.shape[0] == 1, "single trunk layer"
    batch, seq, _ = x_prompt.shape
    dec_batch, dec_seq, _ = x_sample.shape
    n_prompt = batch * seq
    n_sample = dec_batch * dec_seq
    bf16 = jnp.bfloat16

    wr = jnp.concatenate([
        w_coarse[0], jnp.zeros((D_MODEL, SUBLANES - N_EXPERT_GROUPS), jnp.float32),
        jnp.transpose(w_fine[0], (1, 0, 2)).reshape(D_MODEL, N_EXPERTS),
        jnp.zeros((D_MODEL, LANES - ROUTER_ROWS), jnp.float32)], axis=1)
    wr_hi = wr.astype(bf16)
    wr_lo = (wr - wr_hi.astype(jnp.float32)).astype(bf16)
    wr_both = jnp.concatenate([wr_hi, wr_lo], axis=1)
    br = jnp.concatenate([
        b_coarse[0], jnp.full((SUBLANES - N_EXPERT_GROUPS,), NEG_BIG, jnp.float32),
        b_fine[0].reshape(N_EXPERTS)]).reshape(ROUTER_ROWS, 1)

    params = (g_mix, w_in[0].astype(bf16), conv_a_w, conv_b_w, conv_b_bias, ln_g, ln_b,
              w_out[0].astype(bf16), g_ffn, wr_both, br)

    experts = (w_gate[0], w_up[0], w_down[0])
    gfin = g_final.reshape(1, D_MODEL)

    bufs = _mixer_prompt(x_prompt, params, n_prompt + n_sample, MIXER_TILE, 0, batch)
    na_p, nb_p = bufs[4:]
    x1, h2p, eid, p, na_s, nb_s = _mixer_sample(
        x_sample, state_conv_a, state_conv_b, params, bufs[:4], n_prompt, nseq=32)
    yk = _routed_experts(h2p, eid, *experts)
    y_p = _final(x1, yk, p, gfin, 0, n_prompt, FINAL_TILE, n_prompt, 0)
    y_s = _final(x1, yk, p, gfin, n_prompt, n_sample, FINAL_TILE, n_sample, 0)
    return (y_p.reshape(batch, seq, D_MODEL), y_s.reshape(dec_batch, dec_seq, D_MODEL),
            na_p[None], nb_p[None], na_s[None], nb_s[None])
```

```python
import functools
import math

import jax
import jax.numpy as jnp
from jax import lax
from jax.experimental import pallas as pl
from jax.experimental.pallas import tpu as pltpu
from jax.experimental.pallas import tpu_sc as plsc

D_MODEL = 1024
D_A = 512
D_B = 512
CONV_A = 3
CONV_B = 31
HALO_A = CONV_A - 1
HALO_B = CONV_B - 1
IN_COLS = 3 * D_A + 2 * D_B
N_EXPERT_GROUPS = 4
EXPERTS_PER_GROUP = 8
N_EXPERTS = N_EXPERT_GROUPS * EXPERTS_PER_GROUP
TOP_K = 2
D_EXPERT = D_MODEL // 4
RMS_EPS = 1e-6
LN_EPS = 1e-5

SUBLANES = 8
LANES = 128
PAD_A = SUBLANES
PAD_B = 32
ROUTER_ROWS = SUBLANES + N_EXPERTS
NEG_BIG = -1e30
VMEM_LIMIT = 56 * 1024 * 1024
HALF = D_MODEL // 2
HI_MASK = 0xFFFF0000

SC_CORES = 2
SC_SUBCORES = 16
SC_WORKERS = SC_CORES * SC_SUBCORES
SC_BATCH = 64
SC_DISPATCH_BATCH = 32

PLAN_TILE = 1024
FINAL_TILE = 1024
MIXER_TILE = 512


def _rms_scale(x):
    return x * lax.rsqrt(jnp.mean(x * x, axis=-1, keepdims=True) + RMS_EPS)


def _pack_bf16_halves(x):
    bits = lax.bitcast_convert_type(x.astype(jnp.bfloat16).astype(jnp.float32), jnp.uint32)
    return bits[:, :HALF] | (bits[:, HALF:] >> 16)


def _unpack_bf16_halves(w):
    hi = lax.bitcast_convert_type(w & jnp.uint32(HI_MASK), jnp.float32)
    lo = lax.bitcast_convert_type(w << 16, jnp.float32)
    return hi, lo


def _route(logits_t):
    rows = logits_t.shape[1]
    iota = lax.broadcasted_iota(jnp.int32, (SUBLANES, rows), 0)
    lc = logits_t[0:SUBLANES]
    cmax = jnp.max(lc, axis=0, keepdims=True)
    grp = jnp.min(jnp.where(lc == cmax, iota, SUBLANES), axis=0, keepdims=True)
    p_grp = 1.0 / jnp.sum(jnp.exp(lc - cmax), axis=0, keepdims=True)
    sel = logits_t[SUBLANES:2 * SUBLANES]
    for g in range(1, N_EXPERT_GROUPS):
        sel = jnp.where(grp == g, logits_t[(g + 1) * SUBLANES:(g + 2) * SUBLANES], sel)
    v1 = jnp.max(sel, axis=0, keepdims=True)
    i1 = jnp.min(jnp.where(sel == v1, iota, SUBLANES), axis=0, keepdims=True)
    sel2 = jnp.where(iota == i1, -jnp.inf, sel)
    v2 = jnp.max(sel2, axis=0, keepdims=True)
    i2 = jnp.min(jnp.where(sel2 == v2, iota, SUBLANES), axis=0, keepdims=True)
    e2 = jnp.exp(v2 - v1)
    den = 1.0 + e2
    p1 = p_grp / den
    p2 = p_grp * e2 / den
    base = grp * EXPERTS_PER_GROUP
    return (base + i1, base + i2), (p1, p2)


def _mixer_body(x_ref, sa_ref, sb_ref, gmix_ref, win_ref, caw_ref, cbw_ref, cbb_ref, lng_ref, lnb_ref,
                wout_ref, gffn_ref, wr_ref, br_ref,
                x1_ref, h2p_ref, eid_ref, p_ref, na_ref, nb_ref,
                proj_ref, uext_ref, gext_ref, z_ref, ush_ref, gsh_ref, *, nseq, tt, seq_chunk, row_chunk, n_sub):
    carried = sa_ref is None
    t = pl.program_id(1) if carried else None
    assert n_sub == 1 or nseq == 1
    sub = tt // n_sub

    if carried:
        @pl.when(t == 0)
        def _():
            uext_ref[:, 0:PAD_A, :] = jnp.zeros((nseq, PAD_A, D_A), jnp.float32)
            gext_ref[:, 0:PAD_B, :] = jnp.zeros((nseq, PAD_B, D_B), jnp.float32)
    else:
        uext_ref[:, PAD_A - HALO_A:PAD_A, :] = sa_ref[...]
        gext_ref[:, PAD_B - HALO_B:PAD_B, :] = sb_ref[...]

    def window(base_ref, shifted_ref, first_shift, s0, off, n_rows):
        r = off % SUBLANES
        a8 = off - r
        if r == 0:
            return base_ref[s0:s0 + seq_chunk, a8:a8 + n_rows, :]
        return shifted_ref[r - first_shift, s0:s0 + seq_chunk, a8:a8 + n_rows, :]

    n = seq_chunk * row_chunk
    col_chunk = 2 * LANES
    caw = caw_ref[...]
    cbw = cbw_ref[...]
    hs_bf = {}

    def rows_of(si):
        q0 = si * sub
        f0 = q0 if nseq == 1 else 0
        return q0, f0, nseq * sub

    def prep(si):
        q0, _, m = rows_of(si)
        x = x_ref[:, q0:q0 + sub, :].reshape(m, D_MODEL)
        hs_bf[si] = (_rms_scale(x) * gmix_ref[...]).astype(jnp.bfloat16)

    def dot_b_items(si):
        def item(c0):
            def run():
                q0, _, _ = rows_of(si)
                v_b = jnp.dot(hs_bf[si], win_ref[:, 3 * D_A + c0:3 * D_A + c0 + col_chunk],
                              preferred_element_type=jnp.float32)
                g_b = jnp.dot(hs_bf[si], win_ref[:, 3 * D_A + D_B + c0:3 * D_A + D_B + c0 + col_chunk],
                              preferred_element_type=jnp.float32)
                gext_ref[:, PAD_B + q0:PAD_B + q0 + sub, c0:c0 + col_chunk] = (
                    v_b * jax.nn.sigmoid(g_b)).reshape(nseq, sub, col_chunk)
            return run
        return [item(c0) for c0 in range(0, D_B, col_chunk)]

    def dot_a_items(si):
        def item(c0):
            def run():
                _, f0, m = rows_of(si)
                proj_ref[f0:f0 + m, c0:c0 + col_chunk] = jnp.dot(
                    hs_bf[si], win_ref[:, c0:c0 + col_chunk], preferred_element_type=jnp.float32)
            return run
        return [item(c0) for c0 in range(0, 3 * D_A, col_chunk)]

    def shift_b(si):
        q0, _, _ = rows_of(si)
        j_lo = 0 if si == 0 else q0 + PAD_B - SUBLANES
        j_hi = q0 + sub + PAD_B - SUBLANES
        for r in range(1, SUBLANES):
            gsh_ref[r - 1, :, j_lo:j_hi, :] = gext_ref[:, j_lo + r:j_hi + r, :]

    def chunks_of(si):
        q0, _, _ = rows_of(si)
        return [(s0, r0) for s0 in range(0, nseq, seq_chunk) for r0 in range(q0, q0 + sub, row_chunk)]

    def conv_b_items(si):
        def item(s0, r0):
            def run():
                lo = s0 * tt + r0
                acc_b = None
                for k in range(CONV_B):
                    off = PAD_B - HALO_B + k + r0
                    term = window(gext_ref, gsh_ref, 1, s0, off, row_chunk) * cbw[k:k + 1, :]
                    acc_b = term if acc_b is None else acc_b + term
                zb = acc_b.reshape(n, D_B) + cbb_ref[...]
                mu = jnp.mean(zb, axis=-1, keepdims=True)
                zc = zb - mu
                var = jnp.mean(zc * zc, axis=-1, keepdims=True)
                y = zc * lax.rsqrt(var + LN_EPS) * lng_ref[...] + lnb_ref[...]
                z_ref[lo:lo + n, D_A:] = (y * jax.nn.sigmoid(y)).astype(jnp.bfloat16)
            return run
        return [item(s0, r0) for s0, r0 in chunks_of(si)]

    def conv_a_item(si):
        def run():
            q0, f0, m = rows_of(si)
            c_a = proj_ref[f0:f0 + m, D_A:2 * D_A]
            v_a = proj_ref[f0:f0 + m, 2 * D_A:3 * D_A]
            uext_ref[:, PAD_A + q0:PAD_A + q0 + sub, :] = (c_a * v_a).reshape(nseq, sub, D_A)
            for r in range(SUBLANES - HALO_A, SUBLANES):
                ush_ref[r - (SUBLANES - HALO_A), :, q0:q0 + sub, :] = uext_ref[:, q0 + r:q0 + sub + r, :]
            for s0, r0 in chunks_of(si):
                lo = s0 * tt + r0
                acc_a = None
                for k in range(CONV_A):
                    off = PAD_A - HALO_A + k + r0
                    term = window(uext_ref, ush_ref, SUBLANES - HALO_A, s0, off, row_chunk) * caw[k:k + 1, :]
                    acc_a = term if acc_a is None else acc_a + term
                z_a = proj_ref[lo:lo + n, 0:D_A] * acc_a.reshape(n, D_A)
                z_ref[lo:lo + n, 0:D_A] = z_a.astype(jnp.bfloat16)
        return run

    def finish_item(si):
        def run():
            q0, f0, m = rows_of(si)
            x = x_ref[:, q0:q0 + sub, :].reshape(m, D_MODEL)
            x1 = x + jnp.dot(z_ref[f0:f0 + m, :], wout_ref[...], preferred_element_type=jnp.float32)
            x1_ref[f0:f0 + m, :] = x1
            h2 = _rms_scale(x1) * gffn_ref[...]
            h2_hi = h2.astype(jnp.bfloat16)
            h2p_ref[f0:f0 + m, :] = _pack_bf16_halves(h2)
            h2_lo = (h2 - h2_hi.astype(jnp.float32)).astype(jnp.bfloat16)
            both = jnp.dot(h2_hi, wr_ref[...], preferred_element_type=jnp.float32)
            cross = jnp.dot(h2_lo, wr_ref[:, 0:LANES], preferred_element_type=jnp.float32)
            logits = both[:, 0:LANES] + both[:, LANES:] + cross
            logits_t = jnp.transpose(logits)[0:ROUTER_ROWS] + br_ref[...]
            (e1, e2), (p1, p2) = _route(logits_t)
            iota = lax.broadcasted_iota(jnp.int32, (SUBLANES, m), 0)
            eid_ref[:, f0:f0 + m] = jnp.where(iota == 0, e1, jnp.where(iota == 1, e2, 0))
            p_ref[:, f0:f0 + m] = jnp.where(iota == 0, p1, jnp.where(iota == 1, p2, 0.0))
        return run

    def interleave(main, fill):
        for i, item in enumerate(main):
            item()
            for f in fill[i * len(fill) // len(main):(i + 1) * len(fill) // len(main)]:
                f()

    for si in range(n_sub):
        prep(si)
    for item in dot_b_items(0):
        item()
    shift_b(0)
    carry_over = []
    for si in range(n_sub):
        fill = carry_over + dot_a_items(si)
        if si + 1 < n_sub:
            fill = fill + dot_b_items(si + 1)
        interleave(conv_b_items(si), fill + [conv_a_item(si)])
        if si + 1 < n_sub:
            shift_b(si + 1)
        carry_over = [finish_item(si)]
    for item in carry_over:
        item()

    if carried:
        @pl.when(t == pl.num_programs(1) - 1)
        def _():
            na_ref[...] = uext_ref[:, PAD_A + tt - HALO_A:PAD_A + tt, :]
            nb_ref[...] = gext_ref[:, PAD_B + tt - HALO_B:PAD_B + tt, :]
        uext_ref[:, 0:PAD_A, :] = uext_ref[:, tt:tt + PAD_A, :]
        gext_ref[:, 0:PAD_B, :] = gext_ref[:, tt:tt + PAD_B, :]
    else:
        na_ref[...] = uext_ref[:, PAD_A + tt - HALO_A:PAD_A + tt, :]
        nb_ref[...] = gext_ref[:, PAD_B + tt - HALO_B:PAD_B + tt, :]


def _mixer_prompt_kernel(x_ref, *refs, **kw):
    _mixer_body(x_ref, None, None, *refs, **kw)


def _mixer_sample_kernel(x_ref, sa_ref, sb_ref, *refs, **kw):
    n_params = 11
    params = refs[:n_params]
    rest = refs[n_params + 4:]
    _mixer_body(x_ref, sa_ref, sb_ref, *params, *rest, **kw)


def _full(shape):
    return pl.BlockSpec(shape, lambda *_: (0,) * len(shape))


def _mixer_param_specs():
    return [
        _full((1, D_MODEL)),
        _full((D_MODEL, IN_COLS)),
        pl.BlockSpec((None, CONV_A, D_A), lambda *_: (0, 0, 0)),
        pl.BlockSpec((None, CONV_B, D_B), lambda *_: (0, 0, 0)),
        _full((1, D_B)),
        _full((1, D_B)),
        _full((1, D_B)),
        _full((D_MODEL, D_MODEL)),
        _full((1, D_MODEL)),
        _full((D_MODEL, 2 * LANES)),
        _full((ROUTER_ROWS, 1)),
    ]


def _mixer_scratch(nseq, tt):
    rows = nseq * tt
    return [
        pltpu.VMEM((rows, 3 * D_A), jnp.float32),
        pltpu.VMEM((nseq, PAD_A + tt, D_A), jnp.float32),
        pltpu.VMEM((nseq, PAD_B + tt, D_B), jnp.float32),
        pltpu.VMEM((rows, D_MODEL), jnp.bfloat16),
        pltpu.VMEM((HALO_A, nseq, PAD_A + tt - SUBLANES, D_A), jnp.float32),
        pltpu.VMEM((SUBLANES - 1, nseq, PAD_B + tt - SUBLANES, D_B), jnp.float32),
    ]


def _mixer_prompt(x, params, n_tokens_total, tt, seq_first, batch):
    seq = x.shape[1]
    n_t = seq // tt
    tok = lambda b, t: (b * n_t + t, 0)
    lane_tok = lambda b, t: (0, b * n_t + t)
    out_shape = [
        jax.ShapeDtypeStruct((n_tokens_total, D_MODEL), jnp.float32),
        jax.ShapeDtypeStruct((n_tokens_total, HALF), jnp.uint32),
        jax.ShapeDtypeStruct((SUBLANES, n_tokens_total), jnp.int32),
        jax.ShapeDtypeStruct((SUBLANES, n_tokens_total), jnp.float32),
        jax.ShapeDtypeStruct((batch, HALO_A, D_A), jnp.float32),
        jax.ShapeDtypeStruct((batch, HALO_B, D_B), jnp.float32),
    ]
    out_specs = [
        pl.BlockSpec((tt, D_MODEL), tok),
        pl.BlockSpec((tt, HALF), tok),
        pl.BlockSpec((SUBLANES, tt), lane_tok),
        pl.BlockSpec((SUBLANES, tt), lane_tok),
        pl.BlockSpec((1, HALO_A, D_A), lambda b, t: (b, 0, 0)),
        pl.BlockSpec((1, HALO_B, D_B), lambda b, t: (b, 0, 0)),
    ]
    return pl.pallas_call(
        functools.partial(_mixer_prompt_kernel, nseq=1, tt=tt, seq_chunk=1, row_chunk=64, n_sub=2),
        grid=(batch, n_t),
        in_specs=[pl.BlockSpec((1, tt, D_MODEL), lambda b, t: (b + seq_first, t, 0))] + _mixer_param_specs(),
        out_specs=out_specs,
        out_shape=out_shape,
        scratch_shapes=_mixer_scratch(1, tt),
        compiler_params=pltpu.CompilerParams(
            dimension_semantics=("arbitrary", "arbitrary"), vmem_limit_bytes=VMEM_LIMIT),
        name="mixer_prompt",
    )(x, *params)


def _mixer_sample(x, state_a, state_b, params, bufs, row_offset, nseq):
    batch, tt, _ = x.shape
    rows = nseq * tt
    first = row_offset // rows
    tok = lambda i: (first + i, 0)
    lane_tok = lambda i: (0, first + i)
    x1, h2p, eid, p = bufs
    out_shape = [
        jax.ShapeDtypeStruct(x1.shape, x1.dtype),
        jax.ShapeDtypeStruct(h2p.shape, h2p.dtype),
        jax.ShapeDtypeStruct(eid.shape, eid.dtype),
        jax.ShapeDtypeStruct(p.shape, p.dtype),
        jax.ShapeDtypeStruct((batch, HALO_A, D_A), jnp.float32),
        jax.ShapeDtypeStruct((batch, HALO_B, D_B), jnp.float32),
    ]
    out_specs = [
        pl.BlockSpec((rows, D_MODEL), tok),
        pl.BlockSpec((rows, HALF), tok),
        pl.BlockSpec((SUBLANES, rows), lane_tok),
        pl.BlockSpec((SUBLANES, rows), lane_tok),
        pl.BlockSpec((nseq, HALO_A, D_A), lambda i: (i, 0, 0)),
        pl.BlockSpec((nseq, HALO_B, D_B), lambda i: (i, 0, 0)),
    ]
    any_spec = pl.BlockSpec(memory_space=pl.ANY)
    in_specs = ([pl.BlockSpec((nseq, tt, D_MODEL), lambda i: (i, 0, 0)),
                 pl.BlockSpec((None, nseq, HALO_A, D_A), lambda i: (0, i, 0, 0)),
                 pl.BlockSpec((None, nseq, HALO_B, D_B), lambda i: (0, i, 0, 0))]
                + _mixer_param_specs() + [any_spec] * 4)
    n_in = len(in_specs)
    return pl.pallas_call(
        functools.partial(_mixer_sample_kernel, nseq=nseq, tt=tt, seq_chunk=8, row_chunk=tt, n_sub=1),
        grid=(batch // nseq,),
        in_specs=in_specs,
        out_specs=out_specs,
        out_shape=out_shape,
        scratch_shapes=_mixer_scratch(nseq, tt),
        input_output_aliases={n_in - 4: 0, n_in - 3: 1, n_in - 2: 2, n_in - 1: 3},
        compiler_params=pltpu.CompilerParams(
            dimension_semantics=("arbitrary",), vmem_limit_bytes=VMEM_LIMIT),
        name="mixer_sample",
    )(x, state_a, state_b, *params, x1, h2p, eid, p)


def _plan_kernel(eid_ref, pos_ref, seg_end_ref, carry_ref, start_ref, earlier_ref, *, tile):
    ph = pl.program_id(0)
    i = pl.program_id(1)
    tt = eid_ref.shape[1]

    @pl.when(i == 0)
    def _():
        @pl.when(ph == 1)
        def _():
            tiles = jnp.floor((carry_ref[...] + (tile - 0.5)) * (1.0 / tile))
            below = (lax.broadcasted_iota(jnp.int32, (N_EXPERTS, N_EXPERTS), 0)
                     > lax.broadcasted_iota(jnp.int32, (N_EXPERTS, N_EXPERTS), 1))
            start = jnp.dot(jnp.where(below, 1.0, 0.0).astype(jnp.bfloat16), tiles.astype(jnp.bfloat16),
                            preferred_element_type=jnp.float32) * tile
            start_ref[...] = start
            seg_end_ref[...] = (start + tiles * tile).astype(jnp.int32)
            earlier = (lax.broadcasted_iota(jnp.int32, (tt, tt), 0)
                       < lax.broadcasted_iota(jnp.int32, (tt, tt), 1))
            earlier_ref[...] = jnp.where(earlier, 1.0, 0.0).astype(jnp.bfloat16)
        carry_ref[...] = jnp.zeros_like(carry_ref)

    eid = eid_ref[...]
    experts = lax.broadcasted_iota(jnp.int32, (N_EXPERTS, tt), 0)
    oh0 = experts == eid[0:1]
    oh1 = experts == eid[1:2]
    oh = jnp.where(oh0 | oh1, 1.0, 0.0)

    @pl.when(ph == 1)
    def _():
        within = jnp.dot(oh.astype(jnp.bfloat16), earlier_ref[...], preferred_element_type=jnp.float32)
        slot_of = within + carry_ref[:, 0:1] + start_ref[:, 0:1]
        s0 = jnp.sum(jnp.where(oh0, slot_of, 0.0), axis=0, keepdims=True)
        s1 = jnp.sum(jnp.where(oh1, slot_of, 0.0), axis=0, keepdims=True)
        k = lax.broadcasted_iota(jnp.int32, (SUBLANES, tt), 0)
        pos_ref[...] = jnp.where(k == 0, s0, jnp.where(k == 1, s1, 0.0)).astype(jnp.int32)

    carry_ref[...] = carry_ref[...] + jnp.sum(oh, axis=1, keepdims=True)


def _plan(eid, tile):
    n_tokens = eid.shape[1]
    return pl.pallas_call(
        functools.partial(_plan_kernel, tile=tile),
        grid=(2, n_tokens // PLAN_TILE),
        in_specs=[pl.BlockSpec((SUBLANES, PLAN_TILE), lambda ph, i: (0, i))],
        out_specs=[pl.BlockSpec((SUBLANES, PLAN_TILE), lambda ph, i: (0, i * ph)),
                   pl.BlockSpec((N_EXPERTS, LANES), lambda ph, i: (0, 0))],
        out_shape=[jax.ShapeDtypeStruct((SUBLANES, n_tokens), jnp.int32),
                   jax.ShapeDtypeStruct((N_EXPERTS, LANES), jnp.int32)],
        scratch_shapes=[pltpu.VMEM((N_EXPERTS, LANES), jnp.float32),
                        pltpu.VMEM((N_EXPERTS, LANES), jnp.float32),
                        pltpu.VMEM((PLAN_TILE, PLAN_TILE), jnp.bfloat16)],
        compiler_params=pltpu.CompilerParams(dimension_semantics=("arbitrary", "arbitrary")),
        name="route_plan",
    )(eid)


def _sc_mesh():
    return plsc.VectorSubcoreMesh(core_axis_name="c", subcore_axis_name="s")


def _sc_dispatch_rows(table, pos, n_out):
    n_workers, n_batches, top_k, batch = pos.shape
    n_rows, words = table.shape
    assert n_workers * n_batches * batch == n_rows and n_batches >= 2

    @functools.partial(
        pl.kernel, mesh=_sc_mesh(),
        out_type=jax.ShapeDtypeStruct((n_out, words), table.dtype),
        scratch_types=[pltpu.VMEM((n_batches, top_k, batch), jnp.int32),
                       pltpu.VMEM((2, batch, words), table.dtype),
                       pltpu.SemaphoreType.DMA((2,)),
                       pltpu.SemaphoreType.DMA((2, top_k))],
    )
    def dispatch(table_hbm, pos_hbm, out_hbm, idx_v, rows_v, sem_in, sem_out):
        worker = lax.axis_index("s") * SC_CORES + lax.axis_index("c")
        pltpu.sync_copy(pos_hbm.at[worker], idx_v)

        def read(b):
            src = table_hbm.at[pl.ds((worker * n_batches + b) * batch, batch)]
            return pltpu.async_copy(src, rows_v.at[b % 2], sem_in.at[b % 2])

        def write(b):
            return [pltpu.async_copy(rows_v.at[b % 2], out_hbm.at[idx_v.at[b, k]], sem_out.at[b % 2, k])
                    for k in range(top_k)]

        reads = {0: read(0)}
        writes = {}
        for b in range(n_batches):
            reads[b].wait()
            if b + 1 < n_batches:
                if b >= 1:
                    for w in writes[b - 1]:
                        w.wait()
                reads[b + 1] = read(b + 1)
            writes[b] = write(b)
        for b in (n_batches - 2, n_batches - 1):
            for w in writes[b]:
                w.wait()

    return dispatch(table, pos)


def _sc_gather_rows(table, idx):
    n_workers, n_batches, batch = idx.shape
    words = table.shape[1]
    assert n_batches >= 2

    @functools.partial(
        pl.kernel, mesh=_sc_mesh(),
        out_type=jax.ShapeDtypeStruct((n_workers * n_batches * batch, words), table.dtype),
        scratch_types=[pltpu.VMEM((n_batches, batch), jnp.int32),
                       pltpu.VMEM((2, batch, words), table.dtype),
                       pltpu.SemaphoreType.DMA((2,)),
                       pltpu.SemaphoreType.DMA((2,))],
    )
    def gather(table_hbm, idx_hbm, out_hbm, idx_v, rows_v, sem_in, sem_out):
        worker = lax.axis_index("s") * SC_CORES + lax.axis_index("c")
        pltpu.sync_copy(idx_hbm.at[worker], idx_v)

        def read(b):
            return pltpu.async_copy(table_hbm.at[idx_v.at[b]], rows_v.at[b % 2], sem_in.at[b % 2])

        def write(b):
            dst = out_hbm.at[pl.ds((worker * n_batches + b) * batch, batch)]
            return pltpu.async_copy(rows_v.at[b % 2], dst, sem_out.at[b % 2])

        reads = {0: read(0)}
        writes = {}
        for b in range(n_batches):
            reads[b].wait()
            if b + 1 < n_batches:
                if b >= 1:
                    writes[b - 1].wait()
                reads[b + 1] = read(b + 1)
            writes[b] = write(b)
        writes[n_batches - 2].wait()
        writes[n_batches - 1].wait()

    return gather(table, idx)


def _experts_kernel(te_ref, xs_ref, wg_ref, wu_ref, wd_ref, ys_ref):
    hi, lo = _unpack_bf16_halves(xs_ref[...])
    x = jnp.concatenate([hi.astype(jnp.bfloat16), lo.astype(jnp.bfloat16)], axis=1)
    w_gate_up = jnp.concatenate([wg_ref[0].astype(jnp.bfloat16), wu_ref[0].astype(jnp.bfloat16)], axis=1)
    wd = wd_ref[0].astype(jnp.bfloat16)
    gate_up = jnp.dot(x, w_gate_up, preferred_element_type=jnp.float32)
    gate = gate_up[:, :D_EXPERT]
    up = gate_up[:, D_EXPERT:]
    hid = (gate * jax.nn.sigmoid(gate) * up).astype(jnp.bfloat16)
    ys_ref[...] = _pack_bf16_halves(jnp.dot(hid, wd, preferred_element_type=jnp.float32))


def _experts(xs, tile_expert, n_valid, wg, wu, wd, tm):
    n_slots = xs.shape[0]
    row_block = lambda i, te: (i, 0)
    w_block = lambda i, te: (te[i], 0, 0)
    return pl.pallas_call(
        _experts_kernel,
        grid_spec=pltpu.PrefetchScalarGridSpec(
            num_scalar_prefetch=1,
            grid=(n_valid,),
            in_specs=[pl.BlockSpec((tm, HALF), row_block),
                      pl.BlockSpec((1, D_MODEL, D_EXPERT), w_block, pipeline_mode=pl.Buffered(2)),
                      pl.BlockSpec((1, D_MODEL, D_EXPERT), w_block, pipeline_mode=pl.Buffered(2)),
                      pl.BlockSpec((1, D_EXPERT, D_MODEL), w_block, pipeline_mode=pl.Buffered(2))],
            out_specs=pl.BlockSpec((tm, HALF), row_block),
        ),
        out_shape=jax.ShapeDtypeStruct((n_slots, HALF), jnp.uint32),
        compiler_params=pltpu.CompilerParams(
            dimension_semantics=("arbitrary",), vmem_limit_bytes=VMEM_LIMIT),
        name="experts",
    )(tile_expert, xs, wg, wu, wd)


def _final_kernel(x1_ref, y0_ref, y1_ref, p_ref, gfin_ref, *rest):
    out_ref = rest[-1]
    pt = jnp.transpose(p_ref[...])
    p0 = pt[:, 0:1]
    p1 = pt[:, 1:2]
    a_hi, a_lo = _unpack_bf16_halves(y0_ref[...])
    b_hi, b_lo = _unpack_bf16_halves(y1_ref[...])
    x1 = x1_ref[...]
    x2_hi = x1[:, :HALF] + (p0 * a_hi + p1 * b_hi)
    x2_lo = x1[:, HALF:] + (p0 * a_lo + p1 * b_lo)
    ms = (jnp.sum(x2_hi * x2_hi, axis=-1, keepdims=True)
          + jnp.sum(x2_lo * x2_lo, axis=-1, keepdims=True)) / D_MODEL
    scale = lax.rsqrt(ms + RMS_EPS)
    g = gfin_ref[...]
    out_ref[:, :HALF] = x2_hi * scale * g[:, :HALF]
    out_ref[:, HALF:] = x2_lo * scale * g[:, HALF:]


def _final(x1, yk, p, gfin, row_offset, n_rows, tm, out_rows, out_offset, out_buf=None):
    n_tokens = x1.shape[0]
    first = row_offset // tm
    second = (n_tokens + row_offset) // tm
    out_first = out_offset // tm
    in_specs = [pl.BlockSpec((tm, D_MODEL), lambda i: (first + i, 0)),
                pl.BlockSpec((tm, HALF), lambda i: (first + i, 0)),
                pl.BlockSpec((tm, HALF), lambda i: (second + i, 0)),
                pl.BlockSpec((SUBLANES, tm), lambda i: (0, first + i)),
                pl.BlockSpec((1, D_MODEL), lambda i: (0, 0))]
    args = [x1, yk, yk, p, gfin]
    aliases = {}
    if out_buf is not None:
        in_specs.append(pl.BlockSpec(memory_space=pl.ANY))
        args.append(out_buf)
        aliases = {len(args) - 1: 0}
    return pl.pallas_call(
        _final_kernel,
        grid=(n_rows // tm,),
        in_specs=in_specs,
        out_specs=pl.BlockSpec((tm, D_MODEL), lambda i: (out_first + i, 0)),
        out_shape=jax.ShapeDtypeStruct((out_rows, D_MODEL), jnp.float32),
        input_output_aliases=aliases,
        compiler_params=pltpu.CompilerParams(
            dimension_semantics=("arbitrary",), vmem_limit_bytes=VMEM_LIMIT),
        name="final",
    )(*args)


def _expert_tile(n_tokens):
    mean_rows = TOP_K * n_tokens // N_EXPERTS
    return -(-(mean_rows + 2 * math.isqrt(mean_rows)) // (2 * SUBLANES)) * (2 * SUBLANES)


def _routed_experts(h2p, eid, w_gate, w_up, w_down):
    n_tokens = h2p.shape[0]
    assert n_tokens % (SC_WORKERS * SC_DISPATCH_BATCH) == 0 and n_tokens % PLAN_TILE == 0
    assert (TOP_K * n_tokens) % (SC_WORKERS * SC_BATCH) == 0
    tile = _expert_tile(n_tokens)
    n_tiles = -(-(TOP_K * n_tokens + N_EXPERTS * (tile - 1)) // tile)
    pos, seg_end = _plan(eid, tile)
    pos = pos[:TOP_K]
    by_token = jnp.transpose(pos.reshape(TOP_K, SC_WORKERS, -1, SC_DISPATCH_BATCH), (1, 2, 0, 3))
    seg_end = seg_end[:, 0]
    n_valid = seg_end[N_EXPERTS - 1] // tile
    tile_row = jnp.arange(n_tiles, dtype=jnp.int32) * tile
    tile_expert = jnp.minimum(jnp.sum(seg_end[None, :] <= tile_row[:, None], axis=1), N_EXPERTS - 1)
    xs = _sc_dispatch_rows(h2p, by_token, n_tiles * tile)
    ys = _experts(xs, tile_expert.astype(jnp.int32), n_valid, w_gate, w_up, w_down, tile)
    return _sc_gather_rows(ys, pos.reshape(SC_WORKERS, -1, SC_BATCH))


def kernel(x_prompt, x_sample, state_conv_a, state_conv_b, g_mix, w_in, conv_a_w, conv_b_w, conv_b_bias,
           ln_g, ln_b, w_out, g_ffn, w_coarse, b_coarse, w_fine, b_fine, w_gate, w_up, w_down, g_final):
    assert g_mix.shape[0] == 1, "single trunk layer"
    batch, seq, _ = x_prompt.shape
    dec_batch, dec_seq, _ = x_sample.shape
    n_prompt = batch * seq
    n_sample = dec_batch * dec_seq
    bf16 = jnp.bfloat16

    wr = jnp.concatenate([
        w_coarse[0], jnp.zeros((D_MODEL, SUBLANES - N_EXPERT_GROUPS), jnp.float32),
        jnp.transpose(w_fine[0], (1, 0, 2)).reshape(D_MODEL, N_EXPERTS),
        jnp.zeros((D_MODEL, LANES - ROUTER_ROWS), jnp.float32)], axis=1)
    wr_hi = wr.astype(bf16)
    wr_lo = (wr - wr_hi.astype(jnp.float32)).astype(bf16)
    wr_both = jnp.concatenate([wr_hi, wr_lo], axis=1)
    br = jnp.concatenate([
        b_coarse[0], jnp.full((SUBLANES - N_EXPERT_GROUPS,), NEG_BIG, jnp.float32),
        b_fine[0].reshape(N_EXPERTS)]).reshape(ROUTER_ROWS, 1)

    params = (g_mix, w_in[0].astype(bf16), conv_a_w, conv_b_w, conv_b_bias, ln_g, ln_b,
              w_out[0].astype(bf16), g_ffn, wr_both, br)

    experts = (w_gate[0], w_up[0], w_down[0])
    gfin = g_final.reshape(1, D_MODEL)

    bufs = _mixer_prompt(x_prompt, params, n_prompt + n_sample, MIXER_TILE, 0, batch)
    na_p, nb_p = bufs[4:]
    x1, h2p, eid, p, na_s, nb_s = _mixer_sample(
        x_sample, state_conv_a, state_conv_b, params, bufs[:4], n_prompt, nseq=32)
    yk = _routed_experts(h2p, eid, *experts)
    y_p = _final(x1, yk, p, gfin, 0, n_prompt, FINAL_TILE, n_prompt, 0)
    y_s = _final(x1, yk, p, gfin, n_prompt, n_sample, FINAL_TILE, n_sample, 0)
    return (y_p.reshape(batch, seq, D_MODEL), y_s.reshape(dec_batch, dec_seq, D_MODEL),
            na_p[None], nb_p[None], na_s[None], nb_s[None])
```

```python
import functools
import math

import jax
import jax.numpy as jnp
from jax import lax
from jax.experimental import pallas as pl
from jax.experimental.pallas import tpu as pltpu
from jax.experimental.pallas import tpu_sc as plsc

D_MODEL = 1024
D_A = 512
D_B = 512
CONV_A = 3
CONV_B = 31
HALO_A = CONV_A - 1
HALO_B = CONV_B - 1
IN_COLS = 3 * D_A + 2 * D_B
N_EXPERT_GROUPS = 4
EXPERTS_PER_GROUP = 8
N_EXPERTS = N_EXPERT_GROUPS * EXPERTS_PER_GROUP
TOP_K = 2
D_EXPERT = D_MODEL // 4
RMS_EPS = 1e-6
LN_EPS = 1e-5

SUBLANES = 8
LANES = 128
PAD_A = SUBLANES
PAD_B = 32
ROUTER_ROWS = SUBLANES + N_EXPERTS
NEG_BIG = -1e30
VMEM_LIMIT = 56 * 1024 * 1024
HALF = D_MODEL // 2
HI_MASK = 0xFFFF0000

SC_CORES = 2
SC_SUBCORES = 16
SC_WORKERS = SC_CORES * SC_SUBCORES
SC_BATCH = 64
SC_DISPATCH_BATCH = 32

PLAN_TILE = 1024
FINAL_TILE = 1024
MIXER_TILE = 512


def _rms_scale(x):
    return x * lax.rsqrt(jnp.mean(x * x, axis=-1, keepdims=True) + RMS_EPS)


def _pack_bf16_halves(x):
    bits = lax.bitcast_convert_type(x.astype(jnp.bfloat16).astype(jnp.float32), jnp.uint32)
    return bits[:, :HALF] | (bits[:, HALF:] >> 16)


def _unpack_bf16_halves(w):
    hi = lax.bitcast_convert_type(w & jnp.uint32(HI_MASK), jnp.float32)
    lo = lax.bitcast_convert_type(w << 16, jnp.float32)
    return hi, lo


def _route(logits_t):
    rows = logits_t.shape[1]
    iota = lax.broadcasted_iota(jnp.int32, (SUBLANES, rows), 0)
    lc = logits_t[0:SUBLANES]
    cmax = jnp.max(lc, axis=0, keepdims=True)
    grp = jnp.min(jnp.where(lc == cmax, iota, SUBLANES), axis=0, keepdims=True)
    p_grp = 1.0 / jnp.sum(jnp.exp(lc - cmax), axis=0, keepdims=True)
    sel = logits_t[SUBLANES:2 * SUBLANES]
    for g in range(1, N_EXPERT_GROUPS):
        sel = jnp.where(grp == g, logits_t[(g + 1) * SUBLANES:(g + 2) * SUBLANES], sel)
    v1 = jnp.max(sel, axis=0, keepdims=True)
    i1 = jnp.min(jnp.where(sel == v1, iota, SUBLANES), axis=0, keepdims=True)
    sel2 = jnp.where(iota == i1, -jnp.inf, sel)
    v2 = jnp.max(sel2, axis=0, keepdims=True)
    i2 = jnp.min(jnp.where(sel2 == v2, iota, SUBLANES), axis=0, keepdims=True)
    e2 = jnp.exp(v2 - v1)
    den = 1.0 + e2
    p1 = p_grp / den
    p2 = p_grp * e2 / den
    base = grp * EXPERTS_PER_GROUP
    return (base + i1, base + i2), (p1, p2)


def _mixer_body(x_ref, sa_ref, sb_ref, gmix_ref, win_ref, caw_ref, cbw_ref, cbb_ref, lng_ref, lnb_ref,
                wout_ref, gffn_ref, wr_ref, br_ref,
                x1_ref, h2p_ref, eid_ref, p_ref, na_ref, nb_ref,
                proj_ref, uext_ref, gext_ref, z_ref, ush_ref, gsh_ref, *, nseq, tt, seq_chunk, row_chunk, n_sub):
    carried = sa_ref is None
    t = pl.program_id(1) if carried else None
    assert n_sub == 1 or nseq == 1
    sub = tt // n_sub

    if carried:
        @pl.when(t == 0)
        def _():
            uext_ref[:, 0:PAD_A, :] = jnp.zeros((nseq, PAD_A, D_A), jnp.float32)
            gext_ref[:, 0:PAD_B, :] = jnp.zeros((nseq, PAD_B, D_B), jnp.float32)
    else:
        uext_ref[:, PAD_A - HALO_A:PAD_A, :] = sa_ref[...]
        gext_ref[:, PAD_B - HALO_B:PAD_B, :] = sb_ref[...]

    def window(base_ref, shifted_ref, first_shift, s0, off, n_rows):
        r = off % SUBLANES
        a8 = off - r
        if r == 0:
            return base_ref[s0:s0 + seq_chunk, a8:a8 + n_rows, :]
        return shifted_ref[r - first_shift, s0:s0 + seq_chunk, a8:a8 + n_rows, :]

    n = seq_chunk * row_chunk
    col_chunk = 2 * LANES
    caw = caw_ref[...]
    cbw = cbw_ref[...]
    hs_bf = {}

    def rows_of(si):
        q0 = si * sub
        f0 = q0 if nseq == 1 else 0
        return q0, f0, nseq * sub

    def prep(si):
        q0, _, m = rows_of(si)
        x = x_ref[:, q0:q0 + sub, :].reshape(m, D_MODEL)
        hs_bf[si] = (_rms_scale(x) * gmix_ref[...]).astype(jnp.bfloat16)

    def dot_b_items(si):
        def item(c0):
            def run():
                q0, _, _ = rows_of(si)
                v_b = jnp.dot(hs_bf[si], win_ref[:, 3 * D_A + c0:3 * D_A + c0 + col_chunk],
                              preferred_element_type=jnp.float32)
                g_b = jnp.dot(hs_bf[si], win_ref[:, 3 * D_A + D_B + c0:3 * D_A + D_B + c0 + col_chunk],
                              preferred_element_type=jnp.float32)
                gext_ref[:, PAD_B + q0:PAD_B + q0 + sub, c0:c0 + col_chunk] = (
                    v_b * jax.nn.sigmoid(g_b)).reshape(nseq, sub, col_chunk)
            return run
        return [item(c0) for c0 in range(0, D_B, col_chunk)]

    def dot_a_items(si):
        def item(c0):
            def run():
                _, f0, m = rows_of(si)
                proj_ref[f0:f0 + m, c0:c0 + col_chunk] = jnp.dot(
                    hs_bf[si], win_ref[:, c0:c0 + col_chunk], preferred_element_type=jnp.float32)
            return run
        return [item(c0) for c0 in range(0, 3 * D_A, col_chunk)]

    def shift_b(si):
        q0, _, _ = rows_of(si)
        j_lo = 0 if si == 0 else q0 + PAD_B - SUBLANES
        j_hi = q0 + sub + PAD_B - SUBLANES
        for r in range(1, SUBLANES):
            gsh_ref[r - 1, :, j_lo:j_hi, :] = gext_ref[:, j_lo + r:j_hi + r, :]

    def chunks_of(si):
        q0, _, _ = rows_of(si)
        return [(s0, r0) for s0 in range(0, nseq, seq_chunk) for r0 in range(q0, q0 + sub, row_chunk)]

    def conv_b_items(si):
        def item(s0, r0):
            def run():
                lo = s0 * tt + r0
                acc_b = None
                for k in range(CONV_B):
                    off = PAD_B - HALO_B + k + r0
                    term = window(gext_ref, gsh_ref, 1, s0, off, row_chunk) * cbw[k:k + 1, :]
                    acc_b = term if acc_b is None else acc_b + term
                zb = acc_b.reshape(n, D_B) + cbb_ref[...]
                mu = jnp.mean(zb, axis=-1, keepdims=True)
                zc = zb - mu
                var = jnp.mean(zc * zc, axis=-1, keepdims=True)
                y = zc * lax.rsqrt(var + LN_EPS) * lng_ref[...] + lnb_ref[...]
                z_ref[lo:lo + n, D_A:] = (y * jax.nn.sigmoid(y)).astype(jnp.bfloat16)
            return run
        return [item(s0, r0) for s0, r0 in chunks_of(si)]

    def conv_a_item(si):
        def run():
            q0, f0, m = rows_of(si)
            c_a = proj_ref[f0:f0 + m, D_A:2 * D_A]
            v_a = proj_ref[f0:f0 + m, 2 * D_A:3 * D_A]
            uext_ref[:, PAD_A + q0:PAD_A + q0 + sub, :] = (c_a * v_a).reshape(nseq, sub, D_A)
            for r in range(SUBLANES - HALO_A, SUBLANES):
                ush_ref[r - (SUBLANES - HALO_A), :, q0:q0 + sub, :] = uext_ref[:, q0 + r:q0 + sub + r, :]
            for s0, r0 in chunks_of(si):
                lo = s0 * tt + r0
                acc_a = None
                for k in range(CONV_A):
                    off = PAD_A - HALO_A + k + r0
                    term = window(uext_ref, ush_ref, SUBLANES - HALO_A, s0, off, row_chunk) * caw[k:k + 1, :]
                    acc_a = term if acc_a is None else acc_a + term
                z_a = proj_ref[lo:lo + n, 0:D_A] * acc_a.reshape(n, D_A)
                z_ref[lo:lo + n, 0:D_A] = z_a.astype(jnp.bfloat16)
        return run

    def finish_item(si):
        def run():
            q0, f0, m = rows_of(si)
            x = x_ref[:, q0:q0 + sub, :].reshape(m, D_MODEL)
            x1 = x + jnp.dot(z_ref[f0:f0 + m, :], wout_ref[...], preferred_element_type=jnp.float32)
            x1_ref[f0:f0 + m, :] = x1
            h2 = _rms_scale(x1) * gffn_ref[...]
            h2_hi = h2.astype(jnp.bfloat16)
            h2p_ref[f0:f0 + m, :] = _pack_bf16_halves(h2)
            h2_lo = (h2 - h2_hi.astype(jnp.float32)).astype(jnp.bfloat16)
            both = jnp.dot(h2_hi, wr_ref[...], preferred_element_type=jnp.float32)
            cross = jnp.dot(h2_lo, wr_ref[:, 0:LANES], preferred_element_type=jnp.float32)
            logits = both[:, 0:LANES] + both[:, LANES:] + cross
            logits_t = jnp.transpose(logits)[0:ROUTER_ROWS] + br_ref[...]
            (e1, e2), (p1, p2) = _route(logits_t)
            iota = lax.broadcasted_iota(jnp.int32, (SUBLANES, m), 0)
            eid_ref[:, f0:f0 + m] = jnp.where(iota == 0, e1, jnp.where(iota == 1, e2, 0))
            p_ref[:, f0:f0 + m] = jnp.where(iota == 0, p1, jnp.where(iota == 1, p2, 0.0))
        return run

    def interleave(main, fill):
        for i, item in enumerate(main):
            item()
            for f in fill[i * len(fill) // len(main):(i + 1) * len(fill) // len(main)]:
                f()

    for si in range(n_sub):
        prep(si)
    for item in dot_b_items(0):
        item()
    shift_b(0)
    carry_over = []
    for si in range(n_sub):
        fill = carry_over + dot_a_items(si)
        if si + 1 < n_sub:
            fill = fill + dot_b_items(si + 1)
        interleave(conv_b_items(si), fill + [conv_a_item(si)])
        if si + 1 < n_sub:
            shift_b(si + 1)
        carry_over = [finish_item(si)]
    for item in carry_over:
        item()

    if carried:
        @pl.when(t == pl.num_programs(1) - 1)
        def _():
            na_ref[...] = uext_ref[:, PAD_A + tt - HALO_A:PAD_A + tt, :]
            nb_ref[...] = gext_ref[:, PAD_B + tt - HALO_B:PAD_B + tt, :]
        uext_ref[:, 0:PAD_A, :] = uext_ref[:, tt:tt + PAD_A, :]
        gext_ref[:, 0:PAD_B, :] = gext_ref[:, tt:tt + PAD_B, :]
    else:
        na_ref[...] = uext_ref[:, PAD_A + tt - HALO_A:PAD_A + tt, :]
        nb_ref[...] = gext_ref[:, PAD_B + tt - HALO_B:PAD_B + tt, :]


def _mixer_prompt_kernel(x_ref, *refs, **kw):
    _mixer_body(x_ref, None, None, *refs, **kw)


def _mixer_sample_kernel(x_ref, sa_ref, sb_ref, *refs, **kw):
    n_params = 11
    params = refs[:n_params]
    rest = refs[n_params + 4:]
    _mixer_body(x_ref, sa_ref, sb_ref, *params, *rest, **kw)


def _full(shape):
    return pl.BlockSpec(shape, lambda *_: (0,) * len(shape))


def _mixer_param_specs():
    return [
        _full((1, D_MODEL)),
        _full((D_MODEL, IN_COLS)),
        pl.BlockSpec((None, CONV_A, D_A), lambda *_: (0, 0, 0)),
        pl.BlockSpec((None, CONV_B, D_B), lambda *_: (0, 0, 0)),
        _full((1, D_B)),
        _full((1, D_B)),
        _full((1, D_B)),
        _full((D_MODEL, D_MODEL)),
        _full((1, D_MODEL)),
        _full((D_MODEL, 2 * LANES)),
        _full((ROUTER_ROWS, 1)),
    ]


def _mixer_scratch(nseq, tt):
    rows = nseq * tt
    return [
        pltpu.VMEM((rows, 3 * D_A), jnp.float32),
        pltpu.VMEM((nseq, PAD_A + tt, D_A), jnp.float32),
        pltpu.VMEM((nseq, PAD_B + tt, D_B), jnp.float32),
        pltpu.VMEM((rows, D_MODEL), jnp.bfloat16),
        pltpu.VMEM((HALO_A, nseq, PAD_A + tt - SUBLANES, D_A), jnp.float32),
        pltpu.VMEM((SUBLANES - 1, nseq, PAD_B + tt - SUBLANES, D_B), jnp.float32),
    ]


def _mixer_prompt(x, params, n_tokens_total, tt, seq_first, batch):
    seq = x.shape[1]
    n_t = seq // tt
    tok = lambda b, t: (b * n_t + t, 0)
    lane_tok = lambda b, t: (0, b * n_t + t)
    out_shape = [
        jax.ShapeDtypeStruct((n_tokens_total, D_MODEL), jnp.float32),
        jax.ShapeDtypeStruct((n_tokens_total, HALF), jnp.uint32),
        jax.ShapeDtypeStruct((SUBLANES, n_tokens_total), jnp.int32),
        jax.ShapeDtypeStruct((SUBLANES, n_tokens_total), jnp.float32),
        jax.ShapeDtypeStruct((batch, HALO_A, D_A), jnp.float32),
        jax.ShapeDtypeStruct((batch, HALO_B, D_B), jnp.float32),
    ]
    out_specs = [
        pl.BlockSpec((tt, D_MODEL), tok),
        pl.BlockSpec((tt, HALF), tok),
        pl.BlockSpec((SUBLANES, tt), lane_tok),
        pl.BlockSpec((SUBLANES, tt), lane_tok),
        pl.BlockSpec((1, HALO_A, D_A), lambda b, t: (b, 0, 0)),
        pl.BlockSpec((1, HALO_B, D_B), lambda b, t: (b, 0, 0)),
    ]
    return pl.pallas_call(
        functools.partial(_mixer_prompt_kernel, nseq=1, tt=tt, seq_chunk=1, row_chunk=64, n_sub=1),
        grid=(batch, n_t),
        in_specs=[pl.BlockSpec((1, tt, D_MODEL), lambda b, t: (b + seq_first, t, 0))] + _mixer_param_specs(),
        out_specs=out_specs,
        out_shape=out_shape,
        scratch_shapes=_mixer_scratch(1, tt),
        compiler_params=pltpu.CompilerParams(
            dimension_semantics=("arbitrary", "arbitrary"), vmem_limit_bytes=VMEM_LIMIT),
        name="mixer_prompt",
    )(x, *params)


def _mixer_sample(x, state_a, state_b, params, bufs, row_offset, nseq):
    batch, tt, _ = x.shape
    rows = nseq * tt
    first = row_offset // rows
    tok = lambda i: (first + i, 0)
    lane_tok = lambda i: (0, first + i)
    x1, h2p, eid, p = bufs
    out_shape = [
        jax.ShapeDtypeStruct(x1.shape, x1.dtype),
        jax.ShapeDtypeStruct(h2p.shape, h2p.dtype),
        jax.ShapeDtypeStruct(eid.shape, eid.dtype),
        jax.ShapeDtypeStruct(p.shape, p.dtype),
        jax.ShapeDtypeStruct((batch, HALO_A, D_A), jnp.float32),
        jax.ShapeDtypeStruct((batch, HALO_B, D_B), jnp.float32),
    ]
    out_specs = [
        pl.BlockSpec((rows, D_MODEL), tok),
        pl.BlockSpec((rows, HALF), tok),
        pl.BlockSpec((SUBLANES, rows), lane_tok),
        pl.BlockSpec((SUBLANES, rows), lane_tok),
        pl.BlockSpec((nseq, HALO_A, D_A), lambda i: (i, 0, 0)),
        pl.BlockSpec((nseq, HALO_B, D_B), lambda i: (i, 0, 0)),
    ]
    any_spec = pl.BlockSpec(memory_space=pl.ANY)
    in_specs = ([pl.BlockSpec((nseq, tt, D_MODEL), lambda i: (i, 0, 0)),
                 pl.BlockSpec((None, nseq, HALO_A, D_A), lambda i: (0, i, 0, 0)),
                 pl.BlockSpec((None, nseq, HALO_B, D_B), lambda i: (0, i, 0, 0))]
                + _mixer_param_specs() + [any_spec] * 4)
    n_in = len(in_specs)
    return pl.pallas_call(
        functools.partial(_mixer_sample_kernel, nseq=nseq, tt=tt, seq_chunk=8, row_chunk=tt, n_sub=1),
        grid=(batch // nseq,),
        in_specs=in_specs,
        out_specs=out_specs,
        out_shape=out_shape,
        scratch_shapes=_mixer_scratch(nseq, tt),
        input_output_aliases={n_in - 4: 0, n_in - 3: 1, n_in - 2: 2, n_in - 1: 3},
        compiler_params=pltpu.CompilerParams(
            dimension_semantics=("arbitrary",), vmem_limit_bytes=VMEM_LIMIT),
        name="mixer_sample",
    )(x, state_a, state_b, *params, x1, h2p, eid, p)


def _plan_kernel(eid_ref, pos_ref, seg_end_ref, carry_ref, start_ref, earlier_ref, *, tile):
    ph = pl.program_id(0)
    i = pl.program_id(1)
    tt = eid_ref.shape[1]

    @pl.when(i == 0)
    def _():
        @pl.when(ph == 1)
        def _():
            tiles = jnp.floor((carry_ref[...] + (tile - 0.5)) * (1.0 / tile))
            below = (lax.broadcasted_iota(jnp.int32, (N_EXPERTS, N_EXPERTS), 0)
                     > lax.broadcasted_iota(jnp.int32, (N_EXPERTS, N_EXPERTS), 1))
            start = jnp.dot(jnp.where(below, 1.0, 0.0).astype(jnp.bfloat16), tiles.astype(jnp.bfloat16),
                            preferred_element_type=jnp.float32) * tile
            start_ref[...] = start
            seg_end_ref[...] = (start + tiles * tile).astype(jnp.int32)
            earlier = (lax.broadcasted_iota(jnp.int32, (tt, tt), 0)
                       < lax.broadcasted_iota(jnp.int32, (tt, tt), 1))
            earlier_ref[...] = jnp.where(earlier, 1.0, 0.0).astype(jnp.bfloat16)
        carry_ref[...] = jnp.zeros_like(carry_ref)

    eid = eid_ref[...]
    experts = lax.broadcasted_iota(jnp.int32, (N_EXPERTS, tt), 0)
    oh0 = experts == eid[0:1]
    oh1 = experts == eid[1:2]
    oh = jnp.where(oh0 | oh1, 1.0, 0.0)

    @pl.when(ph == 1)
    def _():
        within = jnp.dot(oh.astype(jnp.bfloat16), earlier_ref[...], preferred_element_type=jnp.float32)
        slot_of = within + carry_ref[:, 0:1] + start_ref[:, 0:1]
        s0 = jnp.sum(jnp.where(oh0, slot_of, 0.0), axis=0, keepdims=True)
        s1 = jnp.sum(jnp.where(oh1, slot_of, 0.0), axis=0, keepdims=True)
        k = lax.broadcasted_iota(jnp.int32, (SUBLANES, tt), 0)
        pos_ref[...] = jnp.where(k == 0, s0, jnp.where(k == 1, s1, 0.0)).astype(jnp.int32)

    carry_ref[...] = carry_ref[...] + jnp.sum(oh, axis=1, keepdims=True)


def _plan(eid, tile):
    n_tokens = eid.shape[1]
    return pl.pallas_call(
        functools.partial(_plan_kernel, tile=tile),
        grid=(2, n_tokens // PLAN_TILE),
        in_specs=[pl.BlockSpec((SUBLANES, PLAN_TILE), lambda ph, i: (0, i))],
        out_specs=[pl.BlockSpec((SUBLANES, PLAN_TILE), lambda ph, i: (0, i * ph)),
                   pl.BlockSpec((N_EXPERTS, LANES), lambda ph, i: (0, 0))],
        out_shape=[jax.ShapeDtypeStruct((SUBLANES, n_tokens), jnp.int32),
                   jax.ShapeDtypeStruct((N_EXPERTS, LANES), jnp.int32)],
        scratch_shapes=[pltpu.VMEM((N_EXPERTS, LANES), jnp.float32),
                        pltpu.VMEM((N_EXPERTS, LANES), jnp.float32),
                        pltpu.VMEM((PLAN_TILE, PLAN_TILE), jnp.bfloat16)],
        compiler_params=pltpu.CompilerParams(dimension_semantics=("arbitrary", "arbitrary")),
        name="route_plan",
    )(eid)


def _sc_mesh():
    return plsc.VectorSubcoreMesh(core_axis_name="c", subcore_axis_name="s")


def _sc_dispatch_rows(table, pos, n_out):
    n_workers, n_batches, top_k, batch = pos.shape
    n_rows, words = table.shape
    assert n_workers * n_batches * batch == n_rows and n_batches >= 2

    @functools.partial(
        pl.kernel, mesh=_sc_mesh(),
        out_type=jax.ShapeDtypeStruct((n_out, words), table.dtype),
        scratch_types=[pltpu.VMEM((n_batches, top_k, batch), jnp.int32),
                       pltpu.VMEM((2, batch, words), table.dtype),
                       pltpu.SemaphoreType.DMA((2,)),
                       pltpu.SemaphoreType.DMA((2, top_k))],
    )
    def dispatch(table_hbm, pos_hbm, out_hbm, idx_v, rows_v, sem_in, sem_out):
        worker = lax.axis_index("s") * SC_CORES + lax.axis_index("c")
        pltpu.sync_copy(pos_hbm.at[worker], idx_v)

        def read(b):
            src = table_hbm.at[pl.ds((worker * n_batches + b) * batch, batch)]
            return pltpu.async_copy(src, rows_v.at[b % 2], sem_in.at[b % 2])

        def write(b):
            return [pltpu.async_copy(rows_v.at[b % 2], out_hbm.at[idx_v.at[b, k]], sem_out.at[b % 2, k])
                    for k in range(top_k)]

        reads = {0: read(0)}
        writes = {}
        for b in range(n_batches):
            reads[b].wait()
            if b + 1 < n_batches:
                if b >= 1:
                    for w in writes[b - 1]:
                        w.wait()
                reads[b + 1] = read(b + 1)
            writes[b] = write(b)
        for b in (n_batches - 2, n_batches - 1):
            for w in writes[b]:
                w.wait()

    return dispatch(table, pos)


def _sc_gather_rows(table, idx):
    n_workers, n_batches, batch = idx.shape
    words = table.shape[1]
    assert n_batches >= 2

    @functools.partial(
        pl.kernel, mesh=_sc_mesh(),
        out_type=jax.ShapeDtypeStruct((n_workers * n_batches * batch, words), table.dtype),
        scratch_types=[pltpu.VMEM((n_batches, batch), jnp.int32),
                       pltpu.VMEM((2, batch, words), table.dtype),
                       pltpu.SemaphoreType.DMA((2,)),
                       pltpu.SemaphoreType.DMA((2,))],
    )
    def gather(table_hbm, idx_hbm, out_hbm, idx_v, rows_v, sem_in, sem_out):
        worker = lax.axis_index("s") * SC_CORES + lax.axis_index("c")
        pltpu.sync_copy(idx_hbm.at[worker], idx_v)

        def read(b):
            return pltpu.async_copy(table_hbm.at[idx_v.at[b]], rows_v.at[b % 2], sem_in.at[b % 2])

        def write(b):
            dst = out_hbm.at[pl.ds((worker * n_batches + b) * batch, batch)]
            return pltpu.async_copy(rows_v.at[b % 2], dst, sem_out.at[b % 2])

        reads = {0: read(0)}
        writes = {}
        for b in range(n_batches):
            reads[b].wait()
            if b + 1 < n_batches:
                if b >= 1:
                    writes[b - 1].wait()
                reads[b + 1] = read(b + 1)
            writes[b] = write(b)
        writes[n_batches - 2].wait()
        writes[n_batches - 1].wait()

    return gather(table, idx)


def _experts_kernel(te_ref, xs_ref, wg_ref, wu_ref, wd_ref, ys_ref):
    hi, lo = _unpack_bf16_halves(xs_ref[...])
    x = jnp.concatenate([hi.astype(jnp.bfloat16), lo.astype(jnp.bfloat16)], axis=1)
    w_gate_up = jnp.concatenate([wg_ref[0].astype(jnp.bfloat16), wu_ref[0].astype(jnp.bfloat16)], axis=1)
    wd = wd_ref[0].astype(jnp.bfloat16)
    gate_up = jnp.dot(x, w_gate_up, preferred_element_type=jnp.float32)
    gate = gate_up[:, :D_EXPERT]
    up = gate_up[:, D_EXPERT:]
    hid = (gate * jax.nn.sigmoid(gate) * up).astype(jnp.bfloat16)
    ys_ref[...] = _pack_bf16_halves(jnp.dot(hid, wd, preferred_element_type=jnp.float32))


def _experts(xs, tile_expert, n_valid, wg, wu, wd, tm):
    n_slots = xs.shape[0]
    row_block = lambda i, te: (i, 0)
    w_block = lambda i, te: (te[i], 0, 0)
    return pl.pallas_call(
        _experts_kernel,
        grid_spec=pltpu.PrefetchScalarGridSpec(
            num_scalar_prefetch=1,
            grid=(n_valid,),
            in_specs=[pl.BlockSpec((tm, HALF), row_block),
                      pl.BlockSpec((1, D_MODEL, D_EXPERT), w_block, pipeline_mode=pl.Buffered(2)),
                      pl.BlockSpec((1, D_MODEL, D_EXPERT), w_block, pipeline_mode=pl.Buffered(2)),
                      pl.BlockSpec((1, D_EXPERT, D_MODEL), w_block, pipeline_mode=pl.Buffered(2))],
            out_specs=pl.BlockSpec((tm, HALF), row_block),
        ),
        out_shape=jax.ShapeDtypeStruct((n_slots, HALF), jnp.uint32),
        compiler_params=pltpu.CompilerParams(
            dimension_semantics=("arbitrary",), vmem_limit_bytes=VMEM_LIMIT),
        name="experts",
    )(tile_expert, xs, wg, wu, wd)


def _final_kernel(x1_ref, y0_ref, y1_ref, p_ref, gfin_ref, *rest):
    out_ref = rest[-1]
    pt = jnp.transpose(p_ref[...])
    p0 = pt[:, 0:1]
    p1 = pt[:, 1:2]
    a_hi, a_lo = _unpack_bf16_halves(y0_ref[...])
    b_hi, b_lo = _unpack_bf16_halves(y1_ref[...])
    x1 = x1_ref[...]
    x2_hi = x1[:, :HALF] + (p0 * a_hi + p1 * b_hi)
    x2_lo = x1[:, HALF:] + (p0 * a_lo + p1 * b_lo)
    ms = (jnp.sum(x2_hi * x2_hi, axis=-1, keepdims=True)
          + jnp.sum(x2_lo * x2_lo, axis=-1, keepdims=True)) / D_MODEL
    scale = lax.rsqrt(ms + RMS_EPS)
    g = gfin_ref[...]
    out_ref[:, :HALF] = x2_hi * scale * g[:, :HALF]
    out_ref[:, HALF:] = x2_lo * scale * g[:, HALF:]


def _final(x1, yk, p, gfin, row_offset, n_rows, tm, out_rows, out_offset, out_buf=None):
    n_tokens = x1.shape[0]
    first = row_offset // tm
    second = (n_tokens + row_offset) // tm
    out_first = out_offset // tm
    in_specs = [pl.BlockSpec((tm, D_MODEL), lambda i: (first + i, 0)),
                pl.BlockSpec((tm, HALF), lambda i: (first + i, 0)),
                pl.BlockSpec((tm, HALF), lambda i: (second + i, 0)),
                pl.BlockSpec((SUBLANES, tm), lambda i: (0, first + i)),
                pl.BlockSpec((1, D_MODEL), lambda i: (0, 0))]
    args = [x1, yk, yk, p, gfin]
    aliases = {}
    if out_buf is not None:
        in_specs.append(pl.BlockSpec(memory_space=pl.ANY))
        args.append(out_buf)
        aliases = {len(args) - 1: 0}
    return pl.pallas_call(
        _final_kernel,
        grid=(n_rows // tm,),
        in_specs=in_specs,
        out_specs=pl.BlockSpec((tm, D_MODEL), lambda i: (out_first + i, 0)),
        out_shape=jax.ShapeDtypeStruct((out_rows, D_MODEL), jnp.float32),
        input_output_aliases=aliases,
        compiler_params=pltpu.CompilerParams(
            dimension_semantics=("arbitrary",), vmem_limit_bytes=VMEM_LIMIT),
        name="final",
    )(*args)


def _expert_tile(n_tokens):
    mean_rows = TOP_K * n_tokens // N_EXPERTS
    return -(-(mean_rows + 2 * math.isqrt(mean_rows)) // (2 * SUBLANES)) * (2 * SUBLANES)


def _routed_experts(h2p, eid, w_gate, w_up, w_down):
    n_tokens = h2p.shape[0]
    assert n_tokens % (SC_WORKERS * SC_DISPATCH_BATCH) == 0 and n_tokens % PLAN_TILE == 0
    assert (TOP_K * n_tokens) % (SC_WORKERS * SC_BATCH) == 0
    tile = _expert_tile(n_tokens)
    n_tiles = -(-(TOP_K * n_tokens + N_EXPERTS * (tile - 1)) // tile)
    pos, seg_end = _plan(eid, tile)
    pos = pos[:TOP_K]
    by_token = jnp.transpose(pos.reshape(TOP_K, SC_WORKERS, -1, SC_DISPATCH_BATCH), (1, 2, 0, 3))
    seg_end = seg_end[:, 0]
    n_valid = seg_end[N_EXPERTS - 1] // tile
    tile_row = jnp.arange(n_tiles, dtype=jnp.int32) * tile
    tile_expert = jnp.minimum(jnp.sum(seg_end[None, :] <= tile_row[:, None], axis=1), N_EXPERTS - 1)
    xs = _sc_dispatch_rows(h2p, by_token, n_tiles * tile)
    ys = _experts(xs, tile_expert.astype(jnp.int32), n_valid, w_gate, w_up, w_down, tile)
    return _sc_gather_rows(ys, pos.reshape(SC_WORKERS, -1, SC_BATCH))


def kernel(x_prompt, x_sample, state_conv_a, state_conv_b, g_mix, w_in, conv_a_w, conv_b_w, conv_b_bias,
           ln_g, ln_b, w_out, g_ffn, w_coarse, b_coarse, w_fine, b_fine, w_gate, w_up, w_down, g_final):
    assert g_mix.shape[0] == 1, "single trunk layer"
    batch, seq, _ = x_prompt.shape
    dec_batch, dec_seq, _ = x_sample.shape
    n_prompt = batch * seq
    n_sample = dec_batch * dec_seq
    bf16 = jnp.bfloat16

    wr = jnp.concatenate([
        w_coarse[0], jnp.zeros((D_MODEL, SUBLANES - N_EXPERT_GROUPS), jnp.float32),
        jnp.transpose(w_fine[0], (1, 0, 2)).reshape(D_MODEL, N_EXPERTS),
        jnp.zeros((D_MODEL, LANES - ROUTER_ROWS), jnp.float32)], axis=1)
    wr_hi = wr.astype(bf16)
    wr_lo = (wr - wr_hi.astype(jnp.float32)).astype(bf16)
    wr_both = jnp.concatenate([wr_hi, wr_lo], axis=1)
    br = jnp.concatenate([
        b_coarse[0], jnp.full((SUBLANES - N_EXPERT_GROUPS,), NEG_BIG, jnp.float32),
        b_fine[0].reshape(N_EXPERTS)]).reshape(ROUTER_ROWS, 1)

    params = (g_mix, w_in[0].astype(bf16), conv_a_w, conv_b_w, conv_b_bias, ln_g, ln_b,
              w_out[0].astype(bf16), g_ffn, wr_both, br)

    experts = (w_gate[0], w_up[0], w_down[0])
    gfin = g_final.reshape(1, D_MODEL)

    bufs = _mixer_prompt(x_prompt, params, n_prompt + n_sample, MIXER_TILE, 0, batch)
    na_p, nb_p = bufs[4:]
    x1, h2p, eid, p, na_s, nb_s = _mixer_sample(
        x_sample, state_conv_a, state_conv_b, params, bufs[:4], n_prompt, nseq=32)
    yk = _routed_experts(h2p, eid, *experts)
    y_p = _final(x1, yk, p, gfin, 0, n_prompt, FINAL_TILE, n_prompt, 0)
    y_s = _final(x1, yk, p, gfin, n_prompt, n_sample, FINAL_TILE, n_sample, 0)
    return (y_p.reshape(batch, seq, D_MODEL), y_s.reshape(dec_batch, dec_seq, D_MODEL),
            na_p[None], nb_p[None], na_s[None], nb_s[None])
```

```python
import functools
import math

import jax
import jax.numpy as jnp
from jax import lax
from jax.experimental import pallas as pl
from jax.experimental.pallas import tpu as pltpu
from jax.experimental.pallas import tpu_sc as plsc

D_MODEL = 1024
D_A = 512
D_B = 512
CONV_A = 3
CONV_B = 31
HALO_A = CONV_A - 1
HALO_B = CONV_B - 1
IN_COLS = 3 * D_A + 2 * D_B
N_EXPERT_GROUPS = 4
EXPERTS_PER_GROUP = 8
N_EXPERTS = N_EXPERT_GROUPS * EXPERTS_PER_GROUP
TOP_K = 2
D_EXPERT = D_MODEL // 4
RMS_EPS = 1e-6
LN_EPS = 1e-5

SUBLANES = 8
LANES = 128
PAD_A = SUBLANES
PAD_B = 32
ROUTER_ROWS = SUBLANES + N_EXPERTS
NEG_BIG = -1e30
VMEM_LIMIT = 56 * 1024 * 1024
HALF = D_MODEL // 2
HI_MASK = 0xFFFF0000

SC_CORES = 2
SC_SUBCORES = 16
SC_WORKERS = SC_CORES * SC_SUBCORES
SC_BATCH = 64
SC_DISPATCH_BATCH = 32

PLAN_TILE = 1024
FINAL_TILE = 1024
MIXER_TILE = 512


def _rms_scale(x):
    return x * lax.rsqrt(jnp.mean(x * x, axis=-1, keepdims=True) + RMS_EPS)


def _pack_bf16_halves(x):
    bits = lax.bitcast_convert_type(x.astype(jnp.bfloat16).astype(jnp.float32), jnp.uint32)
    return bits[:, :HALF] | (bits[:, HALF:] >> 16)


def _unpack_bf16_halves(w):
    hi = lax.bitcast_convert_type(w & jnp.uint32(HI_MASK), jnp.float32)
    lo = lax.bitcast_convert_type(w << 16, jnp.float32)
    return hi, lo


def _route(logits_t):
    rows = logits_t.shape[1]
    iota = lax.broadcasted_iota(jnp.int32, (SUBLANES, rows), 0)
    lc = logits_t[0:SUBLANES]
    cmax = jnp.max(lc, axis=0, keepdims=True)
    grp = jnp.min(jnp.where(lc == cmax, iota, SUBLANES), axis=0, keepdims=True)
    p_grp = 1.0 / jnp.sum(jnp.exp(lc - cmax), axis=0, keepdims=True)
    sel = logits_t[SUBLANES:2 * SUBLANES]
    for g in range(1, N_EXPERT_GROUPS):
        sel = jnp.where(grp == g, logits_t[(g + 1) * SUBLANES:(g + 2) * SUBLANES], sel)
    v1 = jnp.max(sel, axis=0, keepdims=True)
    i1 = jnp.min(jnp.where(sel == v1, iota, SUBLANES), axis=0, keepdims=True)
    sel2 = jnp.where(iota == i1, -jnp.inf, sel)
    v2 = jnp.max(sel2, axis=0, keepdims=True)
    i2 = jnp.min(jnp.where(sel2 == v2, iota, SUBLANES), axis=0, keepdims=True)
    e2 = jnp.exp(v2 - v1)
    den = 1.0 + e2
    p1 = p_grp / den
    p2 = p_grp * e2 / den
    base = grp * EXPERTS_PER_GROUP
    return (base + i1, base + i2), (p1, p2)


def _mixer_body(x_ref, sa_ref, sb_ref, gmix_ref, win_ref, caw_ref, cbw_ref, cbb_ref, lng_ref, lnb_ref,
                wout_ref, gffn_ref, wr_ref, br_ref,
                x1_ref, h2p_ref, eid_ref, p_ref, na_ref, nb_ref, cnt_ref,
                proj_ref, uext_ref, gext_ref, z_ref, ush_ref, gsh_ref, *, nseq, tt, seq_chunk, row_chunk, n_sub,
                cnt_in_ref=None):
    carried = sa_ref is None
    t = pl.program_id(1) if carried else None

    first_step = ((pl.program_id(0) == 0) & (t == 0)) if carried else (pl.program_id(0) == 0)

    @pl.when(first_step)
    def _():
        cnt_ref[...] = jnp.zeros_like(cnt_ref) if cnt_in_ref is None else cnt_in_ref[...]
    assert n_sub == 1 or nseq == 1
    sub = tt // n_sub

    if carried:
        @pl.when(t == 0)
        def _():
            uext_ref[:, 0:PAD_A, :] = jnp.zeros((nseq, PAD_A, D_A), jnp.float32)
            gext_ref[:, 0:PAD_B, :] = jnp.zeros((nseq, PAD_B, D_B), jnp.float32)
    else:
        uext_ref[:, PAD_A - HALO_A:PAD_A, :] = sa_ref[...]
        gext_ref[:, PAD_B - HALO_B:PAD_B, :] = sb_ref[...]

    def window(base_ref, shifted_ref, first_shift, s0, off, n_rows):
        r = off % SUBLANES
        a8 = off - r
        if r == 0:
            return base_ref[s0:s0 + seq_chunk, a8:a8 + n_rows, :]
        return shifted_ref[r - first_shift, s0:s0 + seq_chunk, a8:a8 + n_rows, :]

    n = seq_chunk * row_chunk
    col_chunk = 2 * LANES
    caw = caw_ref[...]
    cbw = cbw_ref[...]
    hs_bf = {}

    def rows_of(si):
        q0 = si * sub
        f0 = q0 if nseq == 1 else 0
        return q0, f0, nseq * sub

    def prep(si):
        q0, _, m = rows_of(si)
        x = x_ref[:, q0:q0 + sub, :].reshape(m, D_MODEL)
        hs_bf[si] = (_rms_scale(x) * gmix_ref[...]).astype(jnp.bfloat16)

    def dot_b_items(si):
        def item(c0):
            def run():
                q0, _, _ = rows_of(si)
                v_b = jnp.dot(hs_bf[si], win_ref[:, 3 * D_A + c0:3 * D_A + c0 + col_chunk],
                              preferred_element_type=jnp.float32)
                g_b = jnp.dot(hs_bf[si], win_ref[:, 3 * D_A + D_B + c0:3 * D_A + D_B + c0 + col_chunk],
                              preferred_element_type=jnp.float32)
                gext_ref[:, PAD_B + q0:PAD_B + q0 + sub, c0:c0 + col_chunk] = (
                    v_b * jax.nn.sigmoid(g_b)).reshape(nseq, sub, col_chunk)
            return run
        return [item(c0) for c0 in range(0, D_B, col_chunk)]

    def dot_a_items(si):
        def item(c0):
            def run():
                _, f0, m = rows_of(si)
                proj_ref[f0:f0 + m, c0:c0 + col_chunk] = jnp.dot(
                    hs_bf[si], win_ref[:, c0:c0 + col_chunk], preferred_element_type=jnp.float32)
            return run
        return [item(c0) for c0 in range(0, 3 * D_A, col_chunk)]

    def shift_b(si):
        q0, _, _ = rows_of(si)
        j_lo = 0 if si == 0 else q0 + PAD_B - SUBLANES
        j_hi = q0 + sub + PAD_B - SUBLANES
        for r in range(1, SUBLANES):
            gsh_ref[r - 1, :, j_lo:j_hi, :] = gext_ref[:, j_lo + r:j_hi + r, :]

    def chunks_of(si):
        q0, _, _ = rows_of(si)
        return [(s0, r0) for s0 in range(0, nseq, seq_chunk) for r0 in range(q0, q0 + sub, row_chunk)]

    def conv_b_items(si):
        def item(s0, r0):
            def run():
                lo = s0 * tt + r0
                acc_b = None
                for k in range(CONV_B):
                    off = PAD_B - HALO_B + k + r0
                    term = window(gext_ref, gsh_ref, 1, s0, off, row_chunk) * cbw[k:k + 1, :]
                    acc_b = term if acc_b is None else acc_b + term
                zb = acc_b.reshape(n, D_B) + cbb_ref[...]
                mu = jnp.mean(zb, axis=-1, keepdims=True)
                zc = zb - mu
                var = jnp.mean(zc * zc, axis=-1, keepdims=True)
                y = zc * lax.rsqrt(var + LN_EPS) * lng_ref[...] + lnb_ref[...]
                z_ref[lo:lo + n, D_A:] = (y * jax.nn.sigmoid(y)).astype(jnp.bfloat16)
            return run
        return [item(s0, r0) for s0, r0 in chunks_of(si)]

    def conv_a_item(si):
        def run():
            q0, f0, m = rows_of(si)
            c_a = proj_ref[f0:f0 + m, D_A:2 * D_A]
            v_a = proj_ref[f0:f0 + m, 2 * D_A:3 * D_A]
            uext_ref[:, PAD_A + q0:PAD_A + q0 + sub, :] = (c_a * v_a).reshape(nseq, sub, D_A)
            for r in range(SUBLANES - HALO_A, SUBLANES):
                ush_ref[r - (SUBLANES - HALO_A), :, q0:q0 + sub, :] = uext_ref[:, q0 + r:q0 + sub + r, :]
            for s0, r0 in chunks_of(si):
                lo = s0 * tt + r0
                acc_a = None
                for k in range(CONV_A):
                    off = PAD_A - HALO_A + k + r0
                    term = window(uext_ref, ush_ref, SUBLANES - HALO_A, s0, off, row_chunk) * caw[k:k + 1, :]
                    acc_a = term if acc_a is None else acc_a + term
                z_a = proj_ref[lo:lo + n, 0:D_A] * acc_a.reshape(n, D_A)
                z_ref[lo:lo + n, 0:D_A] = z_a.astype(jnp.bfloat16)
        return run

    def finish_item(si):
        def run():
            q0, f0, m = rows_of(si)
            x = x_ref[:, q0:q0 + sub, :].reshape(m, D_MODEL)
            x1 = x + jnp.dot(z_ref[f0:f0 + m, :], wout_ref[...], preferred_element_type=jnp.float32)
            x1_ref[f0:f0 + m, :] = x1
            h2 = _rms_scale(x1) * gffn_ref[...]
            h2_hi = h2.astype(jnp.bfloat16)
            h2p_ref[f0:f0 + m, :] = _pack_bf16_halves(h2)
            h2_lo = (h2 - h2_hi.astype(jnp.float32)).astype(jnp.bfloat16)
            both = jnp.dot(h2_hi, wr_ref[...], preferred_element_type=jnp.float32)
            cross = jnp.dot(h2_lo, wr_ref[:, 0:LANES], preferred_element_type=jnp.float32)
            logits = both[:, 0:LANES] + both[:, LANES:] + cross
            logits_t = jnp.transpose(logits)[0:ROUTER_ROWS] + br_ref[...]
            (e1, e2), (p1, p2) = _route(logits_t)
            iota = lax.broadcasted_iota(jnp.int32, (SUBLANES, m), 0)
            eid_ref[:, f0:f0 + m] = jnp.where(iota == 0, e1, jnp.where(iota == 1, e2, 0))
            p_ref[:, f0:f0 + m] = jnp.where(iota == 0, p1, jnp.where(iota == 1, p2, 0.0))
            experts = lax.broadcasted_iota(jnp.int32, (N_EXPERTS, m), 0)
            routed = jnp.where((experts == e1) | (experts == e2), 1.0, 0.0)
            cnt_ref[...] = cnt_ref[...] + jnp.sum(routed, axis=1, keepdims=True)
        return run

    def interleave(main, fill):
        for i, item in enumerate(main):
            item()
            for f in fill[i * len(fill) // len(main):(i + 1) * len(fill) // len(main)]:
                f()

    for si in range(n_sub):
        prep(si)
    for item in dot_b_items(0):
        item()
    shift_b(0)
    carry_over = []
    for si in range(n_sub):
        fill = carry_over + dot_a_items(si)
        if si + 1 < n_sub:
            fill = fill + dot_b_items(si + 1)
        interleave(conv_b_items(si), fill + [conv_a_item(si)])
        if si + 1 < n_sub:
            shift_b(si + 1)
        carry_over = [finish_item(si)]
    for item in carry_over:
        item()

    if carried:
        @pl.when(t == pl.num_programs(1) - 1)
        def _():
            na_ref[...] = uext_ref[:, PAD_A + tt - HALO_A:PAD_A + tt, :]
            nb_ref[...] = gext_ref[:, PAD_B + tt - HALO_B:PAD_B + tt, :]
        uext_ref[:, 0:PAD_A, :] = uext_ref[:, tt:tt + PAD_A, :]
        gext_ref[:, 0:PAD_B, :] = gext_ref[:, tt:tt + PAD_B, :]
    else:
        na_ref[...] = uext_ref[:, PAD_A + tt - HALO_A:PAD_A + tt, :]
        nb_ref[...] = gext_ref[:, PAD_B + tt - HALO_B:PAD_B + tt, :]


def _mixer_prompt_kernel(x_ref, *refs, **kw):
    _mixer_body(x_ref, None, None, *refs, **kw)


def _mixer_sample_kernel(x_ref, sa_ref, sb_ref, *refs, **kw):
    n_params = 11
    params = refs[:n_params]
    rest = refs[n_params + 1 + 4:]
    _mixer_body(x_ref, sa_ref, sb_ref, *params, *rest, cnt_in_ref=refs[n_params], **kw)


def _full(shape):
    return pl.BlockSpec(shape, lambda *_: (0,) * len(shape))


def _mixer_param_specs():
    return [
        _full((1, D_MODEL)),
        _full((D_MODEL, IN_COLS)),
        pl.BlockSpec((None, CONV_A, D_A), lambda *_: (0, 0, 0)),
        pl.BlockSpec((None, CONV_B, D_B), lambda *_: (0, 0, 0)),
        _full((1, D_B)),
        _full((1, D_B)),
        _full((1, D_B)),
        _full((D_MODEL, D_MODEL)),
        _full((1, D_MODEL)),
        _full((D_MODEL, 2 * LANES)),
        _full((ROUTER_ROWS, 1)),
    ]


def _mixer_scratch(nseq, tt):
    rows = nseq * tt
    return [
        pltpu.VMEM((rows, 3 * D_A), jnp.float32),
        pltpu.VMEM((nseq, PAD_A + tt, D_A), jnp.float32),
        pltpu.VMEM((nseq, PAD_B + tt, D_B), jnp.float32),
        pltpu.VMEM((rows, D_MODEL), jnp.bfloat16),
        pltpu.VMEM((HALO_A, nseq, PAD_A + tt - SUBLANES, D_A), jnp.float32),
        pltpu.VMEM((SUBLANES - 1, nseq, PAD_B + tt - SUBLANES, D_B), jnp.float32),
    ]


def _mixer_prompt(x, params, n_tokens_total, tt, seq_first, batch):
    seq = x.shape[1]
    n_t = seq // tt
    tok = lambda b, t: (b * n_t + t, 0)
    lane_tok = lambda b, t: (0, b * n_t + t)
    out_shape = [
        jax.ShapeDtypeStruct((n_tokens_total, D_MODEL), jnp.float32),
        jax.ShapeDtypeStruct((n_tokens_total, HALF), jnp.uint32),
        jax.ShapeDtypeStruct((SUBLANES, n_tokens_total), jnp.int32),
        jax.ShapeDtypeStruct((SUBLANES, n_tokens_total), jnp.float32),
        jax.ShapeDtypeStruct((batch, HALO_A, D_A), jnp.float32),
        jax.ShapeDtypeStruct((batch, HALO_B, D_B), jnp.float32),
        jax.ShapeDtypeStruct((N_EXPERTS, LANES), jnp.float32),
    ]
    out_specs = [
        pl.BlockSpec((tt, D_MODEL), tok),
        pl.BlockSpec((tt, HALF), tok),
        pl.BlockSpec((SUBLANES, tt), lane_tok),
        pl.BlockSpec((SUBLANES, tt), lane_tok),
        pl.BlockSpec((1, HALO_A, D_A), lambda b, t: (b, 0, 0)),
        pl.BlockSpec((1, HALO_B, D_B), lambda b, t: (b, 0, 0)),
        pl.BlockSpec((N_EXPERTS, LANES), lambda b, t: (0, 0)),
    ]
    return pl.pallas_call(
        functools.partial(_mixer_prompt_kernel, nseq=1, tt=tt, seq_chunk=1, row_chunk=64, n_sub=1),
        grid=(batch, n_t),
        in_specs=[pl.BlockSpec((1, tt, D_MODEL), lambda b, t: (b + seq_first, t, 0))] + _mixer_param_specs(),
        out_specs=out_specs,
        out_shape=out_shape,
        scratch_shapes=_mixer_scratch(1, tt),
        compiler_params=pltpu.CompilerParams(
            dimension_semantics=("arbitrary", "arbitrary"), vmem_limit_bytes=VMEM_LIMIT),
        name="mixer_prompt",
    )(x, *params)


def _mixer_sample(x, state_a, state_b, params, counts, bufs, row_offset, nseq):
    batch, tt, _ = x.shape
    rows = nseq * tt
    first = row_offset // rows
    tok = lambda i: (first + i, 0)
    lane_tok = lambda i: (0, first + i)
    x1, h2p, eid, p = bufs
    out_shape = [
        jax.ShapeDtypeStruct(x1.shape, x1.dtype),
        jax.ShapeDtypeStruct(h2p.shape, h2p.dtype),
        jax.ShapeDtypeStruct(eid.shape, eid.dtype),
        jax.ShapeDtypeStruct(p.shape, p.dtype),
        jax.ShapeDtypeStruct((batch, HALO_A, D_A), jnp.float32),
        jax.ShapeDtypeStruct((batch, HALO_B, D_B), jnp.float32),
        jax.ShapeDtypeStruct((N_EXPERTS, LANES), jnp.float32),
    ]
    out_specs = [
        pl.BlockSpec((rows, D_MODEL), tok),
        pl.BlockSpec((rows, HALF), tok),
        pl.BlockSpec((SUBLANES, rows), lane_tok),
        pl.BlockSpec((SUBLANES, rows), lane_tok),
        pl.BlockSpec((nseq, HALO_A, D_A), lambda i: (i, 0, 0)),
        pl.BlockSpec((nseq, HALO_B, D_B), lambda i: (i, 0, 0)),
        pl.BlockSpec((N_EXPERTS, LANES), lambda i: (0, 0)),
    ]
    any_spec = pl.BlockSpec(memory_space=pl.ANY)
    in_specs = ([pl.BlockSpec((nseq, tt, D_MODEL), lambda i: (i, 0, 0)),
                 pl.BlockSpec((None, nseq, HALO_A, D_A), lambda i: (0, i, 0, 0)),
                 pl.BlockSpec((None, nseq, HALO_B, D_B), lambda i: (0, i, 0, 0))]
                + _mixer_param_specs() + [pl.BlockSpec((N_EXPERTS, LANES), lambda i: (0, 0))] + [any_spec] * 4)
    n_in = len(in_specs)
    return pl.pallas_call(
        functools.partial(_mixer_sample_kernel, nseq=nseq, tt=tt, seq_chunk=8, row_chunk=tt, n_sub=1),
        grid=(batch // nseq,),
        in_specs=in_specs,
        out_specs=out_specs,
        out_shape=out_shape,
        scratch_shapes=_mixer_scratch(nseq, tt),
        input_output_aliases={n_in - 4: 0, n_in - 3: 1, n_in - 2: 2, n_in - 1: 3},
        compiler_params=pltpu.CompilerParams(
            dimension_semantics=("arbitrary",), vmem_limit_bytes=VMEM_LIMIT),
        name="mixer_sample",
    )(x, state_a, state_b, *params, counts, x1, h2p, eid, p)


def _plan_kernel(eid_ref, cnt_ref, pos_ref, table_ref, carry_ref, start_ref, earlier_ref, *, tile):
    i = pl.program_id(0)
    tt = eid_ref.shape[1]

    @pl.when(i == 0)
    def _():
        tiles = jnp.floor((cnt_ref[...] + (tile - 0.5)) * (1.0 / tile))
        below = (lax.broadcasted_iota(jnp.int32, (N_EXPERTS, N_EXPERTS), 0)
                 > lax.broadcasted_iota(jnp.int32, (N_EXPERTS, N_EXPERTS), 1))
        start = jnp.dot(jnp.where(below, 1.0, 0.0).astype(jnp.bfloat16), tiles.astype(jnp.bfloat16),
                        preferred_element_type=jnp.float32) * tile
        start_ref[...] = start
        seg_end = start + tiles * tile
        tile_row = lax.broadcasted_iota(jnp.int32, (N_EXPERTS, LANES), 1).astype(jnp.float32) * tile
        owner = jnp.sum(jnp.where(seg_end <= tile_row, 1.0, 0.0), axis=0, keepdims=True)
        owner = jnp.minimum(owner, N_EXPERTS - 1.0)
        n_used = jnp.sum(tiles, axis=0, keepdims=True)
        row = lax.broadcasted_iota(jnp.int32, (SUBLANES, LANES), 0)
        table_ref[...] = jnp.where(row == 0, owner, jnp.where(row == 1, n_used, 0.0)).astype(jnp.int32)
        earlier = (lax.broadcasted_iota(jnp.int32, (tt, tt), 0)
                   < lax.broadcasted_iota(jnp.int32, (tt, tt), 1))
        earlier_ref[...] = jnp.where(earlier, 1.0, 0.0).astype(jnp.bfloat16)
        carry_ref[...] = jnp.zeros_like(carry_ref)

    eid = eid_ref[...]
    experts = lax.broadcasted_iota(jnp.int32, (N_EXPERTS, tt), 0)
    oh0 = experts == eid[0:1]
    oh1 = experts == eid[1:2]
    oh = jnp.where(oh0 | oh1, 1.0, 0.0)
    within = jnp.dot(oh.astype(jnp.bfloat16), earlier_ref[...], preferred_element_type=jnp.float32)
    slot_of = within + carry_ref[:, 0:1] + start_ref[:, 0:1]
    s0 = jnp.sum(jnp.where(oh0, slot_of, 0.0), axis=0, keepdims=True)
    s1 = jnp.sum(jnp.where(oh1, slot_of, 0.0), axis=0, keepdims=True)
    k = lax.broadcasted_iota(jnp.int32, (SUBLANES, tt), 0)
    pos_ref[...] = jnp.where(k == 0, s0, jnp.where(k == 1, s1, 0.0)).astype(jnp.int32)
    carry_ref[...] = carry_ref[...] + jnp.sum(oh, axis=1, keepdims=True)


def _plan(eid, counts, tile):
    n_tokens = eid.shape[1]
    return pl.pallas_call(
        functools.partial(_plan_kernel, tile=tile),
        grid=(n_tokens // PLAN_TILE,),
        in_specs=[pl.BlockSpec((SUBLANES, PLAN_TILE), lambda i: (0, i)),
                  pl.BlockSpec((N_EXPERTS, LANES), lambda i: (0, 0))],
        out_specs=[pl.BlockSpec((SUBLANES, PLAN_TILE), lambda i: (0, i)),
                   pl.BlockSpec((SUBLANES, LANES), lambda i: (0, 0))],
        out_shape=[jax.ShapeDtypeStruct((SUBLANES, n_tokens), jnp.int32),
                   jax.ShapeDtypeStruct((SUBLANES, LANES), jnp.int32)],
        scratch_shapes=[pltpu.VMEM((N_EXPERTS, LANES), jnp.float32),
                        pltpu.VMEM((N_EXPERTS, LANES), jnp.float32),
                        pltpu.VMEM((PLAN_TILE, PLAN_TILE), jnp.bfloat16)],
        compiler_params=pltpu.CompilerParams(dimension_semantics=("arbitrary",)),
        name="route_plan",
    )(eid, counts)


def _sc_mesh():
    return plsc.VectorSubcoreMesh(core_axis_name="c", subcore_axis_name="s")


def _sc_dispatch_rows(table, pos, n_out):
    n_workers, n_batches, top_k, batch = pos.shape
    n_rows, words = table.shape
    assert n_workers * n_batches * batch == n_rows and n_batches >= 2

    @functools.partial(
        pl.kernel, mesh=_sc_mesh(),
        out_type=jax.ShapeDtypeStruct((n_out, words), table.dtype),
        scratch_types=[pltpu.VMEM((n_batches, top_k, batch), jnp.int32),
                       pltpu.VMEM((2, batch, words), table.dtype),
                       pltpu.SemaphoreType.DMA((2,)),
                       pltpu.SemaphoreType.DMA((2, top_k))],
    )
    def dispatch(table_hbm, pos_hbm, out_hbm, idx_v, rows_v, sem_in, sem_out):
        worker = lax.axis_index("s") * SC_CORES + lax.axis_index("c")
        pltpu.sync_copy(pos_hbm.at[worker], idx_v)

        def read(b):
            src = table_hbm.at[pl.ds((worker * n_batches + b) * batch, batch)]
            return pltpu.async_copy(src, rows_v.at[b % 2], sem_in.at[b % 2])

        def write(b):
            return [pltpu.async_copy(rows_v.at[b % 2], out_hbm.at[idx_v.at[b, k]], sem_out.at[b % 2, k])
                    for k in range(top_k)]

        reads = {0: read(0)}
        writes = {}
        for b in range(n_batches):
            reads[b].wait()
            if b + 1 < n_batches:
                if b >= 1:
                    for w in writes[b - 1]:
                        w.wait()
                reads[b + 1] = read(b + 1)
            writes[b] = write(b)
        for b in (n_batches - 2, n_batches - 1):
            for w in writes[b]:
                w.wait()

    return dispatch(table, pos)


def _sc_gather_rows(table, idx):
    n_workers, n_batches, batch = idx.shape
    words = table.shape[1]
    assert n_batches >= 2

    @functools.partial(
        pl.kernel, mesh=_sc_mesh(),
        out_type=jax.ShapeDtypeStruct((n_workers * n_batches * batch, words), table.dtype),
        scratch_types=[pltpu.VMEM((n_batches, batch), jnp.int32),
                       pltpu.VMEM((2, batch, words), table.dtype),
                       pltpu.SemaphoreType.DMA((2,)),
                       pltpu.SemaphoreType.DMA((2,))],
    )
    def gather(table_hbm, idx_hbm, out_hbm, idx_v, rows_v, sem_in, sem_out):
        worker = lax.axis_index("s") * SC_CORES + lax.axis_index("c")
        pltpu.sync_copy(idx_hbm.at[worker], idx_v)

        def read(b):
            return pltpu.async_copy(table_hbm.at[idx_v.at[b]], rows_v.at[b % 2], sem_in.at[b % 2])

        def write(b):
            dst = out_hbm.at[pl.ds((worker * n_batches + b) * batch, batch)]
            return pltpu.async_copy(rows_v.at[b % 2], dst, sem_out.at[b % 2])

        reads = {0: read(0)}
        writes = {}
        for b in range(n_batches):
            reads[b].wait()
            if b + 1 < n_batches:
                if b >= 1:
                    writes[b - 1].wait()
                reads[b + 1] = read(b + 1)
            writes[b] = write(b)
        writes[n_batches - 2].wait()
        writes[n_batches - 1].wait()

    return gather(table, idx)


def _experts_kernel(te_ref, xs_ref, wg_ref, wu_ref, wd_ref, ys_ref):
    hi, lo = _unpack_bf16_halves(xs_ref[...])
    x = jnp.concatenate([hi.astype(jnp.bfloat16), lo.astype(jnp.bfloat16)], axis=1)
    w_gate_up = jnp.concatenate([wg_ref[0].astype(jnp.bfloat16), wu_ref[0].astype(jnp.bfloat16)], axis=1)
    wd = wd_ref[0].astype(jnp.bfloat16)
    gate_up = jnp.dot(x, w_gate_up, preferred_element_type=jnp.float32)
    gate = gate_up[:, :D_EXPERT]
    up = gate_up[:, D_EXPERT:]
    hid = (gate * jax.nn.sigmoid(gate) * up).astype(jnp.bfloat16)
    ys_ref[...] = _pack_bf16_halves(jnp.dot(hid, wd, preferred_element_type=jnp.float32))


def _experts(xs, tile_expert, n_valid, wg, wu, wd, tm):
    n_slots = xs.shape[0]
    row_block = lambda i, te: (i, 0)
    w_block = lambda i, te: (te[i], 0, 0)
    return pl.pallas_call(
        _experts_kernel,
        grid_spec=pltpu.PrefetchScalarGridSpec(
            num_scalar_prefetch=1,
            grid=(n_valid,),
            in_specs=[pl.BlockSpec((tm, HALF), row_block),
                      pl.BlockSpec((1, D_MODEL, D_EXPERT), w_block, pipeline_mode=pl.Buffered(2)),
                      pl.BlockSpec((1, D_MODEL, D_EXPERT), w_block, pipeline_mode=pl.Buffered(2)),
                      pl.BlockSpec((1, D_EXPERT, D_MODEL), w_block, pipeline_mode=pl.Buffered(2))],
            out_specs=pl.BlockSpec((tm, HALF), row_block),
        ),
        out_shape=jax.ShapeDtypeStruct((n_slots, HALF), jnp.uint32),
        compiler_params=pltpu.CompilerParams(
            dimension_semantics=("arbitrary",), vmem_limit_bytes=VMEM_LIMIT),
        name="experts",
    )(tile_expert, xs, wg, wu, wd)


def _final_kernel(x1_ref, y0_ref, y1_ref, p_ref, gfin_ref, *rest):
    out_ref = rest[-1]
    pt = jnp.transpose(p_ref[...])
    p0 = pt[:, 0:1]
    p1 = pt[:, 1:2]
    a_hi, a_lo = _unpack_bf16_halves(y0_ref[...])
    b_hi, b_lo = _unpack_bf16_halves(y1_ref[...])
    x1 = x1_ref[...]
    x2_hi = x1[:, :HALF] + (p0 * a_hi + p1 * b_hi)
    x2_lo = x1[:, HALF:] + (p0 * a_lo + p1 * b_lo)
    ms = (jnp.sum(x2_hi * x2_hi, axis=-1, keepdims=True)
          + jnp.sum(x2_lo * x2_lo, axis=-1, keepdims=True)) / D_MODEL
    scale = lax.rsqrt(ms + RMS_EPS)
    g = gfin_ref[...]
    out_ref[:, :HALF] = x2_hi * scale * g[:, :HALF]
    out_ref[:, HALF:] = x2_lo * scale * g[:, HALF:]


def _final(x1, yk, p, gfin, row_offset, n_rows, tm, out_rows, out_offset, out_buf=None):
    n_tokens = x1.shape[0]
    first = row_offset // tm
    second = (n_tokens + row_offset) // tm
    out_first = out_offset // tm
    in_specs = [pl.BlockSpec((tm, D_MODEL), lambda i: (first + i, 0)),
                pl.BlockSpec((tm, HALF), lambda i: (first + i, 0)),
                pl.BlockSpec((tm, HALF), lambda i: (second + i, 0)),
                pl.BlockSpec((SUBLANES, tm), lambda i: (0, first + i)),
                pl.BlockSpec((1, D_MODEL), lambda i: (0, 0))]
    args = [x1, yk, yk, p, gfin]
    aliases = {}
    if out_buf is not None:
        in_specs.append(pl.BlockSpec(memory_space=pl.ANY))
        args.append(out_buf)
        aliases = {len(args) - 1: 0}
    return pl.pallas_call(
        _final_kernel,
        grid=(n_rows // tm,),
        in_specs=in_specs,
        out_specs=pl.BlockSpec((tm, D_MODEL), lambda i: (out_first + i, 0)),
        out_shape=jax.ShapeDtypeStruct((out_rows, D_MODEL), jnp.float32),
        input_output_aliases=aliases,
        compiler_params=pltpu.CompilerParams(
            dimension_semantics=("arbitrary",), vmem_limit_bytes=VMEM_LIMIT),
        name="final",
    )(*args)


def _expert_tile(n_tokens):
    mean_rows = TOP_K * n_tokens // N_EXPERTS
    return -(-(mean_rows + 2 * math.isqrt(mean_rows)) // (2 * SUBLANES)) * (2 * SUBLANES)


def _routed_experts(h2p, eid, counts, w_gate, w_up, w_down):
    n_tokens = h2p.shape[0]
    assert n_tokens % (SC_WORKERS * SC_DISPATCH_BATCH) == 0 and n_tokens % PLAN_TILE == 0
    assert (TOP_K * n_tokens) % (SC_WORKERS * SC_BATCH) == 0
    tile = _expert_tile(n_tokens)
    n_tiles = -(-(TOP_K * n_tokens + N_EXPERTS * (tile - 1)) // tile)
    assert n_tiles <= LANES
    pos, table = _plan(eid, counts, tile)
    pos = pos[:TOP_K]
    by_token = jnp.transpose(pos.reshape(TOP_K, SC_WORKERS, -1, SC_DISPATCH_BATCH), (1, 2, 0, 3))
    xs = _sc_dispatch_rows(h2p, by_token, n_tiles * tile)
    ys = _experts(xs, table[0], table[1, 0], w_gate, w_up, w_down, tile)
    return _sc_gather_rows(ys, pos.reshape(SC_WORKERS, -1, SC_BATCH))


def kernel(x_prompt, x_sample, state_conv_a, state_conv_b, g_mix, w_in, conv_a_w, conv_b_w, conv_b_bias,
           ln_g, ln_b, w_out, g_ffn, w_coarse, b_coarse, w_fine, b_fine, w_gate, w_up, w_down, g_final):
    assert g_mix.shape[0] == 1, "single trunk layer"
    batch, seq, _ = x_prompt.shape
    dec_batch, dec_seq, _ = x_sample.shape
    n_prompt = batch * seq
    n_sample = dec_batch * dec_seq
    bf16 = jnp.bfloat16

    wr = jnp.concatenate([
        w_coarse[0], jnp.zeros((D_MODEL, SUBLANES - N_EXPERT_GROUPS), jnp.float32),
        jnp.transpose(w_fine[0], (1, 0, 2)).reshape(D_MODEL, N_EXPERTS),
        jnp.zeros((D_MODEL, LANES - ROUTER_ROWS), jnp.float32)], axis=1)
    wr_hi = wr.astype(bf16)
    wr_lo = (wr - wr_hi.astype(jnp.float32)).astype(bf16)
    wr_both = jnp.concatenate([wr_hi, wr_lo], axis=1)
    br = jnp.concatenate([
        b_coarse[0], jnp.full((SUBLANES - N_EXPERT_GROUPS,), NEG_BIG, jnp.float32),
        b_fine[0].reshape(N_EXPERTS)]).reshape(ROUTER_ROWS, 1)

    params = (g_mix, w_in[0].astype(bf16), conv_a_w, conv_b_w, conv_b_bias, ln_g, ln_b,
              w_out[0].astype(bf16), g_ffn, wr_both, br)

    experts = (w_gate[0], w_up[0], w_down[0])
    gfin = g_final.reshape(1, D_MODEL)

    bufs = _mixer_prompt(x_prompt, params, n_prompt + n_sample, MIXER_TILE, 0, batch)
    na_p, nb_p, counts = bufs[4:]
    x1, h2p, eid, p, na_s, nb_s, counts = _mixer_sample(
        x_sample, state_conv_a, state_conv_b, params, counts, bufs[:4], n_prompt, nseq=32)
    yk = _routed_experts(h2p, eid, counts, *experts)
    y_p = _final(x1, yk, p, gfin, 0, n_prompt, FINAL_TILE, n_prompt, 0)
    y_s = _final(x1, yk, p, gfin, n_prompt, n_sample, FINAL_TILE, n_sample, 0)
    return (y_p.reshape(batch, seq, D_MODEL), y_s.reshape(dec_batch, dec_seq, D_MODEL),
            na_p[None], nb_p[None], na_s[None], nb_s[None])
```

```python
import functools
import math

import jax
import jax.numpy as jnp
from jax import lax
from jax.experimental import pallas as pl
from jax.experimental.pallas import tpu as pltpu
from jax.experimental.pallas import tpu_sc as plsc

D_MODEL = 1024
D_A = 512
D_B = 512
CONV_A = 3
CONV_B = 31
HALO_A = CONV_A - 1
HALO_B = CONV_B - 1
IN_COLS = 3 * D_A + 2 * D_B
N_EXPERT_GROUPS = 4
EXPERTS_PER_GROUP = 8
N_EXPERTS = N_EXPERT_GROUPS * EXPERTS_PER_GROUP
TOP_K = 2
D_EXPERT = D_MODEL // 4
RMS_EPS = 1e-6
LN_EPS = 1e-5

SUBLANES = 8
LANES = 128
PAD_A = SUBLANES
PAD_B = 32
ROUTER_ROWS = SUBLANES + N_EXPERTS
NEG_BIG = -1e30
VMEM_LIMIT = 56 * 1024 * 1024
HALF = D_MODEL // 2
HI_MASK = 0xFFFF0000

SC_CORES = 2
SC_SUBCORES = 16
SC_WORKERS = SC_CORES * SC_SUBCORES
SC_BATCH = 64
SC_DISPATCH_BATCH = 32

PLAN_TILE = 1024
FINAL_TILE = 1024
MIXER_TILE = 512


def _rms_scale(x):
    return x * lax.rsqrt(jnp.mean(x * x, axis=-1, keepdims=True) + RMS_EPS)


def _pack_bf16_halves(x):
    bits = lax.bitcast_convert_type(x.astype(jnp.bfloat16).astype(jnp.float32), jnp.uint32)
    return bits[:, :HALF] | (bits[:, HALF:] >> 16)


def _unpack_bf16_halves(w):
    hi = lax.bitcast_convert_type(w & jnp.uint32(HI_MASK), jnp.float32)
    lo = lax.bitcast_convert_type(w << 16, jnp.float32)
    return hi, lo


def _route(logits_t):
    rows = logits_t.shape[1]
    iota = lax.broadcasted_iota(jnp.int32, (SUBLANES, rows), 0)
    lc = logits_t[0:SUBLANES]
    cmax = jnp.max(lc, axis=0, keepdims=True)
    grp = jnp.min(jnp.where(lc == cmax, iota, SUBLANES), axis=0, keepdims=True)
    p_grp = 1.0 / jnp.sum(jnp.exp(lc - cmax), axis=0, keepdims=True)
    sel = logits_t[SUBLANES:2 * SUBLANES]
    for g in range(1, N_EXPERT_GROUPS):
        sel = jnp.where(grp == g, logits_t[(g + 1) * SUBLANES:(g + 2) * SUBLANES], sel)
    v1 = jnp.max(sel, axis=0, keepdims=True)
    i1 = jnp.min(jnp.where(sel == v1, iota, SUBLANES), axis=0, keepdims=True)
    sel2 = jnp.where(iota == i1, -jnp.inf, sel)
    v2 = jnp.max(sel2, axis=0, keepdims=True)
    i2 = jnp.min(jnp.where(sel2 == v2, iota, SUBLANES), axis=0, keepdims=True)
    e2 = jnp.exp(v2 - v1)
    den = 1.0 + e2
    p1 = p_grp / den
    p2 = p_grp * e2 / den
    base = grp * EXPERTS_PER_GROUP
    return (base + i1, base + i2), (p1, p2)


def _after_mix(x1, f0, gffn_ref, wr_ref, br_ref, x1_ref, h2p_ref, eid_ref, p_ref, cnt_ref):
    m = x1.shape[0]
    x1_ref[f0:f0 + m, :] = x1
    h2 = _rms_scale(x1) * gffn_ref[...]
    h2_hi = h2.astype(jnp.bfloat16)
    h2p_ref[f0:f0 + m, :] = _pack_bf16_halves(h2)
    h2_lo = (h2 - h2_hi.astype(jnp.float32)).astype(jnp.bfloat16)
    both = jnp.dot(h2_hi, wr_ref[...], preferred_element_type=jnp.float32)
    cross = jnp.dot(h2_lo, wr_ref[:, 0:LANES], preferred_element_type=jnp.float32)
    logits = both[:, 0:LANES] + both[:, LANES:] + cross
    logits_t = jnp.transpose(logits)[0:ROUTER_ROWS] + br_ref[...]
    (e1, e2), (p1, p2) = _route(logits_t)
    iota = lax.broadcasted_iota(jnp.int32, (SUBLANES, m), 0)
    eid_ref[:, f0:f0 + m] = jnp.where(iota == 0, e1, jnp.where(iota == 1, e2, 0))
    p_ref[:, f0:f0 + m] = jnp.where(iota == 0, p1, jnp.where(iota == 1, p2, 0.0))
    experts = lax.broadcasted_iota(jnp.int32, (N_EXPERTS, m), 0)
    routed = jnp.where((experts == e1) | (experts == e2), 1.0, 0.0)
    cnt_ref[...] = cnt_ref[...] + jnp.sum(routed, axis=1, keepdims=True)


def _mixer_body(x_ref, sa_ref, sb_ref, gmix_ref, win_ref, caw_ref, cbw_ref, cbb_ref, lng_ref, lnb_ref,
                wout_ref, gffn_ref, wr_ref, br_ref,
                x1_ref, h2p_ref, eid_ref, p_ref, na_ref, nb_ref, cnt_ref,
                proj_ref, uext_ref, gext_ref, z_ref, ush_ref, gsh_ref, *, nseq, tt, seq_chunk, row_chunk, n_sub,
                cnt_in_ref=None):
    carried = sa_ref is None
    t = pl.program_id(1) if carried else None

    first_step = ((pl.program_id(0) == 0) & (t == 0)) if carried else (pl.program_id(0) == 0)

    @pl.when(first_step)
    def _():
        cnt_ref[...] = jnp.zeros_like(cnt_ref) if cnt_in_ref is None else cnt_in_ref[...]
    assert n_sub == 1 or nseq == 1
    sub = tt // n_sub

    if carried:
        @pl.when(t == 0)
        def _():
            uext_ref[:, 0:PAD_A, :] = jnp.zeros((nseq, PAD_A, D_A), jnp.float32)
            gext_ref[:, 0:PAD_B, :] = jnp.zeros((nseq, PAD_B, D_B), jnp.float32)
    else:
        uext_ref[:, PAD_A - HALO_A:PAD_A, :] = sa_ref[...]
        gext_ref[:, PAD_B - HALO_B:PAD_B, :] = sb_ref[...]

    def window(base_ref, shifted_ref, first_shift, s0, off, n_rows):
        r = off % SUBLANES
        a8 = off - r
        if r == 0:
            return base_ref[s0:s0 + seq_chunk, a8:a8 + n_rows, :]
        return shifted_ref[r - first_shift, s0:s0 + seq_chunk, a8:a8 + n_rows, :]

    n = seq_chunk * row_chunk
    col_chunk = 2 * LANES
    caw = caw_ref[...]
    cbw = cbw_ref[...]
    hs_bf = {}

    def rows_of(si):
        q0 = si * sub
        f0 = q0 if nseq == 1 else 0
        return q0, f0, nseq * sub

    def prep(si):
        q0, _, m = rows_of(si)
        x = x_ref[:, q0:q0 + sub, :].reshape(m, D_MODEL)
        hs_bf[si] = (_rms_scale(x) * gmix_ref[...]).astype(jnp.bfloat16)

    def dot_b_items(si):
        def item(c0):
            def run():
                q0, _, _ = rows_of(si)
                v_b = jnp.dot(hs_bf[si], win_ref[:, 3 * D_A + c0:3 * D_A + c0 + col_chunk],
                              preferred_element_type=jnp.float32)
                g_b = jnp.dot(hs_bf[si], win_ref[:, 3 * D_A + D_B + c0:3 * D_A + D_B + c0 + col_chunk],
                              preferred_element_type=jnp.float32)
                gext_ref[:, PAD_B + q0:PAD_B + q0 + sub, c0:c0 + col_chunk] = (
                    v_b * jax.nn.sigmoid(g_b)).reshape(nseq, sub, col_chunk)
            return run
        return [item(c0) for c0 in range(0, D_B, col_chunk)]

    def dot_a_items(si):
        def item(c0):
            def run():
                _, f0, m = rows_of(si)
                proj_ref[f0:f0 + m, c0:c0 + col_chunk] = jnp.dot(
                    hs_bf[si], win_ref[:, c0:c0 + col_chunk], preferred_element_type=jnp.float32)
            return run
        return [item(c0) for c0 in range(0, 3 * D_A, col_chunk)]

    def shift_b(si):
        q0, _, _ = rows_of(si)
        j_lo = 0 if si == 0 else q0 + PAD_B - SUBLANES
        j_hi = q0 + sub + PAD_B - SUBLANES
        for r in range(1, SUBLANES):
            gsh_ref[r - 1, :, j_lo:j_hi, :] = gext_ref[:, j_lo + r:j_hi + r, :]

    def chunks_of(si):
        q0, _, _ = rows_of(si)
        return [(s0, r0) for s0 in range(0, nseq, seq_chunk) for r0 in range(q0, q0 + sub, row_chunk)]

    def conv_b_items(si):
        def item(s0, r0):
            def run():
                lo = s0 * tt + r0
                acc_b = None
                for k in range(CONV_B):
                    off = PAD_B - HALO_B + k + r0
                    term = window(gext_ref, gsh_ref, 1, s0, off, row_chunk) * cbw[k:k + 1, :]
                    acc_b = term if acc_b is None else acc_b + term
                zb = acc_b.reshape(n, D_B) + cbb_ref[...]
                mu = jnp.mean(zb, axis=-1, keepdims=True)
                zc = zb - mu
                var = jnp.mean(zc * zc, axis=-1, keepdims=True)
                y = zc * lax.rsqrt(var + LN_EPS) * lng_ref[...] + lnb_ref[...]
                z_ref[lo:lo + n, D_A:] = (y * jax.nn.sigmoid(y)).astype(jnp.bfloat16)
            return run
        return [item(s0, r0) for s0, r0 in chunks_of(si)]

    def conv_a_item(si):
        def run():
            q0, f0, m = rows_of(si)
            c_a = proj_ref[f0:f0 + m, D_A:2 * D_A]
            v_a = proj_ref[f0:f0 + m, 2 * D_A:3 * D_A]
            uext_ref[:, PAD_A + q0:PAD_A + q0 + sub, :] = (c_a * v_a).reshape(nseq, sub, D_A)
            for r in range(SUBLANES - HALO_A, SUBLANES):
                ush_ref[r - (SUBLANES - HALO_A), :, q0:q0 + sub, :] = uext_ref[:, q0 + r:q0 + sub + r, :]
            for s0, r0 in chunks_of(si):
                lo = s0 * tt + r0
                acc_a = None
                for k in range(CONV_A):
                    off = PAD_A - HALO_A + k + r0
                    term = window(uext_ref, ush_ref, SUBLANES - HALO_A, s0, off, row_chunk) * caw[k:k + 1, :]
                    acc_a = term if acc_a is None else acc_a + term
                z_a = proj_ref[lo:lo + n, 0:D_A] * acc_a.reshape(n, D_A)
                z_ref[lo:lo + n, 0:D_A] = z_a.astype(jnp.bfloat16)
        return run

    def finish_item(si):
        def run():
            q0, f0, m = rows_of(si)
            x = x_ref[:, q0:q0 + sub, :].reshape(m, D_MODEL)
            x1 = x + jnp.dot(z_ref[f0:f0 + m, :], wout_ref[...], preferred_element_type=jnp.float32)
            _after_mix(x1, f0, gffn_ref, wr_ref, br_ref, x1_ref, h2p_ref, eid_ref, p_ref, cnt_ref)
        return run

    def interleave(main, fill):
        for i, item in enumerate(main):
            item()
            for f in fill[i * len(fill) // len(main):(i + 1) * len(fill) // len(main)]:
                f()

    for si in range(n_sub):
        prep(si)
    for item in dot_b_items(0):
        item()
    shift_b(0)
    carry_over = []
    for si in range(n_sub):
        fill = carry_over + dot_a_items(si)
        if si + 1 < n_sub:
            fill = fill + dot_b_items(si + 1)
        interleave(conv_b_items(si), fill + [conv_a_item(si)])
        if si + 1 < n_sub:
            shift_b(si + 1)
        carry_over = [finish_item(si)]
    for item in carry_over:
        item()

    if carried:
        @pl.when(t == pl.num_programs(1) - 1)
        def _():
            na_ref[...] = uext_ref[:, PAD_A + tt - HALO_A:PAD_A + tt, :]
            nb_ref[...] = gext_ref[:, PAD_B + tt - HALO_B:PAD_B + tt, :]
        uext_ref[:, 0:PAD_A, :] = uext_ref[:, tt:tt + PAD_A, :]
        gext_ref[:, 0:PAD_B, :] = gext_ref[:, tt:tt + PAD_B, :]
    else:
        na_ref[...] = uext_ref[:, PAD_A + tt - HALO_A:PAD_A + tt, :]
        nb_ref[...] = gext_ref[:, PAD_B + tt - HALO_B:PAD_B + tt, :]


def _mixer_prompt_kernel(x_ref, *refs, **kw):
    _mixer_body(x_ref, None, None, *refs, **kw)


def _mixer_sample_kernel(x_ref, sa_ref, sb_ref, gmix_ref, win_ref, caw_ref, cbw_ref, cbb_ref, lng_ref, lnb_ref,
                         wout_ref, gffn_ref, wr_ref, br_ref, cnt_in_ref, _x1_in, _h2p_in, _eid_in, _p_in,
                         x1_ref, h2p_ref, eid_ref, p_ref, na_ref, nb_ref, cnt_ref,
                         proj_ref, gnt_ref, unt_ref, gtm_ref, utm_ref, ynt_ref, ant_ref, z_ref, *, nseq, tt):
    rows = nseq * tt
    col_chunk = 2 * LANES

    @pl.when(pl.program_id(0) == 0)
    def _():
        cnt_ref[...] = cnt_in_ref[...]

    x = x_ref[...].reshape(rows, D_MODEL)
    h = (_rms_scale(x) * gmix_ref[...]).astype(jnp.bfloat16)
    for c0 in range(0, D_B, col_chunk):
        v_b = jnp.dot(h, win_ref[:, 3 * D_A + c0:3 * D_A + c0 + col_chunk], preferred_element_type=jnp.float32)
        g_b = jnp.dot(h, win_ref[:, 3 * D_A + D_B + c0:3 * D_A + D_B + c0 + col_chunk],
                      preferred_element_type=jnp.float32)
        g = v_b * jax.nn.sigmoid(g_b)
        for c in range(0, col_chunk, LANES):
            gnt_ref[(c0 + c) // LANES] = g[:, c:c + LANES]
    for c0 in range(0, 3 * D_A, col_chunk):
        proj_ref[:, c0:c0 + col_chunk] = jnp.dot(h, win_ref[:, c0:c0 + col_chunk],
                                                 preferred_element_type=jnp.float32)
    u = proj_ref[:, D_A:2 * D_A] * proj_ref[:, 2 * D_A:3 * D_A]
    for c in range(0, D_A, LANES):
        unt_ref[c // LANES] = u[:, c:c + LANES]

    utm_ref[0:HALO_A] = sa_ref[...]
    gtm_ref[0:HALO_B] = sb_ref[...]
    for t in range(tt):
        for c in range(0, D_A, LANES):
            utm_ref[HALO_A + t, :, c:c + LANES] = unt_ref[c // LANES, pl.ds(t, nseq, stride=tt), :]
        for c in range(0, D_B, LANES):
            gtm_ref[HALO_B + t, :, c:c + LANES] = gnt_ref[c // LANES, pl.ds(t, nseq, stride=tt), :]
    na_ref[...] = utm_ref[tt:tt + HALO_A]
    nb_ref[...] = gtm_ref[tt:tt + HALO_B]

    caw = caw_ref[...]
    cbw = cbw_ref[...]
    for t in range(tt):
        acc_b = None
        for k in range(CONV_B):
            term = gtm_ref[t + k] * cbw[k:k + 1, :]
            acc_b = term if acc_b is None else acc_b + term
        zb = acc_b + cbb_ref[...]
        mu = jnp.mean(zb, axis=-1, keepdims=True)
        zc = zb - mu
        var = jnp.mean(zc * zc, axis=-1, keepdims=True)
        y = zc * lax.rsqrt(var + LN_EPS) * lng_ref[...] + lnb_ref[...]
        y = y * jax.nn.sigmoid(y)
        for c in range(0, D_B, LANES):
            ynt_ref[c // LANES, pl.ds(t, nseq, stride=tt), :] = y[:, c:c + LANES]
        acc_a = None
        for k in range(CONV_A):
            term = utm_ref[t + k] * caw[k:k + 1, :]
            acc_a = term if acc_a is None else acc_a + term
        for c in range(0, D_A, LANES):
            ant_ref[c // LANES, pl.ds(t, nseq, stride=tt), :] = acc_a[:, c:c + LANES]

    for c in range(0, D_A, LANES):
        z_ref[:, c:c + LANES] = (proj_ref[:, c:c + LANES] * ant_ref[c // LANES]).astype(jnp.bfloat16)
    for c in range(0, D_B, LANES):
        z_ref[:, D_A + c:D_A + c + LANES] = ynt_ref[c // LANES].astype(jnp.bfloat16)
    x1 = x + jnp.dot(z_ref[...], wout_ref[...], preferred_element_type=jnp.float32)
    _after_mix(x1, 0, gffn_ref, wr_ref, br_ref, x1_ref, h2p_ref, eid_ref, p_ref, cnt_ref)


def _full(shape):
    return pl.BlockSpec(shape, lambda *_: (0,) * len(shape))


def _mixer_param_specs():
    return [
        _full((1, D_MODEL)),
        _full((D_MODEL, IN_COLS)),
        pl.BlockSpec((None, CONV_A, D_A), lambda *_: (0, 0, 0)),
        pl.BlockSpec((None, CONV_B, D_B), lambda *_: (0, 0, 0)),
        _full((1, D_B)),
        _full((1, D_B)),
        _full((1, D_B)),
        _full((D_MODEL, D_MODEL)),
        _full((1, D_MODEL)),
        _full((D_MODEL, 2 * LANES)),
        _full((ROUTER_ROWS, 1)),
    ]


def _mixer_scratch(nseq, tt):
    rows = nseq * tt
    return [
        pltpu.VMEM((rows, 3 * D_A), jnp.float32),
        pltpu.VMEM((nseq, PAD_A + tt, D_A), jnp.float32),
        pltpu.VMEM((nseq, PAD_B + tt, D_B), jnp.float32),
        pltpu.VMEM((rows, D_MODEL), jnp.bfloat16),
        pltpu.VMEM((HALO_A, nseq, PAD_A + tt - SUBLANES, D_A), jnp.float32),
        pltpu.VMEM((SUBLANES - 1, nseq, PAD_B + tt - SUBLANES, D_B), jnp.float32),
    ]


def _mixer_prompt(x, params, n_tokens_total, tt, seq_first, batch):
    seq = x.shape[1]
    n_t = seq // tt
    tok = lambda b, t: (b * n_t + t, 0)
    lane_tok = lambda b, t: (0, b * n_t + t)
    out_shape = [
        jax.ShapeDtypeStruct((n_tokens_total, D_MODEL), jnp.float32),
        jax.ShapeDtypeStruct((n_tokens_total, HALF), jnp.uint32),
        jax.ShapeDtypeStruct((SUBLANES, n_tokens_total), jnp.int32),
        jax.ShapeDtypeStruct((SUBLANES, n_tokens_total), jnp.float32),
        jax.ShapeDtypeStruct((batch, HALO_A, D_A), jnp.float32),
        jax.ShapeDtypeStruct((batch, HALO_B, D_B), jnp.float32),
        jax.ShapeDtypeStruct((N_EXPERTS, LANES), jnp.float32),
    ]
    out_specs = [
        pl.BlockSpec((tt, D_MODEL), tok),
        pl.BlockSpec((tt, HALF), tok),
        pl.BlockSpec((SUBLANES, tt), lane_tok),
        pl.BlockSpec((SUBLANES, tt), lane_tok),
        pl.BlockSpec((1, HALO_A, D_A), lambda b, t: (b, 0, 0)),
        pl.BlockSpec((1, HALO_B, D_B), lambda b, t: (b, 0, 0)),
        pl.BlockSpec((N_EXPERTS, LANES), lambda b, t: (0, 0)),
    ]
    return pl.pallas_call(
        functools.partial(_mixer_prompt_kernel, nseq=1, tt=tt, seq_chunk=1, row_chunk=64, n_sub=1),
        grid=(batch, n_t),
        in_specs=[pl.BlockSpec((1, tt, D_MODEL), lambda b, t: (b + seq_first, t, 0))] + _mixer_param_specs(),
        out_specs=out_specs,
        out_shape=out_shape,
        scratch_shapes=_mixer_scratch(1, tt),
        compiler_params=pltpu.CompilerParams(
            dimension_semantics=("arbitrary", "arbitrary"), vmem_limit_bytes=VMEM_LIMIT),
        name="mixer_prompt",
    )(x, *params)


def _mixer_sample(x, state_a, state_b, params, counts, bufs, row_offset, nseq):
    batch, tt, _ = x.shape
    rows = nseq * tt
    first = row_offset // rows
    tok = lambda i: (first + i, 0)
    lane_tok = lambda i: (0, first + i)
    x1, h2p, eid, p = bufs
    out_shape = [
        jax.ShapeDtypeStruct(x1.shape, x1.dtype),
        jax.ShapeDtypeStruct(h2p.shape, h2p.dtype),
        jax.ShapeDtypeStruct(eid.shape, eid.dtype),
        jax.ShapeDtypeStruct(p.shape, p.dtype),
        jax.ShapeDtypeStruct((HALO_A, batch, D_A), jnp.float32),
        jax.ShapeDtypeStruct((HALO_B, batch, D_B), jnp.float32),
        jax.ShapeDtypeStruct((N_EXPERTS, LANES), jnp.float32),
    ]
    out_specs = [
        pl.BlockSpec((rows, D_MODEL), tok),
        pl.BlockSpec((rows, HALF), tok),
        pl.BlockSpec((SUBLANES, rows), lane_tok),
        pl.BlockSpec((SUBLANES, rows), lane_tok),
        pl.BlockSpec((HALO_A, nseq, D_A), lambda i: (0, i, 0)),
        pl.BlockSpec((HALO_B, nseq, D_B), lambda i: (0, i, 0)),
        pl.BlockSpec((N_EXPERTS, LANES), lambda i: (0, 0)),
    ]
    any_spec = pl.BlockSpec(memory_space=pl.ANY)
    in_specs = ([pl.BlockSpec((nseq, tt, D_MODEL), lambda i: (i, 0, 0)),
                 pl.BlockSpec((HALO_A, nseq, D_A), lambda i: (0, i, 0)),
                 pl.BlockSpec((HALO_B, nseq, D_B), lambda i: (0, i, 0))]
                + _mixer_param_specs() + [pl.BlockSpec((N_EXPERTS, LANES), lambda i: (0, 0))] + [any_spec] * 4)
    n_in = len(in_specs)
    scratch = [
        pltpu.VMEM((rows, 3 * D_A), jnp.float32),
        pltpu.VMEM((D_B // LANES, rows, LANES), jnp.float32),
        pltpu.VMEM((D_A // LANES, rows, LANES), jnp.float32),
        pltpu.VMEM((HALO_B + tt, nseq, D_B), jnp.float32),
        pltpu.VMEM((HALO_A + tt, nseq, D_A), jnp.float32),
        pltpu.VMEM((D_B // LANES, rows, LANES), jnp.float32),
        pltpu.VMEM((D_A // LANES, rows, LANES), jnp.float32),
        pltpu.VMEM((rows, D_MODEL), jnp.bfloat16),
    ]
    return pl.pallas_call(
        functools.partial(_mixer_sample_kernel, nseq=nseq, tt=tt),
        grid=(batch // nseq,),
        in_specs=in_specs,
        out_specs=out_specs,
        out_shape=out_shape,
        scratch_shapes=scratch,
        input_output_aliases={n_in - 4: 0, n_in - 3: 1, n_in - 2: 2, n_in - 1: 3},
        compiler_params=pltpu.CompilerParams(
            dimension_semantics=("arbitrary",), vmem_limit_bytes=VMEM_LIMIT),
        name="mixer_sample",
    )(x, state_a, state_b, *params, counts, x1, h2p, eid, p)


def _plan_kernel(eid_ref, cnt_ref, pos_ref, table_ref, carry_ref, start_ref, earlier_ref, *, tile):
    i = pl.program_id(0)
    tt = eid_ref.shape[1]

    @pl.when(i == 0)
    def _():
        tiles = jnp.floor((cnt_ref[...] + (tile - 0.5)) * (1.0 / tile))
        below = (lax.broadcasted_iota(jnp.int32, (N_EXPERTS, N_EXPERTS), 0)
                 > lax.broadcasted_iota(jnp.int32, (N_EXPERTS, N_EXPERTS), 1))
        start = jnp.dot(jnp.where(below, 1.0, 0.0).astype(jnp.bfloat16), tiles.astype(jnp.bfloat16),
                        preferred_element_type=jnp.float32) * tile
        start_ref[...] = start
        seg_end = start + tiles * tile
        tile_row = lax.broadcasted_iota(jnp.int32, (N_EXPERTS, LANES), 1).astype(jnp.float32) * tile
        owner = jnp.sum(jnp.where(seg_end <= tile_row, 1.0, 0.0), axis=0, keepdims=True)
        owner = jnp.minimum(owner, N_EXPERTS - 1.0)
        n_used = jnp.sum(tiles, axis=0, keepdims=True)
        row = lax.broadcasted_iota(jnp.int32, (SUBLANES, LANES), 0)
        table_ref[...] = jnp.where(row == 0, owner, jnp.where(row == 1, n_used, 0.0)).astype(jnp.int32)
        earlier = (lax.broadcasted_iota(jnp.int32, (tt, tt), 0)
                   < lax.broadcasted_iota(jnp.int32, (tt, tt), 1))
        earlier_ref[...] = jnp.where(earlier, 1.0, 0.0).astype(jnp.bfloat16)
        carry_ref[...] = jnp.zeros_like(carry_ref)

    eid = eid_ref[...]
    experts = lax.broadcasted_iota(jnp.int32, (N_EXPERTS, tt), 0)
    oh0 = experts == eid[0:1]
    oh1 = experts == eid[1:2]
    oh = jnp.where(oh0 | oh1, 1.0, 0.0)
    within = jnp.dot(oh.astype(jnp.bfloat16), earlier_ref[...], preferred_element_type=jnp.float32)
    slot_of = within + carry_ref[:, 0:1] + start_ref[:, 0:1]
    s0 = jnp.sum(jnp.where(oh0, slot_of, 0.0), axis=0, keepdims=True)
    s1 = jnp.sum(jnp.where(oh1, slot_of, 0.0), axis=0, keepdims=True)
    k = lax.broadcasted_iota(jnp.int32, (SUBLANES, tt), 0)
    pos_ref[...] = jnp.where(k == 0, s0, jnp.where(k == 1, s1, 0.0)).astype(jnp.int32)
    carry_ref[...] = carry_ref[...] + jnp.sum(oh, axis=1, keepdims=True)


def _plan(eid, counts, tile):
    n_tokens = eid.shape[1]
    return pl.pallas_call(
        functools.partial(_plan_kernel, tile=tile),
        grid=(n_tokens // PLAN_TILE,),
        in_specs=[pl.BlockSpec((SUBLANES, PLAN_TILE), lambda i: (0, i)),
                  pl.BlockSpec((N_EXPERTS, LANES), lambda i: (0, 0))],
        out_specs=[pl.BlockSpec((SUBLANES, PLAN_TILE), lambda i: (0, i)),
                   pl.BlockSpec((SUBLANES, LANES), lambda i: (0, 0))],
        out_shape=[jax.ShapeDtypeStruct((SUBLANES, n_tokens), jnp.int32),
                   jax.ShapeDtypeStruct((SUBLANES, LANES), jnp.int32)],
        scratch_shapes=[pltpu.VMEM((N_EXPERTS, LANES), jnp.float32),
                        pltpu.VMEM((N_EXPERTS, LANES), jnp.float32),
                        pltpu.VMEM((PLAN_TILE, PLAN_TILE), jnp.bfloat16)],
        compiler_params=pltpu.CompilerParams(dimension_semantics=("arbitrary",)),
        name="route_plan",
    )(eid, counts)


def _sc_mesh():
    return plsc.VectorSubcoreMesh(core_axis_name="c", subcore_axis_name="s")


def _sc_dispatch_rows(table, pos, n_out):
    n_workers, n_batches, top_k, batch = pos.shape
    n_rows, words = table.shape
    assert n_workers * n_batches * batch == n_rows and n_batches >= 2

    @functools.partial(
        pl.kernel, mesh=_sc_mesh(),
        out_type=jax.ShapeDtypeStruct((n_out, words), table.dtype),
        scratch_types=[pltpu.VMEM((n_batches, top_k, batch), jnp.int32),
                       pltpu.VMEM((2, batch, words), table.dtype),
                       pltpu.SemaphoreType.DMA((2,)),
                       pltpu.SemaphoreType.DMA((2, top_k))],
    )
    def dispatch(table_hbm, pos_hbm, out_hbm, idx_v, rows_v, sem_in, sem_out):
        worker = lax.axis_index("s") * SC_CORES + lax.axis_index("c")
        pltpu.sync_copy(pos_hbm.at[worker], idx_v)

        def read(b):
            src = table_hbm.at[pl.ds((worker * n_batches + b) * batch, batch)]
            return pltpu.async_copy(src, rows_v.at[b % 2], sem_in.at[b % 2])

        def write(b):
            return [pltpu.async_copy(rows_v.at[b % 2], out_hbm.at[idx_v.at[b, k]], sem_out.at[b % 2, k])
                    for k in range(top_k)]

        reads = {0: read(0)}
        writes = {}
        for b in range(n_batches):
            reads[b].wait()
            if b + 1 < n_batches:
                if b >= 1:
                    for w in writes[b - 1]:
                        w.wait()
                reads[b + 1] = read(b + 1)
            writes[b] = write(b)
        for b in (n_batches - 2, n_batches - 1):
            for w in writes[b]:
                w.wait()

    return dispatch(table, pos)


def _sc_gather_rows(table, idx):
    n_workers, n_batches, batch = idx.shape
    words = table.shape[1]
    assert n_batches >= 2

    @functools.partial(
        pl.kernel, mesh=_sc_mesh(),
        out_type=jax.ShapeDtypeStruct((n_workers * n_batches * batch, words), table.dtype),
        scratch_types=[pltpu.VMEM((n_batches, batch), jnp.int32),
                       pltpu.VMEM((2, batch, words), table.dtype),
                       pltpu.SemaphoreType.DMA((2,)),
                       pltpu.SemaphoreType.DMA((2,))],
    )
    def gather(table_hbm, idx_hbm, out_hbm, idx_v, rows_v, sem_in, sem_out):
        worker = lax.axis_index("s") * SC_CORES + lax.axis_index("c")
        pltpu.sync_copy(idx_hbm.at[worker], idx_v)

        def read(b):
            return pltpu.async_copy(table_hbm.at[idx_v.at[b]], rows_v.at[b % 2], sem_in.at[b % 2])

        def write(b):
            dst = out_hbm.at[pl.ds((worker * n_batches + b) * batch, batch)]
            return pltpu.async_copy(rows_v.at[b % 2], dst, sem_out.at[b % 2])

        reads = {0: read(0)}
        writes = {}
        for b in range(n_batches):
            reads[b].wait()
            if b + 1 < n_batches:
                if b >= 1:
                    writes[b - 1].wait()
                reads[b + 1] = read(b + 1)
            writes[b] = write(b)
        writes[n_batches - 2].wait()
        writes[n_batches - 1].wait()

    return gather(table, idx)


def _experts_kernel(te_ref, xs_ref, wg_ref, wu_ref, wd_ref, ys_ref):
    hi, lo = _unpack_bf16_halves(xs_ref[...])
    x = jnp.concatenate([hi.astype(jnp.bfloat16), lo.astype(jnp.bfloat16)], axis=1)
    w_gate_up = jnp.concatenate([wg_ref[0].astype(jnp.bfloat16), wu_ref[0].astype(jnp.bfloat16)], axis=1)
    wd = wd_ref[0].astype(jnp.bfloat16)
    gate_up = jnp.dot(x, w_gate_up, preferred_element_type=jnp.float32)
    gate = gate_up[:, :D_EXPERT]
    up = gate_up[:, D_EXPERT:]
    hid = (gate * jax.nn.sigmoid(gate) * up).astype(jnp.bfloat16)
    ys_ref[...] = _pack_bf16_halves(jnp.dot(hid, wd, preferred_element_type=jnp.float32))


def _experts(xs, tile_expert, n_valid, wg, wu, wd, tm):
    n_slots = xs.shape[0]
    row_block = lambda i, te: (i, 0)
    w_block = lambda i, te: (te[i], 0, 0)
    return pl.pallas_call(
        _experts_kernel,
        grid_spec=pltpu.PrefetchScalarGridSpec(
            num_scalar_prefetch=1,
            grid=(n_valid,),
            in_specs=[pl.BlockSpec((tm, HALF), row_block),
                      pl.BlockSpec((1, D_MODEL, D_EXPERT), w_block, pipeline_mode=pl.Buffered(2)),
                      pl.BlockSpec((1, D_MODEL, D_EXPERT), w_block, pipeline_mode=pl.Buffered(2)),
                      pl.BlockSpec((1, D_EXPERT, D_MODEL), w_block, pipeline_mode=pl.Buffered(2))],
            out_specs=pl.BlockSpec((tm, HALF), row_block),
        ),
        out_shape=jax.ShapeDtypeStruct((n_slots, HALF), jnp.uint32),
        compiler_params=pltpu.CompilerParams(
            dimension_semantics=("arbitrary",), vmem_limit_bytes=VMEM_LIMIT),
        name="experts",
    )(tile_expert, xs, wg, wu, wd)


def _final_kernel(x1_ref, y0_ref, y1_ref, p_ref, gfin_ref, *rest):
    out_ref = rest[-1]
    pt = jnp.transpose(p_ref[...])
    p0 = pt[:, 0:1]
    p1 = pt[:, 1:2]
    a_hi, a_lo = _unpack_bf16_halves(y0_ref[...])
    b_hi, b_lo = _unpack_bf16_halves(y1_ref[...])
    x1 = x1_ref[...]
    x2_hi = x1[:, :HALF] + (p0 * a_hi + p1 * b_hi)
    x2_lo = x1[:, HALF:] + (p0 * a_lo + p1 * b_lo)
    ms = (jnp.sum(x2_hi * x2_hi, axis=-1, keepdims=True)
          + jnp.sum(x2_lo * x2_lo, axis=-1, keepdims=True)) / D_MODEL
    scale = lax.rsqrt(ms + RMS_EPS)
    g = gfin_ref[...]
    out_ref[:, :HALF] = x2_hi * scale * g[:, :HALF]
    out_ref[:, HALF:] = x2_lo * scale * g[:, HALF:]


def _final(x1, yk, p, gfin, row_offset, n_rows, tm, out_rows, out_offset, out_buf=None):
    n_tokens = x1.shape[0]
    first = row_offset // tm
    second = (n_tokens + row_offset) // tm
    out_first = out_offset // tm
    in_specs = [pl.BlockSpec((tm, D_MODEL), lambda i: (first + i, 0)),
                pl.BlockSpec((tm, HALF), lambda i: (first + i, 0)),
                pl.BlockSpec((tm, HALF), lambda i: (second + i, 0)),
                pl.BlockSpec((SUBLANES, tm), lambda i: (0, first + i)),
                pl.BlockSpec((1, D_MODEL), lambda i: (0, 0))]
    args = [x1, yk, yk, p, gfin]
    aliases = {}
    if out_buf is not None:
        in_specs.append(pl.BlockSpec(memory_space=pl.ANY))
        args.append(out_buf)
        aliases = {len(args) - 1: 0}
    return pl.pallas_call(
        _final_kernel,
        grid=(n_rows // tm,),
        in_specs=in_specs,
        out_specs=pl.BlockSpec((tm, D_MODEL), lambda i: (out_first + i, 0)),
        out_shape=jax.ShapeDtypeStruct((out_rows, D_MODEL), jnp.float32),
        input_output_aliases=aliases,
        compiler_params=pltpu.CompilerParams(
            dimension_semantics=("arbitrary",), vmem_limit_bytes=VMEM_LIMIT),
        name="final",
    )(*args)


def _expert_tile(n_tokens):
    mean_rows = TOP_K * n_tokens // N_EXPERTS
    return -(-(mean_rows + 2 * math.isqrt(mean_rows)) // (2 * SUBLANES)) * (2 * SUBLANES)


def _routed_experts(h2p, eid, counts, w_gate, w_up, w_down):
    n_tokens = h2p.shape[0]
    assert n_tokens % (SC_WORKERS * SC_DISPATCH_BATCH) == 0 and n_tokens % PLAN_TILE == 0
    assert (TOP_K * n_tokens) % (SC_WORKERS * SC_BATCH) == 0
    tile = _expert_tile(n_tokens)
    n_tiles = -(-(TOP_K * n_tokens + N_EXPERTS * (tile - 1)) // tile)
    assert n_tiles <= LANES
    pos, table = _plan(eid, counts, tile)
    pos = pos[:TOP_K]
    by_token = jnp.transpose(pos.reshape(TOP_K, SC_WORKERS, -1, SC_DISPATCH_BATCH), (1, 2, 0, 3))
    xs = _sc_dispatch_rows(h2p, by_token, n_tiles * tile)
    ys = _experts(xs, table[0], table[1, 0], w_gate, w_up, w_down, tile)
    return _sc_gather_rows(ys, pos.reshape(SC_WORKERS, -1, SC_BATCH))


def kernel(x_prompt, x_sample, state_conv_a, state_conv_b, g_mix, w_in, conv_a_w, conv_b_w, conv_b_bias,
           ln_g, ln_b, w_out, g_ffn, w_coarse, b_coarse, w_fine, b_fine, w_gate, w_up, w_down, g_final):
    assert g_mix.shape[0] == 1, "single trunk layer"
    batch, seq, _ = x_prompt.shape
    dec_batch, dec_seq, _ = x_sample.shape
    n_prompt = batch * seq
    n_sample = dec_batch * dec_seq
    bf16 = jnp.bfloat16

    wr = jnp.concatenate([
        w_coarse[0], jnp.zeros((D_MODEL, SUBLANES - N_EXPERT_GROUPS), jnp.float32),
        jnp.transpose(w_fine[0], (1, 0, 2)).reshape(D_MODEL, N_EXPERTS),
        jnp.zeros((D_MODEL, LANES - ROUTER_ROWS), jnp.float32)], axis=1)
    wr_hi = wr.astype(bf16)
    wr_lo = (wr - wr_hi.astype(jnp.float32)).astype(bf16)
    wr_both = jnp.concatenate([wr_hi, wr_lo], axis=1)
    br = jnp.concatenate([
        b_coarse[0], jnp.full((SUBLANES - N_EXPERT_GROUPS,), NEG_BIG, jnp.float32),
        b_fine[0].reshape(N_EXPERTS)]).reshape(ROUTER_ROWS, 1)

    params = (g_mix, w_in[0].astype(bf16), conv_a_w, conv_b_w, conv_b_bias, ln_g, ln_b,
              w_out[0].astype(bf16), g_ffn, wr_both, br)

    experts = (w_gate[0], w_up[0], w_down[0])
    gfin = g_final.reshape(1, D_MODEL)

    bufs = _mixer_prompt(x_prompt, params, n_prompt + n_sample, MIXER_TILE, 0, batch)
    na_p, nb_p, counts = bufs[4:]
    x1, h2p, eid, p, na_s, nb_s, counts = _mixer_sample(
        x_sample, jnp.transpose(state_conv_a[0], (1, 0, 2)), jnp.transpose(state_conv_b[0], (1, 0, 2)),
        params, counts, bufs[:4], n_prompt, nseq=32)
    na_s = jnp.transpose(na_s, (1, 0, 2))
    nb_s = jnp.transpose(nb_s, (1, 0, 2))
    yk = _routed_experts(h2p, eid, counts, *experts)
    y_p = _final(x1, yk, p, gfin, 0, n_prompt, FINAL_TILE, n_prompt, 0)
    y_s = _final(x1, yk, p, gfin, n_prompt, n_sample, FINAL_TILE, n_sample, 0)
    return (y_p.reshape(batch, seq, D_MODEL), y_s.reshape(dec_batch, dec_seq, D_MODEL),
            na_p[None], nb_p[None], na_s[None], nb_s[None])
```

```python
import functools
import math

import jax
import jax.numpy as jnp
from jax import lax
from jax.experimental import pallas as pl
from jax.experimental.pallas import tpu as pltpu
from jax.experimental.pallas import tpu_sc as plsc

D_MODEL = 1024
D_A = 512
D_B = 512
CONV_A = 3
CONV_B = 31
HALO_A = CONV_A - 1
HALO_B = CONV_B - 1
IN_COLS = 3 * D_A + 2 * D_B
N_EXPERT_GROUPS = 4
EXPERTS_PER_GROUP = 8
N_EXPERTS = N_EXPERT_GROUPS * EXPERTS_PER_GROUP
TOP_K = 2
D_EXPERT = D_MODEL // 4
RMS_EPS = 1e-6
LN_EPS = 1e-5

SUBLANES = 8
LANES = 128
PAD_A = SUBLANES
PAD_B = 32
ROUTER_ROWS = SUBLANES + N_EXPERTS
NEG_BIG = -1e30
VMEM_LIMIT = 56 * 1024 * 1024
HALF = D_MODEL // 2
HI_MASK = 0xFFFF0000

SC_CORES = 2
SC_SUBCORES = 16
SC_WORKERS = SC_CORES * SC_SUBCORES
SC_BATCH = 64
SC_DISPATCH_BATCH = 32

PLAN_TILE = 1024
FINAL_TILE = 1024
MIXER_TILE = 512
MIXER_ROW_CHUNK = 64


def _rms_scale(x):
    return x * lax.rsqrt(jnp.mean(x * x, axis=-1, keepdims=True) + RMS_EPS)


def _pack_bf16_halves(x):
    bits = lax.bitcast_convert_type(x.astype(jnp.bfloat16).astype(jnp.float32), jnp.uint32)
    return bits[:, :HALF] | (bits[:, HALF:] >> 16)


def _unpack_bf16_halves(w):
    hi = lax.bitcast_convert_type(w & jnp.uint32(HI_MASK), jnp.float32)
    lo = lax.bitcast_convert_type(w << 16, jnp.float32)
    return hi, lo


def _route(logits_t):
    rows = logits_t.shape[1]
    iota = lax.broadcasted_iota(jnp.int32, (SUBLANES, rows), 0)
    lc = logits_t[0:SUBLANES]
    cmax = jnp.max(lc, axis=0, keepdims=True)
    grp = jnp.min(jnp.where(lc == cmax, iota, SUBLANES), axis=0, keepdims=True)
    p_grp = 1.0 / jnp.sum(jnp.exp(lc - cmax), axis=0, keepdims=True)
    sel = logits_t[SUBLANES:2 * SUBLANES]
    for g in range(1, N_EXPERT_GROUPS):
        sel = jnp.where(grp == g, logits_t[(g + 1) * SUBLANES:(g + 2) * SUBLANES], sel)
    v1 = jnp.max(sel, axis=0, keepdims=True)
    i1 = jnp.min(jnp.where(sel == v1, iota, SUBLANES), axis=0, keepdims=True)
    sel2 = jnp.where(iota == i1, -jnp.inf, sel)
    v2 = jnp.max(sel2, axis=0, keepdims=True)
    i2 = jnp.min(jnp.where(sel2 == v2, iota, SUBLANES), axis=0, keepdims=True)
    e2 = jnp.exp(v2 - v1)
    den = 1.0 + e2
    p1 = p_grp / den
    p2 = p_grp * e2 / den
    base = grp * EXPERTS_PER_GROUP
    return (base + i1, base + i2), (p1, p2)


def _after_mix(x1, f0, gffn_ref, wr_ref, br_ref, x1_ref, h2p_ref, eid_ref, p_ref, cnt_ref):
    m = x1.shape[0]
    x1_ref[f0:f0 + m, :] = x1
    h2 = _rms_scale(x1) * gffn_ref[...]
    h2_hi = h2.astype(jnp.bfloat16)
    h2p_ref[f0:f0 + m, :] = _pack_bf16_halves(h2)
    h2_lo = (h2 - h2_hi.astype(jnp.float32)).astype(jnp.bfloat16)
    both = jnp.dot(h2_hi, wr_ref[...], preferred_element_type=jnp.float32)
    cross = jnp.dot(h2_lo, wr_ref[:, 0:LANES], preferred_element_type=jnp.float32)
    logits = both[:, 0:LANES] + both[:, LANES:] + cross
    logits_t = jnp.transpose(logits)[0:ROUTER_ROWS] + br_ref[...]
    (e1, e2), (p1, p2) = _route(logits_t)
    iota = lax.broadcasted_iota(jnp.int32, (SUBLANES, m), 0)
    eid_ref[:, f0:f0 + m] = jnp.where(iota == 0, e1, jnp.where(iota == 1, e2, 0))
    p_ref[:, f0:f0 + m] = jnp.where(iota == 0, p1, jnp.where(iota == 1, p2, 0.0))
    experts = lax.broadcasted_iota(jnp.int32, (N_EXPERTS, m), 0)
    routed = jnp.where((experts == e1) | (experts == e2), 1.0, 0.0)
    cnt_ref[...] = cnt_ref[...] + jnp.sum(routed, axis=1, keepdims=True)


def _mixer_prompt_kernel(x_ref, gmix_ref, win_ref, caw_ref, cbw_ref, cbb_ref, lng_ref, lnb_ref,
                         wout_ref, gffn_ref, wr_ref, br_ref,
                         x1_ref, h2p_ref, eid_ref, p_ref, na_ref, nb_ref, cnt_ref,
                         proj_ref, uext_ref, gext_ref, z_ref, ush_ref, gsh_ref, *, tt, row_chunk):
    t = pl.program_id(1)
    col_chunk = 2 * LANES
    span_b = PAD_B + tt - SUBLANES

    @pl.when((pl.program_id(0) == 0) & (t == 0))
    def _():
        cnt_ref[...] = jnp.zeros_like(cnt_ref)

    @pl.when(t == 0)
    def _():
        uext_ref[0:PAD_A, :] = jnp.zeros((PAD_A, D_A), jnp.float32)
        gext_ref[0:PAD_B, :] = jnp.zeros((PAD_B, D_B), jnp.float32)

    def window(base_ref, shifted_ref, first_shift, off):
        r = off % SUBLANES
        a8 = off - r
        if r == 0:
            return base_ref[a8:a8 + row_chunk, :]
        return shifted_ref[r - first_shift, a8:a8 + row_chunk, :]

    x = x_ref[0]
    h = (_rms_scale(x) * gmix_ref[...]).astype(jnp.bfloat16)
    for c0 in range(0, D_B, col_chunk):
        v_b = jnp.dot(h, win_ref[:, 3 * D_A + c0:3 * D_A + c0 + col_chunk], preferred_element_type=jnp.float32)
        g_b = jnp.dot(h, win_ref[:, 3 * D_A + D_B + c0:3 * D_A + D_B + c0 + col_chunk],
                      preferred_element_type=jnp.float32)
        gext_ref[PAD_B:PAD_B + tt, c0:c0 + col_chunk] = v_b * jax.nn.sigmoid(g_b)
    for r in range(1, SUBLANES):
        gsh_ref[r - 1, 0:span_b, :] = gext_ref[r:r + span_b, :]

    cbw = cbw_ref[...]
    row_starts = list(range(0, tt, row_chunk))
    a_cols = list(range(0, 3 * D_A, col_chunk))
    for i, r0 in enumerate(row_starts):
        acc_b = None
        for k in range(CONV_B):
            term = window(gext_ref, gsh_ref, 1, PAD_B - HALO_B + k + r0) * cbw[k:k + 1, :]
            acc_b = term if acc_b is None else acc_b + term
        zb = acc_b + cbb_ref[...]
        mu = jnp.mean(zb, axis=-1, keepdims=True)
        zc = zb - mu
        var = jnp.mean(zc * zc, axis=-1, keepdims=True)
        y = zc * lax.rsqrt(var + LN_EPS) * lng_ref[...] + lnb_ref[...]
        z_ref[r0:r0 + row_chunk, D_A:] = (y * jax.nn.sigmoid(y)).astype(jnp.bfloat16)
        for c0 in a_cols[i * len(a_cols) // len(row_starts):(i + 1) * len(a_cols) // len(row_starts)]:
            proj_ref[:, c0:c0 + col_chunk] = jnp.dot(h, win_ref[:, c0:c0 + col_chunk],
                                                     preferred_element_type=jnp.float32)

    uext_ref[PAD_A:PAD_A + tt, :] = proj_ref[:, D_A:2 * D_A] * proj_ref[:, 2 * D_A:3 * D_A]
    for r in range(SUBLANES - HALO_A, SUBLANES):
        ush_ref[r - (SUBLANES - HALO_A), :, :] = uext_ref[r:r + tt, :]
    caw = caw_ref[...]
    for r0 in row_starts:
        acc_a = None
        for k in range(CONV_A):
            term = window(uext_ref, ush_ref, SUBLANES - HALO_A, PAD_A - HALO_A + k + r0) * caw[k:k + 1, :]
            acc_a = term if acc_a is None else acc_a + term
        z_ref[r0:r0 + row_chunk, 0:D_A] = (proj_ref[r0:r0 + row_chunk, 0:D_A] * acc_a).astype(jnp.bfloat16)

    x1 = x_ref[0] + jnp.dot(z_ref[...], wout_ref[...], preferred_element_type=jnp.float32)
    _after_mix(x1, 0, gffn_ref, wr_ref, br_ref, x1_ref, h2p_ref, eid_ref, p_ref, cnt_ref)

    @pl.when(t == pl.num_programs(1) - 1)
    def _():
        na_ref[0] = uext_ref[PAD_A + tt - HALO_A:PAD_A + tt, :]
        nb_ref[0] = gext_ref[PAD_B + tt - HALO_B:PAD_B + tt, :]
    uext_ref[0:PAD_A, :] = uext_ref[tt:tt + PAD_A, :]
    gext_ref[0:PAD_B, :] = gext_ref[tt:tt + PAD_B, :]


def _mixer_sample_kernel(x_ref, sa_ref, sb_ref, gmix_ref, win_ref, caw_ref, cbw_ref, cbb_ref, lng_ref, lnb_ref,
                         wout_ref, gffn_ref, wr_ref, br_ref, cnt_in_ref, _x1_in, _h2p_in, _eid_in, _p_in,
                         x1_ref, h2p_ref, eid_ref, p_ref, na_ref, nb_ref, cnt_ref,
                         proj_ref, gnt_ref, unt_ref, gtm_ref, utm_ref, ynt_ref, ant_ref, z_ref, *, nseq, tt):
    rows = nseq * tt
    col_chunk = 2 * LANES

    @pl.when(pl.program_id(0) == 0)
    def _():
        cnt_ref[...] = cnt_in_ref[...]

    x = x_ref[...].reshape(rows, D_MODEL)
    h = (_rms_scale(x) * gmix_ref[...]).astype(jnp.bfloat16)
    for c0 in range(0, D_B, col_chunk):
        v_b = jnp.dot(h, win_ref[:, 3 * D_A + c0:3 * D_A + c0 + col_chunk], preferred_element_type=jnp.float32)
        g_b = jnp.dot(h, win_ref[:, 3 * D_A + D_B + c0:3 * D_A + D_B + c0 + col_chunk],
                      preferred_element_type=jnp.float32)
        g = v_b * jax.nn.sigmoid(g_b)
        for c in range(0, col_chunk, LANES):
            gnt_ref[(c0 + c) // LANES] = g[:, c:c + LANES]
    for c0 in range(0, 3 * D_A, col_chunk):
        proj_ref[:, c0:c0 + col_chunk] = jnp.dot(h, win_ref[:, c0:c0 + col_chunk],
                                                 preferred_element_type=jnp.float32)
    u = proj_ref[:, D_A:2 * D_A] * proj_ref[:, 2 * D_A:3 * D_A]
    for c in range(0, D_A, LANES):
        unt_ref[c // LANES] = u[:, c:c + LANES]

    utm_ref[0:HALO_A] = sa_ref[...]
    gtm_ref[0:HALO_B] = sb_ref[...]
    for t in range(tt):
        for c in range(0, D_A, LANES):
            utm_ref[HALO_A + t, :, c:c + LANES] = unt_ref[c // LANES, pl.ds(t, nseq, stride=tt), :]
        for c in range(0, D_B, LANES):
            gtm_ref[HALO_B + t, :, c:c + LANES] = gnt_ref[c // LANES, pl.ds(t, nseq, stride=tt), :]
    na_ref[...] = utm_ref[tt:tt + HALO_A]
    nb_ref[...] = gtm_ref[tt:tt + HALO_B]

    caw = caw_ref[...]
    cbw = cbw_ref[...]
    for t in range(tt):
        acc_b = None
        for k in range(CONV_B):
            term = gtm_ref[t + k] * cbw[k:k + 1, :]
            acc_b = term if acc_b is None else acc_b + term
        zb = acc_b + cbb_ref[...]
        mu = jnp.mean(zb, axis=-1, keepdims=True)
        zc = zb - mu
        var = jnp.mean(zc * zc, axis=-1, keepdims=True)
        y = zc * lax.rsqrt(var + LN_EPS) * lng_ref[...] + lnb_ref[...]
        y = y * jax.nn.sigmoid(y)
        for c in range(0, D_B, LANES):
            ynt_ref[c // LANES, pl.ds(t, nseq, stride=tt), :] = y[:, c:c + LANES]
        acc_a = None
        for k in range(CONV_A):
            term = utm_ref[t + k] * caw[k:k + 1, :]
            acc_a = term if acc_a is None else acc_a + term
        for c in range(0, D_A, LANES):
            ant_ref[c // LANES, pl.ds(t, nseq, stride=tt), :] = acc_a[:, c:c + LANES]

    for c in range(0, D_A, LANES):
        z_ref[:, c:c + LANES] = (proj_ref[:, c:c + LANES] * ant_ref[c // LANES]).astype(jnp.bfloat16)
    for c in range(0, D_B, LANES):
        z_ref[:, D_A + c:D_A + c + LANES] = ynt_ref[c // LANES].astype(jnp.bfloat16)
    x1 = x + jnp.dot(z_ref[...], wout_ref[...], preferred_element_type=jnp.float32)
    _after_mix(x1, 0, gffn_ref, wr_ref, br_ref, x1_ref, h2p_ref, eid_ref, p_ref, cnt_ref)


def _full(shape):
    return pl.BlockSpec(shape, lambda *_: (0,) * len(shape))


def _mixer_param_specs():
    return [
        _full((1, D_MODEL)),
        _full((D_MODEL, IN_COLS)),
        pl.BlockSpec((None, CONV_A, D_A), lambda *_: (0, 0, 0)),
        pl.BlockSpec((None, CONV_B, D_B), lambda *_: (0, 0, 0)),
        _full((1, D_B)),
        _full((1, D_B)),
        _full((1, D_B)),
        _full((D_MODEL, D_MODEL)),
        _full((1, D_MODEL)),
        _full((D_MODEL, 2 * LANES)),
        _full((ROUTER_ROWS, 1)),
    ]


def _mixer_prompt(x, params, n_tokens_total, tt, seq_first, batch):
    seq = x.shape[1]
    n_t = seq // tt
    tok = lambda b, t: (b * n_t + t, 0)
    lane_tok = lambda b, t: (0, b * n_t + t)
    out_shape = [
        jax.ShapeDtypeStruct((n_tokens_total, D_MODEL), jnp.float32),
        jax.ShapeDtypeStruct((n_tokens_total, HALF), jnp.uint32),
        jax.ShapeDtypeStruct((SUBLANES, n_tokens_total), jnp.int32),
        jax.ShapeDtypeStruct((SUBLANES, n_tokens_total), jnp.float32),
        jax.ShapeDtypeStruct((batch, HALO_A, D_A), jnp.float32),
        jax.ShapeDtypeStruct((batch, HALO_B, D_B), jnp.float32),
        jax.ShapeDtypeStruct((N_EXPERTS, LANES), jnp.float32),
    ]
    out_specs = [
        pl.BlockSpec((tt, D_MODEL), tok),
        pl.BlockSpec((tt, HALF), tok),
        pl.BlockSpec((SUBLANES, tt), lane_tok),
        pl.BlockSpec((SUBLANES, tt), lane_tok),
        pl.BlockSpec((1, HALO_A, D_A), lambda b, t: (b, 0, 0)),
        pl.BlockSpec((1, HALO_B, D_B), lambda b, t: (b, 0, 0)),
        pl.BlockSpec((N_EXPERTS, LANES), lambda b, t: (0, 0)),
    ]
    scratch = [
        pltpu.VMEM((tt, 3 * D_A), jnp.float32),
        pltpu.VMEM((PAD_A + tt, D_A), jnp.float32),
        pltpu.VMEM((PAD_B + tt, D_B), jnp.float32),
        pltpu.VMEM((tt, D_MODEL), jnp.bfloat16),
        pltpu.VMEM((HALO_A, tt, D_A), jnp.float32),
        pltpu.VMEM((SUBLANES - 1, PAD_B + tt - SUBLANES, D_B), jnp.float32),
    ]
    return pl.pallas_call(
        functools.partial(_mixer_prompt_kernel, tt=tt, row_chunk=MIXER_ROW_CHUNK),
        grid=(batch, n_t),
        in_specs=[pl.BlockSpec((1, tt, D_MODEL), lambda b, t: (b + seq_first, t, 0))] + _mixer_param_specs(),
        out_specs=out_specs,
        out_shape=out_shape,
        scratch_shapes=scratch,
        compiler_params=pltpu.CompilerParams(
            dimension_semantics=("arbitrary", "arbitrary"), vmem_limit_bytes=VMEM_LIMIT),
        name="mixer_prompt",
    )(x, *params)


def _mixer_sample(x, state_a, state_b, params, counts, bufs, row_offset, nseq):
    batch, tt, _ = x.shape
    rows = nseq * tt
    first = row_offset // rows
    tok = lambda i: (first + i, 0)
    lane_tok = lambda i: (0, first + i)
    x1, h2p, eid, p = bufs
    out_shape = [
        jax.ShapeDtypeStruct(x1.shape, x1.dtype),
        jax.ShapeDtypeStruct(h2p.shape, h2p.dtype),
        jax.ShapeDtypeStruct(eid.shape, eid.dtype),
        jax.ShapeDtypeStruct(p.shape, p.dtype),
        jax.ShapeDtypeStruct((HALO_A, batch, D_A), jnp.float32),
        jax.ShapeDtypeStruct((HALO_B, batch, D_B), jnp.float32),
        jax.ShapeDtypeStruct((N_EXPERTS, LANES), jnp.float32),
    ]
    out_specs = [
        pl.BlockSpec((rows, D_MODEL), tok),
        pl.BlockSpec((rows, HALF), tok),
        pl.BlockSpec((SUBLANES, rows), lane_tok),
        pl.BlockSpec((SUBLANES, rows), lane_tok),
        pl.BlockSpec((HALO_A, nseq, D_A), lambda i: (0, i, 0)),
        pl.BlockSpec((HALO_B, nseq, D_B), lambda i: (0, i, 0)),
        pl.BlockSpec((N_EXPERTS, LANES), lambda i: (0, 0)),
    ]
    any_spec = pl.BlockSpec(memory_space=pl.ANY)
    in_specs = ([pl.BlockSpec((nseq, tt, D_MODEL), lambda i: (i, 0, 0)),
                 pl.BlockSpec((HALO_A, nseq, D_A), lambda i: (0, i, 0)),
                 pl.BlockSpec((HALO_B, nseq, D_B), lambda i: (0, i, 0))]
                + _mixer_param_specs() + [pl.BlockSpec((N_EXPERTS, LANES), lambda i: (0, 0))] + [any_spec] * 4)
    n_in = len(in_specs)
    scratch = [
        pltpu.VMEM((rows, 3 * D_A), jnp.float32),
        pltpu.VMEM((D_B // LANES, rows, LANES), jnp.float32),
        pltpu.VMEM((D_A // LANES, rows, LANES), jnp.float32),
        pltpu.VMEM((HALO_B + tt, nseq, D_B), jnp.float32),
        pltpu.VMEM((HALO_A + tt, nseq, D_A), jnp.float32),
        pltpu.VMEM((D_B // LANES, rows, LANES), jnp.float32),
        pltpu.VMEM((D_A // LANES, rows, LANES), jnp.float32),
        pltpu.VMEM((rows, D_MODEL), jnp.bfloat16),
    ]
    return pl.pallas_call(
        functools.partial(_mixer_sample_kernel, nseq=nseq, tt=tt),
        grid=(batch // nseq,),
        in_specs=in_specs,
        out_specs=out_specs,
        out_shape=out_shape,
        scratch_shapes=scratch,
        input_output_aliases={n_in - 4: 0, n_in - 3: 1, n_in - 2: 2, n_in - 1: 3},
        compiler_params=pltpu.CompilerParams(
            dimension_semantics=("arbitrary",), vmem_limit_bytes=VMEM_LIMIT),
        name="mixer_sample",
    )(x, state_a, state_b, *params, counts, x1, h2p, eid, p)


def _plan_kernel(eid_ref, cnt_ref, pos_ref, table_ref, carry_ref, start_ref, earlier_ref, *, tile):
    i = pl.program_id(0)
    tt = eid_ref.shape[1]

    @pl.when(i == 0)
    def _():
        tiles = jnp.floor((cnt_ref[...] + (tile - 0.5)) * (1.0 / tile))
        below = (lax.broadcasted_iota(jnp.int32, (N_EXPERTS, N_EXPERTS), 0)
                 > lax.broadcasted_iota(jnp.int32, (N_EXPERTS, N_EXPERTS), 1))
        start = jnp.dot(jnp.where(below, 1.0, 0.0).astype(jnp.bfloat16), tiles.astype(jnp.bfloat16),
                        preferred_element_type=jnp.float32) * tile
        start_ref[...] = start
        seg_end = start + tiles * tile
        tile_row = lax.broadcasted_iota(jnp.int32, (N_EXPERTS, LANES), 1).astype(jnp.float32) * tile
        owner = jnp.sum(jnp.where(seg_end <= tile_row, 1.0, 0.0), axis=0, keepdims=True)
        owner = jnp.minimum(owner, N_EXPERTS - 1.0)
        n_used = jnp.sum(tiles, axis=0, keepdims=True)
        row = lax.broadcasted_iota(jnp.int32, (SUBLANES, LANES), 0)
        table_ref[...] = jnp.where(row == 0, owner, jnp.where(row == 1, n_used, 0.0)).astype(jnp.int32)
        earlier = (lax.broadcasted_iota(jnp.int32, (tt, tt), 0)
                   < lax.broadcasted_iota(jnp.int32, (tt, tt), 1))
        earlier_ref[...] = jnp.where(earlier, 1.0, 0.0).astype(jnp.bfloat16)
        carry_ref[...] = jnp.zeros_like(carry_ref)

    eid = eid_ref[...]
    experts = lax.broadcasted_iota(jnp.int32, (N_EXPERTS, tt), 0)
    oh0 = experts == eid[0:1]
    oh1 = experts == eid[1:2]
    oh = jnp.where(oh0 | oh1, 1.0, 0.0)
    within = jnp.dot(oh.astype(jnp.bfloat16), earlier_ref[...], preferred_element_type=jnp.float32)
    slot_of = within + carry_ref[:, 0:1] + start_ref[:, 0:1]
    s0 = jnp.sum(jnp.where(oh0, slot_of, 0.0), axis=0, keepdims=True)
    s1 = jnp.sum(jnp.where(oh1, slot_of, 0.0), axis=0, keepdims=True)
    k = lax.broadcasted_iota(jnp.int32, (SUBLANES, tt), 0)
    pos_ref[...] = jnp.where(k == 0, s0, jnp.where(k == 1, s1, 0.0)).astype(jnp.int32)
    carry_ref[...] = carry_ref[...] + jnp.sum(oh, axis=1, keepdims=True)


def _plan(eid, counts, tile):
    n_tokens = eid.shape[1]
    return pl.pallas_call(
        functools.partial(_plan_kernel, tile=tile),
        grid=(n_tokens // PLAN_TILE,),
        in_specs=[pl.BlockSpec((SUBLANES, PLAN_TILE), lambda i: (0, i)),
                  pl.BlockSpec((N_EXPERTS, LANES), lambda i: (0, 0))],
        out_specs=[pl.BlockSpec((SUBLANES, PLAN_TILE), lambda i: (0, i)),
                   pl.BlockSpec((SUBLANES, LANES), lambda i: (0, 0))],
        out_shape=[jax.ShapeDtypeStruct((SUBLANES, n_tokens), jnp.int32),
                   jax.ShapeDtypeStruct((SUBLANES, LANES), jnp.int32)],
        scratch_shapes=[pltpu.VMEM((N_EXPERTS, LANES), jnp.float32),
                        pltpu.VMEM((N_EXPERTS, LANES), jnp.float32),
                        pltpu.VMEM((PLAN_TILE, PLAN_TILE), jnp.bfloat16)],
        compiler_params=pltpu.CompilerParams(dimension_semantics=("arbitrary",)),
        name="route_plan",
    )(eid, counts)


def _sc_mesh():
    return plsc.VectorSubcoreMesh(core_axis_name="c", subcore_axis_name="s")


def _sc_dispatch_rows(table, pos, n_out):
    n_workers, n_batches, top_k, batch = pos.shape
    n_rows, words = table.shape
    assert n_workers * n_batches * batch == n_rows and n_batches >= 2

    @functools.partial(
        pl.kernel, mesh=_sc_mesh(),
        out_type=jax.ShapeDtypeStruct((n_out, words), table.dtype),
        scratch_types=[pltpu.VMEM((n_batches, top_k, batch), jnp.int32),
                       pltpu.VMEM((2, batch, words), table.dtype),
                       pltpu.SemaphoreType.DMA((2,)),
                       pltpu.SemaphoreType.DMA((2, top_k))],
    )
    def dispatch(table_hbm, pos_hbm, out_hbm, idx_v, rows_v, sem_in, sem_out):
        worker = lax.axis_index("s") * SC_CORES + lax.axis_index("c")
        pltpu.sync_copy(pos_hbm.at[worker], idx_v)

        def read(b):
            src = table_hbm.at[pl.ds((worker * n_batches + b) * batch, batch)]
            return pltpu.async_copy(src, rows_v.at[b % 2], sem_in.at[b % 2])

        def write(b):
            return [pltpu.async_copy(rows_v.at[b % 2], out_hbm.at[idx_v.at[b, k]], sem_out.at[b % 2, k])
                    for k in range(top_k)]

        reads = {0: read(0)}
        writes = {}
        for b in range(n_batches):
            reads[b].wait()
            if b + 1 < n_batches:
                if b >= 1:
                    for w in writes[b - 1]:
                        w.wait()
                reads[b + 1] = read(b + 1)
            writes[b] = write(b)
        for b in (n_batches - 2, n_batches - 1):
            for w in writes[b]:
                w.wait()

    return dispatch(table, pos)


def _sc_gather_rows(table, idx):
    n_workers, n_batches, batch = idx.shape
    words = table.shape[1]
    assert n_batches >= 2

    @functools.partial(
        pl.kernel, mesh=_sc_mesh(),
        out_type=jax.ShapeDtypeStruct((n_workers * n_batches * batch, words), table.dtype),
        scratch_types=[pltpu.VMEM((n_batches, batch), jnp.int32),
                       pltpu.VMEM((2, batch, words), table.dtype),
                       pltpu.SemaphoreType.DMA((2,)),
                       pltpu.SemaphoreType.DMA((2,))],
    )
    def gather(table_hbm, idx_hbm, out_hbm, idx_v, rows_v, sem_in, sem_out):
        worker = lax.axis_index("s") * SC_CORES + lax.axis_index("c")
        pltpu.sync_copy(idx_hbm.at[worker], idx_v)

        def read(b):
            return pltpu.async_copy(table_hbm.at[idx_v.at[b]], rows_v.at[b % 2], sem_in.at[b % 2])

        def write(b):
            dst = out_hbm.at[pl.ds((worker * n_batches + b) * batch, batch)]
            return pltpu.async_copy(rows_v.at[b % 2], dst, sem_out.at[b % 2])

        reads = {0: read(0)}
        writes = {}
        for b in range(n_batches):
            reads[b].wait()
            if b + 1 < n_batches:
                if b >= 1:
                    writes[b - 1].wait()
                reads[b + 1] = read(b + 1)
            writes[b] = write(b)
        writes[n_batches - 2].wait()
        writes[n_batches - 1].wait()

    return gather(table, idx)


def _experts_kernel(te_ref, xs_ref, wg_ref, wu_ref, wd_ref, ys_ref):
    hi, lo = _unpack_bf16_halves(xs_ref[...])
    x = jnp.concatenate([hi.astype(jnp.bfloat16), lo.astype(jnp.bfloat16)], axis=1)
    w_gate_up = jnp.concatenate([wg_ref[0].astype(jnp.bfloat16), wu_ref[0].astype(jnp.bfloat16)], axis=1)
    wd = wd_ref[0].astype(jnp.bfloat16)
    gate_up = jnp.dot(x, w_gate_up, preferred_element_type=jnp.float32)
    gate = gate_up[:, :D_EXPERT]
    up = gate_up[:, D_EXPERT:]
    hid = (gate * jax.nn.sigmoid(gate) * up).astype(jnp.bfloat16)
    ys_ref[...] = _pack_bf16_halves(jnp.dot(hid, wd, preferred_element_type=jnp.float32))


def _experts(xs, tile_expert, n_valid, wg, wu, wd, tm):
    n_slots = xs.shape[0]
    row_block = lambda i, te: (i, 0)
    w_block = lambda i, te: (te[i], 0, 0)
    return pl.pallas_call(
        _experts_kernel,
        grid_spec=pltpu.PrefetchScalarGridSpec(
            num_scalar_prefetch=1,
            grid=(n_valid,),
            in_specs=[pl.BlockSpec((tm, HALF), row_block),
                      pl.BlockSpec((1, D_MODEL, D_EXPERT), w_block, pipeline_mode=pl.Buffered(2)),
                      pl.BlockSpec((1, D_MODEL, D_EXPERT), w_block, pipeline_mode=pl.Buffered(2)),
                      pl.BlockSpec((1, D_EXPERT, D_MODEL), w_block, pipeline_mode=pl.Buffered(2))],
            out_specs=pl.BlockSpec((tm, HALF), row_block),
        ),
        out_shape=jax.ShapeDtypeStruct((n_slots, HALF), jnp.uint32),
        compiler_params=pltpu.CompilerParams(
            dimension_semantics=("arbitrary",), vmem_limit_bytes=VMEM_LIMIT),
        name="experts",
    )(tile_expert, xs, wg, wu, wd)


def _final_kernel(x1_ref, y0_ref, y1_ref, p_ref, gfin_ref, *rest):
    out_ref = rest[-1]
    pt = jnp.transpose(p_ref[...])
    p0 = pt[:, 0:1]
    p1 = pt[:, 1:2]
    a_hi, a_lo = _unpack_bf16_halves(y0_ref[...])
    b_hi, b_lo = _unpack_bf16_halves(y1_ref[...])
    x1 = x1_ref[...]
    x2_hi = x1[:, :HALF] + (p0 * a_hi + p1 * b_hi)
    x2_lo = x1[:, HALF:] + (p0 * a_lo + p1 * b_lo)
    ms = (jnp.sum(x2_hi * x2_hi, axis=-1, keepdims=True)
          + jnp.sum(x2_lo * x2_lo, axis=-1, keepdims=True)) / D_MODEL
    scale = lax.rsqrt(ms + RMS_EPS)
    g = gfin_ref[...]
    out_ref[:, :HALF] = x2_hi * scale * g[:, :HALF]
    out_ref[:, HALF:] = x2_lo * scale * g[:, HALF:]


def _final(x1, yk, p, gfin, row_offset, n_rows, tm, out_rows, out_offset, out_buf=None):
    n_tokens = x1.shape[0]
    first = row_offset // tm
    second = (n_tokens + row_offset) // tm
    out_first = out_offset // tm
    in_specs = [pl.BlockSpec((tm, D_MODEL), lambda i: (first + i, 0)),
                pl.BlockSpec((tm, HALF), lambda i: (first + i, 0)),
                pl.BlockSpec((tm, HALF), lambda i: (second + i, 0)),
                pl.BlockSpec((SUBLANES, tm), lambda i: (0, first + i)),
                pl.BlockSpec((1, D_MODEL), lambda i: (0, 0))]
    args = [x1, yk, yk, p, gfin]
    aliases = {}
    if out_buf is not None:
        in_specs.append(pl.BlockSpec(memory_space=pl.ANY))
        args.append(out_buf)
        aliases = {len(args) - 1: 0}
    return pl.pallas_call(
        _final_kernel,
        grid=(n_rows // tm,),
        in_specs=in_specs,
        out_specs=pl.BlockSpec((tm, D_MODEL), lambda i: (out_first + i, 0)),
        out_shape=jax.ShapeDtypeStruct((out_rows, D_MODEL), jnp.float32),
        input_output_aliases=aliases,
        compiler_params=pltpu.CompilerParams(
            dimension_semantics=("arbitrary",), vmem_limit_bytes=VMEM_LIMIT),
        name="final",
    )(*args)


def _expert_tile(n_tokens):
    mean_rows = TOP_K * n_tokens // N_EXPERTS
    return -(-(mean_rows + 2 * math.isqrt(mean_rows)) // (2 * SUBLANES)) * (2 * SUBLANES)


def _routed_experts(h2p, eid, counts, w_gate, w_up, w_down):
    n_tokens = h2p.shape[0]
    assert n_tokens % (SC_WORKERS * SC_DISPATCH_BATCH) == 0 and n_tokens % PLAN_TILE == 0
    assert (TOP_K * n_tokens) % (SC_WORKERS * SC_BATCH) == 0
    tile = _expert_tile(n_tokens)
    n_tiles = -(-(TOP_K * n_tokens + N_EXPERTS * (tile - 1)) // tile)
    assert n_tiles <= LANES
    pos, table = _plan(eid, counts, tile)
    pos = pos[:TOP_K]
    by_token = jnp.transpose(pos.reshape(TOP_K, SC_WORKERS, -1, SC_DISPATCH_BATCH), (1, 2, 0, 3))
    xs = _sc_dispatch_rows(h2p, by_token, n_tiles * tile)
    ys = _experts(xs, table[0], table[1, 0], w_gate, w_up, w_down, tile)
    return _sc_gather_rows(ys, pos.reshape(SC_WORKERS, -1, SC_BATCH))


def kernel(x_prompt, x_sample, state_conv_a, state_conv_b, g_mix, w_in, conv_a_w, conv_b_w, conv_b_bias,
           ln_g, ln_b, w_out, g_ffn, w_coarse, b_coarse, w_fine, b_fine, w_gate, w_up, w_down, g_final):
    assert g_mix.shape[0] == 1, "single trunk layer"
    batch, seq, _ = x_prompt.shape
    dec_batch, dec_seq, _ = x_sample.shape
    n_prompt = batch * seq
    n_sample = dec_batch * dec_seq
    bf16 = jnp.bfloat16

    wr = jnp.concatenate([
        w_coarse[0], jnp.zeros((D_MODEL, SUBLANES - N_EXPERT_GROUPS), jnp.float32),
        jnp.transpose(w_fine[0], (1, 0, 2)).reshape(D_MODEL, N_EXPERTS),
        jnp.zeros((D_MODEL, LANES - ROUTER_ROWS), jnp.float32)], axis=1)
    wr_hi = wr.astype(bf16)
    wr_lo = (wr - wr_hi.astype(jnp.float32)).astype(bf16)
    wr_both = jnp.concatenate([wr_hi, wr_lo], axis=1)
    br = jnp.concatenate([
        b_coarse[0], jnp.full((SUBLANES - N_EXPERT_GROUPS,), NEG_BIG, jnp.float32),
        b_fine[0].reshape(N_EXPERTS)]).reshape(ROUTER_ROWS, 1)

    params = (g_mix, w_in[0].astype(bf16), conv_a_w, conv_b_w, conv_b_bias, ln_g, ln_b,
              w_out[0].astype(bf16), g_ffn, wr_both, br)

    experts = (w_gate[0], w_up[0], w_down[0])
    gfin = g_final.reshape(1, D_MODEL)

    bufs = _mixer_prompt(x_prompt, params, n_prompt + n_sample, MIXER_TILE, 0, batch)
    na_p, nb_p, counts = bufs[4:]
    x1, h2p, eid, p, na_s, nb_s, counts = _mixer_sample(
        x_sample, jnp.transpose(state_conv_a[0], (1, 0, 2)), jnp.transpose(state_conv_b[0], (1, 0, 2)),
        params, counts, bufs[:4], n_prompt, nseq=32)
    na_s = jnp.transpose(na_s, (1, 0, 2))
    nb_s = jnp.transpose(nb_s, (1, 0, 2))
    yk = _routed_experts(h2p, eid, counts, *experts)
    y_p = _final(x1, yk, p, gfin, 0, n_prompt, FINAL_TILE, n_prompt, 0)
    y_s = _final(x1, yk, p, gfin, n_prompt, n_sample, FINAL_TILE, n_sample, 0)
    return (y_p.reshape(batch, seq, D_MODEL), y_s.reshape(dec_batch, dec_seq, D_MODEL),
            na_p[None], nb_p[None], na_s[None], nb_s[None])
```

```python
import functools
import math

import jax
import jax.numpy as jnp
from jax import lax
from jax.experimental import pallas as pl
from jax.experimental.pallas import tpu as pltpu
from jax.experimental.pallas import tpu_sc as plsc

D_MODEL = 1024
D_A = 512
D_B = 512
CONV_A = 3
CONV_B = 31
HALO_A = CONV_A - 1
HALO_B = CONV_B - 1
IN_COLS = 3 * D_A + 2 * D_B
N_EXPERT_GROUPS = 4
EXPERTS_PER_GROUP = 8
N_EXPERTS = N_EXPERT_GROUPS * EXPERTS_PER_GROUP
TOP_K = 2
D_EXPERT = D_MODEL // 4
RMS_EPS = 1e-6
LN_EPS = 1e-5

SUBLANES = 8
LANES = 128
PAD_A = SUBLANES
PAD_B = 32
ROUTER_ROWS = SUBLANES + N_EXPERTS
NEG_BIG = -1e30
VMEM_LIMIT = 56 * 1024 * 1024
HALF = D_MODEL // 2
HI_MASK = 0xFFFF0000

SC_CORES = 2
SC_SUBCORES = 16
SC_WORKERS = SC_CORES * SC_SUBCORES
SC_BATCH = 64
SC_DISPATCH_BATCH = 32

PLAN_TILE = 1024
FINAL_TILE = 1024
MIXER_TILE = 512
MIXER_ROW_CHUNK = 64


def _rms_scale(x):
    return x * lax.rsqrt(jnp.mean(x * x, axis=-1, keepdims=True) + RMS_EPS)


def _pack_bf16_halves(x):
    bits = lax.bitcast_convert_type(x.astype(jnp.bfloat16).astype(jnp.float32), jnp.uint32)
    return bits[:, :HALF] | (bits[:, HALF:] >> 16)


def _unpack_bf16_halves(w):
    hi = lax.bitcast_convert_type(w & jnp.uint32(HI_MASK), jnp.float32)
    lo = lax.bitcast_convert_type(w << 16, jnp.float32)
    return hi, lo


def _route(logits_t):
    rows = logits_t.shape[1]
    iota = lax.broadcasted_iota(jnp.int32, (SUBLANES, rows), 0)
    lc = logits_t[0:SUBLANES]
    cmax = jnp.max(lc, axis=0, keepdims=True)
    grp = jnp.min(jnp.where(lc == cmax, iota, SUBLANES), axis=0, keepdims=True)
    p_grp = 1.0 / jnp.sum(jnp.exp(lc - cmax), axis=0, keepdims=True)
    sel = logits_t[SUBLANES:2 * SUBLANES]
    for g in range(1, N_EXPERT_GROUPS):
        sel = jnp.where(grp == g, logits_t[(g + 1) * SUBLANES:(g + 2) * SUBLANES], sel)
    v1 = jnp.max(sel, axis=0, keepdims=True)
    i1 = jnp.min(jnp.where(sel == v1, iota, SUBLANES), axis=0, keepdims=True)
    sel2 = jnp.where(iota == i1, -jnp.inf, sel)
    v2 = jnp.max(sel2, axis=0, keepdims=True)
    i2 = jnp.min(jnp.where(sel2 == v2, iota, SUBLANES), axis=0, keepdims=True)
    e2 = jnp.exp(v2 - v1)
    den = 1.0 + e2
    p1 = p_grp / den
    p2 = p_grp * e2 / den
    base = grp * EXPERTS_PER_GROUP
    return (base + i1, base + i2), (p1, p2)


def _after_mix(x1, f0, gffn_ref, wr_ref, br_ref, x1_ref, h2p_ref, eid_ref, p_ref, cnt_ref):
    m = x1.shape[0]
    x1_ref[f0:f0 + m, :] = x1
    h2 = _rms_scale(x1) * gffn_ref[...]
    h2_hi = h2.astype(jnp.bfloat16)
    h2p_ref[f0:f0 + m, :] = _pack_bf16_halves(h2)
    h2_lo = (h2 - h2_hi.astype(jnp.float32)).astype(jnp.bfloat16)
    both = jnp.dot(h2_hi, wr_ref[...], preferred_element_type=jnp.float32)
    cross = jnp.dot(h2_lo, wr_ref[:, 0:LANES], preferred_element_type=jnp.float32)
    logits = both[:, 0:LANES] + both[:, LANES:] + cross
    logits_t = jnp.transpose(logits)[0:ROUTER_ROWS] + br_ref[...]
    (e1, e2), (p1, p2) = _route(logits_t)
    iota = lax.broadcasted_iota(jnp.int32, (SUBLANES, m), 0)
    eid_ref[:, f0:f0 + m] = jnp.where(iota == 0, e1, jnp.where(iota == 1, e2, 0))
    p_ref[:, f0:f0 + m] = jnp.where(iota == 0, p1, jnp.where(iota == 1, p2, 0.0))
    experts = lax.broadcasted_iota(jnp.int32, (N_EXPERTS, m), 0)
    routed = jnp.where((experts == e1) | (experts == e2), 1.0, 0.0)
    cnt_ref[...] = cnt_ref[...] + jnp.sum(routed, axis=1, keepdims=True)


def _mixer_prompt_kernel(x_ref, gmix_ref, win_ref, caw_ref, cbw_ref, cbb_ref, lng_ref, lnb_ref,
                         wout_ref, gffn_ref, wr_ref, br_ref,
                         x1_ref, h2p_ref, eid_ref, p_ref, na_ref, nb_ref, cnt_ref,
                         proj_ref, uext_ref, gext_ref, z_ref, ush_ref, gsh_ref, *, tt, row_chunk):
    t = pl.program_id(1)
    col_chunk = 2 * LANES
    span_b = PAD_B + tt - SUBLANES

    @pl.when((pl.program_id(0) == 0) & (t == 0))
    def _():
        cnt_ref[...] = jnp.zeros_like(cnt_ref)

    @pl.when(t == 0)
    def _():
        uext_ref[0:PAD_A, :] = jnp.zeros((PAD_A, D_A), jnp.float32)
        gext_ref[0:PAD_B, :] = jnp.zeros((PAD_B, D_B), jnp.float32)

    def window(base_ref, shifted_ref, first_shift, off):
        r = off % SUBLANES
        a8 = off - r
        if r == 0:
            return base_ref[a8:a8 + row_chunk, :]
        return shifted_ref[r - first_shift, a8:a8 + row_chunk, :]

    x = x_ref[0]
    h = (_rms_scale(x) * gmix_ref[...]).astype(jnp.bfloat16)
    for c0 in range(0, D_B, col_chunk):
        v_b = jnp.dot(h, win_ref[:, 3 * D_A + c0:3 * D_A + c0 + col_chunk], preferred_element_type=jnp.float32)
        g_b = jnp.dot(h, win_ref[:, 3 * D_A + D_B + c0:3 * D_A + D_B + c0 + col_chunk],
                      preferred_element_type=jnp.float32)
        gext_ref[PAD_B:PAD_B + tt, c0:c0 + col_chunk] = v_b * jax.nn.sigmoid(g_b)
    for r in range(1, SUBLANES):
        gsh_ref[r - 1, 0:span_b, :] = gext_ref[r:r + span_b, :]

    cbw = cbw_ref[...]
    row_starts = list(range(0, tt, row_chunk))
    a_cols = list(range(0, 3 * D_A, col_chunk))
    def conv_b_chunk(r0):
        acc_b = None
        for k in range(CONV_B):
            term = window(gext_ref, gsh_ref, 1, PAD_B - HALO_B + k + r0) * cbw[k:k + 1, :]
            acc_b = term if acc_b is None else acc_b + term
        zb = acc_b + cbb_ref[...]
        mu = jnp.mean(zb, axis=-1, keepdims=True)
        zc = zb - mu
        var = jnp.mean(zc * zc, axis=-1, keepdims=True)
        y = zc * lax.rsqrt(var + LN_EPS) * lng_ref[...] + lnb_ref[...]
        z_ref[r0:r0 + row_chunk, D_A:] = (y * jax.nn.sigmoid(y)).astype(jnp.bfloat16)

    def conv_a_inputs():
        uext_ref[PAD_A:PAD_A + tt, :] = proj_ref[:, D_A:2 * D_A] * proj_ref[:, 2 * D_A:3 * D_A]
        for r in range(SUBLANES - HALO_A, SUBLANES):
            ush_ref[r - (SUBLANES - HALO_A), :, :] = uext_ref[r:r + tt, :]

    def finish(rows):
        caw = caw_ref[...]
        for r0 in rows:
            acc_a = None
            for k in range(CONV_A):
                term = window(uext_ref, ush_ref, SUBLANES - HALO_A, PAD_A - HALO_A + k + r0) * caw[k:k + 1, :]
                acc_a = term if acc_a is None else acc_a + term
            z_ref[r0:r0 + row_chunk, 0:D_A] = (proj_ref[r0:r0 + row_chunk, 0:D_A] * acc_a).astype(jnp.bfloat16)
        lo, hi = rows[0], rows[-1] + row_chunk
        x1 = x_ref[0, lo:hi, :] + jnp.dot(z_ref[lo:hi, :], wout_ref[...], preferred_element_type=jnp.float32)
        _after_mix(x1, lo, gffn_ref, wr_ref, br_ref, x1_ref, h2p_ref, eid_ref, p_ref, cnt_ref)

    first = row_starts[:len(row_starts) // 2]
    second = row_starts[len(row_starts) // 2:]
    for i, r0 in enumerate(first):
        conv_b_chunk(r0)
        for c0 in a_cols[i * len(a_cols) // len(first):(i + 1) * len(a_cols) // len(first)]:
            proj_ref[:, c0:c0 + col_chunk] = jnp.dot(h, win_ref[:, c0:c0 + col_chunk],
                                                     preferred_element_type=jnp.float32)
    conv_a_inputs()
    finish(first)
    for r0 in second:
        conv_b_chunk(r0)
    finish(second)

    @pl.when(t == pl.num_programs(1) - 1)
    def _():
        na_ref[0] = uext_ref[PAD_A + tt - HALO_A:PAD_A + tt, :]
        nb_ref[0] = gext_ref[PAD_B + tt - HALO_B:PAD_B + tt, :]
    uext_ref[0:PAD_A, :] = uext_ref[tt:tt + PAD_A, :]
    gext_ref[0:PAD_B, :] = gext_ref[tt:tt + PAD_B, :]


def _mixer_sample_kernel(x_ref, sa_ref, sb_ref, gmix_ref, win_ref, caw_ref, cbw_ref, cbb_ref, lng_ref, lnb_ref,
                         wout_ref, gffn_ref, wr_ref, br_ref, cnt_in_ref, _x1_in, _h2p_in, _eid_in, _p_in,
                         x1_ref, h2p_ref, eid_ref, p_ref, na_ref, nb_ref, cnt_ref,
                         proj_ref, gnt_ref, unt_ref, gtm_ref, utm_ref, ynt_ref, ant_ref, z_ref, *, nseq, tt):
    rows = nseq * tt
    col_chunk = 2 * LANES

    @pl.when(pl.program_id(0) == 0)
    def _():
        cnt_ref[...] = cnt_in_ref[...]

    x = x_ref[...].reshape(rows, D_MODEL)
    h = (_rms_scale(x) * gmix_ref[...]).astype(jnp.bfloat16)
    for c0 in range(0, D_B, col_chunk):
        v_b = jnp.dot(h, win_ref[:, 3 * D_A + c0:3 * D_A + c0 + col_chunk], preferred_element_type=jnp.float32)
        g_b = jnp.dot(h, win_ref[:, 3 * D_A + D_B + c0:3 * D_A + D_B + c0 + col_chunk],
                      preferred_element_type=jnp.float32)
        g = v_b * jax.nn.sigmoid(g_b)
        for c in range(0, col_chunk, LANES):
            gnt_ref[(c0 + c) // LANES] = g[:, c:c + LANES]
    for c0 in range(0, 3 * D_A, col_chunk):
        proj_ref[:, c0:c0 + col_chunk] = jnp.dot(h, win_ref[:, c0:c0 + col_chunk],
                                                 preferred_element_type=jnp.float32)
    u = proj_ref[:, D_A:2 * D_A] * proj_ref[:, 2 * D_A:3 * D_A]
    for c in range(0, D_A, LANES):
        unt_ref[c // LANES] = u[:, c:c + LANES]

    utm_ref[0:HALO_A] = sa_ref[...]
    gtm_ref[0:HALO_B] = sb_ref[...]
    for t in range(tt):
        for c in range(0, D_A, LANES):
            utm_ref[HALO_A + t, :, c:c + LANES] = unt_ref[c // LANES, pl.ds(t, nseq, stride=tt), :]
        for c in range(0, D_B, LANES):
            gtm_ref[HALO_B + t, :, c:c + LANES] = gnt_ref[c // LANES, pl.ds(t, nseq, stride=tt), :]
    na_ref[...] = utm_ref[tt:tt + HALO_A]
    nb_ref[...] = gtm_ref[tt:tt + HALO_B]

    caw = caw_ref[...]
    cbw = cbw_ref[...]
    for t in range(tt):
        acc_b = None
        for k in range(CONV_B):
            term = gtm_ref[t + k] * cbw[k:k + 1, :]
            acc_b = term if acc_b is None else acc_b + term
        zb = acc_b + cbb_ref[...]
        mu = jnp.mean(zb, axis=-1, keepdims=True)
        zc = zb - mu
        var = jnp.mean(zc * zc, axis=-1, keepdims=True)
        y = zc * lax.rsqrt(var + LN_EPS) * lng_ref[...] + lnb_ref[...]
        y = y * jax.nn.sigmoid(y)
        for c in range(0, D_B, LANES):
            ynt_ref[c // LANES, pl.ds(t, nseq, stride=tt), :] = y[:, c:c + LANES]
        acc_a = None
        for k in range(CONV_A):
            term = utm_ref[t + k] * caw[k:k + 1, :]
            acc_a = term if acc_a is None else acc_a + term
        for c in range(0, D_A, LANES):
            ant_ref[c // LANES, pl.ds(t, nseq, stride=tt), :] = acc_a[:, c:c + LANES]

    for c in range(0, D_A, LANES):
        z_ref[:, c:c + LANES] = (proj_ref[:, c:c + LANES] * ant_ref[c // LANES]).astype(jnp.bfloat16)
    for c in range(0, D_B, LANES):
        z_ref[:, D_A + c:D_A + c + LANES] = ynt_ref[c // LANES].astype(jnp.bfloat16)
    x1 = x + jnp.dot(z_ref[...], wout_ref[...], preferred_element_type=jnp.float32)
    _after_mix(x1, 0, gffn_ref, wr_ref, br_ref, x1_ref, h2p_ref, eid_ref, p_ref, cnt_ref)


def _full(shape):
    return pl.BlockSpec(shape, lambda *_: (0,) * len(shape))


def _mixer_param_specs():
    return [
        _full((1, D_MODEL)),
        _full((D_MODEL, IN_COLS)),
        pl.BlockSpec((None, CONV_A, D_A), lambda *_: (0, 0, 0)),
        pl.BlockSpec((None, CONV_B, D_B), lambda *_: (0, 0, 0)),
        _full((1, D_B)),
        _full((1, D_B)),
        _full((1, D_B)),
        _full((D_MODEL, D_MODEL)),
        _full((1, D_MODEL)),
        _full((D_MODEL, 2 * LANES)),
        _full((ROUTER_ROWS, 1)),
    ]


def _mixer_prompt(x, params, n_tokens_total, tt, seq_first, batch):
    seq = x.shape[1]
    n_t = seq // tt
    tok = lambda b, t: (b * n_t + t, 0)
    lane_tok = lambda b, t: (0, b * n_t + t)
    out_shape = [
        jax.ShapeDtypeStruct((n_tokens_total, D_MODEL), jnp.float32),
        jax.ShapeDtypeStruct((n_tokens_total, HALF), jnp.uint32),
        jax.ShapeDtypeStruct((SUBLANES, n_tokens_total), jnp.int32),
        jax.ShapeDtypeStruct((SUBLANES, n_tokens_total), jnp.float32),
        jax.ShapeDtypeStruct((batch, HALO_A, D_A), jnp.float32),
        jax.ShapeDtypeStruct((batch, HALO_B, D_B), jnp.float32),
        jax.ShapeDtypeStruct((N_EXPERTS, LANES), jnp.float32),
    ]
    out_specs = [
        pl.BlockSpec((tt, D_MODEL), tok),
        pl.BlockSpec((tt, HALF), tok),
        pl.BlockSpec((SUBLANES, tt), lane_tok),
        pl.BlockSpec((SUBLANES, tt), lane_tok),
        pl.BlockSpec((1, HALO_A, D_A), lambda b, t: (b, 0, 0)),
        pl.BlockSpec((1, HALO_B, D_B), lambda b, t: (b, 0, 0)),
        pl.BlockSpec((N_EXPERTS, LANES), lambda b, t: (0, 0)),
    ]
    scratch = [
        pltpu.VMEM((tt, 3 * D_A), jnp.float32),
        pltpu.VMEM((PAD_A + tt, D_A), jnp.float32),
        pltpu.VMEM((PAD_B + tt, D_B), jnp.float32),
        pltpu.VMEM((tt, D_MODEL), jnp.bfloat16),
        pltpu.VMEM((HALO_A, tt, D_A), jnp.float32),
        pltpu.VMEM((SUBLANES - 1, PAD_B + tt - SUBLANES, D_B), jnp.float32),
    ]
    return pl.pallas_call(
        functools.partial(_mixer_prompt_kernel, tt=tt, row_chunk=MIXER_ROW_CHUNK),
        grid=(batch, n_t),
        in_specs=[pl.BlockSpec((1, tt, D_MODEL), lambda b, t: (b + seq_first, t, 0))] + _mixer_param_specs(),
        out_specs=out_specs,
        out_shape=out_shape,
        scratch_shapes=scratch,
        compiler_params=pltpu.CompilerParams(
            dimension_semantics=("arbitrary", "arbitrary"), vmem_limit_bytes=VMEM_LIMIT),
        name="mixer_prompt",
    )(x, *params)


def _mixer_sample(x, state_a, state_b, params, counts, bufs, row_offset, nseq):
    batch, tt, _ = x.shape
    rows = nseq * tt
    first = row_offset // rows
    tok = lambda i: (first + i, 0)
    lane_tok = lambda i: (0, first + i)
    x1, h2p, eid, p = bufs
    out_shape = [
        jax.ShapeDtypeStruct(x1.shape, x1.dtype),
        jax.ShapeDtypeStruct(h2p.shape, h2p.dtype),
        jax.ShapeDtypeStruct(eid.shape, eid.dtype),
        jax.ShapeDtypeStruct(p.shape, p.dtype),
        jax.ShapeDtypeStruct((HALO_A, batch, D_A), jnp.float32),
        jax.ShapeDtypeStruct((HALO_B, batch, D_B), jnp.float32),
        jax.ShapeDtypeStruct((N_EXPERTS, LANES), jnp.float32),
    ]
    out_specs = [
        pl.BlockSpec((rows, D_MODEL), tok),
        pl.BlockSpec((rows, HALF), tok),
        pl.BlockSpec((SUBLANES, rows), lane_tok),
        pl.BlockSpec((SUBLANES, rows), lane_tok),
        pl.BlockSpec((HALO_A, nseq, D_A), lambda i: (0, i, 0)),
        pl.BlockSpec((HALO_B, nseq, D_B), lambda i: (0, i, 0)),
        pl.BlockSpec((N_EXPERTS, LANES), lambda i: (0, 0)),
    ]
    any_spec = pl.BlockSpec(memory_space=pl.ANY)
    in_specs = ([pl.BlockSpec((nseq, tt, D_MODEL), lambda i: (i, 0, 0)),
                 pl.BlockSpec((HALO_A, nseq, D_A), lambda i: (0, i, 0)),
                 pl.BlockSpec((HALO_B, nseq, D_B), lambda i: (0, i, 0))]
                + _mixer_param_specs() + [pl.BlockSpec((N_EXPERTS, LANES), lambda i: (0, 0))] + [any_spec] * 4)
    n_in = len(in_specs)
    scratch = [
        pltpu.VMEM((rows, 3 * D_A), jnp.float32),
        pltpu.VMEM((D_B // LANES, rows, LANES), jnp.float32),
        pltpu.VMEM((D_A // LANES, rows, LANES), jnp.float32),
        pltpu.VMEM((HALO_B + tt, nseq, D_B), jnp.float32),
        pltpu.VMEM((HALO_A + tt, nseq, D_A), jnp.float32),
        pltpu.VMEM((D_B // LANES, rows, LANES), jnp.float32),
        pltpu.VMEM((D_A // LANES, rows, LANES), jnp.float32),
        pltpu.VMEM((rows, D_MODEL), jnp.bfloat16),
    ]
    return pl.pallas_call(
        functools.partial(_mixer_sample_kernel, nseq=nseq, tt=tt),
        grid=(batch // nseq,),
        in_specs=in_specs,
        out_specs=out_specs,
        out_shape=out_shape,
        scratch_shapes=scratch,
        input_output_aliases={n_in - 4: 0, n_in - 3: 1, n_in - 2: 2, n_in - 1: 3},
        compiler_params=pltpu.CompilerParams(
            dimension_semantics=("arbitrary",), vmem_limit_bytes=VMEM_LIMIT),
        name="mixer_sample",
    )(x, state_a, state_b, *params, counts, x1, h2p, eid, p)


def _plan_kernel(eid_ref, cnt_ref, pos_ref, table_ref, carry_ref, start_ref, earlier_ref, *, tile):
    i = pl.program_id(0)
    tt = eid_ref.shape[1]

    @pl.when(i == 0)
    def _():
        tiles = jnp.floor((cnt_ref[...] + (tile - 0.5)) * (1.0 / tile))
        below = (lax.broadcasted_iota(jnp.int32, (N_EXPERTS, N_EXPERTS), 0)
                 > lax.broadcasted_iota(jnp.int32, (N_EXPERTS, N_EXPERTS), 1))
        start = jnp.dot(jnp.where(below, 1.0, 0.0).astype(jnp.bfloat16), tiles.astype(jnp.bfloat16),
                        preferred_element_type=jnp.float32) * tile
        start_ref[...] = start
        seg_end = start + tiles * tile
        tile_row = lax.broadcasted_iota(jnp.int32, (N_EXPERTS, LANES), 1).astype(jnp.float32) * tile
        owner = jnp.sum(jnp.where(seg_end <= tile_row, 1.0, 0.0), axis=0, keepdims=True)
        owner = jnp.minimum(owner, N_EXPERTS - 1.0)
        n_used = jnp.sum(tiles, axis=0, keepdims=True)
        row = lax.broadcasted_iota(jnp.int32, (SUBLANES, LANES), 0)
        table_ref[...] = jnp.where(row == 0, owner, jnp.where(row == 1, n_used, 0.0)).astype(jnp.int32)
        earlier = (lax.broadcasted_iota(jnp.int32, (tt, tt), 0)
                   < lax.broadcasted_iota(jnp.int32, (tt, tt), 1))
        earlier_ref[...] = jnp.where(earlier, 1.0, 0.0).astype(jnp.bfloat16)
        carry_ref[...] = jnp.zeros_like(carry_ref)

    eid = eid_ref[...]
    experts = lax.broadcasted_iota(jnp.int32, (N_EXPERTS, tt), 0)
    oh0 = experts == eid[0:1]
    oh1 = experts == eid[1:2]
    oh = jnp.where(oh0 | oh1, 1.0, 0.0)
    within = jnp.dot(oh.astype(jnp.bfloat16), earlier_ref[...], preferred_element_type=jnp.float32)
    slot_of = within + carry_ref[:, 0:1] + start_ref[:, 0:1]
    s0 = jnp.sum(jnp.where(oh0, slot_of, 0.0), axis=0, keepdims=True)
    s1 = jnp.sum(jnp.where(oh1, slot_of, 0.0), axis=0, keepdims=True)
    k = lax.broadcasted_iota(jnp.int32, (SUBLANES, tt), 0)
    pos_ref[...] = jnp.where(k == 0, s0, jnp.where(k == 1, s1, 0.0)).astype(jnp.int32)
    carry_ref[...] = carry_ref[...] + jnp.sum(oh, axis=1, keepdims=True)


def _plan(eid, counts, tile):
    n_tokens = eid.shape[1]
    return pl.pallas_call(
        functools.partial(_plan_kernel, tile=tile),
        grid=(n_tokens // PLAN_TILE,),
        in_specs=[pl.BlockSpec((SUBLANES, PLAN_TILE), lambda i: (0, i)),
                  pl.BlockSpec((N_EXPERTS, LANES), lambda i: (0, 0))],
        out_specs=[pl.BlockSpec((SUBLANES, PLAN_TILE), lambda i: (0, i)),
                   pl.BlockSpec((SUBLANES, LANES), lambda i: (0, 0))],
        out_shape=[jax.ShapeDtypeStruct((SUBLANES, n_tokens), jnp.int32),
                   jax.ShapeDtypeStruct((SUBLANES, LANES), jnp.int32)],
        scratch_shapes=[pltpu.VMEM((N_EXPERTS, LANES), jnp.float32),
                        pltpu.VMEM((N_EXPERTS, LANES), jnp.float32),
                        pltpu.VMEM((PLAN_TILE, PLAN_TILE), jnp.bfloat16)],
        compiler_params=pltpu.CompilerParams(dimension_semantics=("arbitrary",)),
        name="route_plan",
    )(eid, counts)


def _sc_mesh():
    return plsc.VectorSubcoreMesh(core_axis_name="c", subcore_axis_name="s")


def _sc_dispatch_rows(table, pos, n_out):
    n_workers, n_batches, top_k, batch = pos.shape
    n_rows, words = table.shape
    assert n_workers * n_batches * batch == n_rows and n_batches >= 2

    @functools.partial(
        pl.kernel, mesh=_sc_mesh(),
        out_type=jax.ShapeDtypeStruct((n_out, words), table.dtype),
        scratch_types=[pltpu.VMEM((n_batches, top_k, batch), jnp.int32),
                       pltpu.VMEM((2, batch, words), table.dtype),
                       pltpu.SemaphoreType.DMA((2,)),
                       pltpu.SemaphoreType.DMA((2, top_k))],
    )
    def dispatch(table_hbm, pos_hbm, out_hbm, idx_v, rows_v, sem_in, sem_out):
        worker = lax.axis_index("s") * SC_CORES + lax.axis_index("c")
        pltpu.sync_copy(pos_hbm.at[worker], idx_v)

        def read(b):
            src = table_hbm.at[pl.ds((worker * n_batches + b) * batch, batch)]
            return pltpu.async_copy(src, rows_v.at[b % 2], sem_in.at[b % 2])

        def write(b):
            return [pltpu.async_copy(rows_v.at[b % 2], out_hbm.at[idx_v.at[b, k]], sem_out.at[b % 2, k])
                    for k in range(top_k)]

        reads = {0: read(0)}
        writes = {}
        for b in range(n_batches):
            reads[b].wait()
            if b + 1 < n_batches:
                if b >= 1:
                    for w in writes[b - 1]:
                        w.wait()
                reads[b + 1] = read(b + 1)
            writes[b] = write(b)
        for b in (n_batches - 2, n_batches - 1):
            for w in writes[b]:
                w.wait()

    return dispatch(table, pos)


def _sc_gather_rows(table, idx):
    n_workers, n_batches, batch = idx.shape
    words = table.shape[1]
    assert n_batches >= 2

    @functools.partial(
        pl.kernel, mesh=_sc_mesh(),
        out_type=jax.ShapeDtypeStruct((n_workers * n_batches * batch, words), table.dtype),
        scratch_types=[pltpu.VMEM((n_batches, batch), jnp.int32),
                       pltpu.VMEM((2, batch, words), table.dtype),
                       pltpu.SemaphoreType.DMA((2,)),
                       pltpu.SemaphoreType.DMA((2,))],
    )
    def gather(table_hbm, idx_hbm, out_hbm, idx_v, rows_v, sem_in, sem_out):
        worker = lax.axis_index("s") * SC_CORES + lax.axis_index("c")
        pltpu.sync_copy(idx_hbm.at[worker], idx_v)

        def read(b):
            return pltpu.async_copy(table_hbm.at[idx_v.at[b]], rows_v.at[b % 2], sem_in.at[b % 2])

        def write(b):
            dst = out_hbm.at[pl.ds((worker * n_batches + b) * batch, batch)]
            return pltpu.async_copy(rows_v.at[b % 2], dst, sem_out.at[b % 2])

        reads = {0: read(0)}
        writes = {}
        for b in range(n_batches):
            reads[b].wait()
            if b + 1 < n_batches:
                if b >= 1:
                    writes[b - 1].wait()
                reads[b + 1] = read(b + 1)
            writes[b] = write(b)
        writes[n_batches - 2].wait()
        writes[n_batches - 1].wait()

    return gather(table, idx)


def _experts_kernel(te_ref, xs_ref, wg_ref, wu_ref, wd_ref, ys_ref):
    hi, lo = _unpack_bf16_halves(xs_ref[...])
    x = jnp.concatenate([hi.astype(jnp.bfloat16), lo.astype(jnp.bfloat16)], axis=1)
    w_gate_up = jnp.concatenate([wg_ref[0].astype(jnp.bfloat16), wu_ref[0].astype(jnp.bfloat16)], axis=1)
    wd = wd_ref[0].astype(jnp.bfloat16)
    gate_up = jnp.dot(x, w_gate_up, preferred_element_type=jnp.float32)
    gate = gate_up[:, :D_EXPERT]
    up = gate_up[:, D_EXPERT:]
    hid = (gate * jax.nn.sigmoid(gate) * up).astype(jnp.bfloat16)
    ys_ref[...] = _pack_bf16_halves(jnp.dot(hid, wd, preferred_element_type=jnp.float32))


def _experts(xs, tile_expert, n_valid, wg, wu, wd, tm):
    n_slots = xs.shape[0]
    row_block = lambda i, te: (i, 0)
    w_block = lambda i, te: (te[i], 0, 0)
    return pl.pallas_call(
        _experts_kernel,
        grid_spec=pltpu.PrefetchScalarGridSpec(
            num_scalar_prefetch=1,
            grid=(n_valid,),
            in_specs=[pl.BlockSpec((tm, HALF), row_block),
                      pl.BlockSpec((1, D_MODEL, D_EXPERT), w_block, pipeline_mode=pl.Buffered(2)),
                      pl.BlockSpec((1, D_MODEL, D_EXPERT), w_block, pipeline_mode=pl.Buffered(2)),
                      pl.BlockSpec((1, D_EXPERT, D_MODEL), w_block, pipeline_mode=pl.Buffered(2))],
            out_specs=pl.BlockSpec((tm, HALF), row_block),
        ),
        out_shape=jax.ShapeDtypeStruct((n_slots, HALF), jnp.uint32),
        compiler_params=pltpu.CompilerParams(
            dimension_semantics=("arbitrary",), vmem_limit_bytes=VMEM_LIMIT),
        name="experts",
    )(tile_expert, xs, wg, wu, wd)


def _final_kernel(x1_ref, y0_ref, y1_ref, p_ref, gfin_ref, *rest):
    out_ref = rest[-1]
    pt = jnp.transpose(p_ref[...])
    p0 = pt[:, 0:1]
    p1 = pt[:, 1:2]
    a_hi, a_lo = _unpack_bf16_halves(y0_ref[...])
    b_hi, b_lo = _unpack_bf16_halves(y1_ref[...])
    x1 = x1_ref[...]
    x2_hi = x1[:, :HALF] + (p0 * a_hi + p1 * b_hi)
    x2_lo = x1[:, HALF:] + (p0 * a_lo + p1 * b_lo)
    ms = (jnp.sum(x2_hi * x2_hi, axis=-1, keepdims=True)
          + jnp.sum(x2_lo * x2_lo, axis=-1, keepdims=True)) / D_MODEL
    scale = lax.rsqrt(ms + RMS_EPS)
    g = gfin_ref[...]
    out_ref[:, :HALF] = x2_hi * scale * g[:, :HALF]
    out_ref[:, HALF:] = x2_lo * scale * g[:, HALF:]


def _final(x1, yk, p, gfin, row_offset, n_rows, tm, out_rows, out_offset, out_buf=None):
    n_tokens = x1.shape[0]
    first = row_offset // tm
    second = (n_tokens + row_offset) // tm
    out_first = out_offset // tm
    in_specs = [pl.BlockSpec((tm, D_MODEL), lambda i: (first + i, 0)),
                pl.BlockSpec((tm, HALF), lambda i: (first + i, 0)),
                pl.BlockSpec((tm, HALF), lambda i: (second + i, 0)),
                pl.BlockSpec((SUBLANES, tm), lambda i: (0, first + i)),
                pl.BlockSpec((1, D_MODEL), lambda i: (0, 0))]
    args = [x1, yk, yk, p, gfin]
    aliases = {}
    if out_buf is not None:
        in_specs.append(pl.BlockSpec(memory_space=pl.ANY))
        args.append(out_buf)
        aliases = {len(args) - 1: 0}
    return pl.pallas_call(
        _final_kernel,
        grid=(n_rows // tm,),
        in_specs=in_specs,
        out_specs=pl.BlockSpec((tm, D_MODEL), lambda i: (out_first + i, 0)),
        out_shape=jax.ShapeDtypeStruct((out_rows, D_MODEL), jnp.float32),
        input_output_aliases=aliases,
        compiler_params=pltpu.CompilerParams(
            dimension_semantics=("arbitrary",), vmem_limit_bytes=VMEM_LIMIT),
        name="final",
    )(*args)


def _expert_tile(n_tokens):
    mean_rows = TOP_K * n_tokens // N_EXPERTS
    return -(-(mean_rows + 2 * math.isqrt(mean_rows)) // (2 * SUBLANES)) * (2 * SUBLANES)


def _routed_experts(h2p, eid, counts, w_gate, w_up, w_down):
    n_tokens = h2p.shape[0]
    assert n_tokens % (SC_WORKERS * SC_DISPATCH_BATCH) == 0 and n_tokens % PLAN_TILE == 0
    assert (TOP_K * n_tokens) % (SC_WORKERS * SC_BATCH) == 0
    tile = _expert_tile(n_tokens)
    n_tiles = -(-(TOP_K * n_tokens + N_EXPERTS * (tile - 1)) // tile)
    assert n_tiles <= LANES
    pos, table = _plan(eid, counts, tile)
    pos = pos[:TOP_K]
    by_token = jnp.transpose(pos.reshape(TOP_K, SC_WORKERS, -1, SC_DISPATCH_BATCH), (1, 2, 0, 3))
    xs = _sc_dispatch_rows(h2p, by_token, n_tiles * tile)
    ys = _experts(xs, table[0], table[1, 0], w_gate, w_up, w_down, tile)
    return _sc_gather_rows(ys, pos.reshape(SC_WORKERS, -1, SC_BATCH))


def kernel(x_prompt, x_sample, state_conv_a, state_conv_b, g_mix, w_in, conv_a_w, conv_b_w, conv_b_bias,
           ln_g, ln_b, w_out, g_ffn, w_coarse, b_coarse, w_fine, b_fine, w_gate, w_up, w_down, g_final):
    assert g_mix.shape[0] == 1, "single trunk layer"
    batch, seq, _ = x_prompt.shape
    dec_batch, dec_seq, _ = x_sample.shape
    n_prompt = batch * seq
    n_sample = dec_batch * dec_seq
    bf16 = jnp.bfloat16

    wr = jnp.concatenate([
        w_coarse[0], jnp.zeros((D_MODEL, SUBLANES - N_EXPERT_GROUPS), jnp.float32),
        jnp.transpose(w_fine[0], (1, 0, 2)).reshape(D_MODEL, N_EXPERTS),
        jnp.zeros((D_MODEL, LANES - ROUTER_ROWS), jnp.float32)], axis=1)
    wr_hi = wr.astype(bf16)
    wr_lo = (wr - wr_hi.astype(jnp.float32)).astype(bf16)
    wr_both = jnp.concatenate([wr_hi, wr_lo], axis=1)
    br = jnp.concatenate([
        b_coarse[0], jnp.full((SUBLANES - N_EXPERT_GROUPS,), NEG_BIG, jnp.float32),
        b_fine[0].reshape(N_EXPERTS)]).reshape(ROUTER_ROWS, 1)

    params = (g_mix, w_in[0].astype(bf16), conv_a_w, conv_b_w, conv_b_bias, ln_g, ln_b,
              w_out[0].astype(bf16), g_ffn, wr_both, br)

    experts = (w_gate[0], w_up[0], w_down[0])
    gfin = g_final.reshape(1, D_MODEL)

    bufs = _mixer_prompt(x_prompt, params, n_prompt + n_sample, MIXER_TILE, 0, batch)
    na_p, nb_p, counts = bufs[4:]
    x1, h2p, eid, p, na_s, nb_s, counts = _mixer_sample(
        x_sample, jnp.transpose(state_conv_a[0], (1, 0, 2)), jnp.transpose(state_conv_b[0], (1, 0, 2)),
        params, counts, bufs[:4], n_prompt, nseq=32)
    na_s = jnp.transpose(na_s, (1, 0, 2))
    nb_s = jnp.transpose(nb_s, (1, 0, 2))
    yk = _routed_experts(h2p, eid, counts, *experts)
    y_p = _final(x1, yk, p, gfin, 0, n_prompt, FINAL_TILE, n_prompt, 0)
    y_s = _final(x1, yk, p, gfin, n_prompt, n_sample, FINAL_TILE, n_sample, 0)
    return (y_p.reshape(batch, seq, D_MODEL), y_s.reshape(dec_batch, dec_seq, D_MODEL),
            na_p[None], nb_p[None], na_s[None], nb_s[None])
```

```python
import functools
import math

import jax
import jax.numpy as jnp
from jax import lax
from jax.experimental import pallas as pl
from jax.experimental.pallas import tpu as pltpu
from jax.experimental.pallas import tpu_sc as plsc

D_MODEL = 1024
D_A = 512
D_B = 512
CONV_A = 3
CONV_B = 31
HALO_A = CONV_A - 1
HALO_B = CONV_B - 1
IN_COLS = 3 * D_A + 2 * D_B
N_EXPERT_GROUPS = 4
EXPERTS_PER_GROUP = 8
N_EXPERTS = N_EXPERT_GROUPS * EXPERTS_PER_GROUP
TOP_K = 2
D_EXPERT = D_MODEL // 4
RMS_EPS = 1e-6
LN_EPS = 1e-5

SUBLANES = 8
LANES = 128
PAD_A = SUBLANES
PAD_B = 32
ROUTER_ROWS = SUBLANES + N_EXPERTS
NEG_BIG = -1e30
VMEM_LIMIT = 56 * 1024 * 1024
HALF = D_MODEL // 2
HI_MASK = 0xFFFF0000

SC_CORES = 2
SC_SUBCORES = 16
SC_WORKERS = SC_CORES * SC_SUBCORES
SC_BATCH = 64
SC_DISPATCH_BATCH = 32

PLAN_TILE = 1024
FINAL_TILE = 1024
MIXER_TILE = 512
MIXER_TILES_PER_STEP = 2
MIXER_ROW_CHUNK = 64
MIXER_FINISH_GROUPS = (4, 4)


def _rms_scale(x):
    return x * lax.rsqrt(jnp.mean(x * x, axis=-1, keepdims=True) + RMS_EPS)


def _pack_bf16_halves(x):
    bits = lax.bitcast_convert_type(x.astype(jnp.bfloat16).astype(jnp.float32), jnp.uint32)
    return bits[:, :HALF] | (bits[:, HALF:] >> 16)


def _unpack_bf16_halves(w):
    hi = lax.bitcast_convert_type(w & jnp.uint32(HI_MASK), jnp.float32)
    lo = lax.bitcast_convert_type(w << 16, jnp.float32)
    return hi, lo


def _route(logits_t):
    rows = logits_t.shape[1]
    iota = lax.broadcasted_iota(jnp.int32, (SUBLANES, rows), 0)
    lc = logits_t[0:SUBLANES]
    cmax = jnp.max(lc, axis=0, keepdims=True)
    grp = jnp.min(jnp.where(lc == cmax, iota, SUBLANES), axis=0, keepdims=True)
    p_grp = 1.0 / jnp.sum(jnp.exp(lc - cmax), axis=0, keepdims=True)
    sel = logits_t[SUBLANES:2 * SUBLANES]
    for g in range(1, N_EXPERT_GROUPS):
        sel = jnp.where(grp == g, logits_t[(g + 1) * SUBLANES:(g + 2) * SUBLANES], sel)
    v1 = jnp.max(sel, axis=0, keepdims=True)
    i1 = jnp.min(jnp.where(sel == v1, iota, SUBLANES), axis=0, keepdims=True)
    sel2 = jnp.where(iota == i1, -jnp.inf, sel)
    v2 = jnp.max(sel2, axis=0, keepdims=True)
    i2 = jnp.min(jnp.where(sel2 == v2, iota, SUBLANES), axis=0, keepdims=True)
    e2 = jnp.exp(v2 - v1)
    den = 1.0 + e2
    p1 = p_grp / den
    p2 = p_grp * e2 / den
    base = grp * EXPERTS_PER_GROUP
    return (base + i1, base + i2), (p1, p2)


def _after_mix(x1, f0, gffn_ref, wr_ref, br_ref, x1_ref, h2p_ref, eid_ref, p_ref, cnt_ref):
    m = x1.shape[0]
    x1_ref[f0:f0 + m, :] = x1
    h2 = _rms_scale(x1) * gffn_ref[...]
    h2_hi = h2.astype(jnp.bfloat16)
    h2p_ref[f0:f0 + m, :] = _pack_bf16_halves(h2)
    h2_lo = (h2 - h2_hi.astype(jnp.float32)).astype(jnp.bfloat16)
    both = jnp.dot(h2_hi, wr_ref[...], preferred_element_type=jnp.float32)
    cross = jnp.dot(h2_lo, wr_ref[:, 0:LANES], preferred_element_type=jnp.float32)
    logits = both[:, 0:LANES] + both[:, LANES:] + cross
    logits_t = jnp.transpose(logits)[0:ROUTER_ROWS] + br_ref[...]
    (e1, e2), (p1, p2) = _route(logits_t)
    iota = lax.broadcasted_iota(jnp.int32, (SUBLANES, m), 0)
    eid_ref[:, f0:f0 + m] = jnp.where(iota == 0, e1, jnp.where(iota == 1, e2, 0))
    p_ref[:, f0:f0 + m] = jnp.where(iota == 0, p1, jnp.where(iota == 1, p2, 0.0))
    experts = lax.broadcasted_iota(jnp.int32, (N_EXPERTS, m), 0)
    routed = jnp.where((experts == e1) | (experts == e2), 1.0, 0.0)
    cnt_ref[...] = cnt_ref[...] + jnp.sum(routed, axis=1, keepdims=True)


def _mixer_prompt_kernel(x_ref, gmix_ref, win_ref, caw_ref, cbw_ref, cbb_ref, lng_ref, lnb_ref,
                         wout_ref, gffn_ref, wr_ref, br_ref,
                         x1_ref, h2p_ref, eid_ref, p_ref, na_ref, nb_ref, cnt_ref,
                         proj_ref, uext_ref, gext_ref, z_ref, ush_ref, gsh_ref, *, tt, tiles, row_chunk):
    t = pl.program_id(1)
    col_chunk = 2 * LANES
    span_b = PAD_B + tt - SUBLANES
    step_rows = tiles * tt

    @pl.when((pl.program_id(0) == 0) & (t == 0))
    def _():
        cnt_ref[...] = jnp.zeros_like(cnt_ref)

    @pl.when(t == 0)
    def _():
        uext_ref[0:PAD_A, :] = jnp.zeros((PAD_A, D_A), jnp.float32)
        gext_ref[0:PAD_B, :] = jnp.zeros((PAD_B, D_B), jnp.float32)

    def window(base_ref, shifted_ref, first_shift, base, off):
        r = off % SUBLANES
        a8 = off - r
        if r == 0:
            return base_ref[base + a8:base + a8 + row_chunk, :]
        return shifted_ref[r - first_shift, a8:a8 + row_chunk, :]

    cbw = cbw_ref[...]
    row_starts = list(range(0, tt, row_chunk))
    a_cols = list(range(0, 3 * D_A, col_chunk))
    groups, done = [], 0
    for size in MIXER_FINISH_GROUPS:
        groups.append(row_starts[done:done + size])
        done += size
    assert done == len(row_starts)

    def head(base):
        h = (_rms_scale(x_ref[0, base:base + tt, :]) * gmix_ref[...]).astype(jnp.bfloat16)
        for c0 in range(0, D_B, col_chunk):
            v_b = jnp.dot(h, win_ref[:, 3 * D_A + c0:3 * D_A + c0 + col_chunk], preferred_element_type=jnp.float32)
            g_b = jnp.dot(h, win_ref[:, 3 * D_A + D_B + c0:3 * D_A + D_B + c0 + col_chunk],
                          preferred_element_type=jnp.float32)
            gext_ref[PAD_B + base:PAD_B + base + tt, c0:c0 + col_chunk] = v_b * jax.nn.sigmoid(g_b)
        return h

    def body(base, h):
        for r in range(1, SUBLANES):
            gsh_ref[r - 1, 0:span_b, :] = gext_ref[base + r:base + r + span_b, :]

        def conv_b_chunk(r0):
            acc_b = None
            for k in range(CONV_B):
                term = window(gext_ref, gsh_ref, 1, base, PAD_B - HALO_B + k + r0) * cbw[k:k + 1, :]
                acc_b = term if acc_b is None else acc_b + term
            zb = acc_b + cbb_ref[...]
            mu = jnp.mean(zb, axis=-1, keepdims=True)
            zc = zb - mu
            var = jnp.mean(zc * zc, axis=-1, keepdims=True)
            y = zc * lax.rsqrt(var + LN_EPS) * lng_ref[...] + lnb_ref[...]
            z_ref[base + r0:base + r0 + row_chunk, D_A:] = (y * jax.nn.sigmoid(y)).astype(jnp.bfloat16)

        def conv_a():
            uext_ref[PAD_A + base:PAD_A + base + tt, :] = proj_ref[:, D_A:2 * D_A] * proj_ref[:, 2 * D_A:3 * D_A]
            for r in range(SUBLANES - HALO_A, SUBLANES):
                ush_ref[r - (SUBLANES - HALO_A), :, :] = uext_ref[base + r:base + r + tt, :]
            caw = caw_ref[...]
            for r0 in row_starts:
                acc_a = None
                for k in range(CONV_A):
                    term = window(uext_ref, ush_ref, SUBLANES - HALO_A, base, PAD_A - HALO_A + k + r0) * caw[k:k + 1, :]
                    acc_a = term if acc_a is None else acc_a + term
                z_ref[base + r0:base + r0 + row_chunk, 0:D_A] = (
                    proj_ref[r0:r0 + row_chunk, 0:D_A] * acc_a).astype(jnp.bfloat16)

        def out_a(rows):
            lo, hi = base + rows[0], base + rows[-1] + row_chunk
            return jnp.dot(z_ref[lo:hi, 0:D_A], wout_ref[0:D_A, :], preferred_element_type=jnp.float32)

        def finish(rows, part_a):
            lo, hi = base + rows[0], base + rows[-1] + row_chunk
            part_b = jnp.dot(z_ref[lo:hi, D_A:], wout_ref[D_A:, :], preferred_element_type=jnp.float32)
            _after_mix(x_ref[0, lo:hi, :] + part_a + part_b, lo,
                       gffn_ref, wr_ref, br_ref, x1_ref, h2p_ref, eid_ref, p_ref, cnt_ref)

        parts_a = []
        for gi, rows in enumerate(groups):
            for i, r0 in enumerate(rows):
                conv_b_chunk(r0)
                if gi == 0:
                    for c0 in a_cols[i * len(a_cols) // len(rows):(i + 1) * len(a_cols) // len(rows)]:
                        proj_ref[:, c0:c0 + col_chunk] = jnp.dot(h, win_ref[:, c0:c0 + col_chunk],
                                                                 preferred_element_type=jnp.float32)
            if gi == 0:
                conv_a()
                parts_a = [out_a(g) for g in groups]
            finish(rows, parts_a[gi])

    heads = [head(ti * tt) for ti in range(tiles)]
    for ti in range(tiles):
        body(ti * tt, heads[ti])

    @pl.when(t == pl.num_programs(1) - 1)
    def _():
        na_ref[0] = uext_ref[PAD_A + step_rows - HALO_A:PAD_A + step_rows, :]
        nb_ref[0] = gext_ref[PAD_B + step_rows - HALO_B:PAD_B + step_rows, :]
    uext_ref[0:PAD_A, :] = uext_ref[step_rows:step_rows + PAD_A, :]
    gext_ref[0:PAD_B, :] = gext_ref[step_rows:step_rows + PAD_B, :]


def _mixer_sample_kernel(x_ref, sa_ref, sb_ref, gmix_ref, win_ref, caw_ref, cbw_ref, cbb_ref, lng_ref, lnb_ref,
                         wout_ref, gffn_ref, wr_ref, br_ref, cnt_in_ref, _x1_in, _h2p_in, _eid_in, _p_in,
                         x1_ref, h2p_ref, eid_ref, p_ref, na_ref, nb_ref, cnt_ref,
                         proj_ref, gnt_ref, unt_ref, gtm_ref, utm_ref, ynt_ref, ant_ref, z_ref, *, nseq, tt):
    rows = nseq * tt
    col_chunk = 2 * LANES

    @pl.when(pl.program_id(0) == 0)
    def _():
        cnt_ref[...] = cnt_in_ref[...]

    x = x_ref[...].reshape(rows, D_MODEL)
    h = (_rms_scale(x) * gmix_ref[...]).astype(jnp.bfloat16)
    for c0 in range(0, D_B, col_chunk):
        v_b = jnp.dot(h, win_ref[:, 3 * D_A + c0:3 * D_A + c0 + col_chunk], preferred_element_type=jnp.float32)
        g_b = jnp.dot(h, win_ref[:, 3 * D_A + D_B + c0:3 * D_A + D_B + c0 + col_chunk],
                      preferred_element_type=jnp.float32)
        g = v_b * jax.nn.sigmoid(g_b)
        for c in range(0, col_chunk, LANES):
            gnt_ref[(c0 + c) // LANES] = g[:, c:c + LANES]
    for c0 in range(0, 3 * D_A, col_chunk):
        proj_ref[:, c0:c0 + col_chunk] = jnp.dot(h, win_ref[:, c0:c0 + col_chunk],
                                                 preferred_element_type=jnp.float32)
    u = proj_ref[:, D_A:2 * D_A] * proj_ref[:, 2 * D_A:3 * D_A]
    for c in range(0, D_A, LANES):
        unt_ref[c // LANES] = u[:, c:c + LANES]

    utm_ref[0:HALO_A] = sa_ref[...]
    gtm_ref[0:HALO_B] = sb_ref[...]
    for t in range(tt):
        for c in range(0, D_A, LANES):
            utm_ref[HALO_A + t, :, c:c + LANES] = unt_ref[c // LANES, pl.ds(t, nseq, stride=tt), :]
        for c in range(0, D_B, LANES):
            gtm_ref[HALO_B + t, :, c:c + LANES] = gnt_ref[c // LANES, pl.ds(t, nseq, stride=tt), :]
    na_ref[...] = utm_ref[tt:tt + HALO_A]
    nb_ref[...] = gtm_ref[tt:tt + HALO_B]

    caw = caw_ref[...]
    cbw = cbw_ref[...]
    for t in range(tt):
        acc_b = None
        for k in range(CONV_B):
            term = gtm_ref[t + k] * cbw[k:k + 1, :]
            acc_b = term if acc_b is None else acc_b + term
        zb = acc_b + cbb_ref[...]
        mu = jnp.mean(zb, axis=-1, keepdims=True)
        zc = zb - mu
        var = jnp.mean(zc * zc, axis=-1, keepdims=True)
        y = zc * lax.rsqrt(var + LN_EPS) * lng_ref[...] + lnb_ref[...]
        y = y * jax.nn.sigmoid(y)
        for c in range(0, D_B, LANES):
            ynt_ref[c // LANES, pl.ds(t, nseq, stride=tt), :] = y[:, c:c + LANES]
        acc_a = None
        for k in range(CONV_A):
            term = utm_ref[t + k] * caw[k:k + 1, :]
            acc_a = term if acc_a is None else acc_a + term
        for c in range(0, D_A, LANES):
            ant_ref[c // LANES, pl.ds(t, nseq, stride=tt), :] = acc_a[:, c:c + LANES]

    for c in range(0, D_A, LANES):
        z_ref[:, c:c + LANES] = (proj_ref[:, c:c + LANES] * ant_ref[c // LANES]).astype(jnp.bfloat16)
    for c in range(0, D_B, LANES):
        z_ref[:, D_A + c:D_A + c + LANES] = ynt_ref[c // LANES].astype(jnp.bfloat16)
    x1 = x + jnp.dot(z_ref[...], wout_ref[...], preferred_element_type=jnp.float32)
    _after_mix(x1, 0, gffn_ref, wr_ref, br_ref, x1_ref, h2p_ref, eid_ref, p_ref, cnt_ref)


def _full(shape):
    return pl.BlockSpec(shape, lambda *_: (0,) * len(shape))


def _mixer_param_specs():
    return [
        _full((1, D_MODEL)),
        _full((D_MODEL, IN_COLS)),
        pl.BlockSpec((None, CONV_A, D_A), lambda *_: (0, 0, 0)),
        pl.BlockSpec((None, CONV_B, D_B), lambda *_: (0, 0, 0)),
        _full((1, D_B)),
        _full((1, D_B)),
        _full((1, D_B)),
        _full((D_MODEL, D_MODEL)),
        _full((1, D_MODEL)),
        _full((D_MODEL, 2 * LANES)),
        _full((ROUTER_ROWS, 1)),
    ]


def _mixer_prompt(x, params, n_tokens_total, tt, tiles, seq_first, batch):
    seq = x.shape[1]
    rows = tiles * tt
    n_t = seq // rows
    tok = lambda b, t: (b * n_t + t, 0)
    lane_tok = lambda b, t: (0, b * n_t + t)
    out_shape = [
        jax.ShapeDtypeStruct((n_tokens_total, D_MODEL), jnp.float32),
        jax.ShapeDtypeStruct((n_tokens_total, HALF), jnp.uint32),
        jax.ShapeDtypeStruct((SUBLANES, n_tokens_total), jnp.int32),
        jax.ShapeDtypeStruct((SUBLANES, n_tokens_total), jnp.float32),
        jax.ShapeDtypeStruct((batch, HALO_A, D_A), jnp.float32),
        jax.ShapeDtypeStruct((batch, HALO_B, D_B), jnp.float32),
        jax.ShapeDtypeStruct((N_EXPERTS, LANES), jnp.float32),
    ]
    out_specs = [
        pl.BlockSpec((rows, D_MODEL), tok),
        pl.BlockSpec((rows, HALF), tok),
        pl.BlockSpec((SUBLANES, rows), lane_tok),
        pl.BlockSpec((SUBLANES, rows), lane_tok),
        pl.BlockSpec((1, HALO_A, D_A), lambda b, t: (b, 0, 0)),
        pl.BlockSpec((1, HALO_B, D_B), lambda b, t: (b, 0, 0)),
        pl.BlockSpec((N_EXPERTS, LANES), lambda b, t: (0, 0)),
    ]
    scratch = [
        pltpu.VMEM((tt, 3 * D_A), jnp.float32),
        pltpu.VMEM((PAD_A + rows, D_A), jnp.float32),
        pltpu.VMEM((PAD_B + rows, D_B), jnp.float32),
        pltpu.VMEM((rows, D_MODEL), jnp.bfloat16),
        pltpu.VMEM((HALO_A, tt, D_A), jnp.float32),
        pltpu.VMEM((SUBLANES - 1, PAD_B + tt - SUBLANES, D_B), jnp.float32),
    ]
    params_specs = [pl.BlockSpec(sp.block_shape, sp.index_map, pipeline_mode=pl.Buffered(1))
                    for sp in _mixer_param_specs()]
    return pl.pallas_call(
        functools.partial(_mixer_prompt_kernel, tt=tt, tiles=tiles, row_chunk=MIXER_ROW_CHUNK),
        grid=(batch, n_t),
        in_specs=[pl.BlockSpec((1, rows, D_MODEL), lambda b, t: (b + seq_first, t, 0))] + params_specs,
        out_specs=out_specs,
        out_shape=out_shape,
        scratch_shapes=scratch,
        compiler_params=pltpu.CompilerParams(
            dimension_semantics=("arbitrary", "arbitrary"), vmem_limit_bytes=VMEM_LIMIT),
        name="mixer_prompt",
    )(x, *params)


def _mixer_sample(x, state_a, state_b, params, counts, bufs, row_offset, nseq):
    batch, tt, _ = x.shape
    rows = nseq * tt
    first = row_offset // rows
    tok = lambda i: (first + i, 0)
    lane_tok = lambda i: (0, first + i)
    x1, h2p, eid, p = bufs
    out_shape = [
        jax.ShapeDtypeStruct(x1.shape, x1.dtype),
        jax.ShapeDtypeStruct(h2p.shape, h2p.dtype),
        jax.ShapeDtypeStruct(eid.shape, eid.dtype),
        jax.ShapeDtypeStruct(p.shape, p.dtype),
        jax.ShapeDtypeStruct((HALO_A, batch, D_A), jnp.float32),
        jax.ShapeDtypeStruct((HALO_B, batch, D_B), jnp.float32),
        jax.ShapeDtypeStruct((N_EXPERTS, LANES), jnp.float32),
    ]
    out_specs = [
        pl.BlockSpec((rows, D_MODEL), tok),
        pl.BlockSpec((rows, HALF), tok),
        pl.BlockSpec((SUBLANES, rows), lane_tok),
        pl.BlockSpec((SUBLANES, rows), lane_tok),
        pl.BlockSpec((HALO_A, nseq, D_A), lambda i: (0, i, 0)),
        pl.BlockSpec((HALO_B, nseq, D_B), lambda i: (0, i, 0)),
        pl.BlockSpec((N_EXPERTS, LANES), lambda i: (0, 0)),
    ]
    any_spec = pl.BlockSpec(memory_space=pl.ANY)
    in_specs = ([pl.BlockSpec((nseq, tt, D_MODEL), lambda i: (i, 0, 0)),
                 pl.BlockSpec((HALO_A, nseq, D_A), lambda i: (0, i, 0)),
                 pl.BlockSpec((HALO_B, nseq, D_B), lambda i: (0, i, 0))]
                + _mixer_param_specs() + [pl.BlockSpec((N_EXPERTS, LANES), lambda i: (0, 0))] + [any_spec] * 4)
    n_in = len(in_specs)
    scratch = [
        pltpu.VMEM((rows, 3 * D_A), jnp.float32),
        pltpu.VMEM((D_B // LANES, rows, LANES), jnp.float32),
        pltpu.VMEM((D_A // LANES, rows, LANES), jnp.float32),
        pltpu.VMEM((HALO_B + tt, nseq, D_B), jnp.float32),
        pltpu.VMEM((HALO_A + tt, nseq, D_A), jnp.float32),
        pltpu.VMEM((D_B // LANES, rows, LANES), jnp.float32),
        pltpu.VMEM((D_A // LANES, rows, LANES), jnp.float32),
        pltpu.VMEM((rows, D_MODEL), jnp.bfloat16),
    ]
    return pl.pallas_call(
        functools.partial(_mixer_sample_kernel, nseq=nseq, tt=tt),
        grid=(batch // nseq,),
        in_specs=in_specs,
        out_specs=out_specs,
        out_shape=out_shape,
        scratch_shapes=scratch,
        input_output_aliases={n_in - 4: 0, n_in - 3: 1, n_in - 2: 2, n_in - 1: 3},
        compiler_params=pltpu.CompilerParams(
            dimension_semantics=("arbitrary",), vmem_limit_bytes=VMEM_LIMIT),
        name="mixer_sample",
    )(x, state_a, state_b, *params, counts, x1, h2p, eid, p)


def _plan_kernel(eid_ref, cnt_ref, pos_ref, table_ref, carry_ref, start_ref, earlier_ref, *, tile):
    i = pl.program_id(0)
    tt = eid_ref.shape[1]

    @pl.when(i == 0)
    def _():
        tiles = jnp.floor((cnt_ref[...] + (tile - 0.5)) * (1.0 / tile))
        below = (lax.broadcasted_iota(jnp.int32, (N_EXPERTS, N_EXPERTS), 0)
                 > lax.broadcasted_iota(jnp.int32, (N_EXPERTS, N_EXPERTS), 1))
        start = jnp.dot(jnp.where(below, 1.0, 0.0).astype(jnp.bfloat16), tiles.astype(jnp.bfloat16),
                        preferred_element_type=jnp.float32) * tile
        start_ref[...] = start
        seg_end = start + tiles * tile
        tile_row = lax.broadcasted_iota(jnp.int32, (N_EXPERTS, LANES), 1).astype(jnp.float32) * tile
        owner = jnp.sum(jnp.where(seg_end <= tile_row, 1.0, 0.0), axis=0, keepdims=True)
        owner = jnp.minimum(owner, N_EXPERTS - 1.0)
        n_used = jnp.sum(tiles, axis=0, keepdims=True)
        row = lax.broadcasted_iota(jnp.int32, (SUBLANES, LANES), 0)
        table_ref[...] = jnp.where(row == 0, owner, jnp.where(row == 1, n_used, 0.0)).astype(jnp.int32)
        earlier = (lax.broadcasted_iota(jnp.int32, (tt, tt), 0)
                   < lax.broadcasted_iota(jnp.int32, (tt, tt), 1))
        earlier_ref[...] = jnp.where(earlier, 1.0, 0.0).astype(jnp.bfloat16)
        carry_ref[...] = jnp.zeros_like(carry_ref)

    eid = eid_ref[...]
    experts = lax.broadcasted_iota(jnp.int32, (N_EXPERTS, tt), 0)
    oh0 = experts == eid[0:1]
    oh1 = experts == eid[1:2]
    oh = jnp.where(oh0 | oh1, 1.0, 0.0)
    within = jnp.dot(oh.astype(jnp.bfloat16), earlier_ref[...], preferred_element_type=jnp.float32)
    slot_of = within + carry_ref[:, 0:1] + start_ref[:, 0:1]
    s0 = jnp.sum(jnp.where(oh0, slot_of, 0.0), axis=0, keepdims=True)
    s1 = jnp.sum(jnp.where(oh1, slot_of, 0.0), axis=0, keepdims=True)
    k = lax.broadcasted_iota(jnp.int32, (SUBLANES, tt), 0)
    pos_ref[...] = jnp.where(k == 0, s0, jnp.where(k == 1, s1, 0.0)).astype(jnp.int32)
    carry_ref[...] = carry_ref[...] + jnp.sum(oh, axis=1, keepdims=True)


def _plan(eid, counts, tile):
    n_tokens = eid.shape[1]
    return pl.pallas_call(
        functools.partial(_plan_kernel, tile=tile),
        grid=(n_tokens // PLAN_TILE,),
        in_specs=[pl.BlockSpec((SUBLANES, PLAN_TILE), lambda i: (0, i)),
                  pl.BlockSpec((N_EXPERTS, LANES), lambda i: (0, 0))],
        out_specs=[pl.BlockSpec((SUBLANES, PLAN_TILE), lambda i: (0, i)),
                   pl.BlockSpec((SUBLANES, LANES), lambda i: (0, 0))],
        out_shape=[jax.ShapeDtypeStruct((SUBLANES, n_tokens), jnp.int32),
                   jax.ShapeDtypeStruct((SUBLANES, LANES), jnp.int32)],
        scratch_shapes=[pltpu.VMEM((N_EXPERTS, LANES), jnp.float32),
                        pltpu.VMEM((N_EXPERTS, LANES), jnp.float32),
                        pltpu.VMEM((PLAN_TILE, PLAN_TILE), jnp.bfloat16)],
        compiler_params=pltpu.CompilerParams(dimension_semantics=("arbitrary",)),
        name="route_plan",
    )(eid, counts)


def _sc_mesh():
    return plsc.VectorSubcoreMesh(core_axis_name="c", subcore_axis_name="s")


def _sc_dispatch_rows(table, pos, n_out):
    n_workers, n_batches, top_k, batch = pos.shape
    n_rows, words = table.shape
    assert n_workers * n_batches * batch == n_rows and n_batches >= 2

    @functools.partial(
        pl.kernel, mesh=_sc_mesh(),
        out_type=jax.ShapeDtypeStruct((n_out, words), table.dtype),
        scratch_types=[pltpu.VMEM((n_batches, top_k, batch), jnp.int32),
                       pltpu.VMEM((2, batch, words), table.dtype),
                       pltpu.SemaphoreType.DMA((2,)),
                       pltpu.SemaphoreType.DMA((2, top_k))],
    )
    def dispatch(table_hbm, pos_hbm, out_hbm, idx_v, rows_v, sem_in, sem_out):
        worker = lax.axis_index("s") * SC_CORES + lax.axis_index("c")
        pltpu.sync_copy(pos_hbm.at[worker], idx_v)

        def read(b):
            src = table_hbm.at[pl.ds((worker * n_batches + b) * batch, batch)]
            return pltpu.async_copy(src, rows_v.at[b % 2], sem_in.at[b % 2])

        def write(b):
            return [pltpu.async_copy(rows_v.at[b % 2], out_hbm.at[idx_v.at[b, k]], sem_out.at[b % 2, k])
                    for k in range(top_k)]

        reads = {0: read(0)}
        writes = {}
        for b in range(n_batches):
            reads[b].wait()
            if b + 1 < n_batches:
                if b >= 1:
                    for w in writes[b - 1]:
                        w.wait()
                reads[b + 1] = read(b + 1)
            writes[b] = write(b)
        for b in (n_batches - 2, n_batches - 1):
            for w in writes[b]:
                w.wait()

    return dispatch(table, pos)


def _sc_gather_rows(table, idx):
    n_workers, n_batches, batch = idx.shape
    words = table.shape[1]
    assert n_batches >= 2

    @functools.partial(
        pl.kernel, mesh=_sc_mesh(),
        out_type=jax.ShapeDtypeStruct((n_workers * n_batches * batch, words), table.dtype),
        scratch_types=[pltpu.VMEM((n_batches, batch), jnp.int32),
                       pltpu.VMEM((2, batch, words), table.dtype),
                       pltpu.SemaphoreType.DMA((2,)),
                       pltpu.SemaphoreType.DMA((2,))],
    )
    def gather(table_hbm, idx_hbm, out_hbm, idx_v, rows_v, sem_in, sem_out):
        worker = lax.axis_index("s") * SC_CORES + lax.axis_index("c")
        pltpu.sync_copy(idx_hbm.at[worker], idx_v)

        def read(b):
            return pltpu.async_copy(table_hbm.at[idx_v.at[b]], rows_v.at[b % 2], sem_in.at[b % 2])

        def write(b):
            dst = out_hbm.at[pl.ds((worker * n_batches + b) * batch, batch)]
            return pltpu.async_copy(rows_v.at[b % 2], dst, sem_out.at[b % 2])

        reads = {0: read(0)}
        writes = {}
        for b in range(n_batches):
            reads[b].wait()
            if b + 1 < n_batches:
                if b >= 1:
                    writes[b - 1].wait()
                reads[b + 1] = read(b + 1)
            writes[b] = write(b)
        writes[n_batches - 2].wait()
        writes[n_batches - 1].wait()

    return gather(table, idx)


def _experts_kernel(te_ref, xs_ref, wg_ref, wu_ref, wd_ref, ys_ref):
    hi, lo = _unpack_bf16_halves(xs_ref[...])
    x = jnp.concatenate([hi.astype(jnp.bfloat16), lo.astype(jnp.bfloat16)], axis=1)
    w_gate_up = jnp.concatenate([wg_ref[0].astype(jnp.bfloat16), wu_ref[0].astype(jnp.bfloat16)], axis=1)
    wd = wd_ref[0].astype(jnp.bfloat16)
    gate_up = jnp.dot(x, w_gate_up, preferred_element_type=jnp.float32)
    gate = gate_up[:, :D_EXPERT]
    up = gate_up[:, D_EXPERT:]
    hid = (gate * jax.nn.sigmoid(gate) * up).astype(jnp.bfloat16)
    ys_ref[...] = _pack_bf16_halves(jnp.dot(hid, wd, preferred_element_type=jnp.float32))


def _experts(xs, tile_expert, n_valid, wg, wu, wd, tm):
    n_slots = xs.shape[0]
    row_block = lambda i, te: (i, 0)
    w_block = lambda i, te: (te[i], 0, 0)
    return pl.pallas_call(
        _experts_kernel,
        grid_spec=pltpu.PrefetchScalarGridSpec(
            num_scalar_prefetch=1,
            grid=(n_valid,),
            in_specs=[pl.BlockSpec((tm, HALF), row_block),
                      pl.BlockSpec((1, D_MODEL, D_EXPERT), w_block, pipeline_mode=pl.Buffered(2)),
                      pl.BlockSpec((1, D_MODEL, D_EXPERT), w_block, pipeline_mode=pl.Buffered(2)),
                      pl.BlockSpec((1, D_EXPERT, D_MODEL), w_block, pipeline_mode=pl.Buffered(2))],
            out_specs=pl.BlockSpec((tm, HALF), row_block),
        ),
        out_shape=jax.ShapeDtypeStruct((n_slots, HALF), jnp.uint32),
        compiler_params=pltpu.CompilerParams(
            dimension_semantics=("arbitrary",), vmem_limit_bytes=VMEM_LIMIT),
        name="experts",
    )(tile_expert, xs, wg, wu, wd)


def _final_kernel(x1_ref, y0_ref, y1_ref, p_ref, gfin_ref, *rest):
    out_ref = rest[-1]
    pt = jnp.transpose(p_ref[...])
    p0 = pt[:, 0:1]
    p1 = pt[:, 1:2]
    a_hi, a_lo = _unpack_bf16_halves(y0_ref[...])
    b_hi, b_lo = _unpack_bf16_halves(y1_ref[...])
    x1 = x1_ref[...]
    x2_hi = x1[:, :HALF] + (p0 * a_hi + p1 * b_hi)
    x2_lo = x1[:, HALF:] + (p0 * a_lo + p1 * b_lo)
    ms = (jnp.sum(x2_hi * x2_hi, axis=-1, keepdims=True)
          + jnp.sum(x2_lo * x2_lo, axis=-1, keepdims=True)) / D_MODEL
    scale = lax.rsqrt(ms + RMS_EPS)
    g = gfin_ref[...]
    out_ref[:, :HALF] = x2_hi * scale * g[:, :HALF]
    out_ref[:, HALF:] = x2_lo * scale * g[:, HALF:]


def _final(x1, yk, p, gfin, row_offset, n_rows, tm, out_rows, out_offset, out_buf=None):
    n_tokens = x1.shape[0]
    first = row_offset // tm
    second = (n_tokens + row_offset) // tm
    out_first = out_offset // tm
    in_specs = [pl.BlockSpec((tm, D_MODEL), lambda i: (first + i, 0)),
                pl.BlockSpec((tm, HALF), lambda i: (first + i, 0)),
                pl.BlockSpec((tm, HALF), lambda i: (second + i, 0)),
                pl.BlockSpec((SUBLANES, tm), lambda i: (0, first + i)),
                pl.BlockSpec((1, D_MODEL), lambda i: (0, 0))]
    args = [x1, yk, yk, p, gfin]
    aliases = {}
    if out_buf is not None:
        in_specs.append(pl.BlockSpec(memory_space=pl.ANY))
        args.append(out_buf)
        aliases = {len(args) - 1: 0}
    return pl.pallas_call(
        _final_kernel,
        grid=(n_rows // tm,),
        in_specs=in_specs,
        out_specs=pl.BlockSpec((tm, D_MODEL), lambda i: (out_first + i, 0)),
        out_shape=jax.ShapeDtypeStruct((out_rows, D_MODEL), jnp.float32),
        input_output_aliases=aliases,
        compiler_params=pltpu.CompilerParams(
            dimension_semantics=("arbitrary",), vmem_limit_bytes=VMEM_LIMIT),
        name="final",
    )(*args)


def _expert_tile(n_tokens):
    mean_rows = TOP_K * n_tokens // N_EXPERTS
    return -(-(mean_rows + 2 * math.isqrt(mean_rows)) // (2 * SUBLANES)) * (2 * SUBLANES)


def _routed_experts(h2p, eid, counts, w_gate, w_up, w_down):
    n_tokens = h2p.shape[0]
    assert n_tokens % (SC_WORKERS * SC_DISPATCH_BATCH) == 0 and n_tokens % PLAN_TILE == 0
    assert (TOP_K * n_tokens) % (SC_WORKERS * SC_BATCH) == 0
    tile = _expert_tile(n_tokens)
    n_tiles = -(-(TOP_K * n_tokens + N_EXPERTS * (tile - 1)) // tile)
    assert n_tiles <= LANES
    pos, table = _plan(eid, counts, tile)
    pos = pos[:TOP_K]
    by_token = jnp.transpose(pos.reshape(TOP_K, SC_WORKERS, -1, SC_DISPATCH_BATCH), (1, 2, 0, 3))
    xs = _sc_dispatch_rows(h2p, by_token, n_tiles * tile)
    ys = _experts(xs, table[0], table[1, 0], w_gate, w_up, w_down, tile)
    return _sc_gather_rows(ys, pos.reshape(SC_WORKERS, -1, SC_BATCH))


def kernel(x_prompt, x_sample, state_conv_a, state_conv_b, g_mix, w_in, conv_a_w, conv_b_w, conv_b_bias,
           ln_g, ln_b, w_out, g_ffn, w_coarse, b_coarse, w_fine, b_fine, w_gate, w_up, w_down, g_final):
    assert g_mix.shape[0] == 1, "single trunk layer"
    batch, seq, _ = x_prompt.shape
    dec_batch, dec_seq, _ = x_sample.shape
    n_prompt = batch * seq
    n_sample = dec_batch * dec_seq
    bf16 = jnp.bfloat16

    wr = jnp.concatenate([
        w_coarse[0], jnp.zeros((D_MODEL, SUBLANES - N_EXPERT_GROUPS), jnp.float32),
        jnp.transpose(w_fine[0], (1, 0, 2)).reshape(D_MODEL, N_EXPERTS),
        jnp.zeros((D_MODEL, LANES - ROUTER_ROWS), jnp.float32)], axis=1)
    wr_hi = wr.astype(bf16)
    wr_lo = (wr - wr_hi.astype(jnp.float32)).astype(bf16)
    wr_both = jnp.concatenate([wr_hi, wr_lo], axis=1)
    br = jnp.concatenate([
        b_coarse[0], jnp.full((SUBLANES - N_EXPERT_GROUPS,), NEG_BIG, jnp.float32),
        b_fine[0].reshape(N_EXPERTS)]).reshape(ROUTER_ROWS, 1)

    params = (g_mix, w_in[0].astype(bf16), conv_a_w, conv_b_w, conv_b_bias, ln_g, ln_b,
              w_out[0].astype(bf16), g_ffn, wr_both, br)

    experts = (w_gate[0], w_up[0], w_down[0])
    gfin = g_final.reshape(1, D_MODEL)

    bufs = _mixer_prompt(x_prompt, params, n_prompt + n_sample, MIXER_TILE, MIXER_TILES_PER_STEP, 0, batch)
    na_p, nb_p, counts = bufs[4:]
    x1, h2p, eid, p, na_s, nb_s, counts = _mixer_sample(
        x_sample, jnp.transpose(state_conv_a[0], (1, 0, 2)), jnp.transpose(state_conv_b[0], (1, 0, 2)),
        params, counts, bufs[:4], n_prompt, nseq=32)
    na_s = jnp.transpose(na_s, (1, 0, 2))
    nb_s = jnp.transpose(nb_s, (1, 0, 2))
    yk = _routed_experts(h2p, eid, counts, *experts)
    y_p = _final(x1, yk, p, gfin, 0, n_prompt, FINAL_TILE, n_prompt, 0)
    y_s = _final(x1, yk, p, gfin, n_prompt, n_sample, FINAL_TILE, n_sample, 0)
    return (y_p.reshape(batch, seq, D_MODEL), y_s.reshape(dec_batch, dec_seq, D_MODEL),
            na_p[None], nb_p[None], na_s[None], nb_s[None])
```

```python
import functools
import math

import jax
import jax.numpy as jnp
from jax import lax
from jax.experimental import pallas as pl
from jax.experimental.pallas import tpu as pltpu
from jax.experimental.pallas import tpu_sc as plsc

D_MODEL = 1024
D_A = 512
D_B = 512
CONV_A = 3
CONV_B = 31
HALO_A = CONV_A - 1
HALO_B = CONV_B - 1
IN_COLS = 3 * D_A + 2 * D_B
N_EXPERT_GROUPS = 4
EXPERTS_PER_GROUP = 8
N_EXPERTS = N_EXPERT_GROUPS * EXPERTS_PER_GROUP
TOP_K = 2
D_EXPERT = D_MODEL // 4
RMS_EPS = 1e-6
LN_EPS = 1e-5

SUBLANES = 8
LANES = 128
PAD_A = SUBLANES
PAD_B = 32
ROUTER_ROWS = SUBLANES + N_EXPERTS
NEG_BIG = -1e30
VMEM_LIMIT = 56 * 1024 * 1024
HALF = D_MODEL // 2
HI_MASK = 0xFFFF0000

SC_CORES = 2
SC_SUBCORES = 16
SC_WORKERS = SC_CORES * SC_SUBCORES
SC_BATCH = 64
SC_DISPATCH_BATCH = 32
SC_GATHER_BUFFERS = 3

PLAN_TILE = 1024
FINAL_TILE = 1024
MIXER_TILE = 512
MIXER_TILES_PER_STEP = 2
MIXER_ROW_CHUNK = 64
MIXER_FINISH_GROUPS = (4, 4)


def _rms_scale(x):
    return x * lax.rsqrt(jnp.mean(x * x, axis=-1, keepdims=True) + RMS_EPS)


def _pack_bf16_halves(x):
    bits = lax.bitcast_convert_type(x.astype(jnp.bfloat16).astype(jnp.float32), jnp.uint32)
    return bits[:, :HALF] | (bits[:, HALF:] >> 16)


def _unpack_bf16_halves(w):
    hi = lax.bitcast_convert_type(w & jnp.uint32(HI_MASK), jnp.float32)
    lo = lax.bitcast_convert_type(w << 16, jnp.float32)
    return hi, lo


def _route(logits_t):
    rows = logits_t.shape[1]
    iota = lax.broadcasted_iota(jnp.int32, (SUBLANES, rows), 0)
    lc = logits_t[0:SUBLANES]
    cmax = jnp.max(lc, axis=0, keepdims=True)
    grp = jnp.min(jnp.where(lc == cmax, iota, SUBLANES), axis=0, keepdims=True)
    p_grp = 1.0 / jnp.sum(jnp.exp(lc - cmax), axis=0, keepdims=True)
    sel = logits_t[SUBLANES:2 * SUBLANES]
    for g in range(1, N_EXPERT_GROUPS):
        sel = jnp.where(grp == g, logits_t[(g + 1) * SUBLANES:(g + 2) * SUBLANES], sel)
    v1 = jnp.max(sel, axis=0, keepdims=True)
    i1 = jnp.min(jnp.where(sel == v1, iota, SUBLANES), axis=0, keepdims=True)
    sel2 = jnp.where(iota == i1, -jnp.inf, sel)
    v2 = jnp.max(sel2, axis=0, keepdims=True)
    i2 = jnp.min(jnp.where(sel2 == v2, iota, SUBLANES), axis=0, keepdims=True)
    e2 = jnp.exp(v2 - v1)
    den = 1.0 + e2
    p1 = p_grp / den
    p2 = p_grp * e2 / den
    base = grp * EXPERTS_PER_GROUP
    return (base + i1, base + i2), (p1, p2)


def _after_mix(x1, f0, gffn_ref, wr_ref, br_ref, x1_ref, h2p_ref, eid_ref, p_ref, cnt_ref):
    m = x1.shape[0]
    x1_ref[f0:f0 + m, :] = x1
    h2 = _rms_scale(x1) * gffn_ref[...]
    h2_hi = h2.astype(jnp.bfloat16)
    h2p_ref[f0:f0 + m, :] = _pack_bf16_halves(h2)
    h2_lo = (h2 - h2_hi.astype(jnp.float32)).astype(jnp.bfloat16)
    both = jnp.dot(h2_hi, wr_ref[...], preferred_element_type=jnp.float32)
    cross = jnp.dot(h2_lo, wr_ref[:, 0:LANES], preferred_element_type=jnp.float32)
    logits = both[:, 0:LANES] + both[:, LANES:] + cross
    logits_t = jnp.transpose(logits)[0:ROUTER_ROWS] + br_ref[...]
    (e1, e2), (p1, p2) = _route(logits_t)
    iota = lax.broadcasted_iota(jnp.int32, (SUBLANES, m), 0)
    eid_ref[:, f0:f0 + m] = jnp.where(iota == 0, e1, jnp.where(iota == 1, e2, 0))
    p_ref[:, f0:f0 + m] = jnp.where(iota == 0, p1, jnp.where(iota == 1, p2, 0.0))
    experts = lax.broadcasted_iota(jnp.int32, (N_EXPERTS, m), 0)
    routed = jnp.where((experts == e1) | (experts == e2), 1.0, 0.0)
    cnt_ref[...] = cnt_ref[...] + jnp.sum(routed, axis=1, keepdims=True)


def _mixer_prompt_kernel(x_ref, gmix_ref, win_ref, caw_ref, cbw_ref, cbb_ref, lng_ref, lnb_ref,
                         wout_ref, gffn_ref, wr_ref, br_ref,
                         x1_ref, h2p_ref, eid_ref, p_ref, na_ref, nb_ref, cnt_ref,
                         proj_ref, uext_ref, gext_ref, z_ref, ush_ref, gsh_ref, *, tt, tiles, row_chunk):
    t = pl.program_id(1)
    col_chunk = 2 * LANES
    span_b = PAD_B + tt - SUBLANES
    step_rows = tiles * tt

    @pl.when((pl.program_id(0) == 0) & (t == 0))
    def _():
        cnt_ref[...] = jnp.zeros_like(cnt_ref)

    @pl.when(t == 0)
    def _():
        uext_ref[0:PAD_A, :] = jnp.zeros((PAD_A, D_A), jnp.float32)
        gext_ref[0:PAD_B, :] = jnp.zeros((PAD_B, D_B), jnp.float32)

    def window(base_ref, shifted_ref, first_shift, base, off):
        r = off % SUBLANES
        a8 = off - r
        if r == 0:
            return base_ref[base + a8:base + a8 + row_chunk, :]
        return shifted_ref[r - first_shift, a8:a8 + row_chunk, :]

    cbw = cbw_ref[...]
    row_starts = list(range(0, tt, row_chunk))
    a_cols = list(range(0, 3 * D_A, col_chunk))
    groups, done = [], 0
    for size in MIXER_FINISH_GROUPS:
        groups.append(row_starts[done:done + size])
        done += size
    assert done == len(row_starts)

    def head(base):
        h = (_rms_scale(x_ref[0, base:base + tt, :]) * gmix_ref[...]).astype(jnp.bfloat16)
        for c0 in range(0, D_B, col_chunk):
            v_b = jnp.dot(h, win_ref[:, 3 * D_A + c0:3 * D_A + c0 + col_chunk], preferred_element_type=jnp.float32)
            g_b = jnp.dot(h, win_ref[:, 3 * D_A + D_B + c0:3 * D_A + D_B + c0 + col_chunk],
                          preferred_element_type=jnp.float32)
            gext_ref[PAD_B + base:PAD_B + base + tt, c0:c0 + col_chunk] = v_b * jax.nn.sigmoid(g_b)
        return h

    def body(base, h):
        for r in range(1, SUBLANES):
            gsh_ref[r - 1, 0:span_b, :] = gext_ref[base + r:base + r + span_b, :]

        def conv_b_chunk(r0):
            acc_b = None
            for k in range(CONV_B):
                term = window(gext_ref, gsh_ref, 1, base, PAD_B - HALO_B + k + r0) * cbw[k:k + 1, :]
                acc_b = term if acc_b is None else acc_b + term
            zb = acc_b + cbb_ref[...]
            mu = jnp.mean(zb, axis=-1, keepdims=True)
            zc = zb - mu
            var = jnp.mean(zc * zc, axis=-1, keepdims=True)
            y = zc * lax.rsqrt(var + LN_EPS) * lng_ref[...] + lnb_ref[...]
            z_ref[base + r0:base + r0 + row_chunk, D_A:] = (y * jax.nn.sigmoid(y)).astype(jnp.bfloat16)

        def conv_a():
            uext_ref[PAD_A + base:PAD_A + base + tt, :] = proj_ref[:, D_A:2 * D_A] * proj_ref[:, 2 * D_A:3 * D_A]
            for r in range(SUBLANES - HALO_A, SUBLANES):
                ush_ref[r - (SUBLANES - HALO_A), :, :] = uext_ref[base + r:base + r + tt, :]
            caw = caw_ref[...]
            for r0 in row_starts:
                acc_a = None
                for k in range(CONV_A):
                    term = window(uext_ref, ush_ref, SUBLANES - HALO_A, base, PAD_A - HALO_A + k + r0) * caw[k:k + 1, :]
                    acc_a = term if acc_a is None else acc_a + term
                z_ref[base + r0:base + r0 + row_chunk, 0:D_A] = (
                    proj_ref[r0:r0 + row_chunk, 0:D_A] * acc_a).astype(jnp.bfloat16)

        def out_a(rows):
            lo, hi = base + rows[0], base + rows[-1] + row_chunk
            return jnp.dot(z_ref[lo:hi, 0:D_A], wout_ref[0:D_A, :], preferred_element_type=jnp.float32)

        def finish(rows, part_a):
            lo, hi = base + rows[0], base + rows[-1] + row_chunk
            part_b = jnp.dot(z_ref[lo:hi, D_A:], wout_ref[D_A:, :], preferred_element_type=jnp.float32)
            _after_mix(x_ref[0, lo:hi, :] + part_a + part_b, lo,
                       gffn_ref, wr_ref, br_ref, x1_ref, h2p_ref, eid_ref, p_ref, cnt_ref)

        parts_a = []
        for gi, rows in enumerate(groups):
            for i, r0 in enumerate(rows):
                conv_b_chunk(r0)
                if gi == 0:
                    for c0 in a_cols[i * len(a_cols) // len(rows):(i + 1) * len(a_cols) // len(rows)]:
                        proj_ref[:, c0:c0 + col_chunk] = jnp.dot(h, win_ref[:, c0:c0 + col_chunk],
                                                                 preferred_element_type=jnp.float32)
            if gi == 0:
                conv_a()
                parts_a = [out_a(g) for g in groups]
            finish(rows, parts_a[gi])

    heads = [head(ti * tt) for ti in range(tiles)]
    for ti in range(tiles):
        body(ti * tt, heads[ti])

    @pl.when(t == pl.num_programs(1) - 1)
    def _():
        na_ref[0] = uext_ref[PAD_A + step_rows - HALO_A:PAD_A + step_rows, :]
        nb_ref[0] = gext_ref[PAD_B + step_rows - HALO_B:PAD_B + step_rows, :]
    uext_ref[0:PAD_A, :] = uext_ref[step_rows:step_rows + PAD_A, :]
    gext_ref[0:PAD_B, :] = gext_ref[step_rows:step_rows + PAD_B, :]


def _mixer_sample_kernel(x_ref, sa_ref, sb_ref, gmix_ref, win_ref, caw_ref, cbw_ref, cbb_ref, lng_ref, lnb_ref,
                         wout_ref, gffn_ref, wr_ref, br_ref, cnt_in_ref, _x1_in, _h2p_in, _eid_in, _p_in,
                         x1_ref, h2p_ref, eid_ref, p_ref, na_ref, nb_ref, cnt_ref,
                         proj_ref, gnt_ref, unt_ref, gtm_ref, utm_ref, ynt_ref, ant_ref, z_ref, *, nseq, tt):
    rows = nseq * tt
    col_chunk = 2 * LANES

    @pl.when(pl.program_id(0) == 0)
    def _():
        cnt_ref[...] = cnt_in_ref[...]

    x = x_ref[...].reshape(rows, D_MODEL)
    h = (_rms_scale(x) * gmix_ref[...]).astype(jnp.bfloat16)
    for c0 in range(0, D_B, col_chunk):
        v_b = jnp.dot(h, win_ref[:, 3 * D_A + c0:3 * D_A + c0 + col_chunk], preferred_element_type=jnp.float32)
        g_b = jnp.dot(h, win_ref[:, 3 * D_A + D_B + c0:3 * D_A + D_B + c0 + col_chunk],
                      preferred_element_type=jnp.float32)
        g = v_b * jax.nn.sigmoid(g_b)
        for c in range(0, col_chunk, LANES):
            gnt_ref[(c0 + c) // LANES] = g[:, c:c + LANES]
    for c0 in range(0, 3 * D_A, col_chunk):
        proj_ref[:, c0:c0 + col_chunk] = jnp.dot(h, win_ref[:, c0:c0 + col_chunk],
                                                 preferred_element_type=jnp.float32)
    u = proj_ref[:, D_A:2 * D_A] * proj_ref[:, 2 * D_A:3 * D_A]
    for c in range(0, D_A, LANES):
        unt_ref[c // LANES] = u[:, c:c + LANES]

    utm_ref[0:HALO_A] = sa_ref[...]
    gtm_ref[0:HALO_B] = sb_ref[...]
    for t in range(tt):
        for c in range(0, D_A, LANES):
            utm_ref[HALO_A + t, :, c:c + LANES] = unt_ref[c // LANES, pl.ds(t, nseq, stride=tt), :]
        for c in range(0, D_B, LANES):
            gtm_ref[HALO_B + t, :, c:c + LANES] = gnt_ref[c // LANES, pl.ds(t, nseq, stride=tt), :]
    na_ref[...] = utm_ref[tt:tt + HALO_A]
    nb_ref[...] = gtm_ref[tt:tt + HALO_B]

    caw = caw_ref[...]
    cbw = cbw_ref[...]
    for t in range(tt):
        acc_b = None
        for k in range(CONV_B):
            term = gtm_ref[t + k] * cbw[k:k + 1, :]
            acc_b = term if acc_b is None else acc_b + term
        zb = acc_b + cbb_ref[...]
        mu = jnp.mean(zb, axis=-1, keepdims=True)
        zc = zb - mu
        var = jnp.mean(zc * zc, axis=-1, keepdims=True)
        y = zc * lax.rsqrt(var + LN_EPS) * lng_ref[...] + lnb_ref[...]
        y = y * jax.nn.sigmoid(y)
        for c in range(0, D_B, LANES):
            ynt_ref[c // LANES, pl.ds(t, nseq, stride=tt), :] = y[:, c:c + LANES]
        acc_a = None
        for k in range(CONV_A):
            term = utm_ref[t + k] * caw[k:k + 1, :]
            acc_a = term if acc_a is None else acc_a + term
        for c in range(0, D_A, LANES):
            ant_ref[c // LANES, pl.ds(t, nseq, stride=tt), :] = acc_a[:, c:c + LANES]

    for c in range(0, D_A, LANES):
        z_ref[:, c:c + LANES] = (proj_ref[:, c:c + LANES] * ant_ref[c // LANES]).astype(jnp.bfloat16)
    for c in range(0, D_B, LANES):
        z_ref[:, D_A + c:D_A + c + LANES] = ynt_ref[c // LANES].astype(jnp.bfloat16)
    x1 = x + jnp.dot(z_ref[...], wout_ref[...], preferred_element_type=jnp.float32)
    _after_mix(x1, 0, gffn_ref, wr_ref, br_ref, x1_ref, h2p_ref, eid_ref, p_ref, cnt_ref)


def _full(shape):
    return pl.BlockSpec(shape, lambda *_: (0,) * len(shape))


def _mixer_param_specs():
    return [
        _full((1, D_MODEL)),
        _full((D_MODEL, IN_COLS)),
        pl.BlockSpec((None, CONV_A, D_A), lambda *_: (0, 0, 0)),
        pl.BlockSpec((None, CONV_B, D_B), lambda *_: (0, 0, 0)),
        _full((1, D_B)),
        _full((1, D_B)),
        _full((1, D_B)),
        _full((D_MODEL, D_MODEL)),
        _full((1, D_MODEL)),
        _full((D_MODEL, 2 * LANES)),
        _full((ROUTER_ROWS, 1)),
    ]


def _mixer_prompt(x, params, n_tokens_total, tt, tiles, seq_first, batch):
    seq = x.shape[1]
    rows = tiles * tt
    n_t = seq // rows
    tok = lambda b, t: (b * n_t + t, 0)
    lane_tok = lambda b, t: (0, b * n_t + t)
    out_shape = [
        jax.ShapeDtypeStruct((n_tokens_total, D_MODEL), jnp.float32),
        jax.ShapeDtypeStruct((n_tokens_total, HALF), jnp.uint32),
        jax.ShapeDtypeStruct((SUBLANES, n_tokens_total), jnp.int32),
        jax.ShapeDtypeStruct((SUBLANES, n_tokens_total), jnp.float32),
        jax.ShapeDtypeStruct((batch, HALO_A, D_A), jnp.float32),
        jax.ShapeDtypeStruct((batch, HALO_B, D_B), jnp.float32),
        jax.ShapeDtypeStruct((N_EXPERTS, LANES), jnp.float32),
    ]
    out_specs = [
        pl.BlockSpec((rows, D_MODEL), tok),
        pl.BlockSpec((rows, HALF), tok),
        pl.BlockSpec((SUBLANES, rows), lane_tok),
        pl.BlockSpec((SUBLANES, rows), lane_tok),
        pl.BlockSpec((1, HALO_A, D_A), lambda b, t: (b, 0, 0)),
        pl.BlockSpec((1, HALO_B, D_B), lambda b, t: (b, 0, 0)),
        pl.BlockSpec((N_EXPERTS, LANES), lambda b, t: (0, 0)),
    ]
    scratch = [
        pltpu.VMEM((tt, 3 * D_A), jnp.float32),
        pltpu.VMEM((PAD_A + rows, D_A), jnp.float32),
        pltpu.VMEM((PAD_B + rows, D_B), jnp.float32),
        pltpu.VMEM((rows, D_MODEL), jnp.bfloat16),
        pltpu.VMEM((HALO_A, tt, D_A), jnp.float32),
        pltpu.VMEM((SUBLANES - 1, PAD_B + tt - SUBLANES, D_B), jnp.float32),
    ]
    params_specs = [pl.BlockSpec(sp.block_shape, sp.index_map, pipeline_mode=pl.Buffered(1))
                    for sp in _mixer_param_specs()]
    return pl.pallas_call(
        functools.partial(_mixer_prompt_kernel, tt=tt, tiles=tiles, row_chunk=MIXER_ROW_CHUNK),
        grid=(batch, n_t),
        in_specs=[pl.BlockSpec((1, rows, D_MODEL), lambda b, t: (b + seq_first, t, 0))] + params_specs,
        out_specs=out_specs,
        out_shape=out_shape,
        scratch_shapes=scratch,
        compiler_params=pltpu.CompilerParams(
            dimension_semantics=("arbitrary", "arbitrary"), vmem_limit_bytes=VMEM_LIMIT),
        name="mixer_prompt",
    )(x, *params)


def _mixer_sample(x, state_a, state_b, params, counts, bufs, row_offset, nseq):
    batch, tt, _ = x.shape
    rows = nseq * tt
    first = row_offset // rows
    tok = lambda i: (first + i, 0)
    lane_tok = lambda i: (0, first + i)
    x1, h2p, eid, p = bufs
    out_shape = [
        jax.ShapeDtypeStruct(x1.shape, x1.dtype),
        jax.ShapeDtypeStruct(h2p.shape, h2p.dtype),
        jax.ShapeDtypeStruct(eid.shape, eid.dtype),
        jax.ShapeDtypeStruct(p.shape, p.dtype),
        jax.ShapeDtypeStruct((HALO_A, batch, D_A), jnp.float32),
        jax.ShapeDtypeStruct((HALO_B, batch, D_B), jnp.float32),
        jax.ShapeDtypeStruct((N_EXPERTS, LANES), jnp.float32),
    ]
    out_specs = [
        pl.BlockSpec((rows, D_MODEL), tok),
        pl.BlockSpec((rows, HALF), tok),
        pl.BlockSpec((SUBLANES, rows), lane_tok),
        pl.BlockSpec((SUBLANES, rows), lane_tok),
        pl.BlockSpec((HALO_A, nseq, D_A), lambda i: (0, i, 0)),
        pl.BlockSpec((HALO_B, nseq, D_B), lambda i: (0, i, 0)),
        pl.BlockSpec((N_EXPERTS, LANES), lambda i: (0, 0)),
    ]
    any_spec = pl.BlockSpec(memory_space=pl.ANY)
    in_specs = ([pl.BlockSpec((nseq, tt, D_MODEL), lambda i: (i, 0, 0)),
                 pl.BlockSpec((HALO_A, nseq, D_A), lambda i: (0, i, 0)),
                 pl.BlockSpec((HALO_B, nseq, D_B), lambda i: (0, i, 0))]
                + _mixer_param_specs() + [pl.BlockSpec((N_EXPERTS, LANES), lambda i: (0, 0))] + [any_spec] * 4)
    n_in = len(in_specs)
    scratch = [
        pltpu.VMEM((rows, 3 * D_A), jnp.float32),
        pltpu.VMEM((D_B // LANES, rows, LANES), jnp.float32),
        pltpu.VMEM((D_A // LANES, rows, LANES), jnp.float32),
        pltpu.VMEM((HALO_B + tt, nseq, D_B), jnp.float32),
        pltpu.VMEM((HALO_A + tt, nseq, D_A), jnp.float32),
        pltpu.VMEM((D_B // LANES, rows, LANES), jnp.float32),
        pltpu.VMEM((D_A // LANES, rows, LANES), jnp.float32),
        pltpu.VMEM((rows, D_MODEL), jnp.bfloat16),
    ]
    return pl.pallas_call(
        functools.partial(_mixer_sample_kernel, nseq=nseq, tt=tt),
        grid=(batch // nseq,),
        in_specs=in_specs,
        out_specs=out_specs,
        out_shape=out_shape,
        scratch_shapes=scratch,
        input_output_aliases={n_in - 4: 0, n_in - 3: 1, n_in - 2: 2, n_in - 1: 3},
        compiler_params=pltpu.CompilerParams(
            dimension_semantics=("arbitrary",), vmem_limit_bytes=VMEM_LIMIT),
        name="mixer_sample",
    )(x, state_a, state_b, *params, counts, x1, h2p, eid, p)


def _plan_kernel(eid_ref, cnt_ref, pos_ref, table_ref, carry_ref, start_ref, earlier_ref, *, tile):
    i = pl.program_id(0)
    tt = eid_ref.shape[1]

    @pl.when(i == 0)
    def _():
        tiles = jnp.floor((cnt_ref[...] + (tile - 0.5)) * (1.0 / tile))
        below = (lax.broadcasted_iota(jnp.int32, (N_EXPERTS, N_EXPERTS), 0)
                 > lax.broadcasted_iota(jnp.int32, (N_EXPERTS, N_EXPERTS), 1))
        start = jnp.dot(jnp.where(below, 1.0, 0.0).astype(jnp.bfloat16), tiles.astype(jnp.bfloat16),
                        preferred_element_type=jnp.float32) * tile
        start_ref[...] = start
        seg_end = start + tiles * tile
        tile_row = lax.broadcasted_iota(jnp.int32, (N_EXPERTS, LANES), 1).astype(jnp.float32) * tile
        owner = jnp.sum(jnp.where(seg_end <= tile_row, 1.0, 0.0), axis=0, keepdims=True)
        owner = jnp.minimum(owner, N_EXPERTS - 1.0)
        n_used = jnp.sum(tiles, axis=0, keepdims=True)
        row = lax.broadcasted_iota(jnp.int32, (SUBLANES, LANES), 0)
        table_ref[...] = jnp.where(row == 0, owner, jnp.where(row == 1, n_used, 0.0)).astype(jnp.int32)
        earlier = (lax.broadcasted_iota(jnp.int32, (tt, tt), 0)
                   < lax.broadcasted_iota(jnp.int32, (tt, tt), 1))
        earlier_ref[...] = jnp.where(earlier, 1.0, 0.0).astype(jnp.bfloat16)
        carry_ref[...] = jnp.zeros_like(carry_ref)

    eid = eid_ref[...]
    experts = lax.broadcasted_iota(jnp.int32, (N_EXPERTS, tt), 0)
    oh0 = experts == eid[0:1]
    oh1 = experts == eid[1:2]
    oh = jnp.where(oh0 | oh1, 1.0, 0.0)
    within = jnp.dot(oh.astype(jnp.bfloat16), earlier_ref[...], preferred_element_type=jnp.float32)
    slot_of = within + carry_ref[:, 0:1] + start_ref[:, 0:1]
    s0 = jnp.sum(jnp.where(oh0, slot_of, 0.0), axis=0, keepdims=True)
    s1 = jnp.sum(jnp.where(oh1, slot_of, 0.0), axis=0, keepdims=True)
    k = lax.broadcasted_iota(jnp.int32, (SUBLANES, tt), 0)
    pos_ref[...] = jnp.where(k == 0, s0, jnp.where(k == 1, s1, 0.0)).astype(jnp.int32)
    carry_ref[...] = carry_ref[...] + jnp.sum(oh, axis=1, keepdims=True)


def _plan(eid, counts, tile):
    n_tokens = eid.shape[1]
    return pl.pallas_call(
        functools.partial(_plan_kernel, tile=tile),
        grid=(n_tokens // PLAN_TILE,),
        in_specs=[pl.BlockSpec((SUBLANES, PLAN_TILE), lambda i: (0, i)),
                  pl.BlockSpec((N_EXPERTS, LANES), lambda i: (0, 0))],
        out_specs=[pl.BlockSpec((SUBLANES, PLAN_TILE), lambda i: (0, i)),
                   pl.BlockSpec((SUBLANES, LANES), lambda i: (0, 0))],
        out_shape=[jax.ShapeDtypeStruct((SUBLANES, n_tokens), jnp.int32),
                   jax.ShapeDtypeStruct((SUBLANES, LANES), jnp.int32)],
        scratch_shapes=[pltpu.VMEM((N_EXPERTS, LANES), jnp.float32),
                        pltpu.VMEM((N_EXPERTS, LANES), jnp.float32),
                        pltpu.VMEM((PLAN_TILE, PLAN_TILE), jnp.bfloat16)],
        compiler_params=pltpu.CompilerParams(dimension_semantics=("arbitrary",)),
        name="route_plan",
    )(eid, counts)


def _sc_mesh():
    return plsc.VectorSubcoreMesh(core_axis_name="c", subcore_axis_name="s")


def _sc_dispatch_rows(table, pos, n_out):
    n_workers, n_batches, top_k, batch = pos.shape
    n_rows, words = table.shape
    assert n_workers * n_batches * batch == n_rows and n_batches >= 2

    @functools.partial(
        pl.kernel, mesh=_sc_mesh(),
        out_type=jax.ShapeDtypeStruct((n_out, words), table.dtype),
        scratch_types=[pltpu.VMEM((n_batches, top_k, batch), jnp.int32),
                       pltpu.VMEM((2, batch, words), table.dtype),
                       pltpu.SemaphoreType.DMA((2,)),
                       pltpu.SemaphoreType.DMA((2, top_k))],
    )
    def dispatch(table_hbm, pos_hbm, out_hbm, idx_v, rows_v, sem_in, sem_out):
        worker = lax.axis_index("s") * SC_CORES + lax.axis_index("c")
        pltpu.sync_copy(pos_hbm.at[worker], idx_v)

        def read(b):
            src = table_hbm.at[pl.ds((worker * n_batches + b) * batch, batch)]
            return pltpu.async_copy(src, rows_v.at[b % 2], sem_in.at[b % 2])

        def write(b):
            return [pltpu.async_copy(rows_v.at[b % 2], out_hbm.at[idx_v.at[b, k]], sem_out.at[b % 2, k])
                    for k in range(top_k)]

        reads = {0: read(0)}
        writes = {}
        for b in range(n_batches):
            reads[b].wait()
            if b + 1 < n_batches:
                if b >= 1:
                    for w in writes[b - 1]:
                        w.wait()
                reads[b + 1] = read(b + 1)
            writes[b] = write(b)
        for b in (n_batches - 2, n_batches - 1):
            for w in writes[b]:
                w.wait()

    return dispatch(table, pos)


def _sc_gather_rows(table, idx):
    n_workers, n_batches, batch = idx.shape
    words = table.shape[1]
    depth = SC_GATHER_BUFFERS

    @functools.partial(
        pl.kernel, mesh=_sc_mesh(),
        out_type=jax.ShapeDtypeStruct((n_workers * n_batches * batch, words), table.dtype),
        scratch_types=[pltpu.VMEM((n_batches, batch), jnp.int32),
                       pltpu.VMEM((depth, batch, words), table.dtype),
                       pltpu.SemaphoreType.DMA((depth,)),
                       pltpu.SemaphoreType.DMA((depth,))],
    )
    def gather(table_hbm, idx_hbm, out_hbm, idx_v, rows_v, sem_in, sem_out):
        worker = lax.axis_index("s") * SC_CORES + lax.axis_index("c")
        pltpu.sync_copy(idx_hbm.at[worker], idx_v)

        def read(b):
            return pltpu.async_copy(table_hbm.at[idx_v.at[b]], rows_v.at[b % depth], sem_in.at[b % depth])

        def write(b):
            dst = out_hbm.at[pl.ds((worker * n_batches + b) * batch, batch)]
            return pltpu.async_copy(rows_v.at[b % depth], dst, sem_out.at[b % depth])

        reads = {b: read(b) for b in range(min(depth - 1, n_batches))}
        writes = {}
        waited = set()
        for b in range(n_batches):
            reads[b].wait()
            writes[b] = write(b)
            nxt = b + depth - 1
            if nxt < n_batches:
                if nxt - depth >= 0:
                    writes[nxt - depth].wait()
                    waited.add(nxt - depth)
                reads[nxt] = read(nxt)
        for b in range(n_batches):
            if b not in waited:
                writes[b].wait()

    return gather(table, idx)


def _experts_kernel(te_ref, xs_ref, wg_ref, wu_ref, wd_ref, ys_ref):
    hi, lo = _unpack_bf16_halves(xs_ref[...])
    x = jnp.concatenate([hi.astype(jnp.bfloat16), lo.astype(jnp.bfloat16)], axis=1)
    w_gate_up = jnp.concatenate([wg_ref[0].astype(jnp.bfloat16), wu_ref[0].astype(jnp.bfloat16)], axis=1)
    wd = wd_ref[0].astype(jnp.bfloat16)
    gate_up = jnp.dot(x, w_gate_up, preferred_element_type=jnp.float32)
    gate = gate_up[:, :D_EXPERT]
    up = gate_up[:, D_EXPERT:]
    hid = (gate * jax.nn.sigmoid(gate) * up).astype(jnp.bfloat16)
    ys_ref[...] = _pack_bf16_halves(jnp.dot(hid, wd, preferred_element_type=jnp.float32))


def _experts(xs, tile_expert, n_valid, wg, wu, wd, tm):
    n_slots = xs.shape[0]
    row_block = lambda i, te: (i, 0)
    w_block = lambda i, te: (te[i], 0, 0)
    return pl.pallas_call(
        _experts_kernel,
        grid_spec=pltpu.PrefetchScalarGridSpec(
            num_scalar_prefetch=1,
            grid=(n_valid,),
            in_specs=[pl.BlockSpec((tm, HALF), row_block),
                      pl.BlockSpec((1, D_MODEL, D_EXPERT), w_block, pipeline_mode=pl.Buffered(2)),
                      pl.BlockSpec((1, D_MODEL, D_EXPERT), w_block, pipeline_mode=pl.Buffered(2)),
                      pl.BlockSpec((1, D_EXPERT, D_MODEL), w_block, pipeline_mode=pl.Buffered(2))],
            out_specs=pl.BlockSpec((tm, HALF), row_block),
        ),
        out_shape=jax.ShapeDtypeStruct((n_slots, HALF), jnp.uint32),
        compiler_params=pltpu.CompilerParams(
            dimension_semantics=("arbitrary",), vmem_limit_bytes=VMEM_LIMIT),
        name="experts",
    )(tile_expert, xs, wg, wu, wd)


def _final_kernel(x1_ref, y0_ref, y1_ref, p_ref, gfin_ref, *rest):
    out_ref = rest[-1]
    pt = jnp.transpose(p_ref[...])
    p0 = pt[:, 0:1]
    p1 = pt[:, 1:2]
    a_hi, a_lo = _unpack_bf16_halves(y0_ref[...])
    b_hi, b_lo = _unpack_bf16_halves(y1_ref[...])
    x1 = x1_ref[...]
    x2_hi = x1[:, :HALF] + (p0 * a_hi + p1 * b_hi)
    x2_lo = x1[:, HALF:] + (p0 * a_lo + p1 * b_lo)
    ms = (jnp.sum(x2_hi * x2_hi, axis=-1, keepdims=True)
          + jnp.sum(x2_lo * x2_lo, axis=-1, keepdims=True)) / D_MODEL
    scale = lax.rsqrt(ms + RMS_EPS)
    g = gfin_ref[...]
    out_ref[:, :HALF] = x2_hi * scale * g[:, :HALF]
    out_ref[:, HALF:] = x2_lo * scale * g[:, HALF:]


def _final(x1, yk, p, gfin, row_offset, n_rows, tm, out_rows, out_offset, out_buf=None):
    n_tokens = x1.shape[0]
    first = row_offset // tm
    second = (n_tokens + row_offset) // tm
    out_first = out_offset // tm
    in_specs = [pl.BlockSpec((tm, D_MODEL), lambda i: (first + i, 0)),
                pl.BlockSpec((tm, HALF), lambda i: (first + i, 0)),
                pl.BlockSpec((tm, HALF), lambda i: (second + i, 0)),
                pl.BlockSpec((SUBLANES, tm), lambda i: (0, first + i)),
                pl.BlockSpec((1, D_MODEL), lambda i: (0, 0))]
    args = [x1, yk, yk, p, gfin]
    aliases = {}
    if out_buf is not None:
        in_specs.append(pl.BlockSpec(memory_space=pl.ANY))
        args.append(out_buf)
        aliases = {len(args) - 1: 0}
    return pl.pallas_call(
        _final_kernel,
        grid=(n_rows // tm,),
        in_specs=in_specs,
        out_specs=pl.BlockSpec((tm, D_MODEL), lambda i: (out_first + i, 0)),
        out_shape=jax.ShapeDtypeStruct((out_rows, D_MODEL), jnp.float32),
        input_output_aliases=aliases,
        compiler_params=pltpu.CompilerParams(
            dimension_semantics=("arbitrary",), vmem_limit_bytes=VMEM_LIMIT),
        name="final",
    )(*args)


def _expert_tile(n_tokens):
    mean_rows = TOP_K * n_tokens // N_EXPERTS
    return -(-(mean_rows + 2 * math.isqrt(mean_rows)) // (2 * SUBLANES)) * (2 * SUBLANES)


def _routed_experts(h2p, eid, counts, w_gate, w_up, w_down):
    n_tokens = h2p.shape[0]
    assert n_tokens % (SC_WORKERS * SC_DISPATCH_BATCH) == 0 and n_tokens % PLAN_TILE == 0
    assert (TOP_K * n_tokens) % (SC_WORKERS * SC_BATCH) == 0
    tile = _expert_tile(n_tokens)
    n_tiles = -(-(TOP_K * n_tokens + N_EXPERTS * (tile - 1)) // tile)
    assert n_tiles <= LANES
    pos, table = _plan(eid, counts, tile)
    pos = pos[:TOP_K]
    by_token = jnp.transpose(pos.reshape(TOP_K, SC_WORKERS, -1, SC_DISPATCH_BATCH), (1, 2, 0, 3))
    xs = _sc_dispatch_rows(h2p, by_token, n_tiles * tile)
    ys = _experts(xs, table[0], table[1, 0], w_gate, w_up, w_down, tile)
    return _sc_gather_rows(ys, pos.reshape(SC_WORKERS, -1, SC_BATCH))


def kernel(x_prompt, x_sample, state_conv_a, state_conv_b, g_mix, w_in, conv_a_w, conv_b_w, conv_b_bias,
           ln_g, ln_b, w_out, g_ffn, w_coarse, b_coarse, w_fine, b_fine, w_gate, w_up, w_down, g_final):
    assert g_mix.shape[0] == 1, "single trunk layer"
    batch, seq, _ = x_prompt.shape
    dec_batch, dec_seq, _ = x_sample.shape
    n_prompt = batch * seq
    n_sample = dec_batch * dec_seq
    bf16 = jnp.bfloat16

    wr = jnp.concatenate([
        w_coarse[0], jnp.zeros((D_MODEL, SUBLANES - N_EXPERT_GROUPS), jnp.float32),
        jnp.transpose(w_fine[0], (1, 0, 2)).reshape(D_MODEL, N_EXPERTS),
        jnp.zeros((D_MODEL, LANES - ROUTER_ROWS), jnp.float32)], axis=1)
    wr_hi = wr.astype(bf16)
    wr_lo = (wr - wr_hi.astype(jnp.float32)).astype(bf16)
    wr_both = jnp.concatenate([wr_hi, wr_lo], axis=1)
    br = jnp.concatenate([
        b_coarse[0], jnp.full((SUBLANES - N_EXPERT_GROUPS,), NEG_BIG, jnp.float32),
        b_fine[0].reshape(N_EXPERTS)]).reshape(ROUTER_ROWS, 1)

    params = (g_mix, w_in[0].astype(bf16), conv_a_w, conv_b_w, conv_b_bias, ln_g, ln_b,
              w_out[0].astype(bf16), g_ffn, wr_both, br)

    experts = (w_gate[0], w_up[0], w_down[0])
    gfin = g_final.reshape(1, D_MODEL)

    bufs = _mixer_prompt(x_prompt, params, n_prompt + n_sample, MIXER_TILE, MIXER_TILES_PER_STEP, 0, batch)
    na_p, nb_p, counts = bufs[4:]
    x1, h2p, eid, p, na_s, nb_s, counts = _mixer_sample(
        x_sample, jnp.transpose(state_conv_a[0], (1, 0, 2)), jnp.transpose(state_conv_b[0], (1, 0, 2)),
        params, counts, bufs[:4], n_prompt, nseq=32)
    na_s = jnp.transpose(na_s, (1, 0, 2))
    nb_s = jnp.transpose(nb_s, (1, 0, 2))
    yk = _routed_experts(h2p, eid, counts, *experts)
    y_p = _final(x1, yk, p, gfin, 0, n_prompt, FINAL_TILE, n_prompt, 0)
    y_s = _final(x1, yk, p, gfin, n_prompt, n_sample, FINAL_TILE, n_sample, 0)
    return (y_p.reshape(batch, seq, D_MODEL), y_s.reshape(dec_batch, dec_seq, D_MODEL),
            na_p[None], nb_p[None], na_s[None], nb_s[None])
```

```python
import functools
import math

import jax
import jax.numpy as jnp
from jax import lax
from jax.experimental import pallas as pl
from jax.experimental.pallas import tpu as pltpu
from jax.experimental.pallas import tpu_sc as plsc

D_MODEL = 1024
D_A = 512
D_B = 512
CONV_A = 3
CONV_B = 31
HALO_A = CONV_A - 1
HALO_B = CONV_B - 1
IN_COLS = 3 * D_A + 2 * D_B
N_EXPERT_GROUPS = 4
EXPERTS_PER_GROUP = 8
N_EXPERTS = N_EXPERT_GROUPS * EXPERTS_PER_GROUP
TOP_K = 2
D_EXPERT = D_MODEL // 4
RMS_EPS = 1e-6
LN_EPS = 1e-5

SUBLANES = 8
LANES = 128
PAD_A = SUBLANES
PAD_B = 32
ROUTER_ROWS = SUBLANES + N_EXPERTS
NEG_BIG = -1e30
VMEM_LIMIT = 56 * 1024 * 1024
HALF = D_MODEL // 2
HI_MASK = 0xFFFF0000

SC_CORES = 2
SC_SUBCORES = 16
SC_WORKERS = SC_CORES * SC_SUBCORES
SC_BATCH = 64
SC_DISPATCH_BATCH = 32

PLAN_TILE = 1024
FINAL_TILE = 1024
COMBINE_FIRST_FRACTION = 0.4
MIXER_TILE = 512
MIXER_TILES_PER_STEP = 2
MIXER_ROW_CHUNK = 64
MIXER_FINISH_GROUPS = (4, 4)


def _rms_scale(x):
    return x * lax.rsqrt(jnp.mean(x * x, axis=-1, keepdims=True) + RMS_EPS)


def _pack_bf16_halves(x):
    bits = lax.bitcast_convert_type(x.astype(jnp.bfloat16).astype(jnp.float32), jnp.uint32)
    return bits[:, :HALF] | (bits[:, HALF:] >> 16)


def _unpack_bf16_halves(w):
    hi = lax.bitcast_convert_type(w & jnp.uint32(HI_MASK), jnp.float32)
    lo = lax.bitcast_convert_type(w << 16, jnp.float32)
    return hi, lo


def _route(logits_t):
    rows = logits_t.shape[1]
    iota = lax.broadcasted_iota(jnp.int32, (SUBLANES, rows), 0)
    lc = logits_t[0:SUBLANES]
    cmax = jnp.max(lc, axis=0, keepdims=True)
    grp = jnp.min(jnp.where(lc == cmax, iota, SUBLANES), axis=0, keepdims=True)
    p_grp = 1.0 / jnp.sum(jnp.exp(lc - cmax), axis=0, keepdims=True)
    sel = logits_t[SUBLANES:2 * SUBLANES]
    for g in range(1, N_EXPERT_GROUPS):
        sel = jnp.where(grp == g, logits_t[(g + 1) * SUBLANES:(g + 2) * SUBLANES], sel)
    v1 = jnp.max(sel, axis=0, keepdims=True)
    i1 = jnp.min(jnp.where(sel == v1, iota, SUBLANES), axis=0, keepdims=True)
    sel2 = jnp.where(iota == i1, -jnp.inf, sel)
    v2 = jnp.max(sel2, axis=0, keepdims=True)
    i2 = jnp.min(jnp.where(sel2 == v2, iota, SUBLANES), axis=0, keepdims=True)
    e2 = jnp.exp(v2 - v1)
    den = 1.0 + e2
    p1 = p_grp / den
    p2 = p_grp * e2 / den
    base = grp * EXPERTS_PER_GROUP
    return (base + i1, base + i2), (p1, p2)


def _after_mix(x1, f0, gffn_ref, wr_ref, br_ref, x1_ref, h2p_ref, eid_ref, p_ref, cnt_ref):
    m = x1.shape[0]
    x1_ref[f0:f0 + m, :] = x1
    h2 = _rms_scale(x1) * gffn_ref[...]
    h2_hi = h2.astype(jnp.bfloat16)
    h2p_ref[f0:f0 + m, :] = _pack_bf16_halves(h2)
    h2_lo = (h2 - h2_hi.astype(jnp.float32)).astype(jnp.bfloat16)
    both = jnp.dot(h2_hi, wr_ref[...], preferred_element_type=jnp.float32)
    cross = jnp.dot(h2_lo, wr_ref[:, 0:LANES], preferred_element_type=jnp.float32)
    logits = both[:, 0:LANES] + both[:, LANES:] + cross
    logits_t = jnp.transpose(logits)[0:ROUTER_ROWS] + br_ref[...]
    (e1, e2), (p1, p2) = _route(logits_t)
    iota = lax.broadcasted_iota(jnp.int32, (SUBLANES, m), 0)
    eid_ref[:, f0:f0 + m] = jnp.where(iota == 0, e1, jnp.where(iota == 1, e2, 0))
    p_ref[:, f0:f0 + m] = jnp.where(iota == 0, p1, jnp.where(iota == 1, p2, 0.0))
    experts = lax.broadcasted_iota(jnp.int32, (N_EXPERTS, m), 0)
    routed = jnp.where((experts == e1) | (experts == e2), 1.0, 0.0)
    cnt_ref[...] = cnt_ref[...] + jnp.sum(routed, axis=1, keepdims=True)


def _mixer_prompt_kernel(x_ref, gmix_ref, win_ref, caw_ref, cbw_ref, cbb_ref, lng_ref, lnb_ref,
                         wout_ref, gffn_ref, wr_ref, br_ref,
                         x1_ref, h2p_ref, eid_ref, p_ref, na_ref, nb_ref, cnt_ref,
                         proj_ref, uext_ref, gext_ref, z_ref, ush_ref, gsh_ref, *, tt, tiles, row_chunk):
    t = pl.program_id(1)
    col_chunk = 2 * LANES
    span_b = PAD_B + tt - SUBLANES
    step_rows = tiles * tt

    @pl.when((pl.program_id(0) == 0) & (t == 0))
    def _():
        cnt_ref[...] = jnp.zeros_like(cnt_ref)

    @pl.when(t == 0)
    def _():
        uext_ref[0:PAD_A, :] = jnp.zeros((PAD_A, D_A), jnp.float32)
        gext_ref[0:PAD_B, :] = jnp.zeros((PAD_B, D_B), jnp.float32)

    def window(base_ref, shifted_ref, first_shift, base, off):
        r = off % SUBLANES
        a8 = off - r
        if r == 0:
            return base_ref[base + a8:base + a8 + row_chunk, :]
        return shifted_ref[r - first_shift, a8:a8 + row_chunk, :]

    cbw = cbw_ref[...]
    row_starts = list(range(0, tt, row_chunk))
    a_cols = list(range(0, 3 * D_A, col_chunk))
    groups, done = [], 0
    for size in MIXER_FINISH_GROUPS:
        groups.append(row_starts[done:done + size])
        done += size
    assert done == len(row_starts)

    def head(base):
        h = (_rms_scale(x_ref[0, base:base + tt, :]) * gmix_ref[...]).astype(jnp.bfloat16)
        for c0 in range(0, D_B, col_chunk):
            v_b = jnp.dot(h, win_ref[:, 3 * D_A + c0:3 * D_A + c0 + col_chunk], preferred_element_type=jnp.float32)
            g_b = jnp.dot(h, win_ref[:, 3 * D_A + D_B + c0:3 * D_A + D_B + c0 + col_chunk],
                          preferred_element_type=jnp.float32)
            gext_ref[PAD_B + base:PAD_B + base + tt, c0:c0 + col_chunk] = v_b * jax.nn.sigmoid(g_b)
        return h

    def body(base, h):
        for r in range(1, SUBLANES):
            gsh_ref[r - 1, 0:span_b, :] = gext_ref[base + r:base + r + span_b, :]

        def conv_b_chunk(r0):
            acc_b = None
            for k in range(CONV_B):
                term = window(gext_ref, gsh_ref, 1, base, PAD_B - HALO_B + k + r0) * cbw[k:k + 1, :]
                acc_b = term if acc_b is None else acc_b + term
            zb = acc_b + cbb_ref[...]
            mu = jnp.mean(zb, axis=-1, keepdims=True)
            zc = zb - mu
            var = jnp.mean(zc * zc, axis=-1, keepdims=True)
            y = zc * lax.rsqrt(var + LN_EPS) * lng_ref[...] + lnb_ref[...]
            z_ref[base + r0:base + r0 + row_chunk, D_A:] = (y * jax.nn.sigmoid(y)).astype(jnp.bfloat16)

        def conv_a():
            uext_ref[PAD_A + base:PAD_A + base + tt, :] = proj_ref[:, D_A:2 * D_A] * proj_ref[:, 2 * D_A:3 * D_A]
            for r in range(SUBLANES - HALO_A, SUBLANES):
                ush_ref[r - (SUBLANES - HALO_A), :, :] = uext_ref[base + r:base + r + tt, :]
            caw = caw_ref[...]
            for r0 in row_starts:
                acc_a = None
                for k in range(CONV_A):
                    term = window(uext_ref, ush_ref, SUBLANES - HALO_A, base, PAD_A - HALO_A + k + r0) * caw[k:k + 1, :]
                    acc_a = term if acc_a is None else acc_a + term
                z_ref[base + r0:base + r0 + row_chunk, 0:D_A] = (
                    proj_ref[r0:r0 + row_chunk, 0:D_A] * acc_a).astype(jnp.bfloat16)

        def out_a(rows):
            lo, hi = base + rows[0], base + rows[-1] + row_chunk
            return jnp.dot(z_ref[lo:hi, 0:D_A], wout_ref[0:D_A, :], preferred_element_type=jnp.float32)

        def finish(rows, part_a):
            lo, hi = base + rows[0], base + rows[-1] + row_chunk
            part_b = jnp.dot(z_ref[lo:hi, D_A:], wout_ref[D_A:, :], preferred_element_type=jnp.float32)
            _after_mix(x_ref[0, lo:hi, :] + part_a + part_b, lo,
                       gffn_ref, wr_ref, br_ref, x1_ref, h2p_ref, eid_ref, p_ref, cnt_ref)

        parts_a = []
        for gi, rows in enumerate(groups):
            for i, r0 in enumerate(rows):
                conv_b_chunk(r0)
                if gi == 0:
                    for c0 in a_cols[i * len(a_cols) // len(rows):(i + 1) * len(a_cols) // len(rows)]:
                        proj_ref[:, c0:c0 + col_chunk] = jnp.dot(h, win_ref[:, c0:c0 + col_chunk],
                                                                 preferred_element_type=jnp.float32)
            if gi == 0:
                conv_a()
                parts_a = [out_a(g) for g in groups]
            finish(rows, parts_a[gi])

    heads = [head(ti * tt) for ti in range(tiles)]
    for ti in range(tiles):
        body(ti * tt, heads[ti])

    @pl.when(t == pl.num_programs(1) - 1)
    def _():
        na_ref[0] = uext_ref[PAD_A + step_rows - HALO_A:PAD_A + step_rows, :]
        nb_ref[0] = gext_ref[PAD_B + step_rows - HALO_B:PAD_B + step_rows, :]
    uext_ref[0:PAD_A, :] = uext_ref[step_rows:step_rows + PAD_A, :]
    gext_ref[0:PAD_B, :] = gext_ref[step_rows:step_rows + PAD_B, :]


def _mixer_sample_kernel(x_ref, sa_ref, sb_ref, gmix_ref, win_ref, caw_ref, cbw_ref, cbb_ref, lng_ref, lnb_ref,
                         wout_ref, gffn_ref, wr_ref, br_ref, cnt_in_ref, _x1_in, _h2p_in, _eid_in, _p_in,
                         x1_ref, h2p_ref, eid_ref, p_ref, na_ref, nb_ref, cnt_ref,
                         proj_ref, gnt_ref, unt_ref, gtm_ref, utm_ref, ynt_ref, ant_ref, z_ref, *, nseq, tt):
    rows = nseq * tt
    col_chunk = 2 * LANES

    @pl.when(pl.program_id(0) == 0)
    def _():
        cnt_ref[...] = cnt_in_ref[...]

    x = x_ref[...].reshape(rows, D_MODEL)
    h = (_rms_scale(x) * gmix_ref[...]).astype(jnp.bfloat16)
    for c0 in range(0, D_B, col_chunk):
        v_b = jnp.dot(h, win_ref[:, 3 * D_A + c0:3 * D_A + c0 + col_chunk], preferred_element_type=jnp.float32)
        g_b = jnp.dot(h, win_ref[:, 3 * D_A + D_B + c0:3 * D_A + D_B + c0 + col_chunk],
                      preferred_element_type=jnp.float32)
        g = v_b * jax.nn.sigmoid(g_b)
        for c in range(0, col_chunk, LANES):
            gnt_ref[(c0 + c) // LANES] = g[:, c:c + LANES]
    for c0 in range(0, 3 * D_A, col_chunk):
        proj_ref[:, c0:c0 + col_chunk] = jnp.dot(h, win_ref[:, c0:c0 + col_chunk],
                                                 preferred_element_type=jnp.float32)
    u = proj_ref[:, D_A:2 * D_A] * proj_ref[:, 2 * D_A:3 * D_A]
    for c in range(0, D_A, LANES):
        unt_ref[c // LANES] = u[:, c:c + LANES]

    utm_ref[0:HALO_A] = sa_ref[...]
    gtm_ref[0:HALO_B] = sb_ref[...]
    for t in range(tt):
        for c in range(0, D_A, LANES):
            utm_ref[HALO_A + t, :, c:c + LANES] = unt_ref[c // LANES, pl.ds(t, nseq, stride=tt), :]
        for c in range(0, D_B, LANES):
            gtm_ref[HALO_B + t, :, c:c + LANES] = gnt_ref[c // LANES, pl.ds(t, nseq, stride=tt), :]
    na_ref[...] = utm_ref[tt:tt + HALO_A]
    nb_ref[...] = gtm_ref[tt:tt + HALO_B]

    caw = caw_ref[...]
    cbw = cbw_ref[...]
    for t in range(tt):
        acc_b = None
        for k in range(CONV_B):
            term = gtm_ref[t + k] * cbw[k:k + 1, :]
            acc_b = term if acc_b is None else acc_b + term
        zb = acc_b + cbb_ref[...]
        mu = jnp.mean(zb, axis=-1, keepdims=True)
        zc = zb - mu
        var = jnp.mean(zc * zc, axis=-1, keepdims=True)
        y = zc * lax.rsqrt(var + LN_EPS) * lng_ref[...] + lnb_ref[...]
        y = y * jax.nn.sigmoid(y)
        for c in range(0, D_B, LANES):
            ynt_ref[c // LANES, pl.ds(t, nseq, stride=tt), :] = y[:, c:c + LANES]
        acc_a = None
        for k in range(CONV_A):
            term = utm_ref[t + k] * caw[k:k + 1, :]
            acc_a = term if acc_a is None else acc_a + term
        for c in range(0, D_A, LANES):
            ant_ref[c // LANES, pl.ds(t, nseq, stride=tt), :] = acc_a[:, c:c + LANES]

    for c in range(0, D_A, LANES):
        z_ref[:, c:c + LANES] = (proj_ref[:, c:c + LANES] * ant_ref[c // LANES]).astype(jnp.bfloat16)
    for c in range(0, D_B, LANES):
        z_ref[:, D_A + c:D_A + c + LANES] = ynt_ref[c // LANES].astype(jnp.bfloat16)
    x1 = x + jnp.dot(z_ref[...], wout_ref[...], preferred_element_type=jnp.float32)
    _after_mix(x1, 0, gffn_ref, wr_ref, br_ref, x1_ref, h2p_ref, eid_ref, p_ref, cnt_ref)


def _full(shape):
    return pl.BlockSpec(shape, lambda *_: (0,) * len(shape))


def _mixer_param_specs():
    return [
        _full((1, D_MODEL)),
        _full((D_MODEL, IN_COLS)),
        pl.BlockSpec((None, CONV_A, D_A), lambda *_: (0, 0, 0)),
        pl.BlockSpec((None, CONV_B, D_B), lambda *_: (0, 0, 0)),
        _full((1, D_B)),
        _full((1, D_B)),
        _full((1, D_B)),
        _full((D_MODEL, D_MODEL)),
        _full((1, D_MODEL)),
        _full((D_MODEL, 2 * LANES)),
        _full((ROUTER_ROWS, 1)),
    ]


def _mixer_prompt(x, params, n_tokens_total, tt, tiles, seq_first, batch):
    seq = x.shape[1]
    rows = tiles * tt
    n_t = seq // rows
    tok = lambda b, t: (b * n_t + t, 0)
    lane_tok = lambda b, t: (0, b * n_t + t)
    out_shape = [
        jax.ShapeDtypeStruct((n_tokens_total, D_MODEL), jnp.float32),
        jax.ShapeDtypeStruct((n_tokens_total, HALF), jnp.uint32),
        jax.ShapeDtypeStruct((SUBLANES, n_tokens_total), jnp.int32),
        jax.ShapeDtypeStruct((SUBLANES, n_tokens_total), jnp.float32),
        jax.ShapeDtypeStruct((batch, HALO_A, D_A), jnp.float32),
        jax.ShapeDtypeStruct((batch, HALO_B, D_B), jnp.float32),
        jax.ShapeDtypeStruct((N_EXPERTS, LANES), jnp.float32),
    ]
    out_specs = [
        pl.BlockSpec((rows, D_MODEL), tok),
        pl.BlockSpec((rows, HALF), tok),
        pl.BlockSpec((SUBLANES, rows), lane_tok),
        pl.BlockSpec((SUBLANES, rows), lane_tok),
        pl.BlockSpec((1, HALO_A, D_A), lambda b, t: (b, 0, 0)),
        pl.BlockSpec((1, HALO_B, D_B), lambda b, t: (b, 0, 0)),
        pl.BlockSpec((N_EXPERTS, LANES), lambda b, t: (0, 0)),
    ]
    scratch = [
        pltpu.VMEM((tt, 3 * D_A), jnp.float32),
        pltpu.VMEM((PAD_A + rows, D_A), jnp.float32),
        pltpu.VMEM((PAD_B + rows, D_B), jnp.float32),
        pltpu.VMEM((rows, D_MODEL), jnp.bfloat16),
        pltpu.VMEM((HALO_A, tt, D_A), jnp.float32),
        pltpu.VMEM((SUBLANES - 1, PAD_B + tt - SUBLANES, D_B), jnp.float32),
    ]
    params_specs = [pl.BlockSpec(sp.block_shape, sp.index_map, pipeline_mode=pl.Buffered(1))
                    for sp in _mixer_param_specs()]
    return pl.pallas_call(
        functools.partial(_mixer_prompt_kernel, tt=tt, tiles=tiles, row_chunk=MIXER_ROW_CHUNK),
        grid=(batch, n_t),
        in_specs=[pl.BlockSpec((1, rows, D_MODEL), lambda b, t: (b + seq_first, t, 0))] + params_specs,
        out_specs=out_specs,
        out_shape=out_shape,
        scratch_shapes=scratch,
        compiler_params=pltpu.CompilerParams(
            dimension_semantics=("arbitrary", "arbitrary"), vmem_limit_bytes=VMEM_LIMIT),
        name="mixer_prompt",
    )(x, *params)


def _mixer_sample(x, state_a, state_b, params, counts, bufs, row_offset, nseq):
    batch, tt, _ = x.shape
    rows = nseq * tt
    first = row_offset // rows
    tok = lambda i: (first + i, 0)
    lane_tok = lambda i: (0, first + i)
    x1, h2p, eid, p = bufs
    out_shape = [
        jax.ShapeDtypeStruct(x1.shape, x1.dtype),
        jax.ShapeDtypeStruct(h2p.shape, h2p.dtype),
        jax.ShapeDtypeStruct(eid.shape, eid.dtype),
        jax.ShapeDtypeStruct(p.shape, p.dtype),
        jax.ShapeDtypeStruct((HALO_A, batch, D_A), jnp.float32),
        jax.ShapeDtypeStruct((HALO_B, batch, D_B), jnp.float32),
        jax.ShapeDtypeStruct((N_EXPERTS, LANES), jnp.float32),
    ]
    out_specs = [
        pl.BlockSpec((rows, D_MODEL), tok),
        pl.BlockSpec((rows, HALF), tok),
        pl.BlockSpec((SUBLANES, rows), lane_tok),
        pl.BlockSpec((SUBLANES, rows), lane_tok),
        pl.BlockSpec((HALO_A, nseq, D_A), lambda i: (0, i, 0)),
        pl.BlockSpec((HALO_B, nseq, D_B), lambda i: (0, i, 0)),
        pl.BlockSpec((N_EXPERTS, LANES), lambda i: (0, 0)),
    ]
    any_spec = pl.BlockSpec(memory_space=pl.ANY)
    in_specs = ([pl.BlockSpec((nseq, tt, D_MODEL), lambda i: (i, 0, 0)),
                 pl.BlockSpec((HALO_A, nseq, D_A), lambda i: (0, i, 0)),
                 pl.BlockSpec((HALO_B, nseq, D_B), lambda i: (0, i, 0))]
                + _mixer_param_specs() + [pl.BlockSpec((N_EXPERTS, LANES), lambda i: (0, 0))] + [any_spec] * 4)
    n_in = len(in_specs)
    scratch = [
        pltpu.VMEM((rows, 3 * D_A), jnp.float32),
        pltpu.VMEM((D_B // LANES, rows, LANES), jnp.float32),
        pltpu.VMEM((D_A // LANES, rows, LANES), jnp.float32),
        pltpu.VMEM((HALO_B + tt, nseq, D_B), jnp.float32),
        pltpu.VMEM((HALO_A + tt, nseq, D_A), jnp.float32),
        pltpu.VMEM((D_B // LANES, rows, LANES), jnp.float32),
        pltpu.VMEM((D_A // LANES, rows, LANES), jnp.float32),
        pltpu.VMEM((rows, D_MODEL), jnp.bfloat16),
    ]
    return pl.pallas_call(
        functools.partial(_mixer_sample_kernel, nseq=nseq, tt=tt),
        grid=(batch // nseq,),
        in_specs=in_specs,
        out_specs=out_specs,
        out_shape=out_shape,
        scratch_shapes=scratch,
        input_output_aliases={n_in - 4: 0, n_in - 3: 1, n_in - 2: 2, n_in - 1: 3},
        compiler_params=pltpu.CompilerParams(
            dimension_semantics=("arbitrary",), vmem_limit_bytes=VMEM_LIMIT),
        name="mixer_sample",
    )(x, state_a, state_b, *params, counts, x1, h2p, eid, p)


def _plan_kernel(eid_ref, cnt_ref, pos_ref, table_ref, carry_ref, start_ref, earlier_ref, *, tile):
    i = pl.program_id(0)
    tt = eid_ref.shape[1]

    @pl.when(i == 0)
    def _():
        tiles = jnp.floor((cnt_ref[...] + (tile - 0.5)) * (1.0 / tile))
        below = (lax.broadcasted_iota(jnp.int32, (N_EXPERTS, N_EXPERTS), 0)
                 > lax.broadcasted_iota(jnp.int32, (N_EXPERTS, N_EXPERTS), 1))
        start = jnp.dot(jnp.where(below, 1.0, 0.0).astype(jnp.bfloat16), tiles.astype(jnp.bfloat16),
                        preferred_element_type=jnp.float32) * tile
        start_ref[...] = start
        seg_end = start + tiles * tile
        tile_row = lax.broadcasted_iota(jnp.int32, (N_EXPERTS, LANES), 1).astype(jnp.float32) * tile
        owner = jnp.sum(jnp.where(seg_end <= tile_row, 1.0, 0.0), axis=0, keepdims=True)
        owner = jnp.minimum(owner, N_EXPERTS - 1.0)
        n_used = jnp.sum(tiles, axis=0, keepdims=True)
        row = lax.broadcasted_iota(jnp.int32, (SUBLANES, LANES), 0)
        table_ref[...] = jnp.where(row == 0, owner, jnp.where(row == 1, n_used, 0.0)).astype(jnp.int32)
        earlier = (lax.broadcasted_iota(jnp.int32, (tt, tt), 0)
                   < lax.broadcasted_iota(jnp.int32, (tt, tt), 1))
        earlier_ref[...] = jnp.where(earlier, 1.0, 0.0).astype(jnp.bfloat16)
        carry_ref[...] = jnp.zeros_like(carry_ref)

    eid = eid_ref[...]
    experts = lax.broadcasted_iota(jnp.int32, (N_EXPERTS, tt), 0)
    oh0 = experts == eid[0:1]
    oh1 = experts == eid[1:2]
    oh = jnp.where(oh0 | oh1, 1.0, 0.0)
    within = jnp.dot(oh.astype(jnp.bfloat16), earlier_ref[...], preferred_element_type=jnp.float32)
    slot_of = within + carry_ref[:, 0:1] + start_ref[:, 0:1]
    s0 = jnp.sum(jnp.where(oh0, slot_of, 0.0), axis=0, keepdims=True)
    s1 = jnp.sum(jnp.where(oh1, slot_of, 0.0), axis=0, keepdims=True)
    k = lax.broadcasted_iota(jnp.int32, (SUBLANES, tt), 0)
    pos_ref[...] = jnp.where(k == 0, s0, jnp.where(k == 1, s1, 0.0)).astype(jnp.int32)
    carry_ref[...] = carry_ref[...] + jnp.sum(oh, axis=1, keepdims=True)


def _plan(eid, counts, tile):
    n_tokens = eid.shape[1]
    return pl.pallas_call(
        functools.partial(_plan_kernel, tile=tile),
        grid=(n_tokens // PLAN_TILE,),
        in_specs=[pl.BlockSpec((SUBLANES, PLAN_TILE), lambda i: (0, i)),
                  pl.BlockSpec((N_EXPERTS, LANES), lambda i: (0, 0))],
        out_specs=[pl.BlockSpec((SUBLANES, PLAN_TILE), lambda i: (0, i)),
                   pl.BlockSpec((SUBLANES, LANES), lambda i: (0, 0))],
        out_shape=[jax.ShapeDtypeStruct((SUBLANES, n_tokens), jnp.int32),
                   jax.ShapeDtypeStruct((SUBLANES, LANES), jnp.int32)],
        scratch_shapes=[pltpu.VMEM((N_EXPERTS, LANES), jnp.float32),
                        pltpu.VMEM((N_EXPERTS, LANES), jnp.float32),
                        pltpu.VMEM((PLAN_TILE, PLAN_TILE), jnp.bfloat16)],
        compiler_params=pltpu.CompilerParams(dimension_semantics=("arbitrary",)),
        name="route_plan",
    )(eid, counts)


def _sc_mesh():
    return plsc.VectorSubcoreMesh(core_axis_name="c", subcore_axis_name="s")


def _sc_dispatch_rows(table, pos, n_out):
    n_workers, n_batches, top_k, batch = pos.shape
    n_rows, words = table.shape
    assert n_workers * n_batches * batch == n_rows and n_batches >= 2

    @functools.partial(
        pl.kernel, mesh=_sc_mesh(),
        out_type=jax.ShapeDtypeStruct((n_out, words), table.dtype),
        scratch_types=[pltpu.VMEM((n_batches, top_k, batch), jnp.int32),
                       pltpu.VMEM((2, batch, words), table.dtype),
                       pltpu.SemaphoreType.DMA((2,)),
                       pltpu.SemaphoreType.DMA((2, top_k))],
    )
    def dispatch(table_hbm, pos_hbm, out_hbm, idx_v, rows_v, sem_in, sem_out):
        worker = lax.axis_index("s") * SC_CORES + lax.axis_index("c")
        pltpu.sync_copy(pos_hbm.at[worker], idx_v)

        def read(b):
            src = table_hbm.at[pl.ds((worker * n_batches + b) * batch, batch)]
            return pltpu.async_copy(src, rows_v.at[b % 2], sem_in.at[b % 2])

        def write(b):
            return [pltpu.async_copy(rows_v.at[b % 2], out_hbm.at[idx_v.at[b, k]], sem_out.at[b % 2, k])
                    for k in range(top_k)]

        reads = {0: read(0)}
        writes = {}
        for b in range(n_batches):
            reads[b].wait()
            if b + 1 < n_batches:
                if b >= 1:
                    for w in writes[b - 1]:
                        w.wait()
                reads[b + 1] = read(b + 1)
            writes[b] = write(b)
        for b in (n_batches - 2, n_batches - 1):
            for w in writes[b]:
                w.wait()

    return dispatch(table, pos)


def _sc_gather_rows(table, idx):
    n_workers, n_batches, batch = idx.shape
    words = table.shape[1]
    assert n_batches >= 2

    @functools.partial(
        pl.kernel, mesh=_sc_mesh(),
        out_type=jax.ShapeDtypeStruct((n_workers * n_batches * batch, words), table.dtype),
        scratch_types=[pltpu.VMEM((n_batches, batch), jnp.int32),
                       pltpu.VMEM((2, batch, words), table.dtype),
                       pltpu.SemaphoreType.DMA((2,)),
                       pltpu.SemaphoreType.DMA((2,))],
    )
    def gather(table_hbm, idx_hbm, out_hbm, idx_v, rows_v, sem_in, sem_out):
        worker = lax.axis_index("s") * SC_CORES + lax.axis_index("c")
        pltpu.sync_copy(idx_hbm.at[worker], idx_v)

        def read(b):
            return pltpu.async_copy(table_hbm.at[idx_v.at[b]], rows_v.at[b % 2], sem_in.at[b % 2])

        def write(b):
            dst = out_hbm.at[pl.ds((worker * n_batches + b) * batch, batch)]
            return pltpu.async_copy(rows_v.at[b % 2], dst, sem_out.at[b % 2])

        reads = {0: read(0)}
        writes = {}
        for b in range(n_batches):
            reads[b].wait()
            if b + 1 < n_batches:
                if b >= 1:
                    writes[b - 1].wait()
                reads[b + 1] = read(b + 1)
            writes[b] = write(b)
        writes[n_batches - 2].wait()
        writes[n_batches - 1].wait()

    return gather(table, idx)


def _experts_kernel(te_ref, xs_ref, wg_ref, wu_ref, wd_ref, ys_ref):
    hi, lo = _unpack_bf16_halves(xs_ref[...])
    x = jnp.concatenate([hi.astype(jnp.bfloat16), lo.astype(jnp.bfloat16)], axis=1)
    w_gate_up = jnp.concatenate([wg_ref[0].astype(jnp.bfloat16), wu_ref[0].astype(jnp.bfloat16)], axis=1)
    wd = wd_ref[0].astype(jnp.bfloat16)
    gate_up = jnp.dot(x, w_gate_up, preferred_element_type=jnp.float32)
    gate = gate_up[:, :D_EXPERT]
    up = gate_up[:, D_EXPERT:]
    hid = (gate * jax.nn.sigmoid(gate) * up).astype(jnp.bfloat16)
    ys_ref[...] = _pack_bf16_halves(jnp.dot(hid, wd, preferred_element_type=jnp.float32))


def _experts(xs, tile_expert, n_valid, wg, wu, wd, tm):
    n_slots = xs.shape[0]
    row_block = lambda i, te: (i, 0)
    w_block = lambda i, te: (te[i], 0, 0)
    return pl.pallas_call(
        _experts_kernel,
        grid_spec=pltpu.PrefetchScalarGridSpec(
            num_scalar_prefetch=1,
            grid=(n_valid,),
            in_specs=[pl.BlockSpec((tm, HALF), row_block),
                      pl.BlockSpec((1, D_MODEL, D_EXPERT), w_block, pipeline_mode=pl.Buffered(2)),
                      pl.BlockSpec((1, D_MODEL, D_EXPERT), w_block, pipeline_mode=pl.Buffered(2)),
                      pl.BlockSpec((1, D_EXPERT, D_MODEL), w_block, pipeline_mode=pl.Buffered(2))],
            out_specs=pl.BlockSpec((tm, HALF), row_block),
        ),
        out_shape=jax.ShapeDtypeStruct((n_slots, HALF), jnp.uint32),
        compiler_params=pltpu.CompilerParams(
            dimension_semantics=("arbitrary",), vmem_limit_bytes=VMEM_LIMIT),
        name="experts",
    )(tile_expert, xs, wg, wu, wd)


def _final_kernel(x1_ref, y0_ref, y1_ref, p_ref, gfin_ref, *rest):
    out_ref = rest[-1]
    pt = jnp.transpose(p_ref[...])
    p0 = pt[:, 0:1]
    p1 = pt[:, 1:2]
    a_hi, a_lo = _unpack_bf16_halves(y0_ref[...])
    b_hi, b_lo = _unpack_bf16_halves(y1_ref[...])
    x1 = x1_ref[...]
    x2_hi = x1[:, :HALF] + (p0 * a_hi + p1 * b_hi)
    x2_lo = x1[:, HALF:] + (p0 * a_lo + p1 * b_lo)
    ms = (jnp.sum(x2_hi * x2_hi, axis=-1, keepdims=True)
          + jnp.sum(x2_lo * x2_lo, axis=-1, keepdims=True)) / D_MODEL
    scale = lax.rsqrt(ms + RMS_EPS)
    g = gfin_ref[...]
    out_ref[:, :HALF] = x2_hi * scale * g[:, :HALF]
    out_ref[:, HALF:] = x2_lo * scale * g[:, HALF:]


def _final(x1, yk, p, gfin, row_offset, n_rows, yk_offset, tm, out_rows, out_offset, out_buf=None):
    yk_tokens = yk.shape[0] // TOP_K
    assert row_offset % tm == 0 and yk_offset % tm == 0 and yk_tokens % tm == 0 and out_offset % tm == 0
    first = row_offset // tm
    slot0 = yk_offset // tm
    slot1 = (yk_tokens + yk_offset) // tm
    out_first = out_offset // tm
    in_specs = [pl.BlockSpec((tm, D_MODEL), lambda i: (first + i, 0)),
                pl.BlockSpec((tm, HALF), lambda i: (slot0 + i, 0)),
                pl.BlockSpec((tm, HALF), lambda i: (slot1 + i, 0)),
                pl.BlockSpec((SUBLANES, tm), lambda i: (0, first + i)),
                pl.BlockSpec((1, D_MODEL), lambda i: (0, 0))]
    args = [x1, yk, yk, p, gfin]
    aliases = {}
    if out_buf is not None:
        in_specs.append(pl.BlockSpec(memory_space=pl.ANY))
        args.append(out_buf)
        aliases = {len(args) - 1: 0}
    return pl.pallas_call(
        _final_kernel,
        grid=(n_rows // tm,),
        in_specs=in_specs,
        out_specs=pl.BlockSpec((tm, D_MODEL), lambda i: (out_first + i, 0)),
        out_shape=jax.ShapeDtypeStruct((out_rows, D_MODEL), jnp.float32),
        input_output_aliases=aliases,
        compiler_params=pltpu.CompilerParams(
            dimension_semantics=("arbitrary",), vmem_limit_bytes=VMEM_LIMIT),
        name="final",
    )(*args)


def _expert_tile(n_tokens):
    mean_rows = TOP_K * n_tokens // N_EXPERTS
    return -(-(mean_rows + 2 * math.isqrt(mean_rows)) // (2 * SUBLANES)) * (2 * SUBLANES)


def _routed_experts(h2p, eid, counts, w_gate, w_up, w_down):
    n_tokens = h2p.shape[0]
    assert n_tokens % (SC_WORKERS * SC_DISPATCH_BATCH) == 0 and n_tokens % PLAN_TILE == 0
    tile = _expert_tile(n_tokens)
    n_tiles = -(-(TOP_K * n_tokens + N_EXPERTS * (tile - 1)) // tile)
    assert n_tiles <= LANES
    pos, table = _plan(eid, counts, tile)
    pos = pos[:TOP_K]
    by_token = jnp.transpose(pos.reshape(TOP_K, SC_WORKERS, -1, SC_DISPATCH_BATCH), (1, 2, 0, 3))
    xs = _sc_dispatch_rows(h2p, by_token, n_tiles * tile)
    ys = _experts(xs, table[0], table[1, 0], w_gate, w_up, w_down, tile)
    return ys, pos


def _combine(x1, ys, pos, p, gfin, n_prompt):
    n_tokens = x1.shape[0]
    n_sample = n_tokens - n_prompt
    tm = FINAL_TILE
    split = round(n_tokens // tm * COMBINE_FIRST_FRACTION) * tm
    assert 0 < split <= n_prompt and n_prompt % tm == 0 and n_tokens % tm == 0
    gathered = []
    for lo, hi in ((0, split), (split, n_tokens)):
        assert (TOP_K * (hi - lo)) % (SC_WORKERS * SC_BATCH) == 0
        gathered.append(_sc_gather_rows(ys, pos[:, lo:hi].reshape(SC_WORKERS, -1, SC_BATCH)))
    yk_a, yk_b = gathered
    y_p = _final(x1, yk_a, p, gfin, 0, split, 0, tm, n_prompt, 0)
    if split < n_prompt:
        y_p = _final(x1, yk_b, p, gfin, split, n_prompt - split, 0, tm, n_prompt, split, out_buf=y_p)
    y_s = _final(x1, yk_b, p, gfin, n_prompt, n_sample, n_prompt - split, tm, n_sample, 0)
    return y_p, y_s


def kernel(x_prompt, x_sample, state_conv_a, state_conv_b, g_mix, w_in, conv_a_w, conv_b_w, conv_b_bias,
           ln_g, ln_b, w_out, g_ffn, w_coarse, b_coarse, w_fine, b_fine, w_gate, w_up, w_down, g_final):
    assert g_mix.shape[0] == 1, "single trunk layer"
    batch, seq, _ = x_prompt.shape
    dec_batch, dec_seq, _ = x_sample.shape
    n_prompt = batch * seq
    n_sample = dec_batch * dec_seq
    bf16 = jnp.bfloat16

    wr = jnp.concatenate([
        w_coarse[0], jnp.zeros((D_MODEL, SUBLANES - N_EXPERT_GROUPS), jnp.float32),
        jnp.transpose(w_fine[0], (1, 0, 2)).reshape(D_MODEL, N_EXPERTS),
        jnp.zeros((D_MODEL, LANES - ROUTER_ROWS), jnp.float32)], axis=1)
    wr_hi = wr.astype(bf16)
    wr_lo = (wr - wr_hi.astype(jnp.float32)).astype(bf16)
    wr_both = jnp.concatenate([wr_hi, wr_lo], axis=1)
    br = jnp.concatenate([
        b_coarse[0], jnp.full((SUBLANES - N_EXPERT_GROUPS,), NEG_BIG, jnp.float32),
        b_fine[0].reshape(N_EXPERTS)]).reshape(ROUTER_ROWS, 1)

    params = (g_mix, w_in[0].astype(bf16), conv_a_w, conv_b_w, conv_b_bias, ln_g, ln_b,
              w_out[0].astype(bf16), g_ffn, wr_both, br)

    experts = (w_gate[0], w_up[0], w_down[0])
    gfin = g_final.reshape(1, D_MODEL)

    bufs = _mixer_prompt(x_prompt, params, n_prompt + n_sample, MIXER_TILE, MIXER_TILES_PER_STEP, 0, batch)
    na_p, nb_p, counts = bufs[4:]
    x1, h2p, eid, p, na_s, nb_s, counts = _mixer_sample(
        x_sample, jnp.transpose(state_conv_a[0], (1, 0, 2)), jnp.transpose(state_conv_b[0], (1, 0, 2)),
        params, counts, bufs[:4], n_prompt, nseq=32)
    na_s = jnp.transpose(na_s, (1, 0, 2))
    nb_s = jnp.transpose(nb_s, (1, 0, 2))
    ys, pos = _routed_experts(h2p, eid, counts, *experts)
    y_p, y_s = _combine(x1, ys, pos, p, gfin, n_prompt)
    return (y_p.reshape(batch, seq, D_MODEL), y_s.reshape(dec_batch, dec_seq, D_MODEL),
            na_p[None], nb_p[None], na_s[None], nb_s[None])
```

```python
import functools
import math

import jax
import jax.numpy as jnp
from jax import lax
from jax.experimental import pallas as pl
from jax.experimental.pallas import tpu as pltpu
from jax.experimental.pallas import tpu_sc as plsc

D_MODEL = 1024
D_A = 512
D_B = 512
CONV_A = 3
CONV_B = 31
HALO_A = CONV_A - 1
HALO_B = CONV_B - 1
IN_COLS = 3 * D_A + 2 * D_B
N_EXPERT_GROUPS = 4
EXPERTS_PER_GROUP = 8
N_EXPERTS = N_EXPERT_GROUPS * EXPERTS_PER_GROUP
TOP_K = 2
D_EXPERT = D_MODEL // 4
RMS_EPS = 1e-6
LN_EPS = 1e-5

SUBLANES = 8
LANES = 128
PAD_A = SUBLANES
PAD_B = 32
ROUTER_ROWS = SUBLANES + N_EXPERTS
NEG_BIG = -1e30
VMEM_LIMIT = 56 * 1024 * 1024
HALF = D_MODEL // 2
HI_MASK = 0xFFFF0000

SC_CORES = 2
SC_SUBCORES = 16
SC_WORKERS = SC_CORES * SC_SUBCORES
SC_BATCH = 64
SC_DISPATCH_BATCH = 32

PLAN_TILE = 1024
FINAL_TILE = 1024
MIXER_TILE = 512
MIXER_TILES_PER_STEP = 2
MIXER_ROW_CHUNK = 64
MIXER_FINISH_GROUPS = (4, 4)


def _rms_scale(x):
    return x * lax.rsqrt(jnp.mean(x * x, axis=-1, keepdims=True) + RMS_EPS)


def _pack_bf16_halves(x):
    bits = lax.bitcast_convert_type(x.astype(jnp.bfloat16).astype(jnp.float32), jnp.uint32)
    return bits[:, :HALF] | (bits[:, HALF:] >> 16)


def _unpack_bf16_halves(w):
    hi = lax.bitcast_convert_type(w & jnp.uint32(HI_MASK), jnp.float32)
    lo = lax.bitcast_convert_type(w << 16, jnp.float32)
    return hi, lo


def _route(logits_t):
    rows = logits_t.shape[1]
    iota = lax.broadcasted_iota(jnp.int32, (SUBLANES, rows), 0)
    lc = logits_t[0:SUBLANES]
    cmax = jnp.max(lc, axis=0, keepdims=True)
    grp = jnp.min(jnp.where(lc == cmax, iota, SUBLANES), axis=0, keepdims=True)
    p_grp = 1.0 / jnp.sum(jnp.exp(lc - cmax), axis=0, keepdims=True)
    sel = logits_t[SUBLANES:2 * SUBLANES]
    for g in range(1, N_EXPERT_GROUPS):
        sel = jnp.where(grp == g, logits_t[(g + 1) * SUBLANES:(g + 2) * SUBLANES], sel)
    v1 = jnp.max(sel, axis=0, keepdims=True)
    i1 = jnp.min(jnp.where(sel == v1, iota, SUBLANES), axis=0, keepdims=True)
    sel2 = jnp.where(iota == i1, -jnp.inf, sel)
    v2 = jnp.max(sel2, axis=0, keepdims=True)
    i2 = jnp.min(jnp.where(sel2 == v2, iota, SUBLANES), axis=0, keepdims=True)
    e2 = jnp.exp(v2 - v1)
    den = 1.0 + e2
    p1 = p_grp / den
    p2 = p_grp * e2 / den
    base = grp * EXPERTS_PER_GROUP
    return (base + i1, base + i2), (p1, p2)


def _after_mix(x1, f0, gffn_ref, wr_ref, br_ref, x1_ref, h2p_ref, eid_ref, p_ref, cnt_ref):
    m = x1.shape[0]
    x1_ref[f0:f0 + m, :] = x1
    h2 = _rms_scale(x1) * gffn_ref[...]
    h2_hi = h2.astype(jnp.bfloat16)
    h2p_ref[f0:f0 + m, :] = _pack_bf16_halves(h2)
    h2_lo = (h2 - h2_hi.astype(jnp.float32)).astype(jnp.bfloat16)
    both = jnp.dot(h2_hi, wr_ref[...], preferred_element_type=jnp.float32)
    cross = jnp.dot(h2_lo, wr_ref[:, 0:LANES], preferred_element_type=jnp.float32)
    logits = both[:, 0:LANES] + both[:, LANES:] + cross
    logits_t = jnp.transpose(logits)[0:ROUTER_ROWS] + br_ref[...]
    (e1, e2), (p1, p2) = _route(logits_t)
    iota = lax.broadcasted_iota(jnp.int32, (SUBLANES, m), 0)
    eid_ref[:, f0:f0 + m] = jnp.where(iota == 0, e1, jnp.where(iota == 1, e2, 0))
    p_ref[:, f0:f0 + m] = jnp.where(iota == 0, p1, jnp.where(iota == 1, p2, 0.0))
    experts = lax.broadcasted_iota(jnp.int32, (N_EXPERTS, m), 0)
    routed = jnp.where((experts == e1) | (experts == e2), 1.0, 0.0)
    cnt_ref[...] = cnt_ref[...] + jnp.sum(routed, axis=1, keepdims=True)


def _mixer_prompt_kernel(x_ref, gmix_ref, win_ref, caw_ref, cbw_ref, cbb_ref, lng_ref, lnb_ref,
                         wout_ref, gffn_ref, wr_ref, br_ref,
                         x1_ref, h2p_ref, eid_ref, p_ref, na_ref, nb_ref, cnt_ref,
                         proj_ref, uext_ref, gext_ref, z_ref, ush_ref, gsh_ref, *, tt, tiles, row_chunk):
    t = pl.program_id(1)
    col_chunk = 2 * LANES
    span_b = PAD_B + tt - SUBLANES
    step_rows = tiles * tt

    @pl.when((pl.program_id(0) == 0) & (t == 0))
    def _():
        cnt_ref[...] = jnp.zeros_like(cnt_ref)

    @pl.when(t == 0)
    def _():
        uext_ref[0:PAD_A, :] = jnp.zeros((PAD_A, D_A), jnp.float32)
        gext_ref[0:PAD_B, :] = jnp.zeros((PAD_B, D_B), jnp.float32)

    def window(base_ref, shifted_ref, first_shift, base, off):
        r = off % SUBLANES
        a8 = off - r
        if r == 0:
            return base_ref[base + a8:base + a8 + row_chunk, :]
        return shifted_ref[r - first_shift, a8:a8 + row_chunk, :]

    cbw = cbw_ref[...]
    row_starts = list(range(0, tt, row_chunk))
    a_cols = list(range(0, 3 * D_A, col_chunk))
    groups, done = [], 0
    for size in MIXER_FINISH_GROUPS:
        groups.append(row_starts[done:done + size])
        done += size
    assert done == len(row_starts)

    def head(base):
        h = (_rms_scale(x_ref[0, base:base + tt, :]) * gmix_ref[...]).astype(jnp.bfloat16)
        for c0 in range(0, D_B, col_chunk):
            v_b = jnp.dot(h, win_ref[:, 3 * D_A + c0:3 * D_A + c0 + col_chunk], preferred_element_type=jnp.float32)
            g_b = jnp.dot(h, win_ref[:, 3 * D_A + D_B + c0:3 * D_A + D_B + c0 + col_chunk],
                          preferred_element_type=jnp.float32)
            gext_ref[PAD_B + base:PAD_B + base + tt, c0:c0 + col_chunk] = v_b * jax.nn.sigmoid(g_b)
        return h

    def body(base, h):
        for r in range(1, SUBLANES):
            gsh_ref[r - 1, 0:span_b, :] = gext_ref[base + r:base + r + span_b, :]

        def conv_b_chunk(r0):
            acc_b = None
            for k in range(CONV_B):
                term = window(gext_ref, gsh_ref, 1, base, PAD_B - HALO_B + k + r0) * cbw[k:k + 1, :]
                acc_b = term if acc_b is None else acc_b + term
            zb = acc_b + cbb_ref[...]
            mu = jnp.mean(zb, axis=-1, keepdims=True)
            zc = zb - mu
            var = jnp.mean(zc * zc, axis=-1, keepdims=True)
            y = zc * lax.rsqrt(var + LN_EPS) * lng_ref[...] + lnb_ref[...]
            z_ref[base + r0:base + r0 + row_chunk, D_A:] = (y * jax.nn.sigmoid(y)).astype(jnp.bfloat16)

        def conv_a():
            uext_ref[PAD_A + base:PAD_A + base + tt, :] = proj_ref[:, D_A:2 * D_A] * proj_ref[:, 2 * D_A:3 * D_A]
            for r in range(SUBLANES - HALO_A, SUBLANES):
                ush_ref[r - (SUBLANES - HALO_A), :, :] = uext_ref[base + r:base + r + tt, :]
            caw = caw_ref[...]
            for r0 in row_starts:
                acc_a = None
                for k in range(CONV_A):
                    term = window(uext_ref, ush_ref, SUBLANES - HALO_A, base, PAD_A - HALO_A + k + r0) * caw[k:k + 1, :]
                    acc_a = term if acc_a is None else acc_a + term
                z_ref[base + r0:base + r0 + row_chunk, 0:D_A] = (
                    proj_ref[r0:r0 + row_chunk, 0:D_A] * acc_a).astype(jnp.bfloat16)

        def out_a(rows):
            lo, hi = base + rows[0], base + rows[-1] + row_chunk
            return jnp.dot(z_ref[lo:hi, 0:D_A], wout_ref[0:D_A, :], preferred_element_type=jnp.float32)

        def finish(rows, part_a):
            lo, hi = base + rows[0], base + rows[-1] + row_chunk
            part_b = jnp.dot(z_ref[lo:hi, D_A:], wout_ref[D_A:, :], preferred_element_type=jnp.float32)
            _after_mix(x_ref[0, lo:hi, :] + part_a + part_b, lo,
                       gffn_ref, wr_ref, br_ref, x1_ref, h2p_ref, eid_ref, p_ref, cnt_ref)

        parts_a = []
        for gi, rows in enumerate(groups):
            for i, r0 in enumerate(rows):
                conv_b_chunk(r0)
                if gi == 0:
                    for c0 in a_cols[i * len(a_cols) // len(rows):(i + 1) * len(a_cols) // len(rows)]:
                        proj_ref[:, c0:c0 + col_chunk] = jnp.dot(h, win_ref[:, c0:c0 + col_chunk],
                                                                 preferred_element_type=jnp.float32)
            if gi == 0:
                conv_a()
                parts_a = [out_a(g) for g in groups]
            finish(rows, parts_a[gi])

    heads = [head(ti * tt) for ti in range(tiles)]
    for ti in range(tiles):
        body(ti * tt, heads[ti])

    @pl.when(t == pl.num_programs(1) - 1)
    def _():
        na_ref[0] = uext_ref[PAD_A + step_rows - HALO_A:PAD_A + step_rows, :]
        nb_ref[0] = gext_ref[PAD_B + step_rows - HALO_B:PAD_B + step_rows, :]
    uext_ref[0:PAD_A, :] = uext_ref[step_rows:step_rows + PAD_A, :]
    gext_ref[0:PAD_B, :] = gext_ref[step_rows:step_rows + PAD_B, :]


def _mixer_sample_kernel(x_ref, sa_ref, sb_ref, gmix_ref, win_ref, caw_ref, cbw_ref, cbb_ref, lng_ref, lnb_ref,
                         wout_ref, gffn_ref, wr_ref, br_ref, cnt_in_ref, _x1_in, _h2p_in, _eid_in, _p_in,
                         x1_ref, h2p_ref, eid_ref, p_ref, na_ref, nb_ref, cnt_ref,
                         proj_ref, gnt_ref, unt_ref, gtm_ref, utm_ref, ynt_ref, ant_ref, z_ref, *, nseq, tt):
    rows = nseq * tt
    col_chunk = 2 * LANES

    @pl.when(pl.program_id(0) == 0)
    def _():
        cnt_ref[...] = cnt_in_ref[...]

    x = x_ref[...].reshape(rows, D_MODEL)
    h = (_rms_scale(x) * gmix_ref[...]).astype(jnp.bfloat16)
    for c0 in range(0, D_B, col_chunk):
        v_b = jnp.dot(h, win_ref[:, 3 * D_A + c0:3 * D_A + c0 + col_chunk], preferred_element_type=jnp.float32)
        g_b = jnp.dot(h, win_ref[:, 3 * D_A + D_B + c0:3 * D_A + D_B + c0 + col_chunk],
                      preferred_element_type=jnp.float32)
        g = v_b * jax.nn.sigmoid(g_b)
        for c in range(0, col_chunk, LANES):
            gnt_ref[(c0 + c) // LANES] = g[:, c:c + LANES]
    for c0 in range(0, 3 * D_A, col_chunk):
        proj_ref[:, c0:c0 + col_chunk] = jnp.dot(h, win_ref[:, c0:c0 + col_chunk],
                                                 preferred_element_type=jnp.float32)
    u = proj_ref[:, D_A:2 * D_A] * proj_ref[:, 2 * D_A:3 * D_A]
    for c in range(0, D_A, LANES):
        unt_ref[c // LANES] = u[:, c:c + LANES]

    utm_ref[0:HALO_A] = sa_ref[...]
    gtm_ref[0:HALO_B] = sb_ref[...]
    for t in range(tt):
        for c in range(0, D_A, LANES):
            utm_ref[HALO_A + t, :, c:c + LANES] = unt_ref[c // LANES, pl.ds(t, nseq, stride=tt), :]
        for c in range(0, D_B, LANES):
            gtm_ref[HALO_B + t, :, c:c + LANES] = gnt_ref[c // LANES, pl.ds(t, nseq, stride=tt), :]
    na_ref[...] = utm_ref[tt:tt + HALO_A]
    nb_ref[...] = gtm_ref[tt:tt + HALO_B]

    caw = caw_ref[...]
    cbw = cbw_ref[...]
    for t in range(tt):
        acc_b = None
        for k in range(CONV_B):
            term = gtm_ref[t + k] * cbw[k:k + 1, :]
            acc_b = term if acc_b is None else acc_b + term
        zb = acc_b + cbb_ref[...]
        mu = jnp.mean(zb, axis=-1, keepdims=True)
        zc = zb - mu
        var = jnp.mean(zc * zc, axis=-1, keepdims=True)
        y = zc * lax.rsqrt(var + LN_EPS) * lng_ref[...] + lnb_ref[...]
        y = y * jax.nn.sigmoid(y)
        for c in range(0, D_B, LANES):
            ynt_ref[c // LANES, pl.ds(t, nseq, stride=tt), :] = y[:, c:c + LANES]
        acc_a = None
        for k in range(CONV_A):
            term = utm_ref[t + k] * caw[k:k + 1, :]
            acc_a = term if acc_a is None else acc_a + term
        for c in range(0, D_A, LANES):
            ant_ref[c // LANES, pl.ds(t, nseq, stride=tt), :] = acc_a[:, c:c + LANES]

    for c in range(0, D_A, LANES):
        z_ref[:, c:c + LANES] = (proj_ref[:, c:c + LANES] * ant_ref[c // LANES]).astype(jnp.bfloat16)
    for c in range(0, D_B, LANES):
        z_ref[:, D_A + c:D_A + c + LANES] = ynt_ref[c // LANES].astype(jnp.bfloat16)
    x1 = x + jnp.dot(z_ref[...], wout_ref[...], preferred_element_type=jnp.float32)
    _after_mix(x1, 0, gffn_ref, wr_ref, br_ref, x1_ref, h2p_ref, eid_ref, p_ref, cnt_ref)


def _full(shape):
    return pl.BlockSpec(shape, lambda *_: (0,) * len(shape))


def _mixer_param_specs():
    return [
        _full((1, D_MODEL)),
        _full((D_MODEL, IN_COLS)),
        pl.BlockSpec((None, CONV_A, D_A), lambda *_: (0, 0, 0)),
        pl.BlockSpec((None, CONV_B, D_B), lambda *_: (0, 0, 0)),
        _full((1, D_B)),
        _full((1, D_B)),
        _full((1, D_B)),
        _full((D_MODEL, D_MODEL)),
        _full((1, D_MODEL)),
        _full((D_MODEL, 2 * LANES)),
        _full((ROUTER_ROWS, 1)),
    ]


def _mixer_prompt(x, params, n_tokens_total, tt, tiles, seq_first, batch):
    seq = x.shape[1]
    rows = tiles * tt
    n_t = seq // rows
    tok = lambda b, t: (b * n_t + t, 0)
    lane_tok = lambda b, t: (0, b * n_t + t)
    out_shape = [
        jax.ShapeDtypeStruct((n_tokens_total, D_MODEL), jnp.float32),
        jax.ShapeDtypeStruct((n_tokens_total, HALF), jnp.uint32),
        jax.ShapeDtypeStruct((SUBLANES, n_tokens_total), jnp.int32),
        jax.ShapeDtypeStruct((SUBLANES, n_tokens_total), jnp.float32),
        jax.ShapeDtypeStruct((batch, HALO_A, D_A), jnp.float32),
        jax.ShapeDtypeStruct((batch, HALO_B, D_B), jnp.float32),
        jax.ShapeDtypeStruct((N_EXPERTS, LANES), jnp.float32),
    ]
    out_specs = [
        pl.BlockSpec((rows, D_MODEL), tok),
        pl.BlockSpec((rows, HALF), tok),
        pl.BlockSpec((SUBLANES, rows), lane_tok),
        pl.BlockSpec((SUBLANES, rows), lane_tok),
        pl.BlockSpec((1, HALO_A, D_A), lambda b, t: (b, 0, 0)),
        pl.BlockSpec((1, HALO_B, D_B), lambda b, t: (b, 0, 0)),
        pl.BlockSpec((N_EXPERTS, LANES), lambda b, t: (0, 0)),
    ]
    scratch = [
        pltpu.VMEM((tt, 3 * D_A), jnp.float32),
        pltpu.VMEM((PAD_A + rows, D_A), jnp.float32),
        pltpu.VMEM((PAD_B + rows, D_B), jnp.float32),
        pltpu.VMEM((rows, D_MODEL), jnp.bfloat16),
        pltpu.VMEM((HALO_A, tt, D_A), jnp.float32),
        pltpu.VMEM((SUBLANES - 1, PAD_B + tt - SUBLANES, D_B), jnp.float32),
    ]
    params_specs = [pl.BlockSpec(sp.block_shape, sp.index_map, pipeline_mode=pl.Buffered(1))
                    for sp in _mixer_param_specs()]
    return pl.pallas_call(
        functools.partial(_mixer_prompt_kernel, tt=tt, tiles=tiles, row_chunk=MIXER_ROW_CHUNK),
        grid=(batch, n_t),
        in_specs=[pl.BlockSpec((1, rows, D_MODEL), lambda b, t: (b + seq_first, t, 0))] + params_specs,
        out_specs=out_specs,
        out_shape=out_shape,
        scratch_shapes=scratch,
        compiler_params=pltpu.CompilerParams(
            dimension_semantics=("arbitrary", "arbitrary"), vmem_limit_bytes=VMEM_LIMIT),
        name="mixer_prompt",
    )(x, *params)


def _mixer_sample(x, state_a, state_b, params, counts, bufs, row_offset, nseq):
    batch, tt, _ = x.shape
    rows = nseq * tt
    first = row_offset // rows
    tok = lambda i: (first + i, 0)
    lane_tok = lambda i: (0, first + i)
    x1, h2p, eid, p = bufs
    out_shape = [
        jax.ShapeDtypeStruct(x1.shape, x1.dtype),
        jax.ShapeDtypeStruct(h2p.shape, h2p.dtype),
        jax.ShapeDtypeStruct(eid.shape, eid.dtype),
        jax.ShapeDtypeStruct(p.shape, p.dtype),
        jax.ShapeDtypeStruct((HALO_A, batch, D_A), jnp.float32),
        jax.ShapeDtypeStruct((HALO_B, batch, D_B), jnp.float32),
        jax.ShapeDtypeStruct((N_EXPERTS, LANES), jnp.float32),
    ]
    out_specs = [
        pl.BlockSpec((rows, D_MODEL), tok),
        pl.BlockSpec((rows, HALF), tok),
        pl.BlockSpec((SUBLANES, rows), lane_tok),
        pl.BlockSpec((SUBLANES, rows), lane_tok),
        pl.BlockSpec((HALO_A, nseq, D_A), lambda i: (0, i, 0)),
        pl.BlockSpec((HALO_B, nseq, D_B), lambda i: (0, i, 0)),
        pl.BlockSpec((N_EXPERTS, LANES), lambda i: (0, 0)),
    ]
    any_spec = pl.BlockSpec(memory_space=pl.ANY)
    in_specs = ([pl.BlockSpec((nseq, tt, D_MODEL), lambda i: (i, 0, 0)),
                 pl.BlockSpec((HALO_A, nseq, D_A), lambda i: (0, i, 0)),
                 pl.BlockSpec((HALO_B, nseq, D_B), lambda i: (0, i, 0))]
                + _mixer_param_specs() + [pl.BlockSpec((N_EXPERTS, LANES), lambda i: (0, 0))] + [any_spec] * 4)
    n_in = len(in_specs)
    scratch = [
        pltpu.VMEM((rows, 3 * D_A), jnp.float32),
        pltpu.VMEM((D_B // LANES, rows, LANES), jnp.float32),
        pltpu.VMEM((D_A // LANES, rows, LANES), jnp.float32),
        pltpu.VMEM((HALO_B + tt, nseq, D_B), jnp.float32),
        pltpu.VMEM((HALO_A + tt, nseq, D_A), jnp.float32),
        pltpu.VMEM((D_B // LANES, rows, LANES), jnp.float32),
        pltpu.VMEM((D_A // LANES, rows, LANES), jnp.float32),
        pltpu.VMEM((rows, D_MODEL), jnp.bfloat16),
    ]
    return pl.pallas_call(
        functools.partial(_mixer_sample_kernel, nseq=nseq, tt=tt),
        grid=(batch // nseq,),
        in_specs=in_specs,
        out_specs=out_specs,
        out_shape=out_shape,
        scratch_shapes=scratch,
        input_output_aliases={n_in - 4: 0, n_in - 3: 1, n_in - 2: 2, n_in - 1: 3},
        compiler_params=pltpu.CompilerParams(
            dimension_semantics=("arbitrary",), vmem_limit_bytes=VMEM_LIMIT),
        name="mixer_sample",
    )(x, state_a, state_b, *params, counts, x1, h2p, eid, p)


def _plan_kernel(eid_ref, cnt_ref, pos_ref, table_ref, carry_ref, start_ref, earlier_ref, *, tile):
    i = pl.program_id(0)
    tt = eid_ref.shape[1]

    @pl.when(i == 0)
    def _():
        tiles = jnp.floor((cnt_ref[...] + (tile - 0.5)) * (1.0 / tile))
        below = (lax.broadcasted_iota(jnp.int32, (N_EXPERTS, N_EXPERTS), 0)
                 > lax.broadcasted_iota(jnp.int32, (N_EXPERTS, N_EXPERTS), 1))
        start = jnp.dot(jnp.where(below, 1.0, 0.0).astype(jnp.bfloat16), tiles.astype(jnp.bfloat16),
                        preferred_element_type=jnp.float32) * tile
        start_ref[...] = start
        seg_end = start + tiles * tile
        tile_row = lax.broadcasted_iota(jnp.int32, (N_EXPERTS, LANES), 1).astype(jnp.float32) * tile
        owner = jnp.sum(jnp.where(seg_end <= tile_row, 1.0, 0.0), axis=0, keepdims=True)
        owner = jnp.minimum(owner, N_EXPERTS - 1.0)
        n_used = jnp.sum(tiles, axis=0, keepdims=True)
        row = lax.broadcasted_iota(jnp.int32, (SUBLANES, LANES), 0)
        table_ref[...] = jnp.where(row == 0, owner, jnp.where(row == 1, n_used, 0.0)).astype(jnp.int32)
        earlier = (lax.broadcasted_iota(jnp.int32, (tt, tt), 0)
                   < lax.broadcasted_iota(jnp.int32, (tt, tt), 1))
        earlier_ref[...] = jnp.where(earlier, 1.0, 0.0).astype(jnp.bfloat16)
        carry_ref[...] = jnp.zeros_like(carry_ref)

    eid = eid_ref[...]
    experts = lax.broadcasted_iota(jnp.int32, (N_EXPERTS, tt), 0)
    oh0 = experts == eid[0:1]
    oh1 = experts == eid[1:2]
    oh = jnp.where(oh0 | oh1, 1.0, 0.0)
    within = jnp.dot(oh.astype(jnp.bfloat16), earlier_ref[...], preferred_element_type=jnp.float32)
    slot_of = within + carry_ref[:, 0:1] + start_ref[:, 0:1]
    s0 = jnp.sum(jnp.where(oh0, slot_of, 0.0), axis=0, keepdims=True)
    s1 = jnp.sum(jnp.where(oh1, slot_of, 0.0), axis=0, keepdims=True)
    k = lax.broadcasted_iota(jnp.int32, (SUBLANES, tt), 0)
    pos_ref[...] = jnp.where(k == 0, s0, jnp.where(k == 1, s1, 0.0)).astype(jnp.int32)
    carry_ref[...] = carry_ref[...] + jnp.sum(oh, axis=1, keepdims=True)


def _plan(eid, counts, tile):
    n_tokens = eid.shape[1]
    return pl.pallas_call(
        functools.partial(_plan_kernel, tile=tile),
        grid=(n_tokens // PLAN_TILE,),
        in_specs=[pl.BlockSpec((SUBLANES, PLAN_TILE), lambda i: (0, i)),
                  pl.BlockSpec((N_EXPERTS, LANES), lambda i: (0, 0))],
        out_specs=[pl.BlockSpec((SUBLANES, PLAN_TILE), lambda i: (0, i)),
                   pl.BlockSpec((SUBLANES, LANES), lambda i: (0, 0))],
        out_shape=[jax.ShapeDtypeStruct((SUBLANES, n_tokens), jnp.int32),
                   jax.ShapeDtypeStruct((SUBLANES, LANES), jnp.int32)],
        scratch_shapes=[pltpu.VMEM((N_EXPERTS, LANES), jnp.float32),
                        pltpu.VMEM((N_EXPERTS, LANES), jnp.float32),
                        pltpu.VMEM((PLAN_TILE, PLAN_TILE), jnp.bfloat16)],
        compiler_params=pltpu.CompilerParams(dimension_semantics=("arbitrary",)),
        name="route_plan",
    )(eid, counts)


def _sc_mesh():
    return plsc.VectorSubcoreMesh(core_axis_name="c", subcore_axis_name="s")


def _sc_dispatch_rows(table, pos, n_out):
    n_workers, n_batches, top_k, batch = pos.shape
    n_rows, words = table.shape
    assert n_workers * n_batches * batch == n_rows and n_batches >= 2

    @functools.partial(
        pl.kernel, mesh=_sc_mesh(),
        out_type=jax.ShapeDtypeStruct((n_out, words), table.dtype),
        scratch_types=[pltpu.VMEM((n_batches, top_k, batch), jnp.int32),
                       pltpu.VMEM((2, batch, words), table.dtype),
                       pltpu.SemaphoreType.DMA((2,)),
                       pltpu.SemaphoreType.DMA((2, top_k))],
    )
    def dispatch(table_hbm, pos_hbm, out_hbm, idx_v, rows_v, sem_in, sem_out):
        worker = lax.axis_index("s") * SC_CORES + lax.axis_index("c")
        pltpu.sync_copy(pos_hbm.at[worker], idx_v)

        def read(b):
            src = table_hbm.at[pl.ds((worker * n_batches + b) * batch, batch)]
            return pltpu.async_copy(src, rows_v.at[b % 2], sem_in.at[b % 2])

        def write(b):
            return [pltpu.async_copy(rows_v.at[b % 2], out_hbm.at[idx_v.at[b, k]], sem_out.at[b % 2, k])
                    for k in range(top_k)]

        reads = {0: read(0)}
        writes = {}
        for b in range(n_batches):
            reads[b].wait()
            if b + 1 < n_batches:
                if b >= 1:
                    for w in writes[b - 1]:
                        w.wait()
                reads[b + 1] = read(b + 1)
            writes[b] = write(b)
        for b in (n_batches - 2, n_batches - 1):
            for w in writes[b]:
                w.wait()

    return dispatch(table, pos)


def _sc_gather_rows(table, idx):
    n_workers, n_batches, batch = idx.shape
    words = table.shape[1]
    assert n_batches >= 2

    @functools.partial(
        pl.kernel, mesh=_sc_mesh(),
        out_type=jax.ShapeDtypeStruct((n_workers * n_batches * batch, words), table.dtype),
        scratch_types=[pltpu.VMEM((n_batches, batch), jnp.int32),
                       pltpu.VMEM((2, batch, words), table.dtype),
                       pltpu.SemaphoreType.DMA((2,)),
                       pltpu.SemaphoreType.DMA((2,))],
    )
    def gather(table_hbm, idx_hbm, out_hbm, idx_v, rows_v, sem_in, sem_out):
        worker = lax.axis_index("s") * SC_CORES + lax.axis_index("c")
        pltpu.sync_copy(idx_hbm.at[worker], idx_v)

        def read(b):
            return pltpu.async_copy(table_hbm.at[idx_v.at[b]], rows_v.at[b % 2], sem_in.at[b % 2])

        def write(b):
            dst = out_hbm.at[pl.ds((worker * n_batches + b) * batch, batch)]
            return pltpu.async_copy(rows_v.at[b % 2], dst, sem_out.at[b % 2])

        reads = {0: read(0)}
        writes = {}
        for b in range(n_batches):
            reads[b].wait()
            if b + 1 < n_batches:
                if b >= 1:
                    writes[b - 1].wait()
                reads[b + 1] = read(b + 1)
            writes[b] = write(b)
        writes[n_batches - 2].wait()
        writes[n_batches - 1].wait()

    return gather(table, idx)


def _experts_kernel(table_ref, xs_ref, wg_ref, wu_ref, wd_ref, ys_ref):
    hi, lo = _unpack_bf16_halves(xs_ref[...])
    x = jnp.concatenate([hi.astype(jnp.bfloat16), lo.astype(jnp.bfloat16)], axis=1)
    w_gate_up = jnp.concatenate([wg_ref[0].astype(jnp.bfloat16), wu_ref[0].astype(jnp.bfloat16)], axis=1)
    wd = wd_ref[0].astype(jnp.bfloat16)
    gate_up = jnp.dot(x, w_gate_up, preferred_element_type=jnp.float32)
    gate = gate_up[:, :D_EXPERT]
    up = gate_up[:, D_EXPERT:]
    hid = (gate * jax.nn.sigmoid(gate) * up).astype(jnp.bfloat16)
    ys_ref[...] = _pack_bf16_halves(jnp.dot(hid, wd, preferred_element_type=jnp.float32))


def _experts(xs, table, wg, wu, wd, tm):
    n_slots = xs.shape[0]
    n_valid = table[1, 0]
    row_block = lambda i, tab: (i, 0)
    w_block = lambda i, tab: (tab[0, i], 0, 0)
    return pl.pallas_call(
        _experts_kernel,
        grid_spec=pltpu.PrefetchScalarGridSpec(
            num_scalar_prefetch=1,
            grid=(n_valid,),
            in_specs=[pl.BlockSpec((tm, HALF), row_block),
                      pl.BlockSpec((1, D_MODEL, D_EXPERT), w_block, pipeline_mode=pl.Buffered(2)),
                      pl.BlockSpec((1, D_MODEL, D_EXPERT), w_block, pipeline_mode=pl.Buffered(2)),
                      pl.BlockSpec((1, D_EXPERT, D_MODEL), w_block, pipeline_mode=pl.Buffered(2))],
            out_specs=pl.BlockSpec((tm, HALF), row_block),
        ),
        out_shape=jax.ShapeDtypeStruct((n_slots, HALF), jnp.uint32),
        compiler_params=pltpu.CompilerParams(
            dimension_semantics=("arbitrary",), vmem_limit_bytes=VMEM_LIMIT),
        name="experts",
    )(table, xs, wg, wu, wd)


def _final_kernel(x1_ref, y0_ref, y1_ref, p_ref, gfin_ref, prompt_ref, sample_ref, *, prompt_steps):
    pt = jnp.transpose(p_ref[...])
    p0 = pt[:, 0:1]
    p1 = pt[:, 1:2]
    a_hi, a_lo = _unpack_bf16_halves(y0_ref[...])
    b_hi, b_lo = _unpack_bf16_halves(y1_ref[...])
    x1 = x1_ref[...]
    x2_hi = x1[:, :HALF] + (p0 * a_hi + p1 * b_hi)
    x2_lo = x1[:, HALF:] + (p0 * a_lo + p1 * b_lo)
    ms = (jnp.sum(x2_hi * x2_hi, axis=-1, keepdims=True)
          + jnp.sum(x2_lo * x2_lo, axis=-1, keepdims=True)) / D_MODEL
    scale = lax.rsqrt(ms + RMS_EPS)
    g = gfin_ref[...]
    out_hi = x2_hi * scale * g[:, :HALF]
    out_lo = x2_lo * scale * g[:, HALF:]

    @pl.when(pl.program_id(0) < prompt_steps)
    def _():
        prompt_ref[:, :HALF] = out_hi
        prompt_ref[:, HALF:] = out_lo

    @pl.when(pl.program_id(0) >= prompt_steps)
    def _():
        sample_ref[:, :HALF] = out_hi
        sample_ref[:, HALF:] = out_lo


def _final(x1, yk, p, gfin, n_prompt, tm):
    n_tokens = x1.shape[0]
    n_sample = n_tokens - n_prompt
    assert n_prompt % tm == 0 and n_sample % tm == 0 and n_prompt > 0 and n_sample > 0
    prompt_steps = n_prompt // tm
    second = n_tokens // tm
    return pl.pallas_call(
        functools.partial(_final_kernel, prompt_steps=prompt_steps),
        grid=(n_tokens // tm,),
        in_specs=[pl.BlockSpec((tm, D_MODEL), lambda i: (i, 0)),
                  pl.BlockSpec((tm, HALF), lambda i: (i, 0)),
                  pl.BlockSpec((tm, HALF), lambda i: (second + i, 0)),
                  pl.BlockSpec((SUBLANES, tm), lambda i: (0, i)),
                  pl.BlockSpec((1, D_MODEL), lambda i: (0, 0))],
        out_specs=[pl.BlockSpec((tm, D_MODEL), lambda i: (jnp.minimum(i, prompt_steps - 1), 0)),
                   pl.BlockSpec((tm, D_MODEL), lambda i: (jnp.maximum(i - prompt_steps, 0), 0))],
        out_shape=[jax.ShapeDtypeStruct((n_prompt, D_MODEL), jnp.float32),
                   jax.ShapeDtypeStruct((n_sample, D_MODEL), jnp.float32)],
        compiler_params=pltpu.CompilerParams(
            dimension_semantics=("arbitrary",), vmem_limit_bytes=VMEM_LIMIT),
        name="final",
    )(x1, yk, yk, p, gfin)


def _expert_tile(n_tokens):
    mean_rows = TOP_K * n_tokens // N_EXPERTS
    return -(-(mean_rows + 2 * math.isqrt(mean_rows)) // (2 * SUBLANES)) * (2 * SUBLANES)


def _routed_experts(h2p, eid, counts, w_gate, w_up, w_down):
    n_tokens = h2p.shape[0]
    assert n_tokens % (SC_WORKERS * SC_DISPATCH_BATCH) == 0 and n_tokens % PLAN_TILE == 0
    assert (TOP_K * n_tokens) % (SC_WORKERS * SC_BATCH) == 0
    tile = _expert_tile(n_tokens)
    n_tiles = -(-(TOP_K * n_tokens + N_EXPERTS * (tile - 1)) // tile)
    assert n_tiles <= LANES
    pos, table = _plan(eid, counts, tile)
    pos = pos[:TOP_K]
    by_token = jnp.transpose(pos.reshape(TOP_K, SC_WORKERS, -1, SC_DISPATCH_BATCH), (1, 2, 0, 3))
    xs = _sc_dispatch_rows(h2p, by_token, n_tiles * tile)
    ys = _experts(xs, table, w_gate, w_up, w_down, tile)
    return _sc_gather_rows(ys, pos.reshape(SC_WORKERS, -1, SC_BATCH))


def kernel(x_prompt, x_sample, state_conv_a, state_conv_b, g_mix, w_in, conv_a_w, conv_b_w, conv_b_bias,
           ln_g, ln_b, w_out, g_ffn, w_coarse, b_coarse, w_fine, b_fine, w_gate, w_up, w_down, g_final):
    assert g_mix.shape[0] == 1, "single trunk layer"
    batch, seq, _ = x_prompt.shape
    dec_batch, dec_seq, _ = x_sample.shape
    n_prompt = batch * seq
    n_sample = dec_batch * dec_seq
    bf16 = jnp.bfloat16

    wr = jnp.concatenate([
        w_coarse[0], jnp.zeros((D_MODEL, SUBLANES - N_EXPERT_GROUPS), jnp.float32),
        jnp.transpose(w_fine[0], (1, 0, 2)).reshape(D_MODEL, N_EXPERTS),
        jnp.zeros((D_MODEL, LANES - ROUTER_ROWS), jnp.float32)], axis=1)
    wr_hi = wr.astype(bf16)
    wr_lo = (wr - wr_hi.astype(jnp.float32)).astype(bf16)
    wr_both = jnp.concatenate([wr_hi, wr_lo], axis=1)
    br = jnp.concatenate([
        b_coarse[0], jnp.full((SUBLANES - N_EXPERT_GROUPS,), NEG_BIG, jnp.float32),
        b_fine[0].reshape(N_EXPERTS)]).reshape(ROUTER_ROWS, 1)

    params = (g_mix, w_in[0].astype(bf16), conv_a_w, conv_b_w, conv_b_bias, ln_g, ln_b,
              w_out[0].astype(bf16), g_ffn, wr_both, br)

    experts = (w_gate[0], w_up[0], w_down[0])
    gfin = g_final.reshape(1, D_MODEL)

    bufs = _mixer_prompt(x_prompt, params, n_prompt + n_sample, MIXER_TILE, MIXER_TILES_PER_STEP, 0, batch)
    na_p, nb_p, counts = bufs[4:]
    x1, h2p, eid, p, na_s, nb_s, counts = _mixer_sample(
        x_sample, jnp.transpose(state_conv_a[0], (1, 0, 2)), jnp.transpose(state_conv_b[0], (1, 0, 2)),
        params, counts, bufs[:4], n_prompt, nseq=32)
    na_s = jnp.transpose(na_s, (1, 0, 2))
    nb_s = jnp.transpose(nb_s, (1, 0, 2))
    yk = _routed_experts(h2p, eid, counts, *experts)
    y_p, y_s = _final(x1, yk, p, gfin, n_prompt, FINAL_TILE)
    return (y_p.reshape(batch, seq, D_MODEL), y_s.reshape(dec_batch, dec_seq, D_MODEL),
            na_p[None], nb_p[None], na_s[None], nb_s[None])
```
